```python
import math
import jax
import jax.numpy as jnp
from jax import lax
import numpy as np

D_MODEL = 1024
BATCH = 16
SEQ = 4096
DEPTH = 1
DEC_BATCH = 8
DEC_SEQ = 16
PAST_LEN = 2048

CHUNK = 64
WINDOW = 128
WIN_CHUNKS = WINDOW // CHUNK
N_HEADS_A = 8
N_KV_A = 2
GROUP_A = N_HEADS_A // N_KV_A
HEAD_DIM_A = 64
N_BUCKETS = 32
MAX_DIST = 128
N_HEADS_R = 4
DK_R = 128
DV_R = 128
ROPE_BASE = 10000.0
N_BRANCH = 2
N_EXPERTS = 256
TOP_K = 8
N_GROUPS = 8
TOPK_GROUPS = 4
D_EXPERT = 256
D_SHARED = 256
ROUTED_SCALE = 2.5
MOE_BLOCK = 64
D_PLE = 256
LN_EPS = 1e-5
NEG_INF = -1e30
DN_ALPHA = (2 * DEPTH) ** 0.25
DN_BETA = (8 * DEPTH) ** -0.25
QA_W = N_HEADS_A * HEAD_DIM_A
KA_W = N_KV_A * HEAD_DIM_A
QR_W = N_HEADS_R * DK_R
VR_W = N_HEADS_R * DV_R
IN_SIZES = (QA_W, KA_W, KA_W, QR_W, QR_W, VR_W, VR_W, N_BRANCH * D_MODEL)
IN_W = QA_W + 2 * KA_W + 2 * QR_W + 2 * VR_W + N_BRANCH * D_MODEL

kernel_name = 'hybrid_stream_encoder_step'


def layer_norm(x, g, b):
    xf = x.astype(jnp.float32)
    mu = jnp.mean(xf, -1, keepdims=True)
    var = jnp.mean(jnp.square(xf - mu), -1, keepdims=True)
    return ((xf - mu) * lax.rsqrt(var + LN_EPS) * g.astype(jnp.float32) + b.astype(jnp.float32)).astype(x.dtype)


def head_group_norm(o, g):
    of = o.astype(jnp.float32)
    mu = jnp.mean(of, -1, keepdims=True)
    var = jnp.mean(jnp.square(of - mu), -1, keepdims=True)
    y = ((of - mu) * lax.rsqrt(var + LN_EPS)).reshape(o.shape[:-2] + (VR_W,))
    return (y * g.astype(jnp.float32)).astype(o.dtype)


def split_in(z):
    out, o = [], 0
    for n in IN_SIZES:
        out.append(z[..., o:o + n])
        o += n
    return out


def t5_bucket(rel):
    nb = N_BUCKETS // 2
    max_exact = nb // 2
    ret = jnp.where(rel > 0, nb, 0)
    n = jnp.abs(rel)
    nf = jnp.maximum(n, 1).astype(jnp.float32)
    large = max_exact + (jnp.log(nf / max_exact) / math.log(MAX_DIST / max_exact) * (nb - max_exact)).astype(jnp.int32)
    large = jnp.minimum(large, nb - 1)
    return ret + jnp.where(n < max_exact, n, large)


def rel_bias(table, q_len, k_len):
    rel = (jnp.arange(k_len, dtype=jnp.int32)[None, :] - WINDOW) - jnp.arange(q_len, dtype=jnp.int32)[:, None]
    b = table[t5_bucket(rel)]
    return jnp.transpose(b, (2, 0, 1)).reshape(N_KV_A, GROUP_A, q_len, k_len)


def rotary(x, pos):
    half = x.shape[-1] // 2
    inv = ROPE_BASE ** (-jnp.arange(half, dtype=jnp.float32) / half)
    ang = pos.astype(jnp.float32)[:, None] * inv[None, :]
    cos = jnp.cos(ang)[:, None, :]
    sin = jnp.sin(ang)[:, None, :]
    x1 = x[..., :half].astype(jnp.float32)
    x2 = x[..., half:].astype(jnp.float32)
    return jnp.concatenate([x1 * cos - x2 * sin, x1 * sin + x2 * cos], -1).astype(x.dtype)


def chunk_band(k):
    B, S = k.shape[:2]
    nC = S // CHUNK
    kp = jnp.concatenate([jnp.zeros((B, WINDOW) + k.shape[2:], k.dtype), k], 1)
    kp = kp.reshape((B, nC + WIN_CHUNKS, CHUNK) + k.shape[2:])
    return jnp.concatenate([kp[:, w:w + nC] for w in range(WIN_CHUNKS + 1)], axis=2)


def sink_attention(q, k, v, bias, valid, sink):
    s = jnp.einsum('bnqhgd,bnkhd->bnhgqk', q, k).astype(jnp.float32) * (HEAD_DIM_A ** -0.5) + bias.astype(jnp.float32)
    s = jnp.where(valid[None, :, None, None, None, :], s, NEG_INF)
    sk = sink.astype(jnp.float32).reshape(N_KV_A, GROUP_A, 1)
    m = jnp.maximum(jnp.max(s, -1), sk)
    e = jnp.exp(s - m[..., None])
    p = e / (jnp.sum(e, -1) + jnp.exp(sk - m))[..., None]
    return jnp.einsum('bnhgqk,bnkhd->bnqhgd', p.astype(v.dtype), v)


def retention_chunks(q, k, v, s0, log_gamma):
    C = q.shape[2]
    idx = jnp.arange(C, dtype=jnp.float32)
    decay_intra = jnp.exp(jnp.abs(idx[:, None] - idx[None, :])[None] * log_gamma[:, None, None])
    sc = jnp.einsum('bnihd,bnjhd->bnhij', q, k).astype(jnp.float32) * decay_intra
    o_intra = jnp.einsum('bnhij,bnjhe->bnihe', sc, v.astype(jnp.float32))
    q_dec = q.astype(jnp.float32) * jnp.exp((idx[:, None] + 1.0) * log_gamma[None, :])[:, :, None]
    k_dec = k.astype(jnp.float32) * jnp.exp((C - 1.0 - idx[:, None]) * log_gamma[None, :])[:, :, None]
    g_chunk = jnp.exp(C * log_gamma)[None, :, None, None]

    def step(state, inp):
        qc, kc, vc = inp
        o_inter = jnp.einsum('bihd,bhde->bihe', qc, state)
        state = g_chunk * state + jnp.einsum('bjhd,bjhe->bhde', kc, vc.astype(jnp.float32))
        return state, o_inter

    xs = (jnp.moveaxis(q_dec, 1, 0), jnp.moveaxis(k_dec, 1, 0), jnp.moveaxis(v, 1, 0))
    s_final, o_inter = lax.scan(step, s0.astype(jnp.float32), xs)
    o = o_intra + jnp.moveaxis(o_inter, 0, 1)
    return o.astype(v.dtype), s_final.astype(s0.dtype)


def mix_sublayer(x, o_attn, o_ret, g_ret, gates, ret_gn_gain, w_branch_attn, w_branch_ret, w_out, ln1_g, ln1_b):
    o_r = head_group_norm(o_ret, ret_gn_gain) * jax.nn.silu(g_ret)
    g = jax.nn.sigmoid(gates.astype(jnp.float32)).astype(x.dtype)
    merged = g[..., :D_MODEL] * (o_attn @ w_branch_attn) + g[..., D_MODEL:] * (o_r @ w_branch_ret)
    return layer_norm(DN_ALPHA * x + merged @ w_out, ln1_g, ln1_b)


def route(xf, w_router, router_bias):
    s = jax.nn.sigmoid(jnp.dot(xf, w_router).astype(jnp.float32))
    choice = s + router_bias.astype(jnp.float32)
    T = s.shape[0]
    grp = choice.reshape(T, N_GROUPS, N_EXPERTS // N_GROUPS)
    grp_score = jnp.sum(lax.top_k(grp, 2)[0], -1)
    _, gidx = lax.top_k(grp_score, TOPK_GROUPS)
    gmask = jnp.sum(jax.nn.one_hot(gidx, N_GROUPS, dtype=jnp.float32), 1) > 0
    emask = jnp.repeat(gmask, N_EXPERTS // N_GROUPS, axis=1)
    _, eidx = lax.top_k(jnp.where(emask, choice, -jnp.inf), TOP_K)
    w = jnp.take_along_axis(s, eidx, axis=1)
    w = w / jnp.sum(w, -1, keepdims=True) * ROUTED_SCALE
    return eidx, w


def routed_experts(xf, eidx, w, w_gate, w_up, w_down):
    T = xf.shape[0]
    A = T * TOP_K
    n_rows = -(-A // MOE_BLOCK) * MOE_BLOCK + N_EXPERTS * MOE_BLOCK
    n_blocks = n_rows // MOE_BLOCK
    e_flat = eidx.reshape(A)
    t_flat = jnp.repeat(jnp.arange(T, dtype=jnp.int32), TOP_K)
    w_flat = w.reshape(A)
    order = jnp.argsort(e_flat)
    e_sorted = e_flat[order]
    counts = jnp.zeros((N_EXPERTS,), jnp.int32).at[e_flat].add(1)
    padded = (counts + MOE_BLOCK - 1) // MOE_BLOCK * MOE_BLOCK
    pad_end = jnp.cumsum(padded)
    pad_start = pad_end - padded
    start = jnp.cumsum(counts) - counts
    dest = pad_start[e_sorted] + jnp.arange(A, dtype=jnp.int32) - start[e_sorted]
    row_tok = jnp.full((n_rows,), T, jnp.int32).at[dest].set(t_flat[order])
    row_w = jnp.zeros((n_rows,), jnp.float32).at[dest].set(w_flat[order])
    blk_start = jnp.arange(n_blocks, dtype=jnp.int32) * MOE_BLOCK
    blk_exp = jnp.minimum(jnp.searchsorted(pad_end, blk_start, side='right'), N_EXPERTS - 1)
    x_pad = jnp.concatenate([xf, jnp.zeros((1, xf.shape[1]), xf.dtype)], 0)

    def step(acc, inp):
        tok, wt, e = inp
        xb = x_pad[tok]
        h = jax.nn.silu(xb @ w_gate[e]) * (xb @ w_up[e])
        yb = (h @ w_down[e]).astype(jnp.float32) * wt[:, None]
        return acc.at[tok].add(yb), None

    acc0 = jnp.zeros((T + 1, xf.shape[1]), jnp.float32)
    acc, _ = lax.scan(step, acc0, (row_tok.reshape(n_blocks, MOE_BLOCK), row_w.reshape(n_blocks, MOE_BLOCK), blk_exp))
    return acc[:T]


def ffn_sublayer(x, p, w_router, router_bias, w_exp_gate, w_exp_up, w_exp_down,
                 w_sh_gate, w_sh_up, w_sh_down, ln2_g, ln2_b, w_ple_proj, w_ple_gate):
    B, S, D = x.shape
    xf = x.reshape(B * S, D)
    eidx, w = route(xf, w_router, router_bias)
    y = routed_experts(xf, eidx, w, w_exp_gate, w_exp_up, w_exp_down).astype(x.dtype)
    y = y + (jax.nn.silu(xf @ w_sh_gate) * (xf @ w_sh_up)) @ w_sh_down
    x = layer_norm(DN_ALPHA * x + y.reshape(B, S, D), ln2_g, ln2_b)
    return x + jax.nn.sigmoid(x @ w_ple_gate) * (p @ w_ple_proj)


def setup_inputs(seed: int = 0) -> dict:
    key = jax.random.key(seed)
    ks = jax.random.split(key, 32)
    f32 = jnp.float32

    def nrm(i, shape, scale):
        return jax.random.normal(ks[i], shape, f32) * scale

    col_scale = jnp.concatenate([
        jnp.ones((QA_W + KA_W,), f32), jnp.full((KA_W,), DN_BETA, f32),
        jnp.ones((2 * QR_W,), f32), jnp.full((VR_W,), DN_BETA, f32),
        jnp.ones((VR_W + N_BRANCH * D_MODEL,), f32)])
    return {
        'x_prompt': nrm(0, (BATCH, SEQ, D_MODEL), 1.0),
        'x_sample': nrm(1, (DEC_BATCH, DEC_SEQ, D_MODEL), 1.0),
        'cache_attn_k': nrm(2, (DEPTH, DEC_BATCH, WINDOW, N_KV_A, HEAD_DIM_A), 1.0),
        'cache_attn_v': nrm(3, (DEPTH, DEC_BATCH, WINDOW, N_KV_A, HEAD_DIM_A), DN_BETA),
        'state_retention': nrm(4, (DEPTH, DEC_BATCH, N_HEADS_R, DK_R, DV_R), 0.5),
        'p_prompt': nrm(5, (DEPTH, BATCH, SEQ, D_PLE), 1.0),
        'p_sample': nrm(6, (DEPTH, DEC_BATCH, DEC_SEQ, D_PLE), 1.0),
        'w_in': nrm(7, (DEPTH, D_MODEL, IN_W), D_MODEL ** -0.5) * col_scale,
        'attn_sinks': nrm(8, (DEPTH, N_HEADS_A), 0.5),
        'rel_bias_table': nrm(9, (N_BUCKETS, N_HEADS_A), 0.3),
        'ret_gn_gain': 1.0 + nrm(10, (DEPTH, VR_W), 0.01),
        'w_branch_attn': nrm(11, (DEPTH, QA_W, D_MODEL), QA_W ** -0.5 * DN_BETA),
        'w_branch_ret': nrm(12, (DEPTH, VR_W, D_MODEL), VR_W ** -0.5 * DN_BETA),
        'w_out': nrm(13, (DEPTH, D_MODEL, D_MODEL), D_MODEL ** -0.5 * DN_BETA),
        'ln1_g': 1.0 + nrm(14, (DEPTH, D_MODEL), 0.01),
        'ln1_b': nrm(15, (DEPTH, D_MODEL), 0.01),
        'w_router': nrm(16, (DEPTH, D_MODEL, N_EXPERTS), D_MODEL ** -0.5),
        'router_bias': nrm(17, (DEPTH, N_EXPERTS), 0.01),
        'w_exp_gate': nrm(18, (DEPTH, N_EXPERTS, D_MODEL, D_EXPERT), D_MODEL ** -0.5),
        'w_exp_up': nrm(19, (DEPTH, N_EXPERTS, D_MODEL, D_EXPERT), D_MODEL ** -0.5),
        'w_exp_down': nrm(20, (DEPTH, N_EXPERTS, D_EXPERT, D_MODEL), D_EXPERT ** -0.5 * DN_BETA),
        'w_sh_gate': nrm(21, (DEPTH, D_MODEL, D_SHARED), D_MODEL ** -0.5),
        'w_sh_up': nrm(22, (DEPTH, D_MODEL, D_SHARED), D_MODEL ** -0.5),
        'w_sh_down': nrm(23, (DEPTH, D_SHARED, D_MODEL), D_SHARED ** -0.5 * DN_BETA),
        'ln2_g': 1.0 + nrm(24, (DEPTH, D_MODEL), 0.01),
        'ln2_b': nrm(25, (DEPTH, D_MODEL), 0.01),
        'w_ple_proj': nrm(26, (DEPTH, D_PLE, D_MODEL), D_PLE ** -0.5),
        'w_ple_gate': nrm(27, (DEPTH, D_MODEL, D_MODEL), D_MODEL ** -0.5),
    }


def reference(x_prompt, x_sample, cache_attn_k, cache_attn_v, state_retention, p_prompt, p_sample,
              w_in, attn_sinks, rel_bias_table, ret_gn_gain, w_branch_attn, w_branch_ret, w_out,
              ln1_g, ln1_b, w_router, router_bias, w_exp_gate, w_exp_up, w_exp_down,
              w_sh_gate, w_sh_up, w_sh_down, ln2_g, ln2_b, w_ple_proj, w_ple_gate):
    B, S, _ = x_prompt.shape
    Bd, L, _ = x_sample.shape
    nC = S // CHUNK
    pos_p = jnp.arange(S, dtype=jnp.int32)
    pos_s = PAST_LEN + jnp.arange(L, dtype=jnp.int32)
    bias_p = rel_bias(rel_bias_table, CHUNK, WINDOW + CHUNK)
    bias_s = rel_bias(rel_bias_table, L, WINDOW + L)
    valid_p = (jnp.arange(nC)[:, None] * CHUNK + jnp.arange(WINDOW + CHUNK)[None, :]) >= WINDOW
    valid_s = jnp.ones((1, WINDOW + L), dtype=bool)
    log_gamma = jnp.log1p(-jnp.exp2(-5.0 - jnp.arange(N_HEADS_R, dtype=jnp.float32)))
    xp, xs = x_prompt, x_sample
    nk_p, nv_p, ns_p, nk_s, nv_s, ns_s = [], [], [], [], [], []
    for i in range(DEPTH):
        qa, ka, va, qr, kr, vr, gr, gm = split_in(jnp.einsum('bsd,de->bse', xp, w_in[i]))
        qa = qa.reshape(B, nC, CHUNK, N_KV_A, GROUP_A, HEAD_DIM_A)
        ka = ka.reshape(B, S, N_KV_A, HEAD_DIM_A)
        va = va.reshape(B, S, N_KV_A, HEAD_DIM_A)
        oa = sink_attention(qa, chunk_band(ka), chunk_band(va), bias_p, valid_p, attn_sinks[i]).reshape(B, S, QA_W)
        qr = rotary(qr.reshape(B, S, N_HEADS_R, DK_R), pos_p).reshape(B, nC, CHUNK, N_HEADS_R, DK_R)
        kr = (rotary(kr.reshape(B, S, N_HEADS_R, DK_R), pos_p) * (DK_R ** -0.5)).reshape(B, nC, CHUNK, N_HEADS_R, DK_R)
        vr = vr.reshape(B, nC, CHUNK, N_HEADS_R, DV_R)
        s0 = jnp.zeros((B, N_HEADS_R, DK_R, DV_R), xp.dtype)
        orr, s_p = retention_chunks(qr, kr, vr, s0, log_gamma)
        orr = orr.reshape(B, S, N_HEADS_R, DV_R)
        xp = mix_sublayer(xp, oa, orr, gr, gm, ret_gn_gain[i], w_branch_attn[i], w_branch_ret[i], w_out[i], ln1_g[i], ln1_b[i])
        xp = ffn_sublayer(xp, p_prompt[i], w_router[i], router_bias[i], w_exp_gate[i], w_exp_up[i], w_exp_down[i],
                          w_sh_gate[i], w_sh_up[i], w_sh_down[i], ln2_g[i], ln2_b[i], w_ple_proj[i], w_ple_gate[i])
        nk_p.append(ka[:, S - WINDOW:])
        nv_p.append(va[:, S - WINDOW:])
        ns_p.append(s_p)

        qa, ka, va, qr, kr, vr, gr, gm = split_in(jnp.einsum('bsd,de->bse', xs, w_in[i]))
        qa = qa.reshape(Bd, 1, L, N_KV_A, GROUP_A, HEAD_DIM_A)
        k_all = jnp.concatenate([cache_attn_k[i], ka.reshape(Bd, L, N_KV_A, HEAD_DIM_A)], 1)
        v_all = jnp.concatenate([cache_attn_v[i], va.reshape(Bd, L, N_KV_A, HEAD_DIM_A)], 1)
        oa = sink_attention(qa, k_all[:, None], v_all[:, None], bias_s, valid_s, attn_sinks[i]).reshape(Bd, L, QA_W)
        qr = rotary(qr.reshape(Bd, L, N_HEADS_R, DK_R), pos_s).reshape(Bd, 1, L, N_HEADS_R, DK_R)
        kr = (rotary(kr.reshape(Bd, L, N_HEADS_R, DK_R), pos_s) * (DK_R ** -0.5)).reshape(Bd, 1, L, N_HEADS_R, DK_R)
        vr = vr.reshape(Bd, 1, L, N_HEADS_R, DV_R)
        orr, s_s = retention_chunks(qr, kr, vr, state_retention[i], log_gamma)
        orr = orr.reshape(Bd, L, N_HEADS_R, DV_R)
        xs = mix_sublayer(xs, oa, orr, gr, gm, ret_gn_gain[i], w_branch_attn[i], w_branch_ret[i], w_out[i], ln1_g[i], ln1_b[i])
        xs = ffn_sublayer(xs, p_sample[i], w_router[i], router_bias[i], w_exp_gate[i], w_exp_up[i], w_exp_down[i],
                          w_sh_gate[i], w_sh_up[i], w_sh_down[i], ln2_g[i], ln2_b[i], w_ple_proj[i], w_ple_gate[i])
        nk_s.append(k_all[:, L:])
        nv_s.append(v_all[:, L:])
        ns_s.append(s_s)
    return (xp, xs, jnp.stack(nk_p), jnp.stack(nv_p), jnp.stack(ns_p), jnp.stack(nk_s), jnp.stack(nv_s), jnp.stack(ns_s))
```

```python
import functools
import math

import jax
import jax.numpy as jnp
from jax import lax
from jax.experimental import pallas as pl
from jax.experimental.pallas import tpu as pltpu

F32 = jnp.float32
BF16 = jnp.bfloat16
I32 = jnp.int32

D_MODEL = 1024
CHUNK = 64
WINDOW = 128
N_HEADS_A = 8
N_KV_A = 2
GROUP_A = 4
HEAD_DIM_A = 64
N_BUCKETS = 32
MAX_DIST = 128
N_HEADS_R = 4
DK_R = 128
ROPE_BASE = 10000.0
N_EXPERTS = 256
TOP_K = 8
N_GROUPS = 8
GROUP_SIZE = N_EXPERTS // N_GROUPS
TOPK_GROUPS = 4
D_EXPERT = 256
ROUTED_SCALE = 2.5
D_PLE = 256
LN_EPS = 1e-5
NEG_INF = -1e30
PAST_LEN = 2048
DEPTH = 1
DN_ALPHA = (2 * DEPTH) ** 0.25

QA_W = N_HEADS_A * HEAD_DIM_A
KA_W = N_KV_A * HEAD_DIM_A
QR_W = N_HEADS_R * DK_R
OFF_KV = QA_W
OFF_QR = QA_W + 2 * KA_W
OFF_VR = OFF_QR + 2 * QR_W
OFF_GM = OFF_VR + 2 * QR_W
IN_W = OFF_GM + 2 * D_MODEL

MOE_ROWS = 256
VMEM_LIMIT = 56 * 1024 * 1024


def _cparams(sem, vmem=VMEM_LIMIT):
    return pltpu.CompilerParams(dimension_semantics=sem, vmem_limit_bytes=vmem)


def _layer_norm(h, g, b):
    mu = jnp.mean(h, axis=-1, keepdims=True)
    d = h - mu
    var = jnp.mean(d * d, axis=-1, keepdims=True)
    return d * lax.rsqrt(var + LN_EPS) * g + b


def _silu(x):
    return x * jax.nn.sigmoid(x)


def _inproj_kernel(x_ref, w_ref, cos_ref, sin_ref, qkv_ref, kvf_ref, qkr_ref, vg_ref, gm_ref):
    xb = x_ref[...].astype(BF16)

    def mm(lo, hi):
        return jnp.dot(xb, w_ref[:, lo:hi], preferred_element_type=F32)

    qkv_ref[:, 0:QA_W] = mm(0, QA_W).astype(BF16)
    kv = mm(OFF_KV, OFF_QR)
    kvf_ref[...] = kv
    qkv_ref[:, OFF_KV:OFF_QR] = kv.astype(BF16)
    cos = cos_ref[...]
    sin = sin_ref[...]
    for part in range(2):
        z = mm(OFF_QR + part * QR_W, OFF_QR + (part + 1) * QR_W)
        for h in range(N_HEADS_R):
            zh = z[:, h * DK_R:(h + 1) * DK_R]
            r = zh * cos + pltpu.roll(zh, DK_R // 2, axis=1) * sin
            if part == 1:
                r = r * (DK_R ** -0.5)
            c0 = part * QR_W + h * DK_R
            qkr_ref[:, c0:c0 + DK_R] = r.astype(BF16)
    vg_ref[...] = mm(OFF_VR, OFF_GM).astype(BF16)
    for half in range(2):
        lo = OFF_GM + half * D_MODEL
        gm_ref[:, half * D_MODEL:(half + 1) * D_MODEL] = jax.nn.sigmoid(mm(lo, lo + D_MODEL)).astype(BF16)


def _inproj(x2d, w_bf, cos_tab, sin_tab, tm):
    t = x2d.shape[0]
    nper = cos_tab.shape[0] // tm
    row = lambda i: (i, 0)
    return pl.pallas_call(
        _inproj_kernel,
        grid=(t // tm,),
        in_specs=[
            pl.BlockSpec((tm, D_MODEL), row),
            pl.BlockSpec((D_MODEL, IN_W), lambda i: (0, 0)),
            pl.BlockSpec((tm, DK_R), lambda i: (i % nper, 0)),
            pl.BlockSpec((tm, DK_R), lambda i: (i % nper, 0)),
        ],
        out_specs=[
            pl.BlockSpec((tm, OFF_QR), row),
            pl.BlockSpec((tm, 2 * KA_W), row),
            pl.BlockSpec((tm, 2 * QR_W), row),
            pl.BlockSpec((tm, 2 * QR_W), row),
            pl.BlockSpec((tm, 2 * D_MODEL), row),
        ],
        out_shape=[
            jax.ShapeDtypeStruct((t, OFF_QR), BF16),
            jax.ShapeDtypeStruct((t, 2 * KA_W), F32),
            jax.ShapeDtypeStruct((t, 2 * QR_W), BF16),
            jax.ShapeDtypeStruct((t, 2 * QR_W), BF16),
            jax.ShapeDtypeStruct((t, 2 * D_MODEL), BF16),
        ],
        compiler_params=_cparams(("parallel",)),
    )(x2d, w_bf, cos_tab, sin_tab)


def _attend(q4, k, v, bias, sink, thresh):
    s = lax.dot_general(q4, k, (((1,), (1,)), ((), ())), preferred_element_type=F32)
    s = s * (HEAD_DIM_A ** -0.5) + bias
    if thresh is not None:
        col = lax.broadcasted_iota(I32, s.shape, 1)
        s = jnp.where(col >= thresh, s, NEG_INF)
    m = jnp.maximum(jnp.max(s, axis=-1, keepdims=True), sink)
    e = jnp.exp(s - m)
    p = e / (jnp.sum(e, axis=-1, keepdims=True) + jnp.exp(sink - m))
    return jnp.dot(p.astype(BF16), v, preferred_element_type=F32)


def _attn_prompt_kernel(q_ref, kvc_ref, kvp_ref, bias_ref, sink_ref, o_ref, kv_buf, *, n_chunks):
    i = pl.program_id(1)
    kv_buf[0:WINDOW, :] = kvp_ref[...]
    kv_buf[WINDOW:, :] = kvc_ref[...]

    def chunk(j, carry):
        r0 = pl.multiple_of(j * CHUNK, CHUNK)
        thresh = jnp.maximum(WINDOW - (i * n_chunks + j) * CHUNK, 0)
        for kv in range(N_KV_A):
            q4 = jnp.concatenate(
                [q_ref[pl.ds(r0, CHUNK), (kv * GROUP_A + g) * HEAD_DIM_A:(kv * GROUP_A + g + 1) * HEAD_DIM_A]
                 for g in range(GROUP_A)], axis=0)
            k = kv_buf[pl.ds(r0, WINDOW + CHUNK), kv * HEAD_DIM_A:(kv + 1) * HEAD_DIM_A]
            v = kv_buf[pl.ds(r0, WINDOW + CHUNK), KA_W + kv * HEAD_DIM_A:KA_W + (kv + 1) * HEAD_DIM_A]
            o4 = _attend(q4, k, v, bias_ref[kv], sink_ref[kv], thresh)
            for g in range(GROUP_A):
                c0 = (kv * GROUP_A + g) * HEAD_DIM_A
                o_ref[pl.ds(r0, CHUNK), c0:c0 + HEAD_DIM_A] = o4[g * CHUNK:(g + 1) * CHUNK].astype(BF16)
        return carry

    lax.fori_loop(0, n_chunks, chunk, 0)


def _attn_prompt(qkv, bias, sink, b, s):
    qb = min(512, s)
    n_chunks = qb // CHUNK
    nq = s // qb
    per = qb // WINDOW
    kv_col = OFF_KV // (2 * KA_W)
    return pl.pallas_call(
        functools.partial(_attn_prompt_kernel, n_chunks=n_chunks),
        grid=(b, nq),
        in_specs=[
            pl.BlockSpec((qb, QA_W), lambda bi, i: (bi * nq + i, 0)),
            pl.BlockSpec((qb, 2 * KA_W), lambda bi, i: (bi * nq + i, kv_col)),
            pl.BlockSpec((WINDOW, 2 * KA_W), lambda bi, i: (jnp.maximum((bi * nq + i) * per - 1, 0), kv_col)),
            pl.BlockSpec((N_KV_A, GROUP_A * CHUNK, WINDOW + CHUNK), lambda bi, i: (0, 0, 0)),
            pl.BlockSpec((N_KV_A, GROUP_A * CHUNK, 1), lambda bi, i: (0, 0, 0)),
        ],
        out_specs=pl.BlockSpec((qb, QA_W), lambda bi, i: (bi * nq + i, 0)),
        out_shape=jax.ShapeDtypeStruct((b * s, QA_W), BF16),
        scratch_shapes=[pltpu.VMEM((WINDOW + qb, 2 * KA_W), BF16)],
        compiler_params=_cparams(("parallel", "parallel")),
    )(qkv, qkv, qkv, bias, sink)


def _attn_sample_kernel(q_ref, kvf_ref, ck_ref, cv_ref, bias_ref, sink_ref, o_ref, nk_ref, nv_ref, *, l):
    k_all = jnp.concatenate([ck_ref[0], kvf_ref[:, 0:KA_W]], axis=0)
    v_all = jnp.concatenate([cv_ref[0], kvf_ref[:, KA_W:2 * KA_W]], axis=0)
    nk_ref[0] = k_all[l:]
    nv_ref[0] = v_all[l:]
    kb = k_all.astype(BF16)
    vb = v_all.astype(BF16)
    for kv in range(N_KV_A):
        q4 = jnp.concatenate(
            [q_ref[:, (kv * GROUP_A + g) * HEAD_DIM_A:(kv * GROUP_A + g + 1) * HEAD_DIM_A] for g in range(GROUP_A)],
            axis=0)
        o4 = _attend(q4, kb[:, kv * HEAD_DIM_A:(kv + 1) * HEAD_DIM_A], vb[:, kv * HEAD_DIM_A:(kv + 1) * HEAD_DIM_A],
                     bias_ref[kv], sink_ref[kv], None)
        for g in range(GROUP_A):
            c0 = (kv * GROUP_A + g) * HEAD_DIM_A
            o_ref[:, c0:c0 + HEAD_DIM_A] = o4[g * l:(g + 1) * l].astype(BF16)


def _attn_sample(qkv, kvf, cache_k, cache_v, bias, sink, bd, l):
    cache_spec = pl.BlockSpec((1, WINDOW, KA_W), lambda bi: (bi, 0, 0))
    cache_shape = jax.ShapeDtypeStruct((bd, WINDOW, KA_W), F32)
    return pl.pallas_call(
        functools.partial(_attn_sample_kernel, l=l),
        grid=(bd,),
        in_specs=[
            pl.BlockSpec((l, OFF_QR), lambda bi: (bi, 0)),
            pl.BlockSpec((l, 2 * KA_W), lambda bi: (bi, 0)),
            cache_spec, cache_spec,
            pl.BlockSpec((N_KV_A, GROUP_A * l, WINDOW + l), lambda bi: (0, 0, 0)),
            pl.BlockSpec((N_KV_A, GROUP_A * l, 1), lambda bi: (0, 0, 0)),
        ],
        out_specs=[pl.BlockSpec((l, QA_W), lambda bi: (bi, 0)), cache_spec, cache_spec],
        out_shape=[jax.ShapeDtypeStruct((bd * l, QA_W), BF16), cache_shape, cache_shape],
        compiler_params=_cparams(("parallel",)),
    )(qkv, kvf, cache_k, cache_v, bias, sink)


def _retention_kernel(q_ref, k_ref, v_ref, g_ref, s0_ref, di_ref, qd_ref, kd_ref, gc_ref, gain_ref,
                      o_ref, s_ref, *, n_chunks, c):
    di = di_ref[0]
    qd = qd_ref[0]
    kd = kd_ref[0]
    gc = gc_ref[0]
    gain = gain_ref[...]

    def chunk(n, state):
        r0 = pl.multiple_of(n * c, c)
        qc = q_ref[pl.ds(r0, c), :]
        kc = k_ref[pl.ds(r0, c), :]
        vc = v_ref[pl.ds(r0, c), :]
        sc = lax.dot_general(qc, kc, (((1,), (1,)), ((), ())), preferred_element_type=F32) * di
        o = jnp.dot(sc.astype(BF16), vc, preferred_element_type=F32)
        q_dec = (qc.astype(F32) * qd).astype(BF16)
        o = o + jnp.dot(q_dec, state.astype(BF16), preferred_element_type=F32)
        k_dec = (kc.astype(F32) * kd).astype(BF16)
        state = gc * state + lax.dot_general(k_dec, vc, (((0,), (0,)), ((), ())), preferred_element_type=F32)
        mu = jnp.mean(o, axis=-1, keepdims=True)
        d = o - mu
        var = jnp.mean(d * d, axis=-1, keepdims=True)
        y = d * lax.rsqrt(var + LN_EPS) * gain
        o_ref[pl.ds(r0, c), :] = (y * _silu(g_ref[pl.ds(r0, c), :].astype(F32))).astype(BF16)
        return state

    s_ref[0, 0] = lax.fori_loop(0, n_chunks, chunk, s0_ref[0, 0])


def _retention(qkr, vg, s0, tabs, gain, b, s, c):
    di, qd, kd, gc = tabs
    seq = lambda off: pl.BlockSpec((s, DK_R), lambda bi, h: (bi, off + h))
    head3 = lambda shape: pl.BlockSpec((1,) + shape, lambda bi, h: (h, 0, 0))
    st_spec = pl.BlockSpec((1, 1, DK_R, DK_R), lambda bi, h: (bi, h, 0, 0))
    return pl.pallas_call(
        functools.partial(_retention_kernel, n_chunks=s // c, c=c),
        grid=(b, N_HEADS_R),
        in_specs=[seq(0), seq(N_HEADS_R), seq(0), seq(N_HEADS_R), st_spec,
                  head3((c, c)), head3((c, DK_R)), head3((c, DK_R)), head3((DK_R, DK_R)),
                  pl.BlockSpec((1, DK_R), lambda bi, h: (0, h))],
        out_specs=[pl.BlockSpec((s, DK_R), lambda bi, h: (bi, h)), st_spec],
        out_shape=[jax.ShapeDtypeStruct((b * s, QR_W), BF16), jax.ShapeDtypeStruct((b, N_HEADS_R, DK_R, DK_R), F32)],
        compiler_params=_cparams(("parallel", "parallel")),
    )(qkr, qkr, vg, vg, s0, di, qd, kd, gc, gain)


def _mix_kernel(x_ref, oa_ref, or_ref, g_ref, wa_ref, wb_ref, wo_ref, lg_ref, lb_ref, wrt_ref, x1_ref, st_ref):
    a = jnp.dot(oa_ref[...], wa_ref[...], preferred_element_type=F32)
    b = jnp.dot(or_ref[...], wb_ref[...], preferred_element_type=F32)
    merged = g_ref[:, 0:D_MODEL].astype(F32) * a + g_ref[:, D_MODEL:].astype(F32) * b
    y = jnp.dot(merged.astype(BF16), wo_ref[...], preferred_element_type=F32)
    x1 = _layer_norm(DN_ALPHA * x_ref[...] + y, lg_ref[...], lb_ref[...])
    x1_ref[...] = x1
    logits = lax.dot_general(wrt_ref[...], x1.astype(BF16), (((1,), (1,)), ((), ())), preferred_element_type=F32)
    st_ref[...] = jax.nn.sigmoid(logits)


def _mix(x2d, oa, orr, gm, wa, wb, wo, lg, lb, wrt, tm):
    t = x2d.shape[0]
    row = lambda w: pl.BlockSpec((tm, w), lambda i: (i, 0))
    full = lambda a: pl.BlockSpec(a.shape, lambda i: (0,) * a.ndim)
    return pl.pallas_call(
        _mix_kernel,
        grid=(t // tm,),
        in_specs=[row(D_MODEL), row(QA_W), row(QR_W), row(2 * D_MODEL),
                  full(wa), full(wb), full(wo), full(lg), full(lb), full(wrt)],
        out_specs=[row(D_MODEL), pl.BlockSpec((N_EXPERTS, tm), lambda i: (0, i))],
        out_shape=[jax.ShapeDtypeStruct((t, D_MODEL), F32), jax.ShapeDtypeStruct((N_EXPERTS, t), F32)],
        compiler_params=_cparams(("parallel",)),
    )(x2d, oa, orr, gm, wa, wb, wo, lg, lb, wrt)


def _route_kernel(s_ref, bias_ref, e_ref, w_ref, r_ref, cnt_ref, carry_ref):
    @pl.when(pl.program_id(0) == 0)
    def _():
        carry_ref[...] = jnp.zeros_like(carry_ref)

    s = s_ref[...]
    tl = s.shape[1]
    choice = s + bias_ref[...]
    row = lax.broadcasted_iota(I32, (N_EXPERTS, tl), 0)
    row_g = lax.broadcasted_iota(I32, (GROUP_SIZE, tl), 0)
    neg = -jnp.inf

    scores = []
    for g in range(N_GROUPS):
        blk = choice[g * GROUP_SIZE:(g + 1) * GROUP_SIZE]
        m1 = jnp.max(blk, axis=0, keepdims=True)
        i1 = jnp.min(jnp.where(blk == m1, row_g, GROUP_SIZE), axis=0, keepdims=True)
        m2 = jnp.max(jnp.where(row_g == i1, neg, blk), axis=0, keepdims=True)
        scores.append(m1 + m2)
    sc = jnp.concatenate(scores, axis=0)
    gi = lax.broadcasted_iota(I32, sc.shape, 0)
    grank = jnp.zeros(sc.shape, I32)
    for g in range(N_GROUPS):
        other = sc[g:g + 1]
        ahead = jnp.where(other > sc, 1, jnp.where(other == sc, jnp.where(gi > g, 1, 0), 0))
        grank = grank + ahead
    cm = jnp.concatenate(
        [jnp.where(grank[g:g + 1] < TOPK_GROUPS, choice[g * GROUP_SIZE:(g + 1) * GROUP_SIZE], neg)
         for g in range(N_GROUPS)], axis=0)

    experts, weights = [], []
    for _ in range(TOP_K):
        m = jnp.max(cm, axis=0, keepdims=True)
        idx = jnp.min(jnp.where(cm == m, row, N_EXPERTS), axis=0, keepdims=True)
        hit = row == idx
        weights.append(jnp.sum(jnp.where(hit, s, 0.0), axis=0, keepdims=True))
        cm = jnp.where(hit, neg, cm)
        experts.append(idx)
    e8 = jnp.concatenate(experts, axis=0)
    w8 = jnp.concatenate(weights, axis=0)
    e_ref[...] = e8
    w_ref[...] = w8 / jnp.sum(w8, axis=0, keepdims=True) * ROUTED_SCALE

    before = (lax.broadcasted_iota(I32, (tl, tl), 0) < lax.broadcasted_iota(I32, (tl, tl), 1))
    before = jnp.where(before, 1.0, 0.0).astype(BF16)
    carry = carry_ref[...]
    ranks = []
    for k in range(TOP_K):
        hit = row == experts[k]
        onehot = jnp.where(hit, 1.0, 0.0)
        prefix = jnp.dot(onehot.astype(BF16), before, preferred_element_type=F32)
        ranks.append(jnp.sum(jnp.where(hit, prefix + carry, 0.0), axis=0, keepdims=True))
        carry = carry + jnp.sum(onehot, axis=1, keepdims=True)
    r_ref[...] = jnp.concatenate(ranks, axis=0).astype(I32)
    carry_ref[...] = carry
    cnt_ref[...] = carry.astype(I32)


def _route(st, bias_col, tl):
    t = st.shape[1]
    tok = pl.BlockSpec((TOP_K, tl), lambda i: (0, i))
    return pl.pallas_call(
        _route_kernel,
        grid=(t // tl,),
        in_specs=[pl.BlockSpec((N_EXPERTS, tl), lambda i: (0, i)), pl.BlockSpec((N_EXPERTS, 1), lambda i: (0, 0))],
        out_specs=[tok, tok, tok, pl.BlockSpec((N_EXPERTS, 1), lambda i: (0, 0))],
        out_shape=[jax.ShapeDtypeStruct((TOP_K, t), I32), jax.ShapeDtypeStruct((TOP_K, t), F32),
                   jax.ShapeDtypeStruct((TOP_K, t), I32), jax.ShapeDtypeStruct((N_EXPERTS, 1), I32)],
        scratch_shapes=[pltpu.VMEM((N_EXPERTS, 1), F32)],
        compiler_params=_cparams(("arbitrary",)),
    )(st, bias_col)


def _row_copy(src, src_row, dst, dst_row, sem):
    return pltpu.make_async_copy(src.at[pl.ds(src_row, 1)], dst.at[pl.ds(dst_row, 1)], sem)


def _dispatch_kernel(ps_ref, e_ref, r_ref, x_hbm, xs_in, xs_hbm, sem, *, td):
    del xs_in
    base = pl.program_id(0) * td

    def start(t, carry):
        for k in range(TOP_K):
            d = ps_ref[e_ref[k, t]] + r_ref[k, t]
            _row_copy(x_hbm, base + t, xs_hbm, d, sem).start()
        return carry

    lax.fori_loop(0, td, start, 0)

    def wait(t, carry):
        for k in range(TOP_K):
            _row_copy(x_hbm, 0, xs_hbm, 0, sem).wait()
        return carry

    lax.fori_loop(0, td, wait, 0)


def _dispatch(pstart, e8, r8, x1, xs0, td):
    t = x1.shape[0]
    smem_tok = pl.BlockSpec((TOP_K, td), lambda i, ps: (0, i), memory_space=pltpu.SMEM)
    return pl.pallas_call(
        functools.partial(_dispatch_kernel, td=td),
        grid_spec=pltpu.PrefetchScalarGridSpec(
            num_scalar_prefetch=1,
            grid=(t // td,),
            in_specs=[smem_tok, smem_tok, pl.BlockSpec(memory_space=pl.ANY), pl.BlockSpec(memory_space=pl.ANY)],
            out_specs=pl.BlockSpec(memory_space=pl.ANY),
            scratch_shapes=[pltpu.SemaphoreType.DMA(())],
        ),
        out_shape=jax.ShapeDtypeStruct(xs0.shape, xs0.dtype),
        input_output_aliases={4: 0},
        compiler_params=_cparams(("arbitrary",)),
    )(pstart, e8, r8, x1, xs0)


def _moe_kernel(be_ref, nv_ref, x_ref, wg_ref, wu_ref, wd_ref, y_ref):
    j = pl.program_id(0)

    @pl.when(nv_ref[j] > 0)
    def _():
        xb = x_ref[...].astype(BF16)
        g = jnp.dot(xb, wg_ref[0].astype(BF16), preferred_element_type=F32)
        u = jnp.dot(xb, wu_ref[0].astype(BF16), preferred_element_type=F32)
        h = (_silu(g) * u).astype(BF16)
        y_ref[...] = jnp.dot(h, wd_ref[0].astype(BF16), preferred_element_type=F32)

    @pl.when(nv_ref[j] == 0)
    def _():
        y_ref[...] = jnp.zeros_like(y_ref)


def _moe(blk_exp, blk_valid, xs, wg, wu, wd):
    n_rows = xs.shape[0]
    rows = pl.BlockSpec((MOE_ROWS, D_MODEL), lambda j, be, nv: (j, 0))
    return pl.pallas_call(
        _moe_kernel,
        grid_spec=pltpu.PrefetchScalarGridSpec(
            num_scalar_prefetch=2,
            grid=(n_rows // MOE_ROWS,),
            in_specs=[rows,
                      pl.BlockSpec((1, D_MODEL, D_EXPERT), lambda j, be, nv: (be[j], 0, 0)),
                      pl.BlockSpec((1, D_MODEL, D_EXPERT), lambda j, be, nv: (be[j], 0, 0)),
                      pl.BlockSpec((1, D_EXPERT, D_MODEL), lambda j, be, nv: (be[j], 0, 0))],
            out_specs=rows,
        ),
        out_shape=jax.ShapeDtypeStruct((n_rows, D_MODEL), F32),
        compiler_params=_cparams(("arbitrary",)),
    )(blk_exp, blk_valid, xs, wg, wu, wd)


def _final_kernel(ps_ref, e_ref, r_ref, x1_ref, w_ref, p_ref, wsg_ref, wsu_ref, wsd_ref, lg_ref, lb_ref,
                  wpp_ref, wpg_ref, ys_hbm, out_ref, gbuf, sem, *, tf):
    def start(t, carry):
        for k in range(TOP_K):
            d = ps_ref[e_ref[k, t]] + r_ref[k, t]
            _row_copy(ys_hbm, d, gbuf.at[k], t, sem).start()
        return carry

    lax.fori_loop(0, tf, start, 0)

    x1 = x1_ref[...]
    xb = x1.astype(BF16)
    hs = _silu(jnp.dot(xb, wsg_ref[...], preferred_element_type=F32)) * jnp.dot(xb, wsu_ref[...],
                                                                               preferred_element_type=F32)
    y = jnp.dot(hs.astype(BF16), wsd_ref[...], preferred_element_type=F32)

    def wait(t, carry):
        for k in range(TOP_K):
            _row_copy(ys_hbm, 0, gbuf.at[k], 0, sem).wait()
        return carry

    lax.fori_loop(0, tf, wait, 0)

    for k in range(TOP_K):
        y = y + w_ref[:, k:k + 1] * gbuf[k]
    x2 = _layer_norm(DN_ALPHA * x1 + y, lg_ref[...], lb_ref[...])
    gate = jax.nn.sigmoid(jnp.dot(x2.astype(BF16), wpg_ref[...], preferred_element_type=F32))
    out_ref[...] = x2 + gate * jnp.dot(p_ref[...].astype(BF16), wpp_ref[...], preferred_element_type=F32)


def _final(pstart, e8, r8, x1, w_tok, p2d, wsg, wsu, wsd, lg, lb, wpp, wpg, ys, tf):
    t = x1.shape[0]
    smem_tok = pl.BlockSpec((TOP_K, tf), lambda i, ps: (0, i), memory_space=pltpu.SMEM)
    row = lambda w: pl.BlockSpec((tf, w), lambda i, ps: (i, 0))
    full = lambda a: pl.BlockSpec(a.shape, lambda i, ps: (0,) * a.ndim)
    return pl.pallas_call(
        functools.partial(_final_kernel, tf=tf),
        grid_spec=pltpu.PrefetchScalarGridSpec(
            num_scalar_prefetch=1,
            grid=(t // tf,),
            in_specs=[smem_tok, smem_tok, row(D_MODEL), row(TOP_K), row(D_PLE),
                      full(wsg), full(wsu), full(wsd), full(lg), full(lb), full(wpp), full(wpg),
                      pl.BlockSpec(memory_space=pl.ANY)],
            out_specs=row(D_MODEL),
            scratch_shapes=[pltpu.VMEM((TOP_K, tf, D_MODEL), F32), pltpu.SemaphoreType.DMA(())],
        ),
        out_shape=jax.ShapeDtypeStruct((t, D_MODEL), F32),
        compiler_params=_cparams(("arbitrary",)),
    )(pstart, e8, r8, x1, w_tok, p2d, wsg, wsu, wsd, lg, lb, wpp, wpg, ys)


def _t5_bucket(rel):
    nb = N_BUCKETS // 2
    max_exact = nb // 2
    ret = jnp.where(rel > 0, nb, 0)
    n = jnp.abs(rel)
    nf = jnp.maximum(n, 1).astype(F32)
    large = max_exact + (jnp.log(nf / max_exact) / math.log(MAX_DIST / max_exact) * (nb - max_exact)).astype(I32)
    large = jnp.minimum(large, nb - 1)
    return ret + jnp.where(n < max_exact, n, large)


def _rel_bias(table, q_len, k_len):
    rel = (jnp.arange(k_len, dtype=I32)[None, :] - WINDOW) - jnp.arange(q_len, dtype=I32)[:, None]
    b = table[_t5_bucket(rel)]
    return jnp.transpose(b, (2, 0, 1)).reshape(N_KV_A, GROUP_A * q_len, k_len).astype(F32)


def _rotary_tables(pos):
    half = DK_R // 2
    inv = ROPE_BASE ** (-jnp.arange(half, dtype=F32) / half)
    ang = pos.astype(F32)[:, None] * inv[None, :]
    cos, sin = jnp.cos(ang), jnp.sin(ang)
    return jnp.concatenate([cos, cos], -1), jnp.concatenate([-sin, sin], -1)


def _decay_tables(c):
    log_gamma = jnp.log1p(-jnp.exp2(-5.0 - jnp.arange(N_HEADS_R, dtype=F32)))
    idx = jnp.arange(c, dtype=F32)
    di = jnp.exp(jnp.abs(idx[:, None] - idx[None, :])[None] * log_gamma[:, None, None])
    qd = jnp.exp((idx[None, :] + 1.0) * log_gamma[:, None])
    kd = jnp.exp((c - 1.0 - idx[None, :]) * log_gamma[:, None])
    gc = jnp.exp(c * log_gamma)
    bc = lambda a: jnp.broadcast_to(a[:, :, None], (N_HEADS_R, c, DK_R))
    return di, bc(qd), bc(kd), jnp.broadcast_to(gc[:, None, None], (N_HEADS_R, DK_R, DK_R))


def _moe_ffn(x1, st, p2d, wts):
    t = x1.shape[0]
    tile = min(256, t)
    e8, w8, r8, counts = _route(st, wts['router_bias'], tile)
    counts = counts[:, 0]
    padded = (counts + MOE_ROWS - 1) // MOE_ROWS * MOE_ROWS
    pad_end = jnp.cumsum(padded)
    pstart = (pad_end - padded).astype(I32)
    n_rows = t * TOP_K + N_EXPERTS * MOE_ROWS
    blk_start = jnp.arange(n_rows // MOE_ROWS, dtype=I32) * MOE_ROWS
    blk_exp = jnp.minimum(jnp.searchsorted(pad_end, blk_start, side='right'), N_EXPERTS - 1).astype(I32)
    blk_valid = jnp.clip(pstart[blk_exp] + counts[blk_exp] - blk_start, 0, MOE_ROWS).astype(I32)
    xs = _dispatch(pstart, e8, r8, x1, jnp.zeros((n_rows, D_MODEL), F32), tile)
    ys = _moe(blk_exp, blk_valid, xs, wts['w_exp_gate'], wts['w_exp_up'], wts['w_exp_down'])
    return _final(pstart, e8, r8, x1, w8.T, p2d, wts['w_sh_gate'], wts['w_sh_up'], wts['w_sh_down'],
                  wts['ln2_g'], wts['ln2_b'], wts['w_ple_proj'], wts['w_ple_gate'], ys, tile)


def _post_mixers(x2d, oa, orr, gm, p2d, wts):
    t = x2d.shape[0]
    x1, st = _mix(x2d, oa, orr, gm, wts['w_branch_attn'], wts['w_branch_ret'], wts['w_out'],
                  wts['ln1_g'], wts['ln1_b'], wts['w_router_t'], min(512, t))
    return _moe_ffn(x1, st, p2d, wts)


def kernel(x_prompt, x_sample, cache_attn_k, cache_attn_v, state_retention, p_prompt, p_sample,
           w_in, attn_sinks, rel_bias_table, ret_gn_gain, w_branch_attn, w_branch_ret, w_out,
           ln1_g, ln1_b, w_router, router_bias, w_exp_gate, w_exp_up, w_exp_down,
           w_sh_gate, w_sh_up, w_sh_down, ln2_g, ln2_b, w_ple_proj, w_ple_gate):
    b, s, _ = x_prompt.shape
    bd, l, _ = x_sample.shape
    i = 0
    row = lambda a: a.reshape(1, -1).astype(F32)
    wts = dict(
        w_branch_attn=w_branch_attn[i].astype(BF16), w_branch_ret=w_branch_ret[i].astype(BF16),
        w_out=w_out[i].astype(BF16), ln1_g=row(ln1_g[i]), ln1_b=row(ln1_b[i]),
        w_router_t=w_router[i].T.astype(BF16), router_bias=router_bias[i].reshape(N_EXPERTS, 1).astype(F32),
        w_exp_gate=w_exp_gate[i], w_exp_up=w_exp_up[i], w_exp_down=w_exp_down[i],
        w_sh_gate=w_sh_gate[i].astype(BF16), w_sh_up=w_sh_up[i].astype(BF16), w_sh_down=w_sh_down[i].astype(BF16),
        ln2_g=row(ln2_g[i]), ln2_b=row(ln2_b[i]),
        w_ple_proj=w_ple_proj[i].astype(BF16), w_ple_gate=w_ple_gate[i].astype(BF16))
    w_in_bf = w_in[i].astype(BF16)
    gain = row(ret_gn_gain[i])
    sinks = attn_sinks[i].astype(F32).reshape(N_KV_A, GROUP_A)

    xp = x_prompt.reshape(b * s, D_MODEL)
    cos_p, sin_p = _rotary_tables(jnp.arange(s, dtype=I32))
    qkv, kvf, qkr, vg, gm = _inproj(xp, w_in_bf, cos_p, sin_p, min(512, s))
    sink_p = jnp.repeat(sinks, CHUNK, axis=1)[..., None]
    oa = _attn_prompt(qkv, _rel_bias(rel_bias_table, CHUNK, WINDOW + CHUNK), sink_p, b, s)
    s0 = jnp.zeros((b, N_HEADS_R, DK_R, DK_R), F32)
    orr, ns_p = _retention(qkr, vg, s0, _decay_tables(CHUNK), gain, b, s, CHUNK)
    y_p = _post_mixers(xp, oa, orr, gm, p_prompt[i].reshape(b * s, D_PLE), wts).reshape(b, s, D_MODEL)
    kv_tail = kvf.reshape(b, s, 2, N_KV_A, HEAD_DIM_A)[:, s - WINDOW:]
    nk_p, nv_p = kv_tail[:, :, 0], kv_tail[:, :, 1]

    xs = x_sample.reshape(bd * l, D_MODEL)
    cos_s, sin_s = _rotary_tables(jnp.tile(PAST_LEN + jnp.arange(l, dtype=I32), bd))
    qkv, kvf, qkr, vg, gm = _inproj(xs, w_in_bf, cos_s, sin_s, bd * l)
    sink_s = jnp.repeat(sinks, l, axis=1)[..., None]
    oa, nk_s, nv_s = _attn_sample(qkv, kvf, cache_attn_k[i].reshape(bd, WINDOW, KA_W),
                                  cache_attn_v[i].reshape(bd, WINDOW, KA_W),
                                  _rel_bias(rel_bias_table, l, WINDOW + l), sink_s, bd, l)
    orr, ns_s = _retention(qkr, vg, state_retention[i].astype(F32), _decay_tables(l), gain, bd, l, l)
    y_s = _post_mixers(xs, oa, orr, gm, p_sample[i].reshape(bd * l, D_PLE), wts).reshape(bd, l, D_MODEL)
    shape_kv = (bd, WINDOW, N_KV_A, HEAD_DIM_A)
    return (y_p, y_s, nk_p[None], nv_p[None], ns_p[None],
            nk_s.reshape(shape_kv)[None], nv_s.reshape(shape_kv)[None], ns_s[None])
```

```python
import functools
import math

import jax
import jax.numpy as jnp
from jax import lax
from jax.experimental import pallas as pl
from jax.experimental.pallas import tpu as pltpu

F32 = jnp.float32
BF16 = jnp.bfloat16
I32 = jnp.int32

D_MODEL = 1024
CHUNK = 64
WINDOW = 128
N_HEADS_A = 8
N_KV_A = 2
GROUP_A = 4
HEAD_DIM_A = 64
N_BUCKETS = 32
MAX_DIST = 128
N_HEADS_R = 4
DK_R = 128
ROPE_BASE = 10000.0
N_EXPERTS = 256
TOP_K = 8
N_GROUPS = 8
GROUP_SIZE = N_EXPERTS // N_GROUPS
TOPK_GROUPS = 4
D_EXPERT = 256
ROUTED_SCALE = 2.5
D_PLE = 256
LN_EPS = 1e-5
NEG_INF = -1e30
PAST_LEN = 2048
DEPTH = 1
DN_ALPHA = (2 * DEPTH) ** 0.25

QA_W = N_HEADS_A * HEAD_DIM_A
KA_W = N_KV_A * HEAD_DIM_A
QR_W = N_HEADS_R * DK_R
OFF_KV = QA_W
OFF_QR = QA_W + 2 * KA_W
OFF_VR = OFF_QR + 2 * QR_W
OFF_GM = OFF_VR + 2 * QR_W
IN_W = OFF_GM + 2 * D_MODEL

MOE_ROWS = 256
VMEM_LIMIT = 56 * 1024 * 1024


def _cparams(sem, vmem=VMEM_LIMIT):
    return pltpu.CompilerParams(dimension_semantics=sem, vmem_limit_bytes=vmem)


def _layer_norm(h, g, b):
    mu = jnp.mean(h, axis=-1, keepdims=True)
    d = h - mu
    var = jnp.mean(d * d, axis=-1, keepdims=True)
    return d * lax.rsqrt(var + LN_EPS) * g + b


def _silu(x):
    return x * jax.nn.sigmoid(x)


def _inproj_kernel(x_ref, w_ref, cos_ref, sin_ref, qkv_ref, kvf_ref, qkr_ref, vg_ref, gm_ref):
    xb = x_ref[...].astype(BF16)

    def mm(lo, hi):
        return jnp.dot(xb, w_ref[:, lo:hi], preferred_element_type=F32)

    qkv_ref[:, 0:QA_W] = mm(0, QA_W).astype(BF16)
    kv = mm(OFF_KV, OFF_QR)
    kvf_ref[...] = kv
    qkv_ref[:, OFF_KV:OFF_QR] = kv.astype(BF16)
    cos = cos_ref[...]
    sin = sin_ref[...]
    for part in range(2):
        z = mm(OFF_QR + part * QR_W, OFF_QR + (part + 1) * QR_W)
        for h in range(N_HEADS_R):
            zh = z[:, h * DK_R:(h + 1) * DK_R]
            r = zh * cos + pltpu.roll(zh, DK_R // 2, axis=1) * sin
            if part == 1:
                r = r * (DK_R ** -0.5)
            c0 = part * QR_W + h * DK_R
            qkr_ref[:, c0:c0 + DK_R] = r.astype(BF16)
    vg_ref[...] = mm(OFF_VR, OFF_GM).astype(BF16)
    for half in range(2):
        lo = OFF_GM + half * D_MODEL
        gm_ref[:, half * D_MODEL:(half + 1) * D_MODEL] = jax.nn.sigmoid(mm(lo, lo + D_MODEL)).astype(BF16)


def _inproj(x2d, w_bf, cos_tab, sin_tab, tm):
    t = x2d.shape[0]
    nper = cos_tab.shape[0] // tm
    row = lambda i: (i, 0)
    return pl.pallas_call(
        _inproj_kernel,
        grid=(t // tm,),
        in_specs=[
            pl.BlockSpec((tm, D_MODEL), row),
            pl.BlockSpec((D_MODEL, IN_W), lambda i: (0, 0)),
            pl.BlockSpec((tm, DK_R), lambda i: (i % nper, 0)),
            pl.BlockSpec((tm, DK_R), lambda i: (i % nper, 0)),
        ],
        out_specs=[
            pl.BlockSpec((tm, OFF_QR), row),
            pl.BlockSpec((tm, 2 * KA_W), row),
            pl.BlockSpec((tm, 2 * QR_W), row),
            pl.BlockSpec((tm, 2 * QR_W), row),
            pl.BlockSpec((tm, 2 * D_MODEL), row),
        ],
        out_shape=[
            jax.ShapeDtypeStruct((t, OFF_QR), BF16),
            jax.ShapeDtypeStruct((t, 2 * KA_W), F32),
            jax.ShapeDtypeStruct((t, 2 * QR_W), BF16),
            jax.ShapeDtypeStruct((t, 2 * QR_W), BF16),
            jax.ShapeDtypeStruct((t, 2 * D_MODEL), BF16),
        ],
        compiler_params=_cparams(("parallel",)),
        name="inproj",
    )(x2d, w_bf, cos_tab, sin_tab)


def _attend(q4, k, v, bias, sink, thresh):
    s = lax.dot_general(q4, k, (((1,), (1,)), ((), ())), preferred_element_type=F32)
    s = s * (HEAD_DIM_A ** -0.5) + bias
    if thresh is not None:
        col = lax.broadcasted_iota(I32, s.shape, 1)
        s = jnp.where(col >= thresh, s, NEG_INF)
    m = jnp.maximum(jnp.max(s, axis=-1, keepdims=True), sink)
    e = jnp.exp(s - m)
    p = e / (jnp.sum(e, axis=-1, keepdims=True) + jnp.exp(sink - m))
    return jnp.dot(p.astype(BF16), v, preferred_element_type=F32)


def _attn_prompt_kernel(q_ref, kvc_ref, kvp_ref, bias_ref, sink_ref, o_ref, kv_buf, *, n_chunks):
    i = pl.program_id(1)
    kv_buf[0:WINDOW, :] = kvp_ref[...]
    kv_buf[WINDOW:, :] = kvc_ref[...]

    def chunk(j, carry):
        r0 = pl.multiple_of(j * CHUNK, CHUNK)
        thresh = jnp.maximum(WINDOW - (i * n_chunks + j) * CHUNK, 0)
        for kv in range(N_KV_A):
            q4 = jnp.concatenate(
                [q_ref[pl.ds(r0, CHUNK), (kv * GROUP_A + g) * HEAD_DIM_A:(kv * GROUP_A + g + 1) * HEAD_DIM_A]
                 for g in range(GROUP_A)], axis=0)
            k = kv_buf[pl.ds(r0, WINDOW + CHUNK), kv * HEAD_DIM_A:(kv + 1) * HEAD_DIM_A]
            v = kv_buf[pl.ds(r0, WINDOW + CHUNK), KA_W + kv * HEAD_DIM_A:KA_W + (kv + 1) * HEAD_DIM_A]
            o4 = _attend(q4, k, v, bias_ref[kv], sink_ref[kv], thresh)
            for g in range(GROUP_A):
                c0 = (kv * GROUP_A + g) * HEAD_DIM_A
                o_ref[pl.ds(r0, CHUNK), c0:c0 + HEAD_DIM_A] = o4[g * CHUNK:(g + 1) * CHUNK].astype(BF16)
        return carry

    lax.fori_loop(0, n_chunks, chunk, 0)


def _attn_prompt(qkv, bias, sink, b, s):
    qb = min(512, s)
    n_chunks = qb // CHUNK
    nq = s // qb
    per = qb // WINDOW
    kv_col = OFF_KV // (2 * KA_W)
    return pl.pallas_call(
        functools.partial(_attn_prompt_kernel, n_chunks=n_chunks),
        grid=(b, nq),
        in_specs=[
            pl.BlockSpec((qb, QA_W), lambda bi, i: (bi * nq + i, 0)),
            pl.BlockSpec((qb, 2 * KA_W), lambda bi, i: (bi * nq + i, kv_col)),
            pl.BlockSpec((WINDOW, 2 * KA_W), lambda bi, i: (jnp.maximum((bi * nq + i) * per - 1, 0), kv_col)),
            pl.BlockSpec((N_KV_A, GROUP_A * CHUNK, WINDOW + CHUNK), lambda bi, i: (0, 0, 0)),
            pl.BlockSpec((N_KV_A, GROUP_A * CHUNK, 1), lambda bi, i: (0, 0, 0)),
        ],
        out_specs=pl.BlockSpec((qb, QA_W), lambda bi, i: (bi * nq + i, 0)),
        out_shape=jax.ShapeDtypeStruct((b * s, QA_W), BF16),
        scratch_shapes=[pltpu.VMEM((WINDOW + qb, 2 * KA_W), BF16)],
        compiler_params=_cparams(("parallel", "parallel")),
        name="attn_prompt",
    )(qkv, qkv, qkv, bias, sink)


def _attn_sample_kernel(q_ref, kvf_ref, ck_ref, cv_ref, bias_ref, sink_ref, o_ref, nk_ref, nv_ref, *, l):
    k_all = jnp.concatenate([ck_ref[0], kvf_ref[:, 0:KA_W]], axis=0)
    v_all = jnp.concatenate([cv_ref[0], kvf_ref[:, KA_W:2 * KA_W]], axis=0)
    nk_ref[0] = k_all[l:]
    nv_ref[0] = v_all[l:]
    kb = k_all.astype(BF16)
    vb = v_all.astype(BF16)
    for kv in range(N_KV_A):
        q4 = jnp.concatenate(
            [q_ref[:, (kv * GROUP_A + g) * HEAD_DIM_A:(kv * GROUP_A + g + 1) * HEAD_DIM_A] for g in range(GROUP_A)],
            axis=0)
        o4 = _attend(q4, kb[:, kv * HEAD_DIM_A:(kv + 1) * HEAD_DIM_A], vb[:, kv * HEAD_DIM_A:(kv + 1) * HEAD_DIM_A],
                     bias_ref[kv], sink_ref[kv], None)
        for g in range(GROUP_A):
            c0 = (kv * GROUP_A + g) * HEAD_DIM_A
            o_ref[:, c0:c0 + HEAD_DIM_A] = o4[g * l:(g + 1) * l].astype(BF16)


def _attn_sample(qkv, kvf, cache_k, cache_v, bias, sink, bd, l):
    cache_spec = pl.BlockSpec((1, WINDOW, KA_W), lambda bi: (bi, 0, 0))
    cache_shape = jax.ShapeDtypeStruct((bd, WINDOW, KA_W), F32)
    return pl.pallas_call(
        functools.partial(_attn_sample_kernel, l=l),
        grid=(bd,),
        in_specs=[
            pl.BlockSpec((l, OFF_QR), lambda bi: (bi, 0)),
            pl.BlockSpec((l, 2 * KA_W), lambda bi: (bi, 0)),
            cache_spec, cache_spec,
            pl.BlockSpec((N_KV_A, GROUP_A * l, WINDOW + l), lambda bi: (0, 0, 0)),
            pl.BlockSpec((N_KV_A, GROUP_A * l, 1), lambda bi: (0, 0, 0)),
        ],
        out_specs=[pl.BlockSpec((l, QA_W), lambda bi: (bi, 0)), cache_spec, cache_spec],
        out_shape=[jax.ShapeDtypeStruct((bd * l, QA_W), BF16), cache_shape, cache_shape],
        compiler_params=_cparams(("parallel",)),
        name="attn_sample",
    )(qkv, kvf, cache_k, cache_v, bias, sink)


def _retention_kernel(q_ref, k_ref, v_ref, g_ref, s0_ref, di_ref, qd_ref, kd_ref, gc_ref, gain_ref,
                      o_ref, s_ref, *, n_chunks, c):
    di = di_ref[0]
    qd = qd_ref[0]
    kd = kd_ref[0]
    gc = gc_ref[0]
    gain = gain_ref[...]

    def chunk(n, state):
        r0 = pl.multiple_of(n * c, c)
        qc = q_ref[pl.ds(r0, c), :]
        kc = k_ref[pl.ds(r0, c), :]
        vc = v_ref[pl.ds(r0, c), :]
        sc = lax.dot_general(qc, kc, (((1,), (1,)), ((), ())), preferred_element_type=F32) * di
        o = jnp.dot(sc.astype(BF16), vc, preferred_element_type=F32)
        q_dec = (qc.astype(F32) * qd).astype(BF16)
        o = o + jnp.dot(q_dec, state.astype(BF16), preferred_element_type=F32)
        k_dec = (kc.astype(F32) * kd).astype(BF16)
        state = gc * state + lax.dot_general(k_dec, vc, (((0,), (0,)), ((), ())), preferred_element_type=F32)
        mu = jnp.mean(o, axis=-1, keepdims=True)
        d = o - mu
        var = jnp.mean(d * d, axis=-1, keepdims=True)
        y = d * lax.rsqrt(var + LN_EPS) * gain
        o_ref[pl.ds(r0, c), :] = (y * _silu(g_ref[pl.ds(r0, c), :].astype(F32))).astype(BF16)
        return state

    s_ref[0, 0] = lax.fori_loop(0, n_chunks, chunk, s0_ref[0, 0])


def _retention(qkr, vg, s0, tabs, gain, b, s, c):
    di, qd, kd, gc = tabs
    seq = lambda off: pl.BlockSpec((s, DK_R), lambda bi, h: (bi, off + h))
    head3 = lambda shape: pl.BlockSpec((1,) + shape, lambda bi, h: (h, 0, 0))
    st_spec = pl.BlockSpec((1, 1, DK_R, DK_R), lambda bi, h: (bi, h, 0, 0))
    return pl.pallas_call(
        functools.partial(_retention_kernel, n_chunks=s // c, c=c),
        grid=(b, N_HEADS_R),
        in_specs=[seq(0), seq(N_HEADS_R), seq(0), seq(N_HEADS_R), st_spec,
                  head3((c, c)), head3((c, DK_R)), head3((c, DK_R)), head3((DK_R, DK_R)),
                  pl.BlockSpec((1, DK_R), lambda bi, h: (0, h))],
        out_specs=[pl.BlockSpec((s, DK_R), lambda bi, h: (bi, h)), st_spec],
        out_shape=[jax.ShapeDtypeStruct((b * s, QR_W), BF16), jax.ShapeDtypeStruct((b, N_HEADS_R, DK_R, DK_R), F32)],
        compiler_params=_cparams(("parallel", "parallel")),
        name="retention",
    )(qkr, qkr, vg, vg, s0, di, qd, kd, gc, gain)


def _mix_kernel(x_ref, oa_ref, or_ref, g_ref, wa_ref, wb_ref, wo_ref, lg_ref, lb_ref, wrt_ref, x1_ref, st_ref):
    a = jnp.dot(oa_ref[...], wa_ref[...], preferred_element_type=F32)
    b = jnp.dot(or_ref[...], wb_ref[...], preferred_element_type=F32)
    merged = g_ref[:, 0:D_MODEL].astype(F32) * a + g_ref[:, D_MODEL:].astype(F32) * b
    y = jnp.dot(merged.astype(BF16), wo_ref[...], preferred_element_type=F32)
    x1 = _layer_norm(DN_ALPHA * x_ref[...] + y, lg_ref[...], lb_ref[...])
    x1_ref[...] = x1
    logits = lax.dot_general(wrt_ref[...], x1.astype(BF16), (((1,), (1,)), ((), ())), preferred_element_type=F32)
    st_ref[...] = jax.nn.sigmoid(logits)


def _mix(x2d, oa, orr, gm, wa, wb, wo, lg, lb, wrt, tm):
    t = x2d.shape[0]
    row = lambda w: pl.BlockSpec((tm, w), lambda i: (i, 0))
    full = lambda a: pl.BlockSpec(a.shape, lambda i: (0,) * a.ndim)
    return pl.pallas_call(
        _mix_kernel,
        grid=(t // tm,),
        in_specs=[row(D_MODEL), row(QA_W), row(QR_W), row(2 * D_MODEL),
                  full(wa), full(wb), full(wo), full(lg), full(lb), full(wrt)],
        out_specs=[row(D_MODEL), pl.BlockSpec((N_EXPERTS, tm), lambda i: (0, i))],
        out_shape=[jax.ShapeDtypeStruct((t, D_MODEL), F32), jax.ShapeDtypeStruct((N_EXPERTS, t), F32)],
        compiler_params=_cparams(("parallel",)),
        name="mix",
    )(x2d, oa, orr, gm, wa, wb, wo, lg, lb, wrt)


def _route_kernel(s_ref, bias_ref, e_ref, w_ref, r_ref, cnt_ref, carry_ref):
    @pl.when(pl.program_id(0) == 0)
    def _():
        carry_ref[...] = jnp.zeros_like(carry_ref)

    s = s_ref[...]
    tl = s.shape[1]
    choice = s + bias_ref[...]
    row = lax.broadcasted_iota(I32, (N_EXPERTS, tl), 0)
    row_g = lax.broadcasted_iota(I32, (GROUP_SIZE, tl), 0)
    neg = -jnp.inf

    scores = []
    for g in range(N_GROUPS):
        blk = choice[g * GROUP_SIZE:(g + 1) * GROUP_SIZE]
        m1 = jnp.max(blk, axis=0, keepdims=True)
        i1 = jnp.min(jnp.where(blk == m1, row_g, GROUP_SIZE), axis=0, keepdims=True)
        m2 = jnp.max(jnp.where(row_g == i1, neg, blk), axis=0, keepdims=True)
        scores.append(m1 + m2)
    sc = jnp.concatenate(scores, axis=0)
    gi = lax.broadcasted_iota(I32, sc.shape, 0)
    grank = jnp.zeros(sc.shape, I32)
    for g in range(N_GROUPS):
        other = sc[g:g + 1]
        ahead = jnp.where(other > sc, 1, jnp.where(other == sc, jnp.where(gi > g, 1, 0), 0))
        grank = grank + ahead
    cm = jnp.concatenate(
        [jnp.where(grank[g:g + 1] < TOPK_GROUPS, choice[g * GROUP_SIZE:(g + 1) * GROUP_SIZE], neg)
         for g in range(N_GROUPS)], axis=0)

    experts, weights = [], []
    for _ in range(TOP_K):
        m = jnp.max(cm, axis=0, keepdims=True)
        idx = jnp.min(jnp.where(cm == m, row, N_EXPERTS), axis=0, keepdims=True)
        hit = row == idx
        weights.append(jnp.sum(jnp.where(hit, s, 0.0), axis=0, keepdims=True))
        cm = jnp.where(hit, neg, cm)
        experts.append(idx)
    e8 = jnp.concatenate(experts, axis=0)
    w8 = jnp.concatenate(weights, axis=0)
    e_ref[...] = e8
    w_ref[...] = w8 / jnp.sum(w8, axis=0, keepdims=True) * ROUTED_SCALE

    before = (lax.broadcasted_iota(I32, (tl, tl), 0) < lax.broadcasted_iota(I32, (tl, tl), 1))
    before = jnp.where(before, 1.0, 0.0).astype(BF16)
    carry = carry_ref[...]
    ranks = []
    for k in range(TOP_K):
        hit = row == experts[k]
        onehot = jnp.where(hit, 1.0, 0.0)
        prefix = jnp.dot(onehot.astype(BF16), before, preferred_element_type=F32)
        ranks.append(jnp.sum(jnp.where(hit, prefix + carry, 0.0), axis=0, keepdims=True))
        carry = carry + jnp.sum(onehot, axis=1, keepdims=True)
    r_ref[...] = jnp.concatenate(ranks, axis=0).astype(I32)
    carry_ref[...] = carry
    cnt_ref[...] = carry.astype(I32)


def _route(st, bias_col, tl):
    t = st.shape[1]
    tok = pl.BlockSpec((TOP_K, tl), lambda i: (0, i))
    return pl.pallas_call(
        _route_kernel,
        grid=(t // tl,),
        in_specs=[pl.BlockSpec((N_EXPERTS, tl), lambda i: (0, i)), pl.BlockSpec((N_EXPERTS, 1), lambda i: (0, 0))],
        out_specs=[tok, tok, tok, pl.BlockSpec((N_EXPERTS, 1), lambda i: (0, 0))],
        out_shape=[jax.ShapeDtypeStruct((TOP_K, t), I32), jax.ShapeDtypeStruct((TOP_K, t), F32),
                   jax.ShapeDtypeStruct((TOP_K, t), I32), jax.ShapeDtypeStruct((N_EXPERTS, 1), I32)],
        scratch_shapes=[pltpu.VMEM((N_EXPERTS, 1), F32)],
        compiler_params=_cparams(("arbitrary",)),
        name="route",
    )(st, bias_col)


def _row_copy(src, src_row, dst, dst_row, sem):
    return pltpu.make_async_copy(src.at[pl.ds(src_row, 1)], dst.at[pl.ds(dst_row, 1)], sem)


def _dispatch_kernel(ps_ref, e_ref, r_ref, x_ref, xs_in, xs_hbm, sem, *, td):
    del xs_in

    def start(t, carry):
        for k in range(TOP_K):
            d = ps_ref[e_ref[k, t]] + r_ref[k, t]
            _row_copy(x_ref, t, xs_hbm, d, sem).start()
        return carry

    lax.fori_loop(0, td, start, 0)

    def wait(t, carry):
        for k in range(TOP_K):
            _row_copy(x_ref, 0, xs_hbm, 0, sem).wait()
        return carry

    lax.fori_loop(0, td, wait, 0)


def _dispatch(pstart, e8, r8, x1, xs0, td):
    t = x1.shape[0]
    smem_tok = pl.BlockSpec((TOP_K, td), lambda i, ps: (0, i), memory_space=pltpu.SMEM)
    return pl.pallas_call(
        functools.partial(_dispatch_kernel, td=td),
        grid_spec=pltpu.PrefetchScalarGridSpec(
            num_scalar_prefetch=1,
            grid=(t // td,),
            in_specs=[smem_tok, smem_tok, pl.BlockSpec((td, D_MODEL), lambda i, ps: (i, 0)),
                      pl.BlockSpec(memory_space=pl.ANY)],
            out_specs=pl.BlockSpec(memory_space=pl.ANY),
            scratch_shapes=[pltpu.SemaphoreType.DMA(())],
        ),
        out_shape=jax.ShapeDtypeStruct(xs0.shape, xs0.dtype),
        input_output_aliases={4: 0},
        compiler_params=_cparams(("arbitrary",)),
        name="dispatch",
    )(pstart, e8, r8, x1, xs0)


def _moe_kernel(be_ref, nv_ref, x_ref, wg_ref, wu_ref, wd_ref, y_ref):
    j = pl.program_id(0)

    @pl.when(nv_ref[j] > 0)
    def _():
        xb = x_ref[...].astype(BF16)
        g = jnp.dot(xb, wg_ref[0].astype(BF16), preferred_element_type=F32)
        u = jnp.dot(xb, wu_ref[0].astype(BF16), preferred_element_type=F32)
        h = (_silu(g) * u).astype(BF16)
        y_ref[...] = jnp.dot(h, wd_ref[0].astype(BF16), preferred_element_type=F32)

    @pl.when(nv_ref[j] == 0)
    def _():
        y_ref[...] = jnp.zeros_like(y_ref)


def _moe(blk_exp, blk_valid, xs, wg, wu, wd):
    n_rows = xs.shape[0]
    rows = pl.BlockSpec((MOE_ROWS, D_MODEL), lambda j, be, nv: (j, 0))
    return pl.pallas_call(
        _moe_kernel,
        grid_spec=pltpu.PrefetchScalarGridSpec(
            num_scalar_prefetch=2,
            grid=(n_rows // MOE_ROWS,),
            in_specs=[rows,
                      pl.BlockSpec((1, D_MODEL, D_EXPERT), lambda j, be, nv: (be[j], 0, 0)),
                      pl.BlockSpec((1, D_MODEL, D_EXPERT), lambda j, be, nv: (be[j], 0, 0)),
                      pl.BlockSpec((1, D_EXPERT, D_MODEL), lambda j, be, nv: (be[j], 0, 0))],
            out_specs=rows,
        ),
        out_shape=jax.ShapeDtypeStruct((n_rows, D_MODEL), F32),
        compiler_params=_cparams(("arbitrary",)),
        name="moe",
    )(blk_exp, blk_valid, xs, wg, wu, wd)


def _final_kernel(ps_ref, e_ref, r_ref, x1_ref, w_ref, p_ref, wsg_ref, wsu_ref, wsd_ref, lg_ref, lb_ref,
                  wpp_ref, wpg_ref, ys_hbm, out_ref, gbuf, sem, *, tf):
    def start(t, carry):
        for k in range(TOP_K):
            d = ps_ref[e_ref[k, t]] + r_ref[k, t]
            _row_copy(ys_hbm, d, gbuf.at[k], t, sem).start()
        return carry

    lax.fori_loop(0, tf, start, 0)

    x1 = x1_ref[...]
    xb = x1.astype(BF16)
    hs = _silu(jnp.dot(xb, wsg_ref[...], preferred_element_type=F32)) * jnp.dot(xb, wsu_ref[...],
                                                                               preferred_element_type=F32)
    y = jnp.dot(hs.astype(BF16), wsd_ref[...], preferred_element_type=F32)

    def wait(t, carry):
        for k in range(TOP_K):
            _row_copy(ys_hbm, 0, gbuf.at[k], 0, sem).wait()
        return carry

    lax.fori_loop(0, tf, wait, 0)

    for k in range(TOP_K):
        y = y + w_ref[:, k:k + 1] * gbuf[k]
    x2 = _layer_norm(DN_ALPHA * x1 + y, lg_ref[...], lb_ref[...])
    gate = jax.nn.sigmoid(jnp.dot(x2.astype(BF16), wpg_ref[...], preferred_element_type=F32))
    out_ref[...] = x2 + gate * jnp.dot(p_ref[...].astype(BF16), wpp_ref[...], preferred_element_type=F32)


def _final(pstart, e8, r8, x1, w_tok, p2d, wsg, wsu, wsd, lg, lb, wpp, wpg, ys, tf):
    t = x1.shape[0]
    smem_tok = pl.BlockSpec((TOP_K, tf), lambda i, ps: (0, i), memory_space=pltpu.SMEM)
    row = lambda w: pl.BlockSpec((tf, w), lambda i, ps: (i, 0))
    full = lambda a: pl.BlockSpec(a.shape, lambda i, ps: (0,) * a.ndim)
    return pl.pallas_call(
        functools.partial(_final_kernel, tf=tf),
        grid_spec=pltpu.PrefetchScalarGridSpec(
            num_scalar_prefetch=1,
            grid=(t // tf,),
            in_specs=[smem_tok, smem_tok, row(D_MODEL), row(TOP_K), row(D_PLE),
                      full(wsg), full(wsu), full(wsd), full(lg), full(lb), full(wpp), full(wpg),
                      pl.BlockSpec(memory_space=pl.ANY)],
            out_specs=row(D_MODEL),
            scratch_shapes=[pltpu.VMEM((TOP_K, tf, D_MODEL), F32), pltpu.SemaphoreType.DMA(())],
        ),
        out_shape=jax.ShapeDtypeStruct((t, D_MODEL), F32),
        compiler_params=_cparams(("arbitrary",)),
        name="final",
    )(pstart, e8, r8, x1, w_tok, p2d, wsg, wsu, wsd, lg, lb, wpp, wpg, ys)


def _t5_bucket(rel):
    nb = N_BUCKETS // 2
    max_exact = nb // 2
    ret = jnp.where(rel > 0, nb, 0)
    n = jnp.abs(rel)
    nf = jnp.maximum(n, 1).astype(F32)
    large = max_exact + (jnp.log(nf / max_exact) / math.log(MAX_DIST / max_exact) * (nb - max_exact)).astype(I32)
    large = jnp.minimum(large, nb - 1)
    return ret + jnp.where(n < max_exact, n, large)


def _rel_bias(table, q_len, k_len):
    rel = (jnp.arange(k_len, dtype=I32)[None, :] - WINDOW) - jnp.arange(q_len, dtype=I32)[:, None]
    b = table[_t5_bucket(rel)]
    return jnp.transpose(b, (2, 0, 1)).reshape(N_KV_A, GROUP_A * q_len, k_len).astype(F32)


def _rotary_tables(pos):
    half = DK_R // 2
    inv = ROPE_BASE ** (-jnp.arange(half, dtype=F32) / half)
    ang = pos.astype(F32)[:, None] * inv[None, :]
    cos, sin = jnp.cos(ang), jnp.sin(ang)
    return jnp.concatenate([cos, cos], -1), jnp.concatenate([-sin, sin], -1)


def _decay_tables(c):
    log_gamma = jnp.log1p(-jnp.exp2(-5.0 - jnp.arange(N_HEADS_R, dtype=F32)))
    idx = jnp.arange(c, dtype=F32)
    di = jnp.exp(jnp.abs(idx[:, None] - idx[None, :])[None] * log_gamma[:, None, None])
    qd = jnp.exp((idx[None, :] + 1.0) * log_gamma[:, None])
    kd = jnp.exp((c - 1.0 - idx[None, :]) * log_gamma[:, None])
    gc = jnp.exp(c * log_gamma)
    bc = lambda a: jnp.broadcast_to(a[:, :, None], (N_HEADS_R, c, DK_R))
    return di, bc(qd), bc(kd), jnp.broadcast_to(gc[:, None, None], (N_HEADS_R, DK_R, DK_R))


def _moe_ffn(x1, st, p2d, wts):
    t = x1.shape[0]
    tile = min(256, t)
    e8, w8, r8, counts = _route(st, wts['router_bias'], tile)
    counts = counts[:, 0]
    padded = (counts + MOE_ROWS - 1) // MOE_ROWS * MOE_ROWS
    pad_end = jnp.cumsum(padded)
    pstart = (pad_end - padded).astype(I32)
    n_rows = t * TOP_K + N_EXPERTS * MOE_ROWS
    blk_start = jnp.arange(n_rows // MOE_ROWS, dtype=I32) * MOE_ROWS
    blk_exp = jnp.minimum(jnp.searchsorted(pad_end, blk_start, side='right'), N_EXPERTS - 1).astype(I32)
    blk_valid = jnp.clip(pstart[blk_exp] + counts[blk_exp] - blk_start, 0, MOE_ROWS).astype(I32)
    xs = _dispatch(pstart, e8, r8, x1, jnp.zeros((n_rows, D_MODEL), F32), tile)
    ys = _moe(blk_exp, blk_valid, xs, wts['w_exp_gate'], wts['w_exp_up'], wts['w_exp_down'])
    return _final(pstart, e8, r8, x1, w8.T, p2d, wts['w_sh_gate'], wts['w_sh_up'], wts['w_sh_down'],
                  wts['ln2_g'], wts['ln2_b'], wts['w_ple_proj'], wts['w_ple_gate'], ys, tile)


def _post_mixers(x2d, oa, orr, gm, p2d, wts):
    t = x2d.shape[0]
    x1, st = _mix(x2d, oa, orr, gm, wts['w_branch_attn'], wts['w_branch_ret'], wts['w_out'],
                  wts['ln1_g'], wts['ln1_b'], wts['w_router_t'], min(512, t))
    return _moe_ffn(x1, st, p2d, wts)


def kernel(x_prompt, x_sample, cache_attn_k, cache_attn_v, state_retention, p_prompt, p_sample,
           w_in, attn_sinks, rel_bias_table, ret_gn_gain, w_branch_attn, w_branch_ret, w_out,
           ln1_g, ln1_b, w_router, router_bias, w_exp_gate, w_exp_up, w_exp_down,
           w_sh_gate, w_sh_up, w_sh_down, ln2_g, ln2_b, w_ple_proj, w_ple_gate):
    b, s, _ = x_prompt.shape
    bd, l, _ = x_sample.shape
    i = 0
    row = lambda a: a.reshape(1, -1).astype(F32)
    wts = dict(
        w_branch_attn=w_branch_attn[i].astype(BF16), w_branch_ret=w_branch_ret[i].astype(BF16),
        w_out=w_out[i].astype(BF16), ln1_g=row(ln1_g[i]), ln1_b=row(ln1_b[i]),
        w_router_t=w_router[i].T.astype(BF16), router_bias=router_bias[i].reshape(N_EXPERTS, 1).astype(F32),
        w_exp_gate=w_exp_gate[i], w_exp_up=w_exp_up[i], w_exp_down=w_exp_down[i],
        w_sh_gate=w_sh_gate[i].astype(BF16), w_sh_up=w_sh_up[i].astype(BF16), w_sh_down=w_sh_down[i].astype(BF16),
        ln2_g=row(ln2_g[i]), ln2_b=row(ln2_b[i]),
        w_ple_proj=w_ple_proj[i].astype(BF16), w_ple_gate=w_ple_gate[i].astype(BF16))
    w_in_bf = w_in[i].astype(BF16)
    gain = row(ret_gn_gain[i])
    sinks = attn_sinks[i].astype(F32).reshape(N_KV_A, GROUP_A)

    xp = x_prompt.reshape(b * s, D_MODEL)
    cos_p, sin_p = _rotary_tables(jnp.arange(s, dtype=I32))
    qkv, kvf, qkr, vg, gm = _inproj(xp, w_in_bf, cos_p, sin_p, min(512, s))
    sink_p = jnp.repeat(sinks, CHUNK, axis=1)[..., None]
    oa = _attn_prompt(qkv, _rel_bias(rel_bias_table, CHUNK, WINDOW + CHUNK), sink_p, b, s)
    s0 = jnp.zeros((b, N_HEADS_R, DK_R, DK_R), F32)
    orr, ns_p = _retention(qkr, vg, s0, _decay_tables(CHUNK), gain, b, s, CHUNK)
    y_p = _post_mixers(xp, oa, orr, gm, p_prompt[i].reshape(b * s, D_PLE), wts).reshape(b, s, D_MODEL)
    kv_tail = kvf.reshape(b, s, 2, N_KV_A, HEAD_DIM_A)[:, s - WINDOW:]
    nk_p, nv_p = kv_tail[:, :, 0], kv_tail[:, :, 1]

    xs = x_sample.reshape(bd * l, D_MODEL)
    cos_s, sin_s = _rotary_tables(jnp.tile(PAST_LEN + jnp.arange(l, dtype=I32), bd))
    qkv, kvf, qkr, vg, gm = _inproj(xs, w_in_bf, cos_s, sin_s, bd * l)
    sink_s = jnp.repeat(sinks, l, axis=1)[..., None]
    oa, nk_s, nv_s = _attn_sample(qkv, kvf, cache_attn_k[i].reshape(bd, WINDOW, KA_W),
                                  cache_attn_v[i].reshape(bd, WINDOW, KA_W),
                                  _rel_bias(rel_bias_table, l, WINDOW + l), sink_s, bd, l)
    orr, ns_s = _retention(qkr, vg, state_retention[i].astype(F32), _decay_tables(l), gain, bd, l, l)
    y_s = _post_mixers(xs, oa, orr, gm, p_sample[i].reshape(bd * l, D_PLE), wts).reshape(bd, l, D_MODEL)
    shape_kv = (bd, WINDOW, N_KV_A, HEAD_DIM_A)
    return (y_p, y_s, nk_p[None], nv_p[None], ns_p[None],
            nk_s.reshape(shape_kv)[None], nv_s.reshape(shape_kv)[None], ns_s[None])
```

```python
import functools
import math

import jax
import jax.numpy as jnp
from jax import lax
from jax.experimental import pallas as pl
from jax.experimental.pallas import tpu as pltpu

F32 = jnp.float32
BF16 = jnp.bfloat16
I32 = jnp.int32

D_MODEL = 1024
CHUNK = 64
WINDOW = 128
N_HEADS_A = 8
N_KV_A = 2
GROUP_A = 4
HEAD_DIM_A = 64
N_BUCKETS = 32
MAX_DIST = 128
N_HEADS_R = 4
DK_R = 128
ROPE_BASE = 10000.0
N_EXPERTS = 256
TOP_K = 8
N_GROUPS = 8
GROUP_SIZE = N_EXPERTS // N_GROUPS
TOPK_GROUPS = 4
D_EXPERT = 256
ROUTED_SCALE = 2.5
D_PLE = 256
LN_EPS = 1e-5
NEG_INF = -1e30
PAST_LEN = 2048
DEPTH = 1
DN_ALPHA = (2 * DEPTH) ** 0.25

QA_W = N_HEADS_A * HEAD_DIM_A
KA_W = N_KV_A * HEAD_DIM_A
QR_W = N_HEADS_R * DK_R
OFF_KV = QA_W
OFF_QR = QA_W + 2 * KA_W
OFF_VR = OFF_QR + 2 * QR_W
OFF_GM = OFF_VR + 2 * QR_W
IN_W = OFF_GM + 2 * D_MODEL

MOE_ROWS = 256
VMEM_LIMIT = 56 * 1024 * 1024


def _cparams(sem, vmem=VMEM_LIMIT):
    return pltpu.CompilerParams(dimension_semantics=sem, vmem_limit_bytes=vmem)


def _layer_norm(h, g, b):
    mu = jnp.mean(h, axis=-1, keepdims=True)
    d = h - mu
    var = jnp.mean(d * d, axis=-1, keepdims=True)
    return d * lax.rsqrt(var + LN_EPS) * g + b


def _silu(x):
    return x * jax.nn.sigmoid(x)


U32 = jnp.uint32
PACK_W = D_MODEL // 2
PACK_SUB = PACK_W // 128


def _bf16_bits(x):
    return pltpu.bitcast(x.astype(BF16).astype(F32), U32)


def _store_packed(ref, x):
    n = x.shape[0]
    words = _bf16_bits(x[:, :PACK_W]) | (_bf16_bits(x[:, PACK_W:]) >> 16)
    for c in range(PACK_SUB):
        ref[pl.ds(c, n, stride=PACK_SUB), :] = words[:, c * 128:(c + 1) * 128]


def _load_packed(ref, n):
    words = jnp.concatenate([ref[pl.ds(c, n, stride=PACK_SUB), :] for c in range(PACK_SUB)], axis=1)
    hi = pltpu.bitcast(words & jnp.uint32(0xFFFF0000), F32)
    lo = pltpu.bitcast(words << 16, F32)
    return hi, lo


def _inproj_kernel(x_ref, w_ref, cos_ref, sin_ref, qkv_ref, kvf_ref, qkr_ref, vg_ref, gm_ref):
    xb = x_ref[...].astype(BF16)

    def mm(lo, hi):
        return jnp.dot(xb, w_ref[:, lo:hi], preferred_element_type=F32)

    qkv_ref[:, 0:QA_W] = mm(0, QA_W).astype(BF16)
    kv = mm(OFF_KV, OFF_QR)
    kvf_ref[...] = kv
    qkv_ref[:, OFF_KV:OFF_QR] = kv.astype(BF16)
    cos = cos_ref[...]
    sin = sin_ref[...]
    for part in range(2):
        z = mm(OFF_QR + part * QR_W, OFF_QR + (part + 1) * QR_W)
        for h in range(N_HEADS_R):
            zh = z[:, h * DK_R:(h + 1) * DK_R]
            r = zh * cos + pltpu.roll(zh, DK_R // 2, axis=1) * sin
            if part == 1:
                r = r * (DK_R ** -0.5)
            c0 = part * QR_W + h * DK_R
            qkr_ref[:, c0:c0 + DK_R] = r.astype(BF16)
    vg_ref[...] = mm(OFF_VR, OFF_GM).astype(BF16)
    for half in range(2):
        lo = OFF_GM + half * D_MODEL
        gm_ref[:, half * D_MODEL:(half + 1) * D_MODEL] = jax.nn.sigmoid(mm(lo, lo + D_MODEL)).astype(BF16)


def _inproj(x2d, w_bf, cos_tab, sin_tab, tm):
    t = x2d.shape[0]
    nper = cos_tab.shape[0] // tm
    row = lambda i: (i, 0)
    return pl.pallas_call(
        _inproj_kernel,
        grid=(t // tm,),
        in_specs=[
            pl.BlockSpec((tm, D_MODEL), row),
            pl.BlockSpec((D_MODEL, IN_W), lambda i: (0, 0)),
            pl.BlockSpec((tm, DK_R), lambda i: (i % nper, 0)),
            pl.BlockSpec((tm, DK_R), lambda i: (i % nper, 0)),
        ],
        out_specs=[
            pl.BlockSpec((tm, OFF_QR), row),
            pl.BlockSpec((tm, 2 * KA_W), row),
            pl.BlockSpec((tm, 2 * QR_W), row),
            pl.BlockSpec((tm, 2 * QR_W), row),
            pl.BlockSpec((tm, 2 * D_MODEL), row),
        ],
        out_shape=[
            jax.ShapeDtypeStruct((t, OFF_QR), BF16),
            jax.ShapeDtypeStruct((t, 2 * KA_W), F32),
            jax.ShapeDtypeStruct((t, 2 * QR_W), BF16),
            jax.ShapeDtypeStruct((t, 2 * QR_W), BF16),
            jax.ShapeDtypeStruct((t, 2 * D_MODEL), BF16),
        ],
        compiler_params=_cparams(("parallel",)),
        name="inproj",
    )(x2d, w_bf, cos_tab, sin_tab)


def _attend(q4, k, v, bias, sink, thresh):
    s = lax.dot_general(q4, k, (((1,), (1,)), ((), ())), preferred_element_type=F32)
    s = s * (HEAD_DIM_A ** -0.5) + bias
    if thresh is not None:
        col = lax.broadcasted_iota(I32, s.shape, 1)
        s = jnp.where(col >= thresh, s, NEG_INF)
    m = jnp.maximum(jnp.max(s, axis=-1, keepdims=True), sink)
    e = jnp.exp(s - m)
    p = e / (jnp.sum(e, axis=-1, keepdims=True) + jnp.exp(sink - m))
    return jnp.dot(p.astype(BF16), v, preferred_element_type=F32)


def _attn_prompt_kernel(q_ref, kvc_ref, kvp_ref, bias_ref, sink_ref, o_ref, kv_buf, *, n_chunks):
    i = pl.program_id(1)
    kv_buf[0:WINDOW, :] = kvp_ref[...]
    kv_buf[WINDOW:, :] = kvc_ref[...]

    def chunk(j, carry):
        r0 = pl.multiple_of(j * CHUNK, CHUNK)
        thresh = jnp.maximum(WINDOW - (i * n_chunks + j) * CHUNK, 0)
        for kv in range(N_KV_A):
            q4 = jnp.concatenate(
                [q_ref[pl.ds(r0, CHUNK), (kv * GROUP_A + g) * HEAD_DIM_A:(kv * GROUP_A + g + 1) * HEAD_DIM_A]
                 for g in range(GROUP_A)], axis=0)
            k = kv_buf[pl.ds(r0, WINDOW + CHUNK), kv * HEAD_DIM_A:(kv + 1) * HEAD_DIM_A]
            v = kv_buf[pl.ds(r0, WINDOW + CHUNK), KA_W + kv * HEAD_DIM_A:KA_W + (kv + 1) * HEAD_DIM_A]
            o4 = _attend(q4, k, v, bias_ref[kv], sink_ref[kv], thresh)
            for g in range(GROUP_A):
                c0 = (kv * GROUP_A + g) * HEAD_DIM_A
                o_ref[pl.ds(r0, CHUNK), c0:c0 + HEAD_DIM_A] = o4[g * CHUNK:(g + 1) * CHUNK].astype(BF16)
        return carry

    lax.fori_loop(0, n_chunks, chunk, 0)


def _attn_prompt(qkv, bias, sink, b, s):
    qb = min(512, s)
    n_chunks = qb // CHUNK
    nq = s // qb
    per = qb // WINDOW
    kv_col = OFF_KV // (2 * KA_W)
    return pl.pallas_call(
        functools.partial(_attn_prompt_kernel, n_chunks=n_chunks),
        grid=(b, nq),
        in_specs=[
            pl.BlockSpec((qb, QA_W), lambda bi, i: (bi * nq + i, 0)),
            pl.BlockSpec((qb, 2 * KA_W), lambda bi, i: (bi * nq + i, kv_col)),
            pl.BlockSpec((WINDOW, 2 * KA_W), lambda bi, i: (jnp.maximum((bi * nq + i) * per - 1, 0), kv_col)),
            pl.BlockSpec((N_KV_A, GROUP_A * CHUNK, WINDOW + CHUNK), lambda bi, i: (0, 0, 0)),
            pl.BlockSpec((N_KV_A, GROUP_A * CHUNK, 1), lambda bi, i: (0, 0, 0)),
        ],
        out_specs=pl.BlockSpec((qb, QA_W), lambda bi, i: (bi * nq + i, 0)),
        out_shape=jax.ShapeDtypeStruct((b * s, QA_W), BF16),
        scratch_shapes=[pltpu.VMEM((WINDOW + qb, 2 * KA_W), BF16)],
        compiler_params=_cparams(("parallel", "parallel")),
        name="attn_prompt",
    )(qkv, qkv, qkv, bias, sink)


def _attn_sample_kernel(q_ref, kvf_ref, ck_ref, cv_ref, bias_ref, sink_ref, o_ref, nk_ref, nv_ref, *, l):
    k_all = jnp.concatenate([ck_ref[0], kvf_ref[:, 0:KA_W]], axis=0)
    v_all = jnp.concatenate([cv_ref[0], kvf_ref[:, KA_W:2 * KA_W]], axis=0)
    nk_ref[0] = k_all[l:]
    nv_ref[0] = v_all[l:]
    kb = k_all.astype(BF16)
    vb = v_all.astype(BF16)
    for kv in range(N_KV_A):
        q4 = jnp.concatenate(
            [q_ref[:, (kv * GROUP_A + g) * HEAD_DIM_A:(kv * GROUP_A + g + 1) * HEAD_DIM_A] for g in range(GROUP_A)],
            axis=0)
        o4 = _attend(q4, kb[:, kv * HEAD_DIM_A:(kv + 1) * HEAD_DIM_A], vb[:, kv * HEAD_DIM_A:(kv + 1) * HEAD_DIM_A],
                     bias_ref[kv], sink_ref[kv], None)
        for g in range(GROUP_A):
            c0 = (kv * GROUP_A + g) * HEAD_DIM_A
            o_ref[:, c0:c0 + HEAD_DIM_A] = o4[g * l:(g + 1) * l].astype(BF16)


def _attn_sample(qkv, kvf, cache_k, cache_v, bias, sink, bd, l):
    cache_spec = pl.BlockSpec((1, WINDOW, KA_W), lambda bi: (bi, 0, 0))
    cache_shape = jax.ShapeDtypeStruct((bd, WINDOW, KA_W), F32)
    return pl.pallas_call(
        functools.partial(_attn_sample_kernel, l=l),
        grid=(bd,),
        in_specs=[
            pl.BlockSpec((l, OFF_QR), lambda bi: (bi, 0)),
            pl.BlockSpec((l, 2 * KA_W), lambda bi: (bi, 0)),
            cache_spec, cache_spec,
            pl.BlockSpec((N_KV_A, GROUP_A * l, WINDOW + l), lambda bi: (0, 0, 0)),
            pl.BlockSpec((N_KV_A, GROUP_A * l, 1), lambda bi: (0, 0, 0)),
        ],
        out_specs=[pl.BlockSpec((l, QA_W), lambda bi: (bi, 0)), cache_spec, cache_spec],
        out_shape=[jax.ShapeDtypeStruct((bd * l, QA_W), BF16), cache_shape, cache_shape],
        compiler_params=_cparams(("parallel",)),
        name="attn_sample",
    )(qkv, kvf, cache_k, cache_v, bias, sink)


def _retention_kernel(q_ref, k_ref, v_ref, g_ref, s0_ref, di_ref, qd_ref, kd_ref, gc_ref, gain_ref,
                      o_ref, s_ref, *, n_chunks, c):
    di = di_ref[0]
    qd = qd_ref[0]
    kd = kd_ref[0]
    gc = gc_ref[0]
    gain = gain_ref[...]

    def chunk(n, state):
        r0 = pl.multiple_of(n * c, c)
        qc = q_ref[pl.ds(r0, c), :]
        kc = k_ref[pl.ds(r0, c), :]
        vc = v_ref[pl.ds(r0, c), :]
        sc = lax.dot_general(qc, kc, (((1,), (1,)), ((), ())), preferred_element_type=F32) * di
        o = jnp.dot(sc.astype(BF16), vc, preferred_element_type=F32)
        q_dec = (qc.astype(F32) * qd).astype(BF16)
        o = o + jnp.dot(q_dec, state.astype(BF16), preferred_element_type=F32)
        k_dec = (kc.astype(F32) * kd).astype(BF16)
        state = gc * state + lax.dot_general(k_dec, vc, (((0,), (0,)), ((), ())), preferred_element_type=F32)
        mu = jnp.mean(o, axis=-1, keepdims=True)
        d = o - mu
        var = jnp.mean(d * d, axis=-1, keepdims=True)
        y = d * lax.rsqrt(var + LN_EPS) * gain
        o_ref[pl.ds(r0, c), :] = (y * _silu(g_ref[pl.ds(r0, c), :].astype(F32))).astype(BF16)
        return state

    s_ref[0, 0] = lax.fori_loop(0, n_chunks, chunk, s0_ref[0, 0])


def _retention(qkr, vg, s0, tabs, gain, b, s, c):
    di, qd, kd, gc = tabs
    seq = lambda off: pl.BlockSpec((s, DK_R), lambda bi, h: (bi, off + h))
    head3 = lambda shape: pl.BlockSpec((1,) + shape, lambda bi, h: (h, 0, 0))
    st_spec = pl.BlockSpec((1, 1, DK_R, DK_R), lambda bi, h: (bi, h, 0, 0))
    return pl.pallas_call(
        functools.partial(_retention_kernel, n_chunks=s // c, c=c),
        grid=(b, N_HEADS_R),
        in_specs=[seq(0), seq(N_HEADS_R), seq(0), seq(N_HEADS_R), st_spec,
                  head3((c, c)), head3((c, DK_R)), head3((c, DK_R)), head3((DK_R, DK_R)),
                  pl.BlockSpec((1, DK_R), lambda bi, h: (0, h))],
        out_specs=[pl.BlockSpec((s, DK_R), lambda bi, h: (bi, h)), st_spec],
        out_shape=[jax.ShapeDtypeStruct((b * s, QR_W), BF16), jax.ShapeDtypeStruct((b, N_HEADS_R, DK_R, DK_R), F32)],
        compiler_params=_cparams(("parallel", "parallel")),
        name="retention",
    )(qkr, qkr, vg, vg, s0, di, qd, kd, gc, gain)


def _mix_kernel(x_ref, oa_ref, or_ref, g_ref, wa_ref, wb_ref, wo_ref, lg_ref, lb_ref, wrt_ref, x1_ref, st_ref, xp_ref):
    a = jnp.dot(oa_ref[...], wa_ref[...], preferred_element_type=F32)
    b = jnp.dot(or_ref[...], wb_ref[...], preferred_element_type=F32)
    merged = g_ref[:, 0:D_MODEL].astype(F32) * a + g_ref[:, D_MODEL:].astype(F32) * b
    y = jnp.dot(merged.astype(BF16), wo_ref[...], preferred_element_type=F32)
    x1 = _layer_norm(DN_ALPHA * x_ref[...] + y, lg_ref[...], lb_ref[...])
    x1_ref[...] = x1
    _store_packed(xp_ref, x1)
    logits = lax.dot_general(wrt_ref[...], x1.astype(BF16), (((1,), (1,)), ((), ())), preferred_element_type=F32)
    st_ref[...] = jax.nn.sigmoid(logits)


def _mix(x2d, oa, orr, gm, wa, wb, wo, lg, lb, wrt, tm):
    t = x2d.shape[0]
    row = lambda w: pl.BlockSpec((tm, w), lambda i: (i, 0))
    full = lambda a: pl.BlockSpec(a.shape, lambda i: (0,) * a.ndim)
    return pl.pallas_call(
        _mix_kernel,
        grid=(t // tm,),
        in_specs=[row(D_MODEL), row(QA_W), row(QR_W), row(2 * D_MODEL),
                  full(wa), full(wb), full(wo), full(lg), full(lb), full(wrt)],
        out_specs=[row(D_MODEL), pl.BlockSpec((N_EXPERTS, tm), lambda i: (0, i)),
                   pl.BlockSpec((tm * PACK_SUB, 128), lambda i: (i, 0))],
        out_shape=[jax.ShapeDtypeStruct((t, D_MODEL), F32), jax.ShapeDtypeStruct((N_EXPERTS, t), F32),
                   jax.ShapeDtypeStruct((t * PACK_SUB, 128), U32)],
        compiler_params=_cparams(("parallel",)),
        name="mix",
    )(x2d, oa, orr, gm, wa, wb, wo, lg, lb, wrt)


def _route_kernel(s_ref, bias_ref, e_ref, w_ref, r_ref, cnt_ref, carry_ref):
    @pl.when(pl.program_id(0) == 0)
    def _():
        carry_ref[...] = jnp.zeros_like(carry_ref)

    s = s_ref[...]
    tl = s.shape[1]
    choice = s + bias_ref[...]
    row = lax.broadcasted_iota(I32, (N_EXPERTS, tl), 0)
    row_g = lax.broadcasted_iota(I32, (GROUP_SIZE, tl), 0)
    neg = -jnp.inf

    scores = []
    for g in range(N_GROUPS):
        blk = choice[g * GROUP_SIZE:(g + 1) * GROUP_SIZE]
        m1 = jnp.max(blk, axis=0, keepdims=True)
        i1 = jnp.min(jnp.where(blk == m1, row_g, GROUP_SIZE), axis=0, keepdims=True)
        m2 = jnp.max(jnp.where(row_g == i1, neg, blk), axis=0, keepdims=True)
        scores.append(m1 + m2)
    sc = jnp.concatenate(scores, axis=0)
    gi = lax.broadcasted_iota(I32, sc.shape, 0)
    grank = jnp.zeros(sc.shape, I32)
    for g in range(N_GROUPS):
        other = sc[g:g + 1]
        ahead = jnp.where(other > sc, 1, jnp.where(other == sc, jnp.where(gi > g, 1, 0), 0))
        grank = grank + ahead
    cm = jnp.concatenate(
        [jnp.where(grank[g:g + 1] < TOPK_GROUPS, choice[g * GROUP_SIZE:(g + 1) * GROUP_SIZE], neg)
         for g in range(N_GROUPS)], axis=0)

    experts, weights = [], []
    for _ in range(TOP_K):
        m = jnp.max(cm, axis=0, keepdims=True)
        idx = jnp.min(jnp.where(cm == m, row, N_EXPERTS), axis=0, keepdims=True)
        hit = row == idx
        weights.append(jnp.sum(jnp.where(hit, s, 0.0), axis=0, keepdims=True))
        cm = jnp.where(hit, neg, cm)
        experts.append(idx)
    e8 = jnp.concatenate(experts, axis=0)
    w8 = jnp.concatenate(weights, axis=0)
    e_ref[...] = e8
    w_ref[...] = w8 / jnp.sum(w8, axis=0, keepdims=True) * ROUTED_SCALE

    before = (lax.broadcasted_iota(I32, (tl, tl), 0) < lax.broadcasted_iota(I32, (tl, tl), 1))
    before = jnp.where(before, 1.0, 0.0).astype(BF16)
    carry = carry_ref[...]
    ranks = []
    for k in range(TOP_K):
        hit = row == experts[k]
        onehot = jnp.where(hit, 1.0, 0.0)
        prefix = jnp.dot(onehot.astype(BF16), before, preferred_element_type=F32)
        ranks.append(jnp.sum(jnp.where(hit, prefix + carry, 0.0), axis=0, keepdims=True))
        carry = carry + jnp.sum(onehot, axis=1, keepdims=True)
    r_ref[...] = jnp.concatenate(ranks, axis=0).astype(I32)
    carry_ref[...] = carry
    cnt_ref[...] = carry.astype(I32)


def _route(st, bias_col, tl):
    t = st.shape[1]
    tok = pl.BlockSpec((TOP_K, tl), lambda i: (0, i))
    return pl.pallas_call(
        _route_kernel,
        grid=(t // tl,),
        in_specs=[pl.BlockSpec((N_EXPERTS, tl), lambda i: (0, i)), pl.BlockSpec((N_EXPERTS, 1), lambda i: (0, 0))],
        out_specs=[tok, tok, tok, pl.BlockSpec((N_EXPERTS, 1), lambda i: (0, 0))],
        out_shape=[jax.ShapeDtypeStruct((TOP_K, t), I32), jax.ShapeDtypeStruct((TOP_K, t), F32),
                   jax.ShapeDtypeStruct((TOP_K, t), I32), jax.ShapeDtypeStruct((N_EXPERTS, 1), I32)],
        scratch_shapes=[pltpu.VMEM((N_EXPERTS, 1), F32)],
        compiler_params=_cparams(("arbitrary",)),
        name="route",
    )(st, bias_col)


TOKEN_GROUP = 8


def _row_copy(src, src_tok, dst, dst_tok, sem):
    s0 = pl.multiple_of(src_tok * PACK_SUB, PACK_SUB)
    d0 = pl.multiple_of(dst_tok * PACK_SUB, PACK_SUB)
    return pltpu.make_async_copy(src.at[pl.ds(s0, PACK_SUB)], dst.at[pl.ds(d0, PACK_SUB)], sem)


def _wait_rows(src, dst, sem, n_tokens):
    def wait(t, carry):
        for k in range(TOP_K):
            _row_copy(src, 0, dst, 0, sem).wait()
        return carry

    lax.fori_loop(0, n_tokens, wait, 0)


def _dispatch_kernel(d_ref, x_ref, xs_hbm, sem, *, td):
    def start(g, carry):
        for j in range(TOKEN_GROUP):
            t = g * TOKEN_GROUP + j
            for k in range(TOP_K):
                _row_copy(x_ref, t, xs_hbm, d_ref[k, t], sem).start(priority=k % 2)
        return carry

    lax.fori_loop(0, td // TOKEN_GROUP, start, 0)
    _wait_rows(x_ref, xs_hbm, sem, td)


def _dispatch(d8, xp, n_rows, td):
    t = d8.shape[1]
    return pl.pallas_call(
        functools.partial(_dispatch_kernel, td=td),
        grid=(t // td,),
        in_specs=[pl.BlockSpec((TOP_K, td), lambda i: (0, i), memory_space=pltpu.SMEM),
                  pl.BlockSpec((td * PACK_SUB, 128), lambda i: (i, 0))],
        out_specs=pl.BlockSpec(memory_space=pl.ANY),
        scratch_shapes=[pltpu.SemaphoreType.DMA(())],
        out_shape=jax.ShapeDtypeStruct((n_rows * PACK_SUB, 128), U32),
        compiler_params=_cparams(("arbitrary",)),
        name="dispatch",
    )(d8, xp)


def _moe_kernel(be_ref, nv_ref, x_ref, wg_ref, wu_ref, wd_ref, y_ref, wg_bf, wu_bf, wd_bf):
    j = pl.program_id(0)
    nv = nv_ref[j]

    @pl.when((j == 0) | (be_ref[j] != be_ref[jnp.maximum(j - 1, 0)]))
    def _():
        wg_bf[...] = wg_ref[0].astype(BF16)
        wu_bf[...] = wu_ref[0].astype(BF16)
        wd_bf[...] = wd_ref[0].astype(BF16)

    @pl.when(nv > 0)
    def _():
        hi, lo = _load_packed(x_ref, MOE_ROWS)
        valid = lax.broadcasted_iota(I32, hi.shape, 0) < nv
        xb = jnp.concatenate([jnp.where(valid, hi, 0.0).astype(BF16), jnp.where(valid, lo, 0.0).astype(BF16)], axis=1)
        g = jnp.dot(xb, wg_bf[...], preferred_element_type=F32)
        u = jnp.dot(xb, wu_bf[...], preferred_element_type=F32)
        h = (_silu(g) * u).astype(BF16)
        _store_packed(y_ref, jnp.dot(h, wd_bf[...], preferred_element_type=F32))

    @pl.when(nv == 0)
    def _():
        y_ref[...] = jnp.zeros_like(y_ref)


def _moe(blk_exp, blk_valid, xs, wg, wu, wd):
    n_blocks = blk_exp.shape[0]
    rows = pl.BlockSpec((MOE_ROWS * PACK_SUB, 128), lambda j, be, nv: (j, 0))
    return pl.pallas_call(
        _moe_kernel,
        grid_spec=pltpu.PrefetchScalarGridSpec(
            num_scalar_prefetch=2,
            grid=(n_blocks,),
            in_specs=[rows,
                      pl.BlockSpec((1, D_MODEL, D_EXPERT), lambda j, be, nv: (be[j], 0, 0)),
                      pl.BlockSpec((1, D_MODEL, D_EXPERT), lambda j, be, nv: (be[j], 0, 0)),
                      pl.BlockSpec((1, D_EXPERT, D_MODEL), lambda j, be, nv: (be[j], 0, 0))],
            out_specs=rows,
            scratch_shapes=[pltpu.VMEM((D_MODEL, D_EXPERT), BF16), pltpu.VMEM((D_MODEL, D_EXPERT), BF16),
                            pltpu.VMEM((D_EXPERT, D_MODEL), BF16)],
        ),
        out_shape=jax.ShapeDtypeStruct(xs.shape, U32),
        compiler_params=_cparams(("arbitrary",)),
        name="moe",
    )(blk_exp, blk_valid, xs, wg, wu, wd)


def _final_kernel(d_ref, x1_ref, w_ref, p_ref, wsg_ref, wsu_ref, wsd_ref, lg_ref, lb_ref,
                  wpp_ref, wpg_ref, ys_hbm, out_ref, gbuf, sem, *, tf):
    def start(g, carry):
        for j in range(TOKEN_GROUP):
            t = g * TOKEN_GROUP + j
            for k in range(TOP_K):
                _row_copy(ys_hbm, d_ref[k, t], gbuf.at[k], t, sem).start(priority=k % 2)
        return carry

    lax.fori_loop(0, tf // TOKEN_GROUP, start, 0)

    x1 = x1_ref[...]
    xb = x1.astype(BF16)
    hs = _silu(jnp.dot(xb, wsg_ref[...], preferred_element_type=F32)) * jnp.dot(xb, wsu_ref[...],
                                                                               preferred_element_type=F32)
    y = jnp.dot(hs.astype(BF16), wsd_ref[...], preferred_element_type=F32)

    _wait_rows(ys_hbm, gbuf.at[0], sem, tf)

    y_hi = jnp.zeros((tf, PACK_W), F32)
    y_lo = jnp.zeros((tf, PACK_W), F32)
    for k in range(TOP_K):
        hi, lo = _load_packed(gbuf.at[k], tf)
        wk = w_ref[:, k:k + 1]
        y_hi = y_hi + wk * hi
        y_lo = y_lo + wk * lo
    y = y + jnp.concatenate([y_hi, y_lo], axis=1)
    x2 = _layer_norm(DN_ALPHA * x1 + y, lg_ref[...], lb_ref[...])
    gate = jax.nn.sigmoid(jnp.dot(x2.astype(BF16), wpg_ref[...], preferred_element_type=F32))
    out_ref[...] = x2 + gate * jnp.dot(p_ref[...].astype(BF16), wpp_ref[...], preferred_element_type=F32)


def _final(d8, x1, w_tok, p2d, wsg, wsu, wsd, lg, lb, wpp, wpg, ys, tf):
    t = x1.shape[0]
    row = lambda w: pl.BlockSpec((tf, w), lambda i: (i, 0))
    full = lambda a: pl.BlockSpec(a.shape, lambda i: (0,) * a.ndim)
    return pl.pallas_call(
        functools.partial(_final_kernel, tf=tf),
        grid=(t // tf,),
        in_specs=[pl.BlockSpec((TOP_K, tf), lambda i: (0, i), memory_space=pltpu.SMEM),
                  row(D_MODEL), row(TOP_K), row(D_PLE),
                  full(wsg), full(wsu), full(wsd), full(lg), full(lb), full(wpp), full(wpg),
                  pl.BlockSpec(memory_space=pl.ANY)],
        out_specs=row(D_MODEL),
        scratch_shapes=[pltpu.VMEM((TOP_K, tf * PACK_SUB, 128), U32), pltpu.SemaphoreType.DMA(())],
        out_shape=jax.ShapeDtypeStruct((t, D_MODEL), F32),
        compiler_params=_cparams(("arbitrary",)),
        name="final",
    )(d8, x1, w_tok, p2d, wsg, wsu, wsd, lg, lb, wpp, wpg, ys)


def _t5_bucket(rel):
    nb = N_BUCKETS // 2
    max_exact = nb // 2
    ret = jnp.where(rel > 0, nb, 0)
    n = jnp.abs(rel)
    nf = jnp.maximum(n, 1).astype(F32)
    large = max_exact + (jnp.log(nf / max_exact) / math.log(MAX_DIST / max_exact) * (nb - max_exact)).astype(I32)
    large = jnp.minimum(large, nb - 1)
    return ret + jnp.where(n < max_exact, n, large)


def _rel_bias(table, q_len, k_len):
    rel = (jnp.arange(k_len, dtype=I32)[None, :] - WINDOW) - jnp.arange(q_len, dtype=I32)[:, None]
    b = table[_t5_bucket(rel)]
    return jnp.transpose(b, (2, 0, 1)).reshape(N_KV_A, GROUP_A * q_len, k_len).astype(F32)


def _rotary_tables(pos):
    half = DK_R // 2
    inv = ROPE_BASE ** (-jnp.arange(half, dtype=F32) / half)
    ang = pos.astype(F32)[:, None] * inv[None, :]
    cos, sin = jnp.cos(ang), jnp.sin(ang)
    return jnp.concatenate([cos, cos], -1), jnp.concatenate([-sin, sin], -1)


def _decay_tables(c):
    log_gamma = jnp.log1p(-jnp.exp2(-5.0 - jnp.arange(N_HEADS_R, dtype=F32)))
    idx = jnp.arange(c, dtype=F32)
    di = jnp.exp(jnp.abs(idx[:, None] - idx[None, :])[None] * log_gamma[:, None, None])
    qd = jnp.exp((idx[None, :] + 1.0) * log_gamma[:, None])
    kd = jnp.exp((c - 1.0 - idx[None, :]) * log_gamma[:, None])
    gc = jnp.exp(c * log_gamma)
    bc = lambda a: jnp.broadcast_to(a[:, :, None], (N_HEADS_R, c, DK_R))
    return di, bc(qd), bc(kd), jnp.broadcast_to(gc[:, None, None], (N_HEADS_R, DK_R, DK_R))


def _moe_ffn(x1, xp, st, p2d, wts):
    t = x1.shape[0]
    tile = min(256, t)
    e8, w8, r8, counts = _route(st, wts['router_bias'], tile)
    counts = counts[:, 0]
    padded = (counts + MOE_ROWS - 1) // MOE_ROWS * MOE_ROWS
    pad_end = jnp.cumsum(padded)
    pstart = (pad_end - padded).astype(I32)
    n_rows = t * TOP_K + N_EXPERTS * MOE_ROWS
    blk_start = jnp.arange(n_rows // MOE_ROWS, dtype=I32) * MOE_ROWS
    blk_exp = jnp.minimum(jnp.searchsorted(pad_end, blk_start, side='right'), N_EXPERTS - 1).astype(I32)
    blk_valid = jnp.clip(pstart[blk_exp] + counts[blk_exp] - blk_start, 0, MOE_ROWS).astype(I32)
    d8 = pstart[e8] + r8
    xs = _dispatch(d8, xp, n_rows, tile)
    ys = _moe(blk_exp, blk_valid, xs, wts['w_exp_gate'], wts['w_exp_up'], wts['w_exp_down'])
    return _final(d8, x1, w8.T, p2d, wts['w_sh_gate'], wts['w_sh_up'], wts['w_sh_down'],
                  wts['ln2_g'], wts['ln2_b'], wts['w_ple_proj'], wts['w_ple_gate'], ys, tile)


def _post_mixers(x2d, oa, orr, gm, p2d, wts):
    t = x2d.shape[0]
    x1, st, xp = _mix(x2d, oa, orr, gm, wts['w_branch_attn'], wts['w_branch_ret'], wts['w_out'],
                      wts['ln1_g'], wts['ln1_b'], wts['w_router_t'], min(512, t))
    return _moe_ffn(x1, xp, st, p2d, wts)


def kernel(x_prompt, x_sample, cache_attn_k, cache_attn_v, state_retention, p_prompt, p_sample,
           w_in, attn_sinks, rel_bias_table, ret_gn_gain, w_branch_attn, w_branch_ret, w_out,
           ln1_g, ln1_b, w_router, router_bias, w_exp_gate, w_exp_up, w_exp_down,
           w_sh_gate, w_sh_up, w_sh_down, ln2_g, ln2_b, w_ple_proj, w_ple_gate):
    b, s, _ = x_prompt.shape
    bd, l, _ = x_sample.shape
    i = 0
    row = lambda a: a.reshape(1, -1).astype(F32)
    wts = dict(
        w_branch_attn=w_branch_attn[i].astype(BF16), w_branch_ret=w_branch_ret[i].astype(BF16),
        w_out=w_out[i].astype(BF16), ln1_g=row(ln1_g[i]), ln1_b=row(ln1_b[i]),
        w_router_t=w_router[i].T.astype(BF16), router_bias=router_bias[i].reshape(N_EXPERTS, 1).astype(F32),
        w_exp_gate=w_exp_gate[i], w_exp_up=w_exp_up[i], w_exp_down=w_exp_down[i],
        w_sh_gate=w_sh_gate[i].astype(BF16), w_sh_up=w_sh_up[i].astype(BF16), w_sh_down=w_sh_down[i].astype(BF16),
        ln2_g=row(ln2_g[i]), ln2_b=row(ln2_b[i]),
        w_ple_proj=w_ple_proj[i].astype(BF16), w_ple_gate=w_ple_gate[i].astype(BF16))
    w_in_bf = w_in[i].astype(BF16)
    gain = row(ret_gn_gain[i])
    sinks = attn_sinks[i].astype(F32).reshape(N_KV_A, GROUP_A)

    xp = x_prompt.reshape(b * s, D_MODEL)
    cos_p, sin_p = _rotary_tables(jnp.arange(s, dtype=I32))
    qkv, kvf, qkr, vg, gm = _inproj(xp, w_in_bf, cos_p, sin_p, min(512, s))
    sink_p = jnp.repeat(sinks, CHUNK, axis=1)[..., None]
    oa = _attn_prompt(qkv, _rel_bias(rel_bias_table, CHUNK, WINDOW + CHUNK), sink_p, b, s)
    s0 = jnp.zeros((b, N_HEADS_R, DK_R, DK_R), F32)
    orr, ns_p = _retention(qkr, vg, s0, _decay_tables(CHUNK), gain, b, s, CHUNK)
    y_p = _post_mixers(xp, oa, orr, gm, p_prompt[i].reshape(b * s, D_PLE), wts).reshape(b, s, D_MODEL)
    kv_tail = kvf.reshape(b, s, 2, N_KV_A, HEAD_DIM_A)[:, s - WINDOW:]
    nk_p, nv_p = kv_tail[:, :, 0], kv_tail[:, :, 1]

    xs = x_sample.reshape(bd * l, D_MODEL)
    cos_s, sin_s = _rotary_tables(jnp.tile(PAST_LEN + jnp.arange(l, dtype=I32), bd))
    qkv, kvf, qkr, vg, gm = _inproj(xs, w_in_bf, cos_s, sin_s, bd * l)
    sink_s = jnp.repeat(sinks, l, axis=1)[..., None]
    oa, nk_s, nv_s = _attn_sample(qkv, kvf, cache_attn_k[i].reshape(bd, WINDOW, KA_W),
                                  cache_attn_v[i].reshape(bd, WINDOW, KA_W),
                                  _rel_bias(rel_bias_table, l, WINDOW + l), sink_s, bd, l)
    orr, ns_s = _retention(qkr, vg, state_retention[i].astype(F32), _decay_tables(l), gain, bd, l, l)
    y_s = _post_mixers(xs, oa, orr, gm, p_sample[i].reshape(bd * l, D_PLE), wts).reshape(bd, l, D_MODEL)
    shape_kv = (bd, WINDOW, N_KV_A, HEAD_DIM_A)
    return (y_p, y_s, nk_p[None], nv_p[None], ns_p[None],
            nk_s.reshape(shape_kv)[None], nv_s.reshape(shape_kv)[None], ns_s[None])
```

```python
import functools
import math

import jax
import jax.numpy as jnp
from jax import lax
from jax.experimental import pallas as pl
from jax.experimental.pallas import tpu as pltpu

F32 = jnp.float32
BF16 = jnp.bfloat16
I32 = jnp.int32

D_MODEL = 1024
CHUNK = 64
WINDOW = 128
N_HEADS_A = 8
N_KV_A = 2
GROUP_A = 4
HEAD_DIM_A = 64
N_BUCKETS = 32
MAX_DIST = 128
N_HEADS_R = 4
DK_R = 128
ROPE_BASE = 10000.0
N_EXPERTS = 256
TOP_K = 8
N_GROUPS = 8
GROUP_SIZE = N_EXPERTS // N_GROUPS
TOPK_GROUPS = 4
D_EXPERT = 256
ROUTED_SCALE = 2.5
D_PLE = 256
LN_EPS = 1e-5
NEG_INF = -1e30
PAST_LEN = 2048
DEPTH = 1
DN_ALPHA = (2 * DEPTH) ** 0.25

QA_W = N_HEADS_A * HEAD_DIM_A
KA_W = N_KV_A * HEAD_DIM_A
QR_W = N_HEADS_R * DK_R
OFF_KV = QA_W
OFF_QR = QA_W + 2 * KA_W
OFF_VR = OFF_QR + 2 * QR_W
OFF_GM = OFF_VR + 2 * QR_W
IN_W = OFF_GM + 2 * D_MODEL

MOE_ROWS = 256
VMEM_LIMIT = 56 * 1024 * 1024


def _cparams(sem, vmem=VMEM_LIMIT):
    return pltpu.CompilerParams(dimension_semantics=sem, vmem_limit_bytes=vmem)


def _layer_norm(h, g, b):
    mu = jnp.mean(h, axis=-1, keepdims=True)
    d = h - mu
    var = jnp.mean(d * d, axis=-1, keepdims=True)
    return d * lax.rsqrt(var + LN_EPS) * g + b


def _silu(x):
    return x * jax.nn.sigmoid(x)


U32 = jnp.uint32
PACK_W = D_MODEL // 2
PACK_SUB = PACK_W // 128


def _bf16_bits(x):
    return pltpu.bitcast(x.astype(BF16).astype(F32), U32)


def _store_packed(ref, x):
    n = x.shape[0]
    words = _bf16_bits(x[:, :PACK_W]) | (_bf16_bits(x[:, PACK_W:]) >> 16)
    for c in range(PACK_SUB):
        ref[pl.ds(c, n, stride=PACK_SUB), :] = words[:, c * 128:(c + 1) * 128]


def _load_packed(ref, n):
    words = jnp.concatenate([ref[pl.ds(c, n, stride=PACK_SUB), :] for c in range(PACK_SUB)], axis=1)
    hi = pltpu.bitcast(words & jnp.uint32(0xFFFF0000), F32)
    lo = pltpu.bitcast(words << 16, F32)
    return hi, lo


def _inproj_kernel(x_ref, w_ref, cos_ref, sin_ref, qkv_ref, kvf_ref, qkr_ref, vg_ref, gm_ref):
    xb = x_ref[...].astype(BF16)

    def mm(lo, hi):
        return jnp.dot(xb, w_ref[:, lo:hi], preferred_element_type=F32)

    qkv_ref[:, 0:QA_W] = mm(0, QA_W).astype(BF16)
    kv = mm(OFF_KV, OFF_QR)
    kvf_ref[...] = kv
    qkv_ref[:, OFF_KV:OFF_QR] = kv.astype(BF16)
    cos = cos_ref[...]
    sin = sin_ref[...]
    for part in range(2):
        z = mm(OFF_QR + part * QR_W, OFF_QR + (part + 1) * QR_W)
        for h in range(N_HEADS_R):
            zh = z[:, h * DK_R:(h + 1) * DK_R]
            r = zh * cos + pltpu.roll(zh, DK_R // 2, axis=1) * sin
            if part == 1:
                r = r * (DK_R ** -0.5)
            c0 = part * QR_W + h * DK_R
            qkr_ref[:, c0:c0 + DK_R] = r.astype(BF16)
    vg_ref[...] = mm(OFF_VR, OFF_GM).astype(BF16)
    for half in range(2):
        lo = OFF_GM + half * D_MODEL
        gm_ref[:, half * D_MODEL:(half + 1) * D_MODEL] = jax.nn.sigmoid(mm(lo, lo + D_MODEL)).astype(BF16)


def _inproj(x2d, w_bf, cos_tab, sin_tab, tm):
    t = x2d.shape[0]
    nper = cos_tab.shape[0] // tm
    row = lambda i: (i, 0)
    return pl.pallas_call(
        _inproj_kernel,
        grid=(t // tm,),
        in_specs=[
            pl.BlockSpec((tm, D_MODEL), row),
            pl.BlockSpec((D_MODEL, IN_W), lambda i: (0, 0)),
            pl.BlockSpec((tm, DK_R), lambda i: (i % nper, 0)),
            pl.BlockSpec((tm, DK_R), lambda i: (i % nper, 0)),
        ],
        out_specs=[
            pl.BlockSpec((tm, OFF_QR), row),
            pl.BlockSpec((tm, 2 * KA_W), row),
            pl.BlockSpec((tm, 2 * QR_W), row),
            pl.BlockSpec((tm, 2 * QR_W), row),
            pl.BlockSpec((tm, 2 * D_MODEL), row),
        ],
        out_shape=[
            jax.ShapeDtypeStruct((t, OFF_QR), BF16),
            jax.ShapeDtypeStruct((t, 2 * KA_W), F32),
            jax.ShapeDtypeStruct((t, 2 * QR_W), BF16),
            jax.ShapeDtypeStruct((t, 2 * QR_W), BF16),
            jax.ShapeDtypeStruct((t, 2 * D_MODEL), BF16),
        ],
        compiler_params=_cparams(("parallel",)),
        name="inproj",
    )(x2d, w_bf, cos_tab, sin_tab)


def _attend(q4, k, v, bias, sink, thresh):
    s = lax.dot_general(q4, k, (((1,), (1,)), ((), ())), preferred_element_type=F32)
    s = s * (HEAD_DIM_A ** -0.5) + bias
    if thresh is not None:
        col = lax.broadcasted_iota(I32, s.shape, 1)
        s = jnp.where(col >= thresh, s, NEG_INF)
    m = jnp.maximum(jnp.max(s, axis=-1, keepdims=True), sink)
    e = jnp.exp(s - m)
    p = e / (jnp.sum(e, axis=-1, keepdims=True) + jnp.exp(sink - m))
    return jnp.dot(p.astype(BF16), v, preferred_element_type=F32)


def _attn_prompt_kernel(q_ref, kvc_ref, kvp_ref, bias_ref, sink_ref, o_ref, kv_buf, *, n_chunks):
    i = pl.program_id(1)
    kv_buf[0:WINDOW, :] = kvp_ref[...]
    kv_buf[WINDOW:, :] = kvc_ref[...]

    def chunk(j, carry):
        r0 = pl.multiple_of(j * CHUNK, CHUNK)
        thresh = jnp.maximum(WINDOW - (i * n_chunks + j) * CHUNK, 0)
        for kv in range(N_KV_A):
            q4 = jnp.concatenate(
                [q_ref[pl.ds(r0, CHUNK), (kv * GROUP_A + g) * HEAD_DIM_A:(kv * GROUP_A + g + 1) * HEAD_DIM_A]
                 for g in range(GROUP_A)], axis=0)
            k = kv_buf[pl.ds(r0, WINDOW + CHUNK), kv * HEAD_DIM_A:(kv + 1) * HEAD_DIM_A]
            v = kv_buf[pl.ds(r0, WINDOW + CHUNK), KA_W + kv * HEAD_DIM_A:KA_W + (kv + 1) * HEAD_DIM_A]
            o4 = _attend(q4, k, v, bias_ref[kv], sink_ref[kv], thresh)
            for g in range(GROUP_A):
                c0 = (kv * GROUP_A + g) * HEAD_DIM_A
                o_ref[pl.ds(r0, CHUNK), c0:c0 + HEAD_DIM_A] = o4[g * CHUNK:(g + 1) * CHUNK].astype(BF16)
        return carry

    lax.fori_loop(0, n_chunks, chunk, 0)


def _attn_prompt(qkv, bias, sink, b, s):
    qb = min(512, s)
    n_chunks = qb // CHUNK
    nq = s // qb
    per = qb // WINDOW
    kv_col = OFF_KV // (2 * KA_W)
    return pl.pallas_call(
        functools.partial(_attn_prompt_kernel, n_chunks=n_chunks),
        grid=(b, nq),
        in_specs=[
            pl.BlockSpec((qb, QA_W), lambda bi, i: (bi * nq + i, 0)),
            pl.BlockSpec((qb, 2 * KA_W), lambda bi, i: (bi * nq + i, kv_col)),
            pl.BlockSpec((WINDOW, 2 * KA_W), lambda bi, i: (jnp.maximum((bi * nq + i) * per - 1, 0), kv_col)),
            pl.BlockSpec((N_KV_A, GROUP_A * CHUNK, WINDOW + CHUNK), lambda bi, i: (0, 0, 0)),
            pl.BlockSpec((N_KV_A, GROUP_A * CHUNK, 1), lambda bi, i: (0, 0, 0)),
        ],
        out_specs=pl.BlockSpec((qb, QA_W), lambda bi, i: (bi * nq + i, 0)),
        out_shape=jax.ShapeDtypeStruct((b * s, QA_W), BF16),
        scratch_shapes=[pltpu.VMEM((WINDOW + qb, 2 * KA_W), BF16)],
        compiler_params=_cparams(("parallel", "parallel")),
        name="attn_prompt",
    )(qkv, qkv, qkv, bias, sink)


def _attn_sample_kernel(q_ref, kvf_ref, ck_ref, cv_ref, bias_ref, sink_ref, o_ref, nk_ref, nv_ref, *, l):
    k_all = jnp.concatenate([ck_ref[0], kvf_ref[:, 0:KA_W]], axis=0)
    v_all = jnp.concatenate([cv_ref[0], kvf_ref[:, KA_W:2 * KA_W]], axis=0)
    nk_ref[0] = k_all[l:]
    nv_ref[0] = v_all[l:]
    kb = k_all.astype(BF16)
    vb = v_all.astype(BF16)
    for kv in range(N_KV_A):
        q4 = jnp.concatenate(
            [q_ref[:, (kv * GROUP_A + g) * HEAD_DIM_A:(kv * GROUP_A + g + 1) * HEAD_DIM_A] for g in range(GROUP_A)],
            axis=0)
        o4 = _attend(q4, kb[:, kv * HEAD_DIM_A:(kv + 1) * HEAD_DIM_A], vb[:, kv * HEAD_DIM_A:(kv + 1) * HEAD_DIM_A],
                     bias_ref[kv], sink_ref[kv], None)
        for g in range(GROUP_A):
            c0 = (kv * GROUP_A + g) * HEAD_DIM_A
            o_ref[:, c0:c0 + HEAD_DIM_A] = o4[g * l:(g + 1) * l].astype(BF16)


def _attn_sample(qkv, kvf, cache_k, cache_v, bias, sink, bd, l):
    cache_spec = pl.BlockSpec((1, WINDOW, KA_W), lambda bi: (bi, 0, 0))
    cache_shape = jax.ShapeDtypeStruct((bd, WINDOW, KA_W), F32)
    return pl.pallas_call(
        functools.partial(_attn_sample_kernel, l=l),
        grid=(bd,),
        in_specs=[
            pl.BlockSpec((l, OFF_QR), lambda bi: (bi, 0)),
            pl.BlockSpec((l, 2 * KA_W), lambda bi: (bi, 0)),
            cache_spec, cache_spec,
            pl.BlockSpec((N_KV_A, GROUP_A * l, WINDOW + l), lambda bi: (0, 0, 0)),
            pl.BlockSpec((N_KV_A, GROUP_A * l, 1), lambda bi: (0, 0, 0)),
        ],
        out_specs=[pl.BlockSpec((l, QA_W), lambda bi: (bi, 0)), cache_spec, cache_spec],
        out_shape=[jax.ShapeDtypeStruct((bd * l, QA_W), BF16), cache_shape, cache_shape],
        compiler_params=_cparams(("parallel",)),
        name="attn_sample",
    )(qkv, kvf, cache_k, cache_v, bias, sink)


def _retention_kernel(q_ref, k_ref, v_ref, g_ref, s0_ref, di_ref, qd_ref, kd_ref, gc_ref, gain_ref,
                      o_ref, s_ref, *, n_chunks, c):
    di = di_ref[0]
    qd = qd_ref[0]
    kd = kd_ref[0]
    gc = gc_ref[0]
    gain = gain_ref[...]

    def chunk(n, state):
        r0 = pl.multiple_of(n * c, c)
        qc = q_ref[pl.ds(r0, c), :]
        kc = k_ref[pl.ds(r0, c), :]
        vc = v_ref[pl.ds(r0, c), :]
        sc = lax.dot_general(qc, kc, (((1,), (1,)), ((), ())), preferred_element_type=F32) * di
        o = jnp.dot(sc.astype(BF16), vc, preferred_element_type=F32)
        q_dec = (qc.astype(F32) * qd).astype(BF16)
        o = o + jnp.dot(q_dec, state.astype(BF16), preferred_element_type=F32)
        k_dec = (kc.astype(F32) * kd).astype(BF16)
        state = gc * state + lax.dot_general(k_dec, vc, (((0,), (0,)), ((), ())), preferred_element_type=F32)
        mu = jnp.mean(o, axis=-1, keepdims=True)
        d = o - mu
        var = jnp.mean(d * d, axis=-1, keepdims=True)
        y = d * lax.rsqrt(var + LN_EPS) * gain
        o_ref[pl.ds(r0, c), :] = (y * _silu(g_ref[pl.ds(r0, c), :].astype(F32))).astype(BF16)
        return state

    s_ref[0, 0] = lax.fori_loop(0, n_chunks, chunk, s0_ref[0, 0])


def _retention(qkr, vg, s0, tabs, gain, b, s, c):
    di, qd, kd, gc = tabs
    seq = lambda off: pl.BlockSpec((s, DK_R), lambda bi, h: (bi, off + h))
    head3 = lambda shape: pl.BlockSpec((1,) + shape, lambda bi, h: (h, 0, 0))
    st_spec = pl.BlockSpec((1, 1, DK_R, DK_R), lambda bi, h: (bi, h, 0, 0))
    return pl.pallas_call(
        functools.partial(_retention_kernel, n_chunks=s // c, c=c),
        grid=(b, N_HEADS_R),
        in_specs=[seq(0), seq(N_HEADS_R), seq(0), seq(N_HEADS_R), st_spec,
                  head3((c, c)), head3((c, DK_R)), head3((c, DK_R)), head3((DK_R, DK_R)),
                  pl.BlockSpec((1, DK_R), lambda bi, h: (0, h))],
        out_specs=[pl.BlockSpec((s, DK_R), lambda bi, h: (bi, h)), st_spec],
        out_shape=[jax.ShapeDtypeStruct((b * s, QR_W), BF16), jax.ShapeDtypeStruct((b, N_HEADS_R, DK_R, DK_R), F32)],
        compiler_params=_cparams(("parallel", "parallel")),
        name="retention",
    )(qkr, qkr, vg, vg, s0, di, qd, kd, gc, gain)


def _mix_kernel(x_ref, oa_ref, or_ref, g_ref, wa_ref, wb_ref, wo_ref, lg_ref, lb_ref, wrt_ref, x1_ref, st_ref, xp_ref):
    a = jnp.dot(oa_ref[...], wa_ref[...], preferred_element_type=F32)
    b = jnp.dot(or_ref[...], wb_ref[...], preferred_element_type=F32)
    merged = g_ref[:, 0:D_MODEL].astype(F32) * a + g_ref[:, D_MODEL:].astype(F32) * b
    y = jnp.dot(merged.astype(BF16), wo_ref[...], preferred_element_type=F32)
    x1 = _layer_norm(DN_ALPHA * x_ref[...] + y, lg_ref[...], lb_ref[...])
    x1_ref[...] = x1
    _store_packed(xp_ref, x1)
    logits = lax.dot_general(wrt_ref[...], x1.astype(BF16), (((1,), (1,)), ((), ())), preferred_element_type=F32)
    st_ref[...] = jax.nn.sigmoid(logits)


def _mix(x2d, oa, orr, gm, wa, wb, wo, lg, lb, wrt, tm):
    t = x2d.shape[0]
    row = lambda w: pl.BlockSpec((tm, w), lambda i: (i, 0))
    full = lambda a: pl.BlockSpec(a.shape, lambda i: (0,) * a.ndim)
    return pl.pallas_call(
        _mix_kernel,
        grid=(t // tm,),
        in_specs=[row(D_MODEL), row(QA_W), row(QR_W), row(2 * D_MODEL),
                  full(wa), full(wb), full(wo), full(lg), full(lb), full(wrt)],
        out_specs=[row(D_MODEL), pl.BlockSpec((N_EXPERTS, tm), lambda i: (0, i)),
                   pl.BlockSpec((tm * PACK_SUB, 128), lambda i: (i, 0))],
        out_shape=[jax.ShapeDtypeStruct((t, D_MODEL), F32), jax.ShapeDtypeStruct((N_EXPERTS, t), F32),
                   jax.ShapeDtypeStruct((t * PACK_SUB, 128), U32)],
        compiler_params=_cparams(("parallel",)),
        name="mix",
    )(x2d, oa, orr, gm, wa, wb, wo, lg, lb, wrt)


def _route_kernel(s_ref, bias_ref, e_ref, w_ref, r_ref, cnt_ref, carry_ref):
    @pl.when(pl.program_id(0) == 0)
    def _():
        carry_ref[...] = jnp.zeros_like(carry_ref)

    s = s_ref[...]
    tl = s.shape[1]
    choice = s + bias_ref[...]
    row = lax.broadcasted_iota(I32, (N_EXPERTS, tl), 0)
    row_g = lax.broadcasted_iota(I32, (GROUP_SIZE, tl), 0)
    neg = -jnp.inf

    scores = []
    for g in range(N_GROUPS):
        blk = choice[g * GROUP_SIZE:(g + 1) * GROUP_SIZE]
        m1 = jnp.max(blk, axis=0, keepdims=True)
        i1 = jnp.min(jnp.where(blk == m1, row_g, GROUP_SIZE), axis=0, keepdims=True)
        m2 = jnp.max(jnp.where(row_g == i1, neg, blk), axis=0, keepdims=True)
        scores.append(m1 + m2)
    sc = jnp.concatenate(scores, axis=0)
    gi = lax.broadcasted_iota(I32, sc.shape, 0)
    grank = jnp.zeros(sc.shape, I32)
    for g in range(N_GROUPS):
        other = sc[g:g + 1]
        ahead = jnp.where(other > sc, 1, jnp.where(other == sc, jnp.where(gi > g, 1, 0), 0))
        grank = grank + ahead
    cm = jnp.concatenate(
        [jnp.where(grank[g:g + 1] < TOPK_GROUPS, choice[g * GROUP_SIZE:(g + 1) * GROUP_SIZE], neg)
         for g in range(N_GROUPS)], axis=0)

    experts, weights = [], []
    for _ in range(TOP_K):
        m = jnp.max(cm, axis=0, keepdims=True)
        idx = jnp.min(jnp.where(cm == m, row, N_EXPERTS), axis=0, keepdims=True)
        hit = row == idx
        weights.append(jnp.sum(jnp.where(hit, s, 0.0), axis=0, keepdims=True))
        cm = jnp.where(hit, neg, cm)
        experts.append(idx)
    e8 = jnp.concatenate(experts, axis=0)
    w8 = jnp.concatenate(weights, axis=0)
    e_ref[...] = e8
    w_ref[...] = w8 / jnp.sum(w8, axis=0, keepdims=True) * ROUTED_SCALE

    before = (lax.broadcasted_iota(I32, (tl, tl), 0) < lax.broadcasted_iota(I32, (tl, tl), 1))
    before = jnp.where(before, 1.0, 0.0).astype(BF16)
    carry = carry_ref[...]
    ranks = []
    for k in range(TOP_K):
        hit = row == experts[k]
        onehot = jnp.where(hit, 1.0, 0.0)
        prefix = jnp.dot(onehot.astype(BF16), before, preferred_element_type=F32)
        ranks.append(jnp.sum(jnp.where(hit, prefix + carry, 0.0), axis=0, keepdims=True))
        carry = carry + jnp.sum(onehot, axis=1, keepdims=True)
    r_ref[...] = jnp.concatenate(ranks, axis=0).astype(I32)
    carry_ref[...] = carry
    cnt_ref[...] = carry.astype(I32)


def _route(st, bias_col, tl):
    t = st.shape[1]
    tok = pl.BlockSpec((TOP_K, tl), lambda i: (0, i))
    return pl.pallas_call(
        _route_kernel,
        grid=(t // tl,),
        in_specs=[pl.BlockSpec((N_EXPERTS, tl), lambda i: (0, i)), pl.BlockSpec((N_EXPERTS, 1), lambda i: (0, 0))],
        out_specs=[tok, tok, tok, pl.BlockSpec((N_EXPERTS, 1), lambda i: (0, 0))],
        out_shape=[jax.ShapeDtypeStruct((TOP_K, t), I32), jax.ShapeDtypeStruct((TOP_K, t), F32),
                   jax.ShapeDtypeStruct((TOP_K, t), I32), jax.ShapeDtypeStruct((N_EXPERTS, 1), I32)],
        scratch_shapes=[pltpu.VMEM((N_EXPERTS, 1), F32)],
        compiler_params=_cparams(("arbitrary",)),
        name="route",
    )(st, bias_col)


def _dest_kernel(e_ref, r_ref, ps_ref, d_ref):
    tl = e_ref.shape[1]
    row = lax.broadcasted_iota(I32, (N_EXPERTS, tl), 0)
    ps = ps_ref[...]
    starts = [jnp.sum(jnp.where(row == e_ref[k:k + 1, :], ps, 0.0), axis=0, keepdims=True) for k in range(TOP_K)]
    d_ref[...] = jnp.concatenate(starts, axis=0).astype(I32) + r_ref[...]


def _dest(e8, r8, pstart_col, tl):
    t = e8.shape[1]
    tok = pl.BlockSpec((TOP_K, tl), lambda i: (0, i))
    return pl.pallas_call(
        _dest_kernel,
        grid=(t // tl,),
        in_specs=[tok, tok, pl.BlockSpec((N_EXPERTS, 1), lambda i: (0, 0))],
        out_specs=tok,
        out_shape=jax.ShapeDtypeStruct((TOP_K, t), I32),
        compiler_params=_cparams(("parallel",)),
        name="dest",
    )(e8, r8, pstart_col)


TOKEN_GROUP = 8


def _row_copy(src, src_tok, dst, dst_tok, sem):
    s0 = pl.multiple_of(src_tok * PACK_SUB, PACK_SUB)
    d0 = pl.multiple_of(dst_tok * PACK_SUB, PACK_SUB)
    return pltpu.make_async_copy(src.at[pl.ds(s0, PACK_SUB)], dst.at[pl.ds(d0, PACK_SUB)], sem)


def _wait_rows(src, dst, sem, n_tokens):
    def wait(t, carry):
        for k in range(TOP_K):
            _row_copy(src, 0, dst, 0, sem).wait()
        return carry

    lax.fori_loop(0, n_tokens, wait, 0)


def _dispatch_kernel(d_ref, x_ref, xs_hbm, sem, *, td):
    def start(g, carry):
        for j in range(TOKEN_GROUP):
            t = g * TOKEN_GROUP + j
            for k in range(TOP_K):
                _row_copy(x_ref, t, xs_hbm, d_ref[k, t], sem).start(priority=k % 2)
        return carry

    lax.fori_loop(0, td // TOKEN_GROUP, start, 0)
    _wait_rows(x_ref, xs_hbm, sem, td)


def _dispatch(d8, xp, n_rows, td):
    t = d8.shape[1]
    return pl.pallas_call(
        functools.partial(_dispatch_kernel, td=td),
        grid=(t // td,),
        in_specs=[pl.BlockSpec((TOP_K, td), lambda i: (0, i), memory_space=pltpu.SMEM),
                  pl.BlockSpec((td * PACK_SUB, 128), lambda i: (i, 0))],
        out_specs=pl.BlockSpec(memory_space=pl.ANY),
        scratch_shapes=[pltpu.SemaphoreType.DMA(())],
        out_shape=jax.ShapeDtypeStruct((n_rows * PACK_SUB, 128), U32),
        compiler_params=_cparams(("arbitrary",)),
        name="dispatch",
    )(d8, xp)


def _moe_kernel(be_ref, nv_ref, x_ref, wg_ref, wu_ref, wd_ref, y_ref, wg_bf, wu_bf, wd_bf):
    j = pl.program_id(0)
    nv = nv_ref[j]

    @pl.when((j == 0) | (be_ref[j] != be_ref[jnp.maximum(j - 1, 0)]))
    def _():
        wg_bf[...] = wg_ref[0].astype(BF16)
        wu_bf[...] = wu_ref[0].astype(BF16)
        wd_bf[...] = wd_ref[0].astype(BF16)

    @pl.when(nv > 0)
    def _():
        hi, lo = _load_packed(x_ref, MOE_ROWS)
        valid = lax.broadcasted_iota(I32, hi.shape, 0) < nv
        xb = jnp.concatenate([jnp.where(valid, hi, 0.0).astype(BF16), jnp.where(valid, lo, 0.0).astype(BF16)], axis=1)
        g = jnp.dot(xb, wg_bf[...], preferred_element_type=F32)
        u = jnp.dot(xb, wu_bf[...], preferred_element_type=F32)
        h = (_silu(g) * u).astype(BF16)
        _store_packed(y_ref, jnp.dot(h, wd_bf[...], preferred_element_type=F32))

    @pl.when(nv == 0)
    def _():
        y_ref[...] = jnp.zeros_like(y_ref)


def _moe(blk_exp, blk_valid, xs, wg, wu, wd):
    n_blocks = blk_exp.shape[0]
    rows = pl.BlockSpec((MOE_ROWS * PACK_SUB, 128), lambda j, be, nv: (j, 0))
    return pl.pallas_call(
        _moe_kernel,
        grid_spec=pltpu.PrefetchScalarGridSpec(
            num_scalar_prefetch=2,
            grid=(n_blocks,),
            in_specs=[rows,
                      pl.BlockSpec((1, D_MODEL, D_EXPERT), lambda j, be, nv: (be[j], 0, 0)),
                      pl.BlockSpec((1, D_MODEL, D_EXPERT), lambda j, be, nv: (be[j], 0, 0)),
                      pl.BlockSpec((1, D_EXPERT, D_MODEL), lambda j, be, nv: (be[j], 0, 0))],
            out_specs=rows,
            scratch_shapes=[pltpu.VMEM((D_MODEL, D_EXPERT), BF16), pltpu.VMEM((D_MODEL, D_EXPERT), BF16),
                            pltpu.VMEM((D_EXPERT, D_MODEL), BF16)],
        ),
        out_shape=jax.ShapeDtypeStruct(xs.shape, U32),
        compiler_params=_cparams(("arbitrary",)),
        name="moe",
    )(blk_exp, blk_valid, xs, wg, wu, wd)


def _final_kernel(d_ref, x1_ref, w_ref, p_ref, wsg_ref, wsu_ref, wsd_ref, lg_ref, lb_ref,
                  wpp_ref, wpg_ref, ys_hbm, out_ref, gbuf, sem, *, tf):
    def start(g, carry):
        for j in range(TOKEN_GROUP):
            t = g * TOKEN_GROUP + j
            for k in range(TOP_K):
                _row_copy(ys_hbm, d_ref[k, t], gbuf.at[k], t, sem).start(priority=k % 2)
        return carry

    lax.fori_loop(0, tf // TOKEN_GROUP, start, 0)

    x1 = x1_ref[...]
    xb = x1.astype(BF16)
    hs = _silu(jnp.dot(xb, wsg_ref[...], preferred_element_type=F32)) * jnp.dot(xb, wsu_ref[...],
                                                                               preferred_element_type=F32)
    y = jnp.dot(hs.astype(BF16), wsd_ref[...], preferred_element_type=F32)

    _wait_rows(ys_hbm, gbuf.at[0], sem, tf)

    y_hi = jnp.zeros((tf, PACK_W), F32)
    y_lo = jnp.zeros((tf, PACK_W), F32)
    for k in range(TOP_K):
        hi, lo = _load_packed(gbuf.at[k], tf)
        wk = w_ref[:, k:k + 1]
        y_hi = y_hi + wk * hi
        y_lo = y_lo + wk * lo
    y = y + jnp.concatenate([y_hi, y_lo], axis=1)
    x2 = _layer_norm(DN_ALPHA * x1 + y, lg_ref[...], lb_ref[...])
    gate = jax.nn.sigmoid(jnp.dot(x2.astype(BF16), wpg_ref[...], preferred_element_type=F32))
    out_ref[...] = x2 + gate * jnp.dot(p_ref[...].astype(BF16), wpp_ref[...], preferred_element_type=F32)


def _final(d8, x1, w_tok, p2d, wsg, wsu, wsd, lg, lb, wpp, wpg, ys, tf):
    t = x1.shape[0]
    row = lambda w: pl.BlockSpec((tf, w), lambda i: (i, 0))
    full = lambda a: pl.BlockSpec(a.shape, lambda i: (0,) * a.ndim)
    return pl.pallas_call(
        functools.partial(_final_kernel, tf=tf),
        grid=(t // tf,),
        in_specs=[pl.BlockSpec((TOP_K, tf), lambda i: (0, i), memory_space=pltpu.SMEM),
                  row(D_MODEL), row(TOP_K), row(D_PLE),
                  full(wsg), full(wsu), full(wsd), full(lg), full(lb), full(wpp), full(wpg),
                  pl.BlockSpec(memory_space=pl.ANY)],
        out_specs=row(D_MODEL),
        scratch_shapes=[pltpu.VMEM((TOP_K, tf * PACK_SUB, 128), U32), pltpu.SemaphoreType.DMA(())],
        out_shape=jax.ShapeDtypeStruct((t, D_MODEL), F32),
        compiler_params=_cparams(("arbitrary",)),
        name="final",
    )(d8, x1, w_tok, p2d, wsg, wsu, wsd, lg, lb, wpp, wpg, ys)


def _t5_bucket(rel):
    nb = N_BUCKETS // 2
    max_exact = nb // 2
    ret = jnp.where(rel > 0, nb, 0)
    n = jnp.abs(rel)
    nf = jnp.maximum(n, 1).astype(F32)
    large = max_exact + (jnp.log(nf / max_exact) / math.log(MAX_DIST / max_exact) * (nb - max_exact)).astype(I32)
    large = jnp.minimum(large, nb - 1)
    return ret + jnp.where(n < max_exact, n, large)


def _rel_bias(table, q_len, k_len):
    rel = (jnp.arange(k_len, dtype=I32)[None, :] - WINDOW) - jnp.arange(q_len, dtype=I32)[:, None]
    b = table[_t5_bucket(rel)]
    return jnp.transpose(b, (2, 0, 1)).reshape(N_KV_A, GROUP_A * q_len, k_len).astype(F32)


def _rotary_tables(pos):
    half = DK_R // 2
    inv = ROPE_BASE ** (-jnp.arange(half, dtype=F32) / half)
    ang = pos.astype(F32)[:, None] * inv[None, :]
    cos, sin = jnp.cos(ang), jnp.sin(ang)
    return jnp.concatenate([cos, cos], -1), jnp.concatenate([-sin, sin], -1)


def _decay_tables(c):
    log_gamma = jnp.log1p(-jnp.exp2(-5.0 - jnp.arange(N_HEADS_R, dtype=F32)))
    idx = jnp.arange(c, dtype=F32)
    di = jnp.exp(jnp.abs(idx[:, None] - idx[None, :])[None] * log_gamma[:, None, None])
    qd = jnp.exp((idx[None, :] + 1.0) * log_gamma[:, None])
    kd = jnp.exp((c - 1.0 - idx[None, :]) * log_gamma[:, None])
    gc = jnp.exp(c * log_gamma)
    bc = lambda a: jnp.broadcast_to(a[:, :, None], (N_HEADS_R, c, DK_R))
    return di, bc(qd), bc(kd), jnp.broadcast_to(gc[:, None, None], (N_HEADS_R, DK_R, DK_R))


def _moe_ffn(x1, xp, st, p2d, wts):
    t = x1.shape[0]
    tile = min(256, t)
    e8, w8, r8, counts = _route(st, wts['router_bias'], tile)
    counts = counts[:, 0]
    padded = (counts + MOE_ROWS - 1) // MOE_ROWS * MOE_ROWS
    pad_end = jnp.cumsum(padded)
    pstart = (pad_end - padded).astype(I32)
    n_rows = t * TOP_K + N_EXPERTS * MOE_ROWS
    blk_start = jnp.arange(n_rows // MOE_ROWS, dtype=I32) * MOE_ROWS
    blk_exp = jnp.minimum(jnp.sum(blk_start[:, None] >= pad_end[None, :], axis=1), N_EXPERTS - 1).astype(I32)
    own = blk_exp[:, None] == jnp.arange(N_EXPERTS, dtype=I32)[None, :]
    blk_end = jnp.sum(jnp.where(own, (pstart + counts)[None, :], 0), axis=1)
    blk_valid = jnp.clip(blk_end - blk_start, 0, MOE_ROWS).astype(I32)
    d8 = _dest(e8, r8, pstart.astype(F32).reshape(N_EXPERTS, 1), tile)
    xs = _dispatch(d8, xp, n_rows, tile)
    ys = _moe(blk_exp, blk_valid, xs, wts['w_exp_gate'], wts['w_exp_up'], wts['w_exp_down'])
    return _final(d8, x1, w8.T, p2d, wts['w_sh_gate'], wts['w_sh_up'], wts['w_sh_down'],
                  wts['ln2_g'], wts['ln2_b'], wts['w_ple_proj'], wts['w_ple_gate'], ys, tile)


def _post_mixers(x2d, oa, orr, gm, p2d, wts):
    t = x2d.shape[0]
    x1, st, xp = _mix(x2d, oa, orr, gm, wts['w_branch_attn'], wts['w_branch_ret'], wts['w_out'],
                      wts['ln1_g'], wts['ln1_b'], wts['w_router_t'], min(512, t))
    return _moe_ffn(x1, xp, st, p2d, wts)


def kernel(x_prompt, x_sample, cache_attn_k, cache_attn_v, state_retention, p_prompt, p_sample,
           w_in, attn_sinks, rel_bias_table, ret_gn_gain, w_branch_attn, w_branch_ret, w_out,
           ln1_g, ln1_b, w_router, router_bias, w_exp_gate, w_exp_up, w_exp_down,
           w_sh_gate, w_sh_up, w_sh_down, ln2_g, ln2_b, w_ple_proj, w_ple_gate):
    b, s, _ = x_prompt.shape
    bd, l, _ = x_sample.shape
    i = 0
    row = lambda a: a.reshape(1, -1).astype(F32)
    wts = dict(
        w_branch_attn=w_branch_attn[i].astype(BF16), w_branch_ret=w_branch_ret[i].astype(BF16),
        w_out=w_out[i].astype(BF16), ln1_g=row(ln1_g[i]), ln1_b=row(ln1_b[i]),
        w_router_t=w_router[i].T.astype(BF16), router_bias=router_bias[i].reshape(N_EXPERTS, 1).astype(F32),
        w_exp_gate=w_exp_gate[i], w_exp_up=w_exp_up[i], w_exp_down=w_exp_down[i],
        w_sh_gate=w_sh_gate[i].astype(BF16), w_sh_up=w_sh_up[i].astype(BF16), w_sh_down=w_sh_down[i].astype(BF16),
        ln2_g=row(ln2_g[i]), ln2_b=row(ln2_b[i]),
        w_ple_proj=w_ple_proj[i].astype(BF16), w_ple_gate=w_ple_gate[i].astype(BF16))
    w_in_bf = w_in[i].astype(BF16)
    gain = row(ret_gn_gain[i])
    sinks = attn_sinks[i].astype(F32).reshape(N_KV_A, GROUP_A)

    xp = x_prompt.reshape(b * s, D_MODEL)
    cos_p, sin_p = _rotary_tables(jnp.arange(s, dtype=I32))
    qkv, kvf, qkr, vg, gm = _inproj(xp, w_in_bf, cos_p, sin_p, min(512, s))
    sink_p = jnp.repeat(sinks, CHUNK, axis=1)[..., None]
    oa = _attn_prompt(qkv, _rel_bias(rel_bias_table, CHUNK, WINDOW + CHUNK), sink_p, b, s)
    s0 = jnp.zeros((b, N_HEADS_R, DK_R, DK_R), F32)
    orr, ns_p = _retention(qkr, vg, s0, _decay_tables(CHUNK), gain, b, s, CHUNK)
    y_p = _post_mixers(xp, oa, orr, gm, p_prompt[i].reshape(b * s, D_PLE), wts).reshape(b, s, D_MODEL)
    kv_tail = kvf.reshape(b, s, 2, N_KV_A, HEAD_DIM_A)[:, s - WINDOW:]
    nk_p, nv_p = kv_tail[:, :, 0], kv_tail[:, :, 1]

    xs = x_sample.reshape(bd * l, D_MODEL)
    cos_s, sin_s = _rotary_tables(jnp.tile(PAST_LEN + jnp.arange(l, dtype=I32), bd))
    qkv, kvf, qkr, vg, gm = _inproj(xs, w_in_bf, cos_s, sin_s, bd * l)
    sink_s = jnp.repeat(sinks, l, axis=1)[..., None]
    oa, nk_s, nv_s = _attn_sample(qkv, kvf, cache_attn_k[i].reshape(bd, WINDOW, KA_W),
                                  cache_attn_v[i].reshape(bd, WINDOW, KA_W),
                                  _rel_bias(rel_bias_table, l, WINDOW + l), sink_s, bd, l)
    orr, ns_s = _retention(qkr, vg, state_retention[i].astype(F32), _decay_tables(l), gain, bd, l, l)
    y_s = _post_mixers(xs, oa, orr, gm, p_sample[i].reshape(bd * l, D_PLE), wts).reshape(bd, l, D_MODEL)
    shape_kv = (bd, WINDOW, N_KV_A, HEAD_DIM_A)
    return (y_p, y_s, nk_p[None], nv_p[None], ns_p[None],
            nk_s.reshape(shape_kv)[None], nv_s.reshape(shape_kv)[None], ns_s[None])
```

```python
import functools
import math

import jax
import jax.numpy as jnp
from jax import lax
from jax.experimental import pallas as pl
from jax.experimental.pallas import tpu as pltpu

F32 = jnp.float32
BF16 = jnp.bfloat16
I32 = jnp.int32

D_MODEL = 1024
CHUNK = 64
WINDOW = 128
N_HEADS_A = 8
N_KV_A = 2
GROUP_A = 4
HEAD_DIM_A = 64
N_BUCKETS = 32
MAX_DIST = 128
N_HEADS_R = 4
DK_R = 128
ROPE_BASE = 10000.0
N_EXPERTS = 256
TOP_K = 8
N_GROUPS = 8
GROUP_SIZE = N_EXPERTS // N_GROUPS
TOPK_GROUPS = 4
D_EXPERT = 256
ROUTED_SCALE = 2.5
D_PLE = 256
LN_EPS = 1e-5
NEG_INF = -1e30
PAST_LEN = 2048
DEPTH = 1
DN_ALPHA = (2 * DEPTH) ** 0.25

QA_W = N_HEADS_A * HEAD_DIM_A
KA_W = N_KV_A * HEAD_DIM_A
QR_W = N_HEADS_R * DK_R
OFF_KV = QA_W
OFF_QR = QA_W + 2 * KA_W
OFF_VR = OFF_QR + 2 * QR_W
OFF_GM = OFF_VR + 2 * QR_W
IN_W = OFF_GM + 2 * D_MODEL

MOE_ROWS = 512
VMEM_LIMIT = 56 * 1024 * 1024


def _cparams(sem, vmem=VMEM_LIMIT):
    return pltpu.CompilerParams(dimension_semantics=sem, vmem_limit_bytes=vmem)


def _layer_norm(h, g, b):
    mu = jnp.mean(h, axis=-1, keepdims=True)
    d = h - mu
    var = jnp.mean(d * d, axis=-1, keepdims=True)
    return d * lax.rsqrt(var + LN_EPS) * g + b


def _silu(x):
    return x * jax.nn.sigmoid(x)


U32 = jnp.uint32
PACK_W = D_MODEL // 2
PACK_SUB = PACK_W // 128


def _bf16_bits(x):
    return pltpu.bitcast(x.astype(BF16).astype(F32), U32)


def _store_packed(ref, x):
    n = x.shape[0]
    words = _bf16_bits(x[:, :PACK_W]) | (_bf16_bits(x[:, PACK_W:]) >> 16)
    for c in range(PACK_SUB):
        ref[pl.ds(c, n, stride=PACK_SUB), :] = words[:, c * 128:(c + 1) * 128]


def _load_packed(ref, n):
    words = jnp.concatenate([ref[pl.ds(c, n, stride=PACK_SUB), :] for c in range(PACK_SUB)], axis=1)
    hi = pltpu.bitcast(words & jnp.uint32(0xFFFF0000), F32)
    lo = pltpu.bitcast(words << 16, F32)
    return hi, lo


def _inproj_kernel(x_ref, w_ref, cos_ref, sin_ref, qkv_ref, kvf_ref, qkr_ref, vg_ref, gm_ref):
    xb = x_ref[...].astype(BF16)

    def mm(lo, hi):
        return jnp.dot(xb, w_ref[:, lo:hi], preferred_element_type=F32)

    qkv_ref[:, 0:QA_W] = mm(0, QA_W).astype(BF16)
    kv = mm(OFF_KV, OFF_QR)
    kvf_ref[...] = kv
    qkv_ref[:, OFF_KV:OFF_QR] = kv.astype(BF16)
    cos = cos_ref[...]
    sin = sin_ref[...]
    for part in range(2):
        z = mm(OFF_QR + part * QR_W, OFF_QR + (part + 1) * QR_W)
        for h in range(N_HEADS_R):
            zh = z[:, h * DK_R:(h + 1) * DK_R]
            r = zh * cos + pltpu.roll(zh, DK_R // 2, axis=1) * sin
            if part == 1:
                r = r * (DK_R ** -0.5)
            c0 = part * QR_W + h * DK_R
            qkr_ref[:, c0:c0 + DK_R] = r.astype(BF16)
    vg_ref[...] = mm(OFF_VR, OFF_GM).astype(BF16)
    for half in range(2):
        lo = OFF_GM + half * D_MODEL
        gm_ref[:, half * D_MODEL:(half + 1) * D_MODEL] = jax.nn.sigmoid(mm(lo, lo + D_MODEL)).astype(BF16)


def _inproj(x2d, w_bf, cos_tab, sin_tab, tm):
    t = x2d.shape[0]
    nper = cos_tab.shape[0] // tm
    row = lambda i: (i, 0)
    return pl.pallas_call(
        _inproj_kernel,
        grid=(t // tm,),
        in_specs=[
            pl.BlockSpec((tm, D_MODEL), row),
            pl.BlockSpec((D_MODEL, IN_W), lambda i: (0, 0)),
            pl.BlockSpec((tm, DK_R), lambda i: (i % nper, 0)),
            pl.BlockSpec((tm, DK_R), lambda i: (i % nper, 0)),
        ],
        out_specs=[
            pl.BlockSpec((tm, OFF_QR), row),
            pl.BlockSpec((tm, 2 * KA_W), row),
            pl.BlockSpec((tm, 2 * QR_W), row),
            pl.BlockSpec((tm, 2 * QR_W), row),
            pl.BlockSpec((tm, 2 * D_MODEL), row),
        ],
        out_shape=[
            jax.ShapeDtypeStruct((t, OFF_QR), BF16),
            jax.ShapeDtypeStruct((t, 2 * KA_W), F32),
            jax.ShapeDtypeStruct((t, 2 * QR_W), BF16),
            jax.ShapeDtypeStruct((t, 2 * QR_W), BF16),
            jax.ShapeDtypeStruct((t, 2 * D_MODEL), BF16),
        ],
        compiler_params=_cparams(("parallel",)),
        name="inproj",
    )(x2d, w_bf, cos_tab, sin_tab)


def _attend(q4, k, v, bias, sink, thresh):
    s = lax.dot_general(q4, k, (((1,), (1,)), ((), ())), preferred_element_type=F32)
    s = s * (HEAD_DIM_A ** -0.5) + bias
    if thresh is not None:
        col = lax.broadcasted_iota(I32, s.shape, 1)
        s = jnp.where(col >= thresh, s, NEG_INF)
    m = jnp.maximum(jnp.max(s, axis=-1, keepdims=True), sink)
    e = jnp.exp(s - m)
    p = e / (jnp.sum(e, axis=-1, keepdims=True) + jnp.exp(sink - m))
    return jnp.dot(p.astype(BF16), v, preferred_element_type=F32)


def _attn_prompt_kernel(q_ref, kvc_ref, kvp_ref, bias_ref, sink_ref, o_ref, kv_buf, *, n_chunks):
    i = pl.program_id(1)
    kv_buf[0:WINDOW, :] = kvp_ref[...]
    kv_buf[WINDOW:, :] = kvc_ref[...]

    def chunk(j, carry):
        r0 = pl.multiple_of(j * CHUNK, CHUNK)
        thresh = jnp.maximum(WINDOW - (i * n_chunks + j) * CHUNK, 0)
        for kv in range(N_KV_A):
            q4 = jnp.concatenate(
                [q_ref[pl.ds(r0, CHUNK), (kv * GROUP_A + g) * HEAD_DIM_A:(kv * GROUP_A + g + 1) * HEAD_DIM_A]
                 for g in range(GROUP_A)], axis=0)
            k = kv_buf[pl.ds(r0, WINDOW + CHUNK), kv * HEAD_DIM_A:(kv + 1) * HEAD_DIM_A]
            v = kv_buf[pl.ds(r0, WINDOW + CHUNK), KA_W + kv * HEAD_DIM_A:KA_W + (kv + 1) * HEAD_DIM_A]
            o4 = _attend(q4, k, v, bias_ref[kv], sink_ref[kv], thresh)
            for g in range(GROUP_A):
                c0 = (kv * GROUP_A + g) * HEAD_DIM_A
                o_ref[pl.ds(r0, CHUNK), c0:c0 + HEAD_DIM_A] = o4[g * CHUNK:(g + 1) * CHUNK].astype(BF16)
        return carry

    lax.fori_loop(0, n_chunks, chunk, 0)


def _attn_prompt(qkv, bias, sink, b, s):
    qb = min(512, s)
    n_chunks = qb // CHUNK
    nq = s // qb
    per = qb // WINDOW
    kv_col = OFF_KV // (2 * KA_W)
    return pl.pallas_call(
        functools.partial(_attn_prompt_kernel, n_chunks=n_chunks),
        grid=(b, nq),
        in_specs=[
            pl.BlockSpec((qb, QA_W), lambda bi, i: (bi * nq + i, 0)),
            pl.BlockSpec((qb, 2 * KA_W), lambda bi, i: (bi * nq + i, kv_col)),
            pl.BlockSpec((WINDOW, 2 * KA_W), lambda bi, i: (jnp.maximum((bi * nq + i) * per - 1, 0), kv_col)),
            pl.BlockSpec((N_KV_A, GROUP_A * CHUNK, WINDOW + CHUNK), lambda bi, i: (0, 0, 0)),
            pl.BlockSpec((N_KV_A, GROUP_A * CHUNK, 1), lambda bi, i: (0, 0, 0)),
        ],
        out_specs=pl.BlockSpec((qb, QA_W), lambda bi, i: (bi * nq + i, 0)),
        out_shape=jax.ShapeDtypeStruct((b * s, QA_W), BF16),
        scratch_shapes=[pltpu.VMEM((WINDOW + qb, 2 * KA_W), BF16)],
        compiler_params=_cparams(("parallel", "parallel")),
        name="attn_prompt",
    )(qkv, qkv, qkv, bias, sink)


def _attn_sample_kernel(q_ref, kvf_ref, ck_ref, cv_ref, bias_ref, sink_ref, o_ref, nk_ref, nv_ref, *, l):
    k_all = jnp.concatenate([ck_ref[0], kvf_ref[:, 0:KA_W]], axis=0)
    v_all = jnp.concatenate([cv_ref[0], kvf_ref[:, KA_W:2 * KA_W]], axis=0)
    nk_ref[0] = k_all[l:]
    nv_ref[0] = v_all[l:]
    kb = k_all.astype(BF16)
    vb = v_all.astype(BF16)
    for kv in range(N_KV_A):
        q4 = jnp.concatenate(
            [q_ref[:, (kv * GROUP_A + g) * HEAD_DIM_A:(kv * GROUP_A + g + 1) * HEAD_DIM_A] for g in range(GROUP_A)],
            axis=0)
        o4 = _attend(q4, kb[:, kv * HEAD_DIM_A:(kv + 1) * HEAD_DIM_A], vb[:, kv * HEAD_DIM_A:(kv + 1) * HEAD_DIM_A],
                     bias_ref[kv], sink_ref[kv], None)
        for g in range(GROUP_A):
            c0 = (kv * GROUP_A + g) * HEAD_DIM_A
            o_ref[:, c0:c0 + HEAD_DIM_A] = o4[g * l:(g + 1) * l].astype(BF16)


def _attn_sample(qkv, kvf, cache_k, cache_v, bias, sink, bd, l):
    cache_spec = pl.BlockSpec((1, WINDOW, KA_W), lambda bi: (bi, 0, 0))
    cache_shape = jax.ShapeDtypeStruct((bd, WINDOW, KA_W), F32)
    return pl.pallas_call(
        functools.partial(_attn_sample_kernel, l=l),
        grid=(bd,),
        in_specs=[
            pl.BlockSpec((l, OFF_QR), lambda bi: (bi, 0)),
            pl.BlockSpec((l, 2 * KA_W), lambda bi: (bi, 0)),
            cache_spec, cache_spec,
            pl.BlockSpec((N_KV_A, GROUP_A * l, WINDOW + l), lambda bi: (0, 0, 0)),
            pl.BlockSpec((N_KV_A, GROUP_A * l, 1), lambda bi: (0, 0, 0)),
        ],
        out_specs=[pl.BlockSpec((l, QA_W), lambda bi: (bi, 0)), cache_spec, cache_spec],
        out_shape=[jax.ShapeDtypeStruct((bd * l, QA_W), BF16), cache_shape, cache_shape],
        compiler_params=_cparams(("parallel",)),
        name="attn_sample",
    )(qkv, kvf, cache_k, cache_v, bias, sink)


def _retention_kernel(q_ref, k_ref, v_ref, g_ref, s0_ref, di_ref, qd_ref, kd_ref, gc_ref, gain_ref,
                      o_ref, s_ref, *, n_chunks, c):
    di = di_ref[0]
    qd = qd_ref[0]
    kd = kd_ref[0]
    gc = gc_ref[0]
    gain = gain_ref[...]

    def chunk(n, state):
        r0 = pl.multiple_of(n * c, c)
        qc = q_ref[pl.ds(r0, c), :]
        kc = k_ref[pl.ds(r0, c), :]
        vc = v_ref[pl.ds(r0, c), :]
        sc = lax.dot_general(qc, kc, (((1,), (1,)), ((), ())), preferred_element_type=F32) * di
        o = jnp.dot(sc.astype(BF16), vc, preferred_element_type=F32)
        q_dec = (qc.astype(F32) * qd).astype(BF16)
        o = o + jnp.dot(q_dec, state.astype(BF16), preferred_element_type=F32)
        k_dec = (kc.astype(F32) * kd).astype(BF16)
        state = gc * state + lax.dot_general(k_dec, vc, (((0,), (0,)), ((), ())), preferred_element_type=F32)
        mu = jnp.mean(o, axis=-1, keepdims=True)
        d = o - mu
        var = jnp.mean(d * d, axis=-1, keepdims=True)
        y = d * lax.rsqrt(var + LN_EPS) * gain
        o_ref[pl.ds(r0, c), :] = (y * _silu(g_ref[pl.ds(r0, c), :].astype(F32))).astype(BF16)
        return state

    s_ref[0, 0] = lax.fori_loop(0, n_chunks, chunk, s0_ref[0, 0])


def _retention(qkr, vg, s0, tabs, gain, b, s, c):
    di, qd, kd, gc = tabs
    seq = lambda off: pl.BlockSpec((s, DK_R), lambda bi, h: (bi, off + h))
    head3 = lambda shape: pl.BlockSpec((1,) + shape, lambda bi, h: (h, 0, 0))
    st_spec = pl.BlockSpec((1, 1, DK_R, DK_R), lambda bi, h: (bi, h, 0, 0))
    return pl.pallas_call(
        functools.partial(_retention_kernel, n_chunks=s // c, c=c),
        grid=(b, N_HEADS_R),
        in_specs=[seq(0), seq(N_HEADS_R), seq(0), seq(N_HEADS_R), st_spec,
                  head3((c, c)), head3((c, DK_R)), head3((c, DK_R)), head3((DK_R, DK_R)),
                  pl.BlockSpec((1, DK_R), lambda bi, h: (0, h))],
        out_specs=[pl.BlockSpec((s, DK_R), lambda bi, h: (bi, h)), st_spec],
        out_shape=[jax.ShapeDtypeStruct((b * s, QR_W), BF16), jax.ShapeDtypeStruct((b, N_HEADS_R, DK_R, DK_R), F32)],
        compiler_params=_cparams(("parallel", "parallel")),
        name="retention",
    )(qkr, qkr, vg, vg, s0, di, qd, kd, gc, gain)


def _mix_kernel(x_ref, oa_ref, or_ref, g_ref, wa_ref, wb_ref, wo_ref, lg_ref, lb_ref, wrt_ref, x1_ref, st_ref, xp_ref):
    a = jnp.dot(oa_ref[...], wa_ref[...], preferred_element_type=F32)
    b = jnp.dot(or_ref[...], wb_ref[...], preferred_element_type=F32)
    merged = g_ref[:, 0:D_MODEL].astype(F32) * a + g_ref[:, D_MODEL:].astype(F32) * b
    y = jnp.dot(merged.astype(BF16), wo_ref[...], preferred_element_type=F32)
    x1 = _layer_norm(DN_ALPHA * x_ref[...] + y, lg_ref[...], lb_ref[...])
    x1_ref[...] = x1
    _store_packed(xp_ref, x1)
    logits = lax.dot_general(wrt_ref[...], x1.astype(BF16), (((1,), (1,)), ((), ())), preferred_element_type=F32)
    st_ref[...] = jax.nn.sigmoid(logits)


def _mix(x2d, oa, orr, gm, wa, wb, wo, lg, lb, wrt, tm):
    t = x2d.shape[0]
    row = lambda w: pl.BlockSpec((tm, w), lambda i: (i, 0))
    full = lambda a: pl.BlockSpec(a.shape, lambda i: (0,) * a.ndim)
    return pl.pallas_call(
        _mix_kernel,
        grid=(t // tm,),
        in_specs=[row(D_MODEL), row(QA_W), row(QR_W), row(2 * D_MODEL),
                  full(wa), full(wb), full(wo), full(lg), full(lb), full(wrt)],
        out_specs=[row(D_MODEL), pl.BlockSpec((N_EXPERTS, tm), lambda i: (0, i)),
                   pl.BlockSpec((tm * PACK_SUB, 128), lambda i: (i, 0))],
        out_shape=[jax.ShapeDtypeStruct((t, D_MODEL), F32), jax.ShapeDtypeStruct((N_EXPERTS, t), F32),
                   jax.ShapeDtypeStruct((t * PACK_SUB, 128), U32)],
        compiler_params=_cparams(("parallel",)),
        name="mix",
    )(x2d, oa, orr, gm, wa, wb, wo, lg, lb, wrt)


def _route_kernel(s_ref, bias_ref, e_ref, w_ref, r_ref, cnt_ref, carry_ref):
    @pl.when(pl.program_id(0) == 0)
    def _():
        carry_ref[...] = jnp.zeros_like(carry_ref)

    s = s_ref[...]
    tl = s.shape[1]
    choice = s + bias_ref[...]
    row = lax.broadcasted_iota(I32, (N_EXPERTS, tl), 0)
    row_g = lax.broadcasted_iota(I32, (GROUP_SIZE, tl), 0)
    neg = -jnp.inf

    scores = []
    for g in range(N_GROUPS):
        blk = choice[g * GROUP_SIZE:(g + 1) * GROUP_SIZE]
        m1 = jnp.max(blk, axis=0, keepdims=True)
        i1 = jnp.min(jnp.where(blk == m1, row_g, GROUP_SIZE), axis=0, keepdims=True)
        m2 = jnp.max(jnp.where(row_g == i1, neg, blk), axis=0, keepdims=True)
        scores.append(m1 + m2)
    sc = jnp.concatenate(scores, axis=0)
    gi = lax.broadcasted_iota(I32, sc.shape, 0)
    grank = jnp.zeros(sc.shape, I32)
    for g in range(N_GROUPS):
        other = sc[g:g + 1]
        ahead = jnp.where(other > sc, 1, jnp.where(other == sc, jnp.where(gi > g, 1, 0), 0))
        grank = grank + ahead
    cm = jnp.concatenate(
        [jnp.where(grank[g:g + 1] < TOPK_GROUPS, choice[g * GROUP_SIZE:(g + 1) * GROUP_SIZE], neg)
         for g in range(N_GROUPS)], axis=0)

    experts, weights = [], []
    for _ in range(TOP_K):
        m = jnp.max(cm, axis=0, keepdims=True)
        idx = jnp.min(jnp.where(cm == m, row, N_EXPERTS), axis=0, keepdims=True)
        hit = row == idx
        weights.append(jnp.sum(jnp.where(hit, s, 0.0), axis=0, keepdims=True))
        cm = jnp.where(hit, neg, cm)
        experts.append(idx)
    e8 = jnp.concatenate(experts, axis=0)
    w8 = jnp.concatenate(weights, axis=0)
    e_ref[...] = e8
    w_ref[...] = w8 / jnp.sum(w8, axis=0, keepdims=True) * ROUTED_SCALE

    before = (lax.broadcasted_iota(I32, (tl, tl), 0) < lax.broadcasted_iota(I32, (tl, tl), 1))
    before = jnp.where(before, 1.0, 0.0).astype(BF16)
    carry = carry_ref[...]
    ranks = []
    for k in range(TOP_K):
        hit = row == experts[k]
        onehot = jnp.where(hit, 1.0, 0.0)
        prefix = jnp.dot(onehot.astype(BF16), before, preferred_element_type=F32)
        ranks.append(jnp.sum(jnp.where(hit, prefix + carry, 0.0), axis=0, keepdims=True))
        carry = carry + jnp.sum(onehot, axis=1, keepdims=True)
    r_ref[...] = jnp.concatenate(ranks, axis=0).astype(I32)
    carry_ref[...] = carry
    cnt_ref[...] = carry.astype(I32)


def _route(st, bias_col, tl):
    t = st.shape[1]
    tok = pl.BlockSpec((TOP_K, tl), lambda i: (0, i))
    return pl.pallas_call(
        _route_kernel,
        grid=(t // tl,),
        in_specs=[pl.BlockSpec((N_EXPERTS, tl), lambda i: (0, i)), pl.BlockSpec((N_EXPERTS, 1), lambda i: (0, 0))],
        out_specs=[tok, tok, tok, pl.BlockSpec((N_EXPERTS, 1), lambda i: (0, 0))],
        out_shape=[jax.ShapeDtypeStruct((TOP_K, t), I32), jax.ShapeDtypeStruct((TOP_K, t), F32),
                   jax.ShapeDtypeStruct((TOP_K, t), I32), jax.ShapeDtypeStruct((N_EXPERTS, 1), I32)],
        scratch_shapes=[pltpu.VMEM((N_EXPERTS, 1), F32)],
        compiler_params=_cparams(("arbitrary",)),
        name="route",
    )(st, bias_col)


def _dest_kernel(e_ref, r_ref, ps_ref, d_ref):
    tl = e_ref.shape[1]
    row = lax.broadcasted_iota(I32, (N_EXPERTS, tl), 0)
    ps = ps_ref[...]
    starts = [jnp.sum(jnp.where(row == e_ref[k:k + 1, :], ps, 0.0), axis=0, keepdims=True) for k in range(TOP_K)]
    d_ref[...] = jnp.concatenate(starts, axis=0).astype(I32) + r_ref[...]


def _dest(e8, r8, pstart_col, tl):
    t = e8.shape[1]
    tok = pl.BlockSpec((TOP_K, tl), lambda i: (0, i))
    return pl.pallas_call(
        _dest_kernel,
        grid=(t // tl,),
        in_specs=[tok, tok, pl.BlockSpec((N_EXPERTS, 1), lambda i: (0, 0))],
        out_specs=tok,
        out_shape=jax.ShapeDtypeStruct((TOP_K, t), I32),
        compiler_params=_cparams(("parallel",)),
        name="dest",
    )(e8, r8, pstart_col)


TOKEN_GROUP = 8


def _row_copy(src, src_tok, dst, dst_tok, sem):
    s0 = pl.multiple_of(src_tok * PACK_SUB, PACK_SUB)
    d0 = pl.multiple_of(dst_tok * PACK_SUB, PACK_SUB)
    return pltpu.make_async_copy(src.at[pl.ds(s0, PACK_SUB)], dst.at[pl.ds(d0, PACK_SUB)], sem)


def _wait_rows(src, dst, sem, n_tokens):
    def wait(t, carry):
        for k in range(TOP_K):
            _row_copy(src, 0, dst, 0, sem).wait()
        return carry

    lax.fori_loop(0, n_tokens, wait, 0)


def _dispatch_kernel(d_ref, x_ref, xs_hbm, sem, *, td):
    def start(g, carry):
        for j in range(TOKEN_GROUP):
            t = g * TOKEN_GROUP + j
            for k in range(TOP_K):
                _row_copy(x_ref, t, xs_hbm, d_ref[t * TOP_K + k], sem).start(priority=k % 2)
        return carry

    lax.fori_loop(0, td // TOKEN_GROUP, start, 0)
    _wait_rows(x_ref, xs_hbm, sem, td)


def _dispatch(d8, xp, n_rows, td):
    t = d8.shape[0] // TOP_K
    return pl.pallas_call(
        functools.partial(_dispatch_kernel, td=td),
        grid=(t // td,),
        in_specs=[pl.BlockSpec((td * TOP_K,), lambda i: (i,), memory_space=pltpu.SMEM),
                  pl.BlockSpec((td * PACK_SUB, 128), lambda i: (i, 0))],
        out_specs=pl.BlockSpec(memory_space=pl.ANY),
        scratch_shapes=[pltpu.SemaphoreType.DMA(())],
        out_shape=jax.ShapeDtypeStruct((n_rows * PACK_SUB, 128), U32),
        compiler_params=_cparams(("arbitrary",)),
        name="dispatch",
    )(d8, xp)


def _moe_kernel(be_ref, nv_ref, x_ref, wg_ref, wu_ref, wd_ref, y_ref, wg_bf, wu_bf, wd_bf):
    j = pl.program_id(0)
    nv = nv_ref[j]

    @pl.when((j == 0) | (be_ref[j] != be_ref[jnp.maximum(j - 1, 0)]))
    def _():
        wg_bf[...] = wg_ref[0].astype(BF16)
        wu_bf[...] = wu_ref[0].astype(BF16)
        wd_bf[...] = wd_ref[0].astype(BF16)

    @pl.when(nv > 0)
    def _():
        hi, lo = _load_packed(x_ref, MOE_ROWS)
        valid = lax.broadcasted_iota(I32, hi.shape, 0) < nv
        xb = jnp.concatenate([jnp.where(valid, hi, 0.0).astype(BF16), jnp.where(valid, lo, 0.0).astype(BF16)], axis=1)
        g = jnp.dot(xb, wg_bf[...], preferred_element_type=F32)
        u = jnp.dot(xb, wu_bf[...], preferred_element_type=F32)
        h = (_silu(g) * u).astype(BF16)
        _store_packed(y_ref, jnp.dot(h, wd_bf[...], preferred_element_type=F32))

    @pl.when(nv == 0)
    def _():
        y_ref[...] = jnp.zeros_like(y_ref)


def _moe(blk_exp, blk_valid, xs, wg, wu, wd):
    n_blocks = blk_exp.shape[0]
    rows = pl.BlockSpec((MOE_ROWS * PACK_SUB, 128), lambda j, be, nv: (j, 0))
    return pl.pallas_call(
        _moe_kernel,
        grid_spec=pltpu.PrefetchScalarGridSpec(
            num_scalar_prefetch=2,
            grid=(n_blocks,),
            in_specs=[rows,
                      pl.BlockSpec((1, D_MODEL, D_EXPERT), lambda j, be, nv: (be[j], 0, 0)),
                      pl.BlockSpec((1, D_MODEL, D_EXPERT), lambda j, be, nv: (be[j], 0, 0)),
                      pl.BlockSpec((1, D_EXPERT, D_MODEL), lambda j, be, nv: (be[j], 0, 0))],
            out_specs=rows,
            scratch_shapes=[pltpu.VMEM((D_MODEL, D_EXPERT), BF16), pltpu.VMEM((D_MODEL, D_EXPERT), BF16),
                            pltpu.VMEM((D_EXPERT, D_MODEL), BF16)],
        ),
        out_shape=jax.ShapeDtypeStruct(xs.shape, U32),
        compiler_params=_cparams(("arbitrary",)),
        name="moe",
    )(blk_exp, blk_valid, xs, wg, wu, wd)


def _final_kernel(d_ref, x1_ref, w_ref, p_ref, wsg_ref, wsu_ref, wsd_ref, lg_ref, lb_ref,
                  wpp_ref, wpg_ref, ys_hbm, out_ref, gbuf, sem, *, tf):
    def start(g, carry):
        for j in range(TOKEN_GROUP):
            t = g * TOKEN_GROUP + j
            for k in range(TOP_K):
                _row_copy(ys_hbm, d_ref[t * TOP_K + k], gbuf.at[k], t, sem).start(priority=k % 2)
        return carry

    lax.fori_loop(0, tf // TOKEN_GROUP, start, 0)

    x1 = x1_ref[...]
    xb = x1.astype(BF16)
    hs = _silu(jnp.dot(xb, wsg_ref[...], preferred_element_type=F32)) * jnp.dot(xb, wsu_ref[...],
                                                                               preferred_element_type=F32)
    y = jnp.dot(hs.astype(BF16), wsd_ref[...], preferred_element_type=F32)

    _wait_rows(ys_hbm, gbuf.at[0], sem, tf)

    y_hi = jnp.zeros((tf, PACK_W), F32)
    y_lo = jnp.zeros((tf, PACK_W), F32)
    for k in range(TOP_K):
        hi, lo = _load_packed(gbuf.at[k], tf)
        wk = w_ref[:, k:k + 1]
        y_hi = y_hi + wk * hi
        y_lo = y_lo + wk * lo
    y = y + jnp.concatenate([y_hi, y_lo], axis=1)
    x2 = _layer_norm(DN_ALPHA * x1 + y, lg_ref[...], lb_ref[...])
    gate = jax.nn.sigmoid(jnp.dot(x2.astype(BF16), wpg_ref[...], preferred_element_type=F32))
    out_ref[...] = x2 + gate * jnp.dot(p_ref[...].astype(BF16), wpp_ref[...], preferred_element_type=F32)


def _final(d8, x1, w_tok, p2d, wsg, wsu, wsd, lg, lb, wpp, wpg, ys, tf):
    t = x1.shape[0]
    row = lambda w: pl.BlockSpec((tf, w), lambda i: (i, 0))
    full = lambda a: pl.BlockSpec(a.shape, lambda i: (0,) * a.ndim)
    return pl.pallas_call(
        functools.partial(_final_kernel, tf=tf),
        grid=(t // tf,),
        in_specs=[pl.BlockSpec((tf * TOP_K,), lambda i: (i,), memory_space=pltpu.SMEM),
                  row(D_MODEL), row(TOP_K), row(D_PLE),
                  full(wsg), full(wsu), full(wsd), full(lg), full(lb), full(wpp), full(wpg),
                  pl.BlockSpec(memory_space=pl.ANY)],
        out_specs=row(D_MODEL),
        scratch_shapes=[pltpu.VMEM((TOP_K, tf * PACK_SUB, 128), U32), pltpu.SemaphoreType.DMA(())],
        out_shape=jax.ShapeDtypeStruct((t, D_MODEL), F32),
        compiler_params=_cparams(("arbitrary",)),
        name="final",
    )(d8, x1, w_tok, p2d, wsg, wsu, wsd, lg, lb, wpp, wpg, ys)


def _t5_bucket(rel):
    nb = N_BUCKETS // 2
    max_exact = nb // 2
    ret = jnp.where(rel > 0, nb, 0)
    n = jnp.abs(rel)
    nf = jnp.maximum(n, 1).astype(F32)
    large = max_exact + (jnp.log(nf / max_exact) / math.log(MAX_DIST / max_exact) * (nb - max_exact)).astype(I32)
    large = jnp.minimum(large, nb - 1)
    return ret + jnp.where(n < max_exact, n, large)


def _rel_bias(table, q_len, k_len):
    rel = (jnp.arange(k_len, dtype=I32)[None, :] - WINDOW) - jnp.arange(q_len, dtype=I32)[:, None]
    b = table[_t5_bucket(rel)]
    return jnp.transpose(b, (2, 0, 1)).reshape(N_KV_A, GROUP_A * q_len, k_len).astype(F32)


def _rotary_tables(pos):
    half = DK_R // 2
    inv = ROPE_BASE ** (-jnp.arange(half, dtype=F32) / half)
    ang = pos.astype(F32)[:, None] * inv[None, :]
    cos, sin = jnp.cos(ang), jnp.sin(ang)
    return jnp.concatenate([cos, cos], -1), jnp.concatenate([-sin, sin], -1)


def _decay_tables(c):
    log_gamma = jnp.log1p(-jnp.exp2(-5.0 - jnp.arange(N_HEADS_R, dtype=F32)))
    idx = jnp.arange(c, dtype=F32)
    di = jnp.exp(jnp.abs(idx[:, None] - idx[None, :])[None] * log_gamma[:, None, None])
    qd = jnp.exp((idx[None, :] + 1.0) * log_gamma[:, None])
    kd = jnp.exp((c - 1.0 - idx[None, :]) * log_gamma[:, None])
    gc = jnp.exp(c * log_gamma)
    bc = lambda a: jnp.broadcast_to(a[:, :, None], (N_HEADS_R, c, DK_R))
    return di, bc(qd), bc(kd), jnp.broadcast_to(gc[:, None, None], (N_HEADS_R, DK_R, DK_R))


def _moe_ffn(x1, xp, st, p2d, wts):
    t = x1.shape[0]
    tile = min(256, t)
    e8, w8, r8, counts = _route(st, wts['router_bias'], tile)
    counts = counts[:, 0]
    padded = (counts + MOE_ROWS - 1) // MOE_ROWS * MOE_ROWS
    pad_end = jnp.cumsum(padded)
    pstart = (pad_end - padded).astype(I32)
    n_rows = t * TOP_K + N_EXPERTS * MOE_ROWS
    blk_start = jnp.arange(n_rows // MOE_ROWS, dtype=I32) * MOE_ROWS
    blk_exp = jnp.minimum(jnp.sum(blk_start[:, None] >= pad_end[None, :], axis=1), N_EXPERTS - 1).astype(I32)
    own = blk_exp[:, None] == jnp.arange(N_EXPERTS, dtype=I32)[None, :]
    blk_end = jnp.sum(jnp.where(own, (pstart + counts)[None, :], 0), axis=1)
    blk_valid = jnp.clip(blk_end - blk_start, 0, MOE_ROWS).astype(I32)
    d8 = _dest(e8, r8, pstart.astype(F32).reshape(N_EXPERTS, 1), tile).T.reshape(-1)
    xs = _dispatch(d8, xp, n_rows, tile)
    ys = _moe(blk_exp, blk_valid, xs, wts['w_exp_gate'], wts['w_exp_up'], wts['w_exp_down'])
    return _final(d8, x1, w8.T, p2d, wts['w_sh_gate'], wts['w_sh_up'], wts['w_sh_down'],
                  wts['ln2_g'], wts['ln2_b'], wts['w_ple_proj'], wts['w_ple_gate'], ys, tile)


def _post_mixers(x2d, oa, orr, gm, p2d, wts):
    t = x2d.shape[0]
    x1, st, xp = _mix(x2d, oa, orr, gm, wts['w_branch_attn'], wts['w_branch_ret'], wts['w_out'],
                      wts['ln1_g'], wts['ln1_b'], wts['w_router_t'], min(512, t))
    return _moe_ffn(x1, xp, st, p2d, wts)


def kernel(x_prompt, x_sample, cache_attn_k, cache_attn_v, state_retention, p_prompt, p_sample,
           w_in, attn_sinks, rel_bias_table, ret_gn_gain, w_branch_attn, w_branch_ret, w_out,
           ln1_g, ln1_b, w_router, router_bias, w_exp_gate, w_exp_up, w_exp_down,
           w_sh_gate, w_sh_up, w_sh_down, ln2_g, ln2_b, w_ple_proj, w_ple_gate):
    b, s, _ = x_prompt.shape
    bd, l, _ = x_sample.shape
    i = 0
    row = lambda a: a.reshape(1, -1).astype(F32)
    wts = dict(
        w_branch_attn=w_branch_attn[i].astype(BF16), w_branch_ret=w_branch_ret[i].astype(BF16),
        w_out=w_out[i].astype(BF16), ln1_g=row(ln1_g[i]), ln1_b=row(ln1_b[i]),
        w_router_t=w_router[i].T.astype(BF16), router_bias=router_bias[i].reshape(N_EXPERTS, 1).astype(F32),
        w_exp_gate=w_exp_gate[i], w_exp_up=w_exp_up[i], w_exp_down=w_exp_down[i],
        w_sh_gate=w_sh_gate[i].astype(BF16), w_sh_up=w_sh_up[i].astype(BF16), w_sh_down=w_sh_down[i].astype(BF16),
        ln2_g=row(ln2_g[i]), ln2_b=row(ln2_b[i]),
        w_ple_proj=w_ple_proj[i].astype(BF16), w_ple_gate=w_ple_gate[i].astype(BF16))
    w_in_bf = w_in[i].astype(BF16)
    gain = row(ret_gn_gain[i])
    sinks = attn_sinks[i].astype(F32).reshape(N_KV_A, GROUP_A)

    xp = x_prompt.reshape(b * s, D_MODEL)
    cos_p, sin_p = _rotary_tables(jnp.arange(s, dtype=I32))
    qkv, kvf, qkr, vg, gm = _inproj(xp, w_in_bf, cos_p, sin_p, min(512, s))
    sink_p = jnp.repeat(sinks, CHUNK, axis=1)[..., None]
    oa = _attn_prompt(qkv, _rel_bias(rel_bias_table, CHUNK, WINDOW + CHUNK), sink_p, b, s)
    s0 = jnp.zeros((b, N_HEADS_R, DK_R, DK_R), F32)
    orr, ns_p = _retention(qkr, vg, s0, _decay_tables(CHUNK), gain, b, s, CHUNK)
    y_p = _post_mixers(xp, oa, orr, gm, p_prompt[i].reshape(b * s, D_PLE), wts).reshape(b, s, D_MODEL)
    kv_tail = kvf.reshape(b, s, 2, N_KV_A, HEAD_DIM_A)[:, s - WINDOW:]
    nk_p, nv_p = kv_tail[:, :, 0], kv_tail[:, :, 1]

    xs = x_sample.reshape(bd * l, D_MODEL)
    cos_s, sin_s = _rotary_tables(jnp.tile(PAST_LEN + jnp.arange(l, dtype=I32), bd))
    qkv, kvf, qkr, vg, gm = _inproj(xs, w_in_bf, cos_s, sin_s, bd * l)
    sink_s = jnp.repeat(sinks, l, axis=1)[..., None]
    oa, nk_s, nv_s = _attn_sample(qkv, kvf, cache_attn_k[i].reshape(bd, WINDOW, KA_W),
                                  cache_attn_v[i].reshape(bd, WINDOW, KA_W),
                                  _rel_bias(rel_bias_table, l, WINDOW + l), sink_s, bd, l)
    orr, ns_s = _retention(qkr, vg, state_retention[i].astype(F32), _decay_tables(l), gain, bd, l, l)
    y_s = _post_mixers(xs, oa, orr, gm, p_sample[i].reshape(bd * l, D_PLE), wts).reshape(bd, l, D_MODEL)
    shape_kv = (bd, WINDOW, N_KV_A, HEAD_DIM_A)
    return (y_p, y_s, nk_p[None], nv_p[None], ns_p[None],
            nk_s.reshape(shape_kv)[None], nv_s.reshape(shape_kv)[None], ns_s[None])
```

```python
import functools
import math

import jax
import jax.numpy as jnp
from jax import lax
from jax.experimental import pallas as pl
from jax.experimental.pallas import tpu as pltpu

F32 = jnp.float32
BF16 = jnp.bfloat16
I32 = jnp.int32

D_MODEL = 1024
CHUNK = 64
WINDOW = 128
N_HEADS_A = 8
N_KV_A = 2
GROUP_A = 4
HEAD_DIM_A = 64
N_BUCKETS = 32
MAX_DIST = 128
N_HEADS_R = 4
DK_R = 128
ROPE_BASE = 10000.0
N_EXPERTS = 256
TOP_K = 8
N_GROUPS = 8
GROUP_SIZE = N_EXPERTS // N_GROUPS
TOPK_GROUPS = 4
D_EXPERT = 256
ROUTED_SCALE = 2.5
D_PLE = 256
LN_EPS = 1e-5
NEG_INF = -1e30
PAST_LEN = 2048
DEPTH = 1
DN_ALPHA = (2 * DEPTH) ** 0.25

QA_W = N_HEADS_A * HEAD_DIM_A
KA_W = N_KV_A * HEAD_DIM_A
QR_W = N_HEADS_R * DK_R
OFF_KV = QA_W
OFF_QR = QA_W + 2 * KA_W
OFF_VR = OFF_QR + 2 * QR_W
OFF_GM = OFF_VR + 2 * QR_W
IN_W = OFF_GM + 2 * D_MODEL

HEAD_PAD = 128
PQ_W = N_HEADS_A * HEAD_PAD
PKV_W = 2 * N_KV_A * HEAD_PAD
A_KVF = PQ_W + PKV_W
A_QR = A_KVF + 2 * KA_W
A_VR = A_QR + 2 * QR_W
A_GM = A_VR + 2 * QR_W
A_W = A_GM + 2 * D_MODEL

MOE_ROWS = 512
VMEM_LIMIT = 56 * 1024 * 1024


def _cparams(sem, vmem=VMEM_LIMIT):
    return pltpu.CompilerParams(dimension_semantics=sem, vmem_limit_bytes=vmem)


def _layer_norm(h, g, b):
    mu = jnp.mean(h, axis=-1, keepdims=True)
    d = h - mu
    var = jnp.mean(d * d, axis=-1, keepdims=True)
    return d * lax.rsqrt(var + LN_EPS) * g + b


def _silu(x):
    return x * jax.nn.sigmoid(x)


U32 = jnp.uint32
PACK_W = D_MODEL // 2
PACK_SUB = PACK_W // 128


def _bf16_bits(x):
    return pltpu.bitcast(x.astype(BF16).astype(F32), U32)


def _store_packed(ref, x):
    n = x.shape[0]
    words = _bf16_bits(x[:, :PACK_W]) | (_bf16_bits(x[:, PACK_W:]) >> 16)
    for c in range(PACK_SUB):
        ref[pl.ds(c, n, stride=PACK_SUB), :] = words[:, c * 128:(c + 1) * 128]


def _load_packed(ref, n):
    words = jnp.concatenate([ref[pl.ds(c, n, stride=PACK_SUB), :] for c in range(PACK_SUB)], axis=1)
    hi = pltpu.bitcast(words & jnp.uint32(0xFFFF0000), F32)
    lo = pltpu.bitcast(words << 16, F32)
    return hi, lo


def _inproj_kernel(x_ref, w_ref, cos_ref, sin_ref, qkv_ref, kvf_ref, qkr_ref, vg_ref, gm_ref):
    xb = x_ref[...].astype(BF16)

    def mm(lo, hi):
        return jnp.dot(xb, w_ref[:, lo:hi], preferred_element_type=F32)

    qkv_ref[:, 0:PQ_W] = mm(0, PQ_W).astype(BF16)
    qkv_ref[:, PQ_W:A_KVF] = mm(PQ_W, A_KVF).astype(BF16)
    kvf_ref[...] = mm(A_KVF, A_QR)
    cos = cos_ref[...]
    sin = sin_ref[...]
    for part in range(2):
        z = mm(A_QR + part * QR_W, A_QR + (part + 1) * QR_W)
        for h in range(N_HEADS_R):
            zh = z[:, h * DK_R:(h + 1) * DK_R]
            r = zh * cos + pltpu.roll(zh, DK_R // 2, axis=1) * sin
            if part == 1:
                r = r * (DK_R ** -0.5)
            c0 = part * QR_W + h * DK_R
            qkr_ref[:, c0:c0 + DK_R] = r.astype(BF16)
    vg_ref[...] = mm(A_VR, A_GM).astype(BF16)
    for half in range(2):
        lo = A_GM + half * D_MODEL
        gm_ref[:, half * D_MODEL:(half + 1) * D_MODEL] = jax.nn.sigmoid(mm(lo, lo + D_MODEL)).astype(BF16)


def _inproj(x2d, w_bf, cos_tab, sin_tab, tm):
    t = x2d.shape[0]
    nper = cos_tab.shape[0] // tm
    row = lambda i: (i, 0)
    return pl.pallas_call(
        _inproj_kernel,
        grid=(t // tm,),
        in_specs=[
            pl.BlockSpec((tm, D_MODEL), row),
            pl.BlockSpec((D_MODEL, A_W), lambda i: (0, 0)),
            pl.BlockSpec((tm, DK_R), lambda i: (i % nper, 0)),
            pl.BlockSpec((tm, DK_R), lambda i: (i % nper, 0)),
        ],
        out_specs=[
            pl.BlockSpec((tm, A_KVF), row),
            pl.BlockSpec((tm, 2 * KA_W), row),
            pl.BlockSpec((tm, 2 * QR_W), row),
            pl.BlockSpec((tm, 2 * QR_W), row),
            pl.BlockSpec((tm, 2 * D_MODEL), row),
        ],
        out_shape=[
            jax.ShapeDtypeStruct((t, A_KVF), BF16),
            jax.ShapeDtypeStruct((t, 2 * KA_W), F32),
            jax.ShapeDtypeStruct((t, 2 * QR_W), BF16),
            jax.ShapeDtypeStruct((t, 2 * QR_W), BF16),
            jax.ShapeDtypeStruct((t, 2 * D_MODEL), BF16),
        ],
        compiler_params=_cparams(("parallel",)),
        name="inproj",
    )(x2d, w_bf, cos_tab, sin_tab)


def _attend(q4, k, v, bias, sink):
    s = lax.dot_general(q4, k, (((1,), (1,)), ((), ())), preferred_element_type=F32) + bias
    m = jnp.maximum(jnp.max(s, axis=-1, keepdims=True), sink)
    e = jnp.exp(s - m)
    p = e / (jnp.sum(e, axis=-1, keepdims=True) + jnp.exp(sink - m))
    return jnp.dot(p.astype(BF16), v, preferred_element_type=F32)


KEY_PAD = 256


def _attend_sink_column(q4, k, v, bias, fill, valid):
    s = lax.dot_general(q4, k, (((1,), (1,)), ((), ())), preferred_element_type=F32)
    s = jnp.where(valid, s + bias, fill)
    e = jnp.exp(s - jnp.max(s, axis=-1, keepdims=True)).astype(BF16)
    den = jnp.dot(e, jnp.ones((KEY_PAD, v.shape[1]), BF16), preferred_element_type=F32)
    return jnp.dot(e[:, 0:WINDOW + CHUNK], v, preferred_element_type=F32) / den


def _attn_prompt_kernel(q_ref, kvc_ref, kvp_ref, bias_ref, fill_ref, o_ref, kv_buf, *, n_chunks):
    i = pl.program_id(1)
    qb = n_chunks * CHUNK
    kv_buf[0:WINDOW, :] = kvp_ref[...]
    kv_buf[WINDOW:WINDOW + qb, :] = kvc_ref[...]
    kv_buf[WINDOW + qb:, :] = jnp.zeros((KEY_PAD - WINDOW - CHUNK, PKV_W), BF16)
    col = lax.broadcasted_iota(I32, (GROUP_A * CHUNK, KEY_PAD), 1)
    key_col = jnp.where(col < WINDOW + CHUNK, col, -1)

    def chunk(j, carry):
        r0 = pl.multiple_of(j * CHUNK, CHUNK)
        valid = key_col >= jnp.maximum(WINDOW - (i * n_chunks + j) * CHUNK, 0)
        for kv in range(N_KV_A):
            q4 = jnp.concatenate(
                [q_ref[pl.ds(r0, CHUNK), (kv * GROUP_A + g) * HEAD_PAD:(kv * GROUP_A + g + 1) * HEAD_PAD]
                 for g in range(GROUP_A)], axis=0)
            k = kv_buf[pl.ds(r0, KEY_PAD), kv * HEAD_PAD:(kv + 1) * HEAD_PAD]
            v = kv_buf[pl.ds(r0, WINDOW + CHUNK), (N_KV_A + kv) * HEAD_PAD:(N_KV_A + kv + 1) * HEAD_PAD]
            o4 = _attend_sink_column(q4, k, v, bias_ref[kv], fill_ref[kv], valid)
            for g in range(GROUP_A):
                c0 = (kv * GROUP_A + g) * HEAD_PAD
                o_ref[pl.ds(r0, CHUNK), c0:c0 + HEAD_PAD] = o4[g * CHUNK:(g + 1) * CHUNK].astype(BF16)
        return carry

    lax.fori_loop(0, n_chunks, chunk, 0, unroll=2 if n_chunks % 2 == 0 else 1)


def _attn_prompt(qkv, bias, fill, b, s):
    qb = min(512, s)
    n_chunks = qb // CHUNK
    nq = s // qb
    per = qb // WINDOW
    kv_col = PQ_W // PKV_W
    return pl.pallas_call(
        functools.partial(_attn_prompt_kernel, n_chunks=n_chunks),
        grid=(b, nq),
        in_specs=[
            pl.BlockSpec((qb, PQ_W), lambda bi, i: (bi * nq + i, 0)),
            pl.BlockSpec((qb, PKV_W), lambda bi, i: (bi * nq + i, kv_col)),
            pl.BlockSpec((WINDOW, PKV_W), lambda bi, i: (jnp.maximum((bi * nq + i) * per - 1, 0), kv_col)),
            pl.BlockSpec((N_KV_A, GROUP_A * CHUNK, KEY_PAD), lambda bi, i: (0, 0, 0)),
            pl.BlockSpec((N_KV_A, GROUP_A * CHUNK, KEY_PAD), lambda bi, i: (0, 0, 0)),
        ],
        out_specs=pl.BlockSpec((qb, PQ_W), lambda bi, i: (bi * nq + i, 0)),
        out_shape=jax.ShapeDtypeStruct((b * s, PQ_W), BF16),
        scratch_shapes=[pltpu.VMEM((qb + KEY_PAD - CHUNK, PKV_W), BF16)],
        compiler_params=_cparams(("parallel", "parallel")),
        name="attn_prompt",
    )(qkv, qkv, qkv, bias, fill)


def _attn_sample_kernel(q_ref, kvf_ref, ck_ref, cv_ref, bias_ref, sink_ref, o_ref, nk_ref, nv_ref, *, l):
    k_all = jnp.concatenate([ck_ref[0], kvf_ref[:, 0:KA_W]], axis=0)
    v_all = jnp.concatenate([cv_ref[0], kvf_ref[:, KA_W:2 * KA_W]], axis=0)
    nk_ref[0] = k_all[l:]
    nv_ref[0] = v_all[l:]
    kb = k_all.astype(BF16)
    vb = v_all.astype(BF16)
    o_ref[...] = jnp.zeros_like(o_ref)
    for kv in range(N_KV_A):
        q4 = jnp.concatenate(
            [q_ref[:, (kv * GROUP_A + g) * HEAD_PAD:(kv * GROUP_A + g) * HEAD_PAD + HEAD_DIM_A]
             for g in range(GROUP_A)], axis=0)
        o4 = _attend(q4, kb[:, kv * HEAD_DIM_A:(kv + 1) * HEAD_DIM_A], vb[:, kv * HEAD_DIM_A:(kv + 1) * HEAD_DIM_A],
                     bias_ref[kv], sink_ref[kv])
        for g in range(GROUP_A):
            c0 = (kv * GROUP_A + g) * HEAD_PAD
            o_ref[:, c0:c0 + HEAD_DIM_A] = o4[g * l:(g + 1) * l].astype(BF16)


def _attn_sample(qkv, kvf, cache_k, cache_v, bias, sink, bd, l):
    cache_spec = pl.BlockSpec((1, WINDOW, KA_W), lambda bi: (bi, 0, 0))
    cache_shape = jax.ShapeDtypeStruct((bd, WINDOW, KA_W), F32)
    return pl.pallas_call(
        functools.partial(_attn_sample_kernel, l=l),
        grid=(bd,),
        in_specs=[
            pl.BlockSpec((l, A_KVF), lambda bi: (bi, 0)),
            pl.BlockSpec((l, 2 * KA_W), lambda bi: (bi, 0)),
            cache_spec, cache_spec,
            pl.BlockSpec((N_KV_A, GROUP_A * l, WINDOW + l), lambda bi: (0, 0, 0)),
            pl.BlockSpec((N_KV_A, GROUP_A * l, 1), lambda bi: (0, 0, 0)),
        ],
        out_specs=[pl.BlockSpec((l, PQ_W), lambda bi: (bi, 0)), cache_spec, cache_spec],
        out_shape=[jax.ShapeDtypeStruct((bd * l, PQ_W), BF16), cache_shape, cache_shape],
        compiler_params=_cparams(("parallel",)),
        name="attn_sample",
    )(qkv, kvf, cache_k, cache_v, bias, sink)


def _retention_kernel(q_ref, k_ref, v_ref, g_ref, s0_ref, di_ref, qd_ref, kd_ref, gc_ref, gain_ref,
                      o_ref, s_ref, *, n_chunks, c):
    @pl.when(pl.program_id(1) == 0)
    def _():
        s_ref[...] = s0_ref[...]

    def chunk(n, carry):
        r0 = pl.multiple_of(n * c, c)
        for h in range(N_HEADS_R):
            cols = slice(h * DK_R, (h + 1) * DK_R)
            qc = q_ref[pl.ds(r0, c), cols]
            kc = k_ref[pl.ds(r0, c), cols]
            vc = v_ref[pl.ds(r0, c), cols]
            state = s_ref[0, h]
            sc = lax.dot_general(qc, kc, (((1,), (1,)), ((), ())), preferred_element_type=F32) * di_ref[h]
            o = jnp.dot(sc.astype(BF16), vc, preferred_element_type=F32)
            q_dec = (qc.astype(F32) * qd_ref[h]).astype(BF16)
            o = o + jnp.dot(q_dec, state.astype(BF16), preferred_element_type=F32)
            k_dec = (kc.astype(F32) * kd_ref[h]).astype(BF16)
            s_ref[0, h] = gc_ref[h] * state + lax.dot_general(k_dec, vc, (((0,), (0,)), ((), ())),
                                                              preferred_element_type=F32)
            mu = jnp.mean(o, axis=-1, keepdims=True)
            d = o - mu
            var = jnp.mean(d * d, axis=-1, keepdims=True)
            y = d * lax.rsqrt(var + LN_EPS) * gain_ref[:, cols]
            o_ref[pl.ds(r0, c), cols] = (y * _silu(g_ref[pl.ds(r0, c), cols].astype(F32))).astype(BF16)
        return carry

    lax.fori_loop(0, n_chunks, chunk, 0, unroll=4 if n_chunks % 4 == 0 else 1)


def _retention(qkr, vg, s0, tabs, gain, b, s, c):
    di, qd, kd, gc = tabs
    st = min(1024, s)
    nt = s // st
    seq = lambda col: pl.BlockSpec((st, QR_W), lambda bi, j: (bi * nt + j, col))
    full = lambda a: pl.BlockSpec(a.shape, lambda bi, j: (0,) * a.ndim)
    st_spec = pl.BlockSpec((1, N_HEADS_R, DK_R, DK_R), lambda bi, j: (bi, 0, 0, 0))
    return pl.pallas_call(
        functools.partial(_retention_kernel, n_chunks=st // c, c=c),
        grid=(b, nt),
        in_specs=[seq(0), seq(1), seq(0), seq(1), st_spec, full(di), full(qd), full(kd), full(gc), full(gain)],
        out_specs=[seq(0), st_spec],
        out_shape=[jax.ShapeDtypeStruct((b * s, QR_W), BF16), jax.ShapeDtypeStruct((b, N_HEADS_R, DK_R, DK_R), F32)],
        compiler_params=_cparams(("parallel", "arbitrary")),
        name="retention",
    )(qkr, qkr, vg, vg, s0, di, qd, kd, gc, gain)


def _mix_kernel(x_ref, oa_ref, or_ref, g_ref, wa_ref, wb_ref, wo_ref, lg_ref, lb_ref, wrt_ref, x1_ref, st_ref, xp_ref):
    a = jnp.dot(oa_ref[...], wa_ref[...], preferred_element_type=F32)
    b = jnp.dot(or_ref[...], wb_ref[...], preferred_element_type=F32)
    merged = g_ref[:, 0:D_MODEL].astype(F32) * a + g_ref[:, D_MODEL:].astype(F32) * b
    y = jnp.dot(merged.astype(BF16), wo_ref[...], preferred_element_type=F32)
    x1 = _layer_norm(DN_ALPHA * x_ref[...] + y, lg_ref[...], lb_ref[...])
    x1_ref[...] = x1
    _store_packed(xp_ref, x1)
    logits = lax.dot_general(wrt_ref[...], x1.astype(BF16), (((1,), (1,)), ((), ())), preferred_element_type=F32)
    st_ref[...] = jax.nn.sigmoid(logits)


def _mix(x2d, oa, orr, gm, wa, wb, wo, lg, lb, wrt, tm):
    t = x2d.shape[0]
    row = lambda w: pl.BlockSpec((tm, w), lambda i: (i, 0))
    full = lambda a: pl.BlockSpec(a.shape, lambda i: (0,) * a.ndim)
    return pl.pallas_call(
        _mix_kernel,
        grid=(t // tm,),
        in_specs=[row(D_MODEL), row(PQ_W), row(QR_W), row(2 * D_MODEL),
                  full(wa), full(wb), full(wo), full(lg), full(lb), full(wrt)],
        out_specs=[row(D_MODEL), pl.BlockSpec((N_EXPERTS, tm), lambda i: (0, i)),
                   pl.BlockSpec((tm * PACK_SUB, 128), lambda i: (i, 0))],
        out_shape=[jax.ShapeDtypeStruct((t, D_MODEL), F32), jax.ShapeDtypeStruct((N_EXPERTS, t), F32),
                   jax.ShapeDtypeStruct((t * PACK_SUB, 128), U32)],
        compiler_params=_cparams(("parallel",)),
        name="mix",
    )(x2d, oa, orr, gm, wa, wb, wo, lg, lb, wrt)


def _route_kernel(s_ref, bias_ref, e_ref, w_ref, r_ref, cnt_ref, carry_ref):
    @pl.when(pl.program_id(0) == 0)
    def _():
        carry_ref[...] = jnp.zeros_like(carry_ref)

    s = s_ref[...]
    tl = s.shape[1]
    choice = s + bias_ref[...]
    row = lax.broadcasted_iota(I32, (N_EXPERTS, tl), 0)
    row_g = lax.broadcasted_iota(I32, (GROUP_SIZE, tl), 0)
    neg = -jnp.inf

    scores = []
    for g in range(N_GROUPS):
        blk = choice[g * GROUP_SIZE:(g + 1) * GROUP_SIZE]
        m1 = jnp.max(blk, axis=0, keepdims=True)
        i1 = jnp.min(jnp.where(blk == m1, row_g, GROUP_SIZE), axis=0, keepdims=True)
        m2 = jnp.max(jnp.where(row_g == i1, neg, blk), axis=0, keepdims=True)
        scores.append(m1 + m2)
    sc = jnp.concatenate(scores, axis=0)
    gi = lax.broadcasted_iota(I32, sc.shape, 0)
    grank = jnp.zeros(sc.shape, I32)
    for g in range(N_GROUPS):
        other = sc[g:g + 1]
        ahead = jnp.where(other > sc, 1, jnp.where(other == sc, jnp.where(gi > g, 1, 0), 0))
        grank = grank + ahead
    cm = jnp.concatenate(
        [jnp.where(grank[g:g + 1] < TOPK_GROUPS, choice[g * GROUP_SIZE:(g + 1) * GROUP_SIZE], neg)
         for g in range(N_GROUPS)], axis=0)

    experts, weights = [], []
    for _ in range(TOP_K):
        m = jnp.max(cm, axis=0, keepdims=True)
        idx = jnp.min(jnp.where(cm == m, row, N_EXPERTS), axis=0, keepdims=True)
        hit = row == idx
        weights.append(jnp.sum(jnp.where(hit, s, 0.0), axis=0, keepdims=True))
        cm = jnp.where(hit, neg, cm)
        experts.append(idx)
    e8 = jnp.concatenate(experts, axis=0)
    w8 = jnp.concatenate(weights, axis=0)
    e_ref[...] = e8
    w_ref[...] = w8 / jnp.sum(w8, axis=0, keepdims=True) * ROUTED_SCALE

    before = (lax.broadcasted_iota(I32, (tl, tl), 0) < lax.broadcasted_iota(I32, (tl, tl), 1))
    before = jnp.where(before, 1.0, 0.0).astype(BF16)
    carry = carry_ref[...]
    ranks = []
    for k in range(TOP_K):
        hit = row == experts[k]
        onehot = jnp.where(hit, 1.0, 0.0)
        prefix = jnp.dot(onehot.astype(BF16), before, preferred_element_type=F32)
        ranks.append(jnp.sum(jnp.where(hit, prefix + carry, 0.0), axis=0, keepdims=True))
        carry = carry + jnp.sum(onehot, axis=1, keepdims=True)
    r_ref[...] = jnp.concatenate(ranks, axis=0).astype(I32)
    carry_ref[...] = carry
    cnt_ref[...] = carry.astype(I32)


def _route(st, bias_col, tl):
    t = st.shape[1]
    tok = pl.BlockSpec((TOP_K, tl), lambda i: (0, i))
    return pl.pallas_call(
        _route_kernel,
        grid=(t // tl,),
        in_specs=[pl.BlockSpec((N_EXPERTS, tl), lambda i: (0, i)), pl.BlockSpec((N_EXPERTS, 1), lambda i: (0, 0))],
        out_specs=[tok, tok, tok, pl.BlockSpec((N_EXPERTS, 1), lambda i: (0, 0))],
        out_shape=[jax.ShapeDtypeStruct((TOP_K, t), I32), jax.ShapeDtypeStruct((TOP_K, t), F32),
                   jax.ShapeDtypeStruct((TOP_K, t), I32), jax.ShapeDtypeStruct((N_EXPERTS, 1), I32)],
        scratch_shapes=[pltpu.VMEM((N_EXPERTS, 1), F32)],
        compiler_params=_cparams(("arbitrary",)),
        name="route",
    )(st, bias_col)


def _dest_kernel(e_ref, r_ref, ps_ref, d_ref):
    tl = e_ref.shape[1]
    row = lax.broadcasted_iota(I32, (N_EXPERTS, tl), 0)
    ps = ps_ref[...]
    starts = [jnp.sum(jnp.where(row == e_ref[k:k + 1, :], ps, 0.0), axis=0, keepdims=True) for k in range(TOP_K)]
    d_ref[...] = jnp.concatenate(starts, axis=0).astype(I32) + r_ref[...]


def _dest(e8, r8, pstart_col, tl):
    t = e8.shape[1]
    tok = pl.BlockSpec((TOP_K, tl), lambda i: (0, i))
    return pl.pallas_call(
        _dest_kernel,
        grid=(t // tl,),
        in_specs=[tok, tok, pl.BlockSpec((N_EXPERTS, 1), lambda i: (0, 0))],
        out_specs=tok,
        out_shape=jax.ShapeDtypeStruct((TOP_K, t), I32),
        compiler_params=_cparams(("parallel",)),
        name="dest",
    )(e8, r8, pstart_col)


TOKEN_GROUP = 8


def _row_copy(src, src_tok, dst, dst_tok, sem):
    s0 = pl.multiple_of(src_tok * PACK_SUB, PACK_SUB)
    d0 = pl.multiple_of(dst_tok * PACK_SUB, PACK_SUB)
    return pltpu.make_async_copy(src.at[pl.ds(s0, PACK_SUB)], dst.at[pl.ds(d0, PACK_SUB)], sem)


def _wait_rows(src, dst, sem, n_tokens):
    def wait(t, carry):
        for k in range(TOP_K):
            _row_copy(src, 0, dst, 0, sem).wait()
        return carry

    lax.fori_loop(0, n_tokens, wait, 0)


def _dispatch_kernel(d_ref, x_ref, xs_hbm, sem, *, td):
    def start(g, carry):
        for j in range(TOKEN_GROUP):
            t = g * TOKEN_GROUP + j
            for k in range(TOP_K):
                _row_copy(x_ref, t, xs_hbm, d_ref[t * TOP_K + k], sem).start(priority=k % 2)
        return carry

    lax.fori_loop(0, td // TOKEN_GROUP, start, 0)
    _wait_rows(x_ref, xs_hbm, sem, td)


def _dispatch(d8, xp, n_rows, td):
    t = d8.shape[0] // TOP_K
    return pl.pallas_call(
        functools.partial(_dispatch_kernel, td=td),
        grid=(t // td,),
        in_specs=[pl.BlockSpec((td * TOP_K,), lambda i: (i,), memory_space=pltpu.SMEM),
                  pl.BlockSpec((td * PACK_SUB, 128), lambda i: (i, 0))],
        out_specs=pl.BlockSpec(memory_space=pl.ANY),
        scratch_shapes=[pltpu.SemaphoreType.DMA(())],
        out_shape=jax.ShapeDtypeStruct((n_rows * PACK_SUB, 128), U32),
        compiler_params=_cparams(("arbitrary",)),
        name="dispatch",
    )(d8, xp)


def _moe_kernel(be_ref, nv_ref, x_ref, wg_ref, wu_ref, wd_ref, y_ref, wg_bf, wu_bf, wd_bf):
    j = pl.program_id(0)
    nv = nv_ref[j]

    @pl.when((j == 0) | (be_ref[j] != be_ref[jnp.maximum(j - 1, 0)]))
    def _():
        wg_bf[...] = wg_ref[0].astype(BF16)
        wu_bf[...] = wu_ref[0].astype(BF16)
        wd_bf[...] = wd_ref[0].astype(BF16)

    @pl.when(nv > 0)
    def _():
        hi, lo = _load_packed(x_ref, MOE_ROWS)
        valid = lax.broadcasted_iota(I32, hi.shape, 0) < nv
        xb = jnp.concatenate([jnp.where(valid, hi, 0.0).astype(BF16), jnp.where(valid, lo, 0.0).astype(BF16)], axis=1)
        g = jnp.dot(xb, wg_bf[...], preferred_element_type=F32)
        u = jnp.dot(xb, wu_bf[...], preferred_element_type=F32)
        h = (_silu(g) * u).astype(BF16)
        _store_packed(y_ref, jnp.dot(h, wd_bf[...], preferred_element_type=F32))

    @pl.when(nv == 0)
    def _():
        y_ref[...] = jnp.zeros_like(y_ref)


def _moe(blk_exp, blk_valid, xs, wg, wu, wd):
    n_blocks = blk_exp.shape[0]
    rows = pl.BlockSpec((MOE_ROWS * PACK_SUB, 128), lambda j, be, nv: (j, 0))
    return pl.pallas_call(
        _moe_kernel,
        grid_spec=pltpu.PrefetchScalarGridSpec(
            num_scalar_prefetch=2,
            grid=(n_blocks,),
            in_specs=[rows,
                      pl.BlockSpec((1, D_MODEL, D_EXPERT), lambda j, be, nv: (be[j], 0, 0)),
                      pl.BlockSpec((1, D_MODEL, D_EXPERT), lambda j, be, nv: (be[j], 0, 0)),
                      pl.BlockSpec((1, D_EXPERT, D_MODEL), lambda j, be, nv: (be[j], 0, 0))],
            out_specs=rows,
            scratch_shapes=[pltpu.VMEM((D_MODEL, D_EXPERT), BF16), pltpu.VMEM((D_MODEL, D_EXPERT), BF16),
                            pltpu.VMEM((D_EXPERT, D_MODEL), BF16)],
        ),
        out_shape=jax.ShapeDtypeStruct(xs.shape, U32),
        compiler_params=_cparams(("arbitrary",)),
        name="moe",
    )(blk_exp, blk_valid, xs, wg, wu, wd)


def _final_kernel(d_ref, x1_ref, w_ref, p_ref, wsg_ref, wsu_ref, wsd_ref, lg_ref, lb_ref,
                  wpp_ref, wpg_ref, ys_hbm, out_ref, gbuf, sem, *, tf):
    def start(g, carry):
        for j in range(TOKEN_GROUP):
            t = g * TOKEN_GROUP + j
            for k in range(TOP_K):
                _row_copy(ys_hbm, d_ref[t * TOP_K + k], gbuf.at[k], t, sem).start(priority=k % 2)
        return carry

    lax.fori_loop(0, tf // TOKEN_GROUP, start, 0)

    x1 = x1_ref[...]
    xb = x1.astype(BF16)
    hs = _silu(jnp.dot(xb, wsg_ref[...], preferred_element_type=F32)) * jnp.dot(xb, wsu_ref[...],
                                                                               preferred_element_type=F32)
    y = jnp.dot(hs.astype(BF16), wsd_ref[...], preferred_element_type=F32)

    _wait_rows(ys_hbm, gbuf.at[0], sem, tf)

    y_hi = jnp.zeros((tf, PACK_W), F32)
    y_lo = jnp.zeros((tf, PACK_W), F32)
    for k in range(TOP_K):
        hi, lo = _load_packed(gbuf.at[k], tf)
        wk = w_ref[:, k:k + 1]
        y_hi = y_hi + wk * hi
        y_lo = y_lo + wk * lo
    y = y + jnp.concatenate([y_hi, y_lo], axis=1)
    x2 = _layer_norm(DN_ALPHA * x1 + y, lg_ref[...], lb_ref[...])
    gate = jax.nn.sigmoid(jnp.dot(x2.astype(BF16), wpg_ref[...], preferred_element_type=F32))
    out_ref[...] = x2 + gate * jnp.dot(p_ref[...].astype(BF16), wpp_ref[...], preferred_element_type=F32)


def _final(d8, x1, w_tok, p2d, wsg, wsu, wsd, lg, lb, wpp, wpg, ys, tf):
    t = x1.shape[0]
    row = lambda w: pl.BlockSpec((tf, w), lambda i: (i, 0))
    full = lambda a: pl.BlockSpec(a.shape, lambda i: (0,) * a.ndim)
    return pl.pallas_call(
        functools.partial(_final_kernel, tf=tf),
        grid=(t // tf,),
        in_specs=[pl.BlockSpec((tf * TOP_K,), lambda i: (i,), memory_space=pltpu.SMEM),
                  row(D_MODEL), row(TOP_K), row(D_PLE),
                  full(wsg), full(wsu), full(wsd), full(lg), full(lb), full(wpp), full(wpg),
                  pl.BlockSpec(memory_space=pl.ANY)],
        out_specs=row(D_MODEL),
        scratch_shapes=[pltpu.VMEM((TOP_K, tf * PACK_SUB, 128), U32), pltpu.SemaphoreType.DMA(())],
        out_shape=jax.ShapeDtypeStruct((t, D_MODEL), F32),
        compiler_params=_cparams(("arbitrary",)),
        name="final",
    )(d8, x1, w_tok, p2d, wsg, wsu, wsd, lg, lb, wpp, wpg, ys)


def _t5_bucket(rel):
    nb = N_BUCKETS // 2
    max_exact = nb // 2
    ret = jnp.where(rel > 0, nb, 0)
    n = jnp.abs(rel)
    nf = jnp.maximum(n, 1).astype(F32)
    large = max_exact + (jnp.log(nf / max_exact) / math.log(MAX_DIST / max_exact) * (nb - max_exact)).astype(I32)
    large = jnp.minimum(large, nb - 1)
    return ret + jnp.where(n < max_exact, n, large)


def _rel_bias(table, q_len, k_len):
    rel = (jnp.arange(k_len, dtype=I32)[None, :] - WINDOW) - jnp.arange(q_len, dtype=I32)[:, None]
    b = table[_t5_bucket(rel)]
    return jnp.transpose(b, (2, 0, 1)).reshape(N_KV_A, GROUP_A * q_len, k_len).astype(F32)


def _rotary_tables(pos):
    half = DK_R // 2
    inv = ROPE_BASE ** (-jnp.arange(half, dtype=F32) / half)
    ang = pos.astype(F32)[:, None] * inv[None, :]
    cos, sin = jnp.cos(ang), jnp.sin(ang)
    return jnp.concatenate([cos, cos], -1), jnp.concatenate([-sin, sin], -1)


def _decay_tables(c):
    log_gamma = jnp.log1p(-jnp.exp2(-5.0 - jnp.arange(N_HEADS_R, dtype=F32)))
    idx = jnp.arange(c, dtype=F32)
    di = jnp.exp(jnp.abs(idx[:, None] - idx[None, :])[None] * log_gamma[:, None, None])
    qd = jnp.exp((idx[None, :] + 1.0) * log_gamma[:, None])
    kd = jnp.exp((c - 1.0 - idx[None, :]) * log_gamma[:, None])
    gc = jnp.exp(c * log_gamma)
    bc = lambda a: jnp.broadcast_to(a[:, :, None], (N_HEADS_R, c, DK_R))
    return di, bc(qd), bc(kd), jnp.broadcast_to(gc[:, None, None], (N_HEADS_R, DK_R, DK_R))


def _pad_heads(w, n_heads):
    rows = w.shape[0]
    w = w.reshape(rows, n_heads, HEAD_DIM_A)
    w = jnp.pad(w, ((0, 0), (0, 0), (0, HEAD_PAD - HEAD_DIM_A)))
    return w.reshape(rows, n_heads * HEAD_PAD)


def _moe_ffn(x1, xp, st, p2d, wts):
    t = x1.shape[0]
    tile = min(256, t)
    e8, w8, r8, counts = _route(st, wts['router_bias'], tile)
    counts = counts[:, 0]
    padded = (counts + MOE_ROWS - 1) // MOE_ROWS * MOE_ROWS
    pad_end = jnp.cumsum(padded)
    pstart = (pad_end - padded).astype(I32)
    n_rows = t * TOP_K + N_EXPERTS * MOE_ROWS
    blk_start = jnp.arange(n_rows // MOE_ROWS, dtype=I32) * MOE_ROWS
    blk_exp = jnp.minimum(jnp.sum(blk_start[:, None] >= pad_end[None, :], axis=1), N_EXPERTS - 1).astype(I32)
    own = blk_exp[:, None] == jnp.arange(N_EXPERTS, dtype=I32)[None, :]
    blk_end = jnp.sum(jnp.where(own, (pstart + counts)[None, :], 0), axis=1)
    blk_valid = jnp.clip(blk_end - blk_start, 0, MOE_ROWS).astype(I32)
    d8 = _dest(e8, r8, pstart.astype(F32).reshape(N_EXPERTS, 1), tile).T.reshape(-1)
    xs = _dispatch(d8, xp, n_rows, tile)
    ys = _moe(blk_exp, blk_valid, xs, wts['w_exp_gate'], wts['w_exp_up'], wts['w_exp_down'])
    return _final(d8, x1, w8.T, p2d, wts['w_sh_gate'], wts['w_sh_up'], wts['w_sh_down'],
                  wts['ln2_g'], wts['ln2_b'], wts['w_ple_proj'], wts['w_ple_gate'], ys, tile)


def _post_mixers(x2d, oa, orr, gm, p2d, wts):
    t = x2d.shape[0]
    x1, st, xp = _mix(x2d, oa, orr, gm, wts['w_branch_attn'], wts['w_branch_ret'], wts['w_out'],
                      wts['ln1_g'], wts['ln1_b'], wts['w_router_t'], min(512, t))
    return _moe_ffn(x1, xp, st, p2d, wts)


def kernel(x_prompt, x_sample, cache_attn_k, cache_attn_v, state_retention, p_prompt, p_sample,
           w_in, attn_sinks, rel_bias_table, ret_gn_gain, w_branch_attn, w_branch_ret, w_out,
           ln1_g, ln1_b, w_router, router_bias, w_exp_gate, w_exp_up, w_exp_down,
           w_sh_gate, w_sh_up, w_sh_down, ln2_g, ln2_b, w_ple_proj, w_ple_gate):
    b, s, _ = x_prompt.shape
    bd, l, _ = x_sample.shape
    i = 0
    row = lambda a: a.reshape(1, -1).astype(F32)
    wts = dict(
        w_branch_attn=_pad_heads(w_branch_attn[i].astype(BF16).T, N_HEADS_A).T,
        w_branch_ret=w_branch_ret[i].astype(BF16),
        w_out=w_out[i].astype(BF16), ln1_g=row(ln1_g[i]), ln1_b=row(ln1_b[i]),
        w_router_t=w_router[i].T.astype(BF16), router_bias=router_bias[i].reshape(N_EXPERTS, 1).astype(F32),
        w_exp_gate=w_exp_gate[i], w_exp_up=w_exp_up[i], w_exp_down=w_exp_down[i],
        w_sh_gate=w_sh_gate[i].astype(BF16), w_sh_up=w_sh_up[i].astype(BF16), w_sh_down=w_sh_down[i].astype(BF16),
        ln2_g=row(ln2_g[i]), ln2_b=row(ln2_b[i]),
        w_ple_proj=w_ple_proj[i].astype(BF16), w_ple_gate=w_ple_gate[i].astype(BF16))
    w_in_bf = w_in[i].astype(BF16)
    w_in_bf = jnp.concatenate([_pad_heads(w_in_bf[:, :QA_W] * (HEAD_DIM_A ** -0.5), N_HEADS_A),
                               _pad_heads(w_in_bf[:, OFF_KV:OFF_QR], 2 * N_KV_A), w_in_bf[:, OFF_KV:]], axis=1)
    gain = row(ret_gn_gain[i])
    sinks = attn_sinks[i].astype(F32).reshape(N_KV_A, GROUP_A)

    xp = x_prompt.reshape(b * s, D_MODEL)
    cos_p, sin_p = _rotary_tables(jnp.arange(s, dtype=I32))
    qkv, kvf, qkr, vg, gm = _inproj(xp, w_in_bf, cos_p, sin_p, min(512, s))
    bias_p = jnp.pad(_rel_bias(rel_bias_table, CHUNK, WINDOW + CHUNK), ((0, 0), (0, 0), (0, KEY_PAD - WINDOW - CHUNK)))
    sink_p = jnp.repeat(sinks, CHUNK, axis=1)[..., None]
    fill_p = jnp.where(jnp.arange(KEY_PAD)[None, None, :] == WINDOW + CHUNK, sink_p, NEG_INF).astype(F32)
    oa = _attn_prompt(qkv, bias_p, fill_p, b, s)
    s0 = jnp.zeros((b, N_HEADS_R, DK_R, DK_R), F32)
    orr, ns_p = _retention(qkr, vg, s0, _decay_tables(CHUNK), gain, b, s, CHUNK)
    y_p = _post_mixers(xp, oa, orr, gm, p_prompt[i].reshape(b * s, D_PLE), wts).reshape(b, s, D_MODEL)
    kv_tail = kvf.reshape(b, s, 2, N_KV_A, HEAD_DIM_A)[:, s - WINDOW:]
    nk_p, nv_p = kv_tail[:, :, 0], kv_tail[:, :, 1]

    xs = x_sample.reshape(bd * l, D_MODEL)
    cos_s, sin_s = _rotary_tables(jnp.tile(PAST_LEN + jnp.arange(l, dtype=I32), bd))
    qkv, kvf, qkr, vg, gm = _inproj(xs, w_in_bf, cos_s, sin_s, bd * l)
    sink_s = jnp.repeat(sinks, l, axis=1)[..., None]
    oa, nk_s, nv_s = _attn_sample(qkv, kvf, cache_attn_k[i].reshape(bd, WINDOW, KA_W),
                                  cache_attn_v[i].reshape(bd, WINDOW, KA_W),
                                  _rel_bias(rel_bias_table, l, WINDOW + l), sink_s, bd, l)
    orr, ns_s = _retention(qkr, vg, state_retention[i].astype(F32), _decay_tables(l), gain, bd, l, l)
    y_s = _post_mixers(xs, oa, orr, gm, p_sample[i].reshape(bd * l, D_PLE), wts).reshape(bd, l, D_MODEL)
    shape_kv = (bd, WINDOW, N_KV_A, HEAD_DIM_A)
    return (y_p, y_s, nk_p[None], nv_p[None], ns_p[None],
            nk_s.reshape(shape_kv)[None], nv_s.reshape(shape_kv)[None], ns_s[None])
```

```python
import functools
import math

import jax
import jax.numpy as jnp
from jax import lax
from jax.experimental import pallas as pl
from jax.experimental.pallas import tpu as pltpu

F32 = jnp.float32
BF16 = jnp.bfloat16
I32 = jnp.int32

D_MODEL = 1024
CHUNK = 64
WINDOW = 128
N_HEADS_A = 8
N_KV_A = 2
GROUP_A = 4
HEAD_DIM_A = 64
N_BUCKETS = 32
MAX_DIST = 128
N_HEADS_R = 4
DK_R = 128
ROPE_BASE = 10000.0
N_EXPERTS = 256
TOP_K = 8
N_GROUPS = 8
GROUP_SIZE = N_EXPERTS // N_GROUPS
TOPK_GROUPS = 4
D_EXPERT = 256
ROUTED_SCALE = 2.5
D_PLE = 256
LN_EPS = 1e-5
NEG_INF = -1e30
PAST_LEN = 2048
DEPTH = 1
DN_ALPHA = (2 * DEPTH) ** 0.25

QA_W = N_HEADS_A * HEAD_DIM_A
KA_W = N_KV_A * HEAD_DIM_A
QR_W = N_HEADS_R * DK_R
OFF_KV = QA_W
OFF_QR = QA_W + 2 * KA_W
OFF_VR = OFF_QR + 2 * QR_W
OFF_GM = OFF_VR + 2 * QR_W
IN_W = OFF_GM + 2 * D_MODEL

HEAD_PAD = 128
PQ_W = N_HEADS_A * HEAD_PAD
PKV_W = 2 * N_KV_A * HEAD_PAD
A_KVF = PQ_W + PKV_W
A_QR = A_KVF + 2 * KA_W
A_VR = A_QR + 2 * QR_W
A_GM = A_VR + 2 * QR_W
A_W = A_GM + 2 * D_MODEL

MOE_ROWS = 512
VMEM_LIMIT = 56 * 1024 * 1024


def _cparams(sem, vmem=VMEM_LIMIT):
    return pltpu.CompilerParams(dimension_semantics=sem, vmem_limit_bytes=vmem)


def _layer_norm(h, g, b):
    mu = jnp.mean(h, axis=-1, keepdims=True)
    d = h - mu
    var = jnp.mean(d * d, axis=-1, keepdims=True)
    return d * lax.rsqrt(var + LN_EPS) * g + b


def _silu(x):
    return x * jax.nn.sigmoid(x)


U32 = jnp.uint32
PACK_W = D_MODEL // 2
PACK_SUB = PACK_W // 128


def _bf16_bits(x):
    return pltpu.bitcast(x.astype(BF16).astype(F32), U32)


def _store_packed(ref, x):
    n = x.shape[0]
    words = _bf16_bits(x[:, :PACK_W]) | (_bf16_bits(x[:, PACK_W:]) >> 16)
    for c in range(PACK_SUB):
        ref[pl.ds(c, n, stride=PACK_SUB), :] = words[:, c * 128:(c + 1) * 128]


def _load_packed(ref, n):
    words = jnp.concatenate([ref[pl.ds(c, n, stride=PACK_SUB), :] for c in range(PACK_SUB)], axis=1)
    hi = pltpu.bitcast(words & jnp.uint32(0xFFFF0000), F32)
    lo = pltpu.bitcast(words << 16, F32)
    return hi, lo


def _inproj_kernel(x_ref, w_ref, cos_ref, sin_ref, qkv_ref, kvf_ref, qkr_ref, vg_ref, gm_ref):
    xb = x_ref[...].astype(BF16)

    def mm(lo, hi):
        return jnp.dot(xb, w_ref[:, lo:hi], preferred_element_type=F32)

    qkv_ref[:, 0:PQ_W] = mm(0, PQ_W).astype(BF16)
    qkv_ref[:, PQ_W:A_KVF] = mm(PQ_W, A_KVF).astype(BF16)
    kvf_ref[...] = mm(A_KVF, A_QR)
    cos = cos_ref[...]
    sin = sin_ref[...]
    for part in range(2):
        z = mm(A_QR + part * QR_W, A_QR + (part + 1) * QR_W)
        for h in range(N_HEADS_R):
            zh = z[:, h * DK_R:(h + 1) * DK_R]
            r = zh * cos + pltpu.roll(zh, DK_R // 2, axis=1) * sin
            if part == 1:
                r = r * (DK_R ** -0.5)
            c0 = part * QR_W + h * DK_R
            qkr_ref[:, c0:c0 + DK_R] = r.astype(BF16)
    vg_ref[...] = mm(A_VR, A_GM).astype(BF16)
    for half in range(2):
        lo = A_GM + half * D_MODEL
        gm_ref[:, half * D_MODEL:(half + 1) * D_MODEL] = jax.nn.sigmoid(mm(lo, lo + D_MODEL)).astype(BF16)


def _inproj(x2d, w_bf, cos_tab, sin_tab, tm):
    t = x2d.shape[0]
    nper = cos_tab.shape[0] // tm
    row = lambda i: (i, 0)
    return pl.pallas_call(
        _inproj_kernel,
        grid=(t // tm,),
        in_specs=[
            pl.BlockSpec((tm, D_MODEL), row),
            pl.BlockSpec((D_MODEL, A_W), lambda i: (0, 0)),
            pl.BlockSpec((tm, DK_R), lambda i: (i % nper, 0)),
            pl.BlockSpec((tm, DK_R), lambda i: (i % nper, 0)),
        ],
        out_specs=[
            pl.BlockSpec((tm, A_KVF), row),
            pl.BlockSpec((tm, 2 * KA_W), row),
            pl.BlockSpec((tm, 2 * QR_W), row),
            pl.BlockSpec((tm, 2 * QR_W), row),
            pl.BlockSpec((tm, 2 * D_MODEL), row),
        ],
        out_shape=[
            jax.ShapeDtypeStruct((t, A_KVF), BF16),
            jax.ShapeDtypeStruct((t, 2 * KA_W), F32),
            jax.ShapeDtypeStruct((t, 2 * QR_W), BF16),
            jax.ShapeDtypeStruct((t, 2 * QR_W), BF16),
            jax.ShapeDtypeStruct((t, 2 * D_MODEL), BF16),
        ],
        compiler_params=_cparams(("parallel",)),
        name="inproj",
    )(x2d, w_bf, cos_tab, sin_tab)


def _attend(q4, k, v, bias, sink):
    s = lax.dot_general(q4, k, (((1,), (1,)), ((), ())), preferred_element_type=F32) + bias
    m = jnp.maximum(jnp.max(s, axis=-1, keepdims=True), sink)
    e = jnp.exp(s - m)
    p = e / (jnp.sum(e, axis=-1, keepdims=True) + jnp.exp(sink - m))
    return jnp.dot(p.astype(BF16), v, preferred_element_type=F32)


KEY_PAD = 256


def _attend_sink_column(q4, k, v, bias, fill, valid):
    s = lax.dot_general(q4, k, (((1,), (1,)), ((), ())), preferred_element_type=F32)
    s = jnp.where(valid, s + bias, fill)
    e = jnp.exp(s - jnp.max(s, axis=-1, keepdims=True)).astype(BF16)
    den = jnp.dot(e, jnp.ones((KEY_PAD, v.shape[1]), BF16), preferred_element_type=F32)
    return jnp.dot(e[:, 0:WINDOW + CHUNK], v, preferred_element_type=F32) / den


def _attn_prompt_kernel(q_ref, kvc_ref, kvp_ref, bias_ref, fill_ref, o_ref, kv_buf, *, n_chunks):
    i = pl.program_id(1)
    qb = n_chunks * CHUNK
    kv_buf[0:WINDOW, :] = kvp_ref[...]
    kv_buf[WINDOW:WINDOW + qb, :] = kvc_ref[...]
    kv_buf[WINDOW + qb:, :] = jnp.zeros((KEY_PAD - WINDOW - CHUNK, PKV_W), BF16)
    col = lax.broadcasted_iota(I32, (GROUP_A * CHUNK, KEY_PAD), 1)
    key_col = jnp.where(col < WINDOW + CHUNK, col, -1)

    def chunk(j, carry):
        r0 = pl.multiple_of(j * CHUNK, CHUNK)
        valid = key_col >= jnp.maximum(WINDOW - (i * n_chunks + j) * CHUNK, 0)
        for kv in range(N_KV_A):
            q4 = jnp.concatenate(
                [q_ref[pl.ds(r0, CHUNK), (kv * GROUP_A + g) * HEAD_PAD:(kv * GROUP_A + g + 1) * HEAD_PAD]
                 for g in range(GROUP_A)], axis=0)
            k = kv_buf[pl.ds(r0, KEY_PAD), kv * HEAD_PAD:(kv + 1) * HEAD_PAD]
            v = kv_buf[pl.ds(r0, WINDOW + CHUNK), (N_KV_A + kv) * HEAD_PAD:(N_KV_A + kv + 1) * HEAD_PAD]
            o4 = _attend_sink_column(q4, k, v, bias_ref[kv], fill_ref[kv], valid)
            for g in range(GROUP_A):
                c0 = (kv * GROUP_A + g) * HEAD_PAD
                o_ref[pl.ds(r0, CHUNK), c0:c0 + HEAD_PAD] = o4[g * CHUNK:(g + 1) * CHUNK].astype(BF16)
        return carry

    lax.fori_loop(0, n_chunks, chunk, 0, unroll=2 if n_chunks % 2 == 0 else 1)


def _attn_prompt(qkv, bias, fill, b, s):
    qb = min(512, s)
    n_chunks = qb // CHUNK
    nq = s // qb
    per = qb // WINDOW
    kv_col = PQ_W // PKV_W
    return pl.pallas_call(
        functools.partial(_attn_prompt_kernel, n_chunks=n_chunks),
        grid=(b, nq),
        in_specs=[
            pl.BlockSpec((qb, PQ_W), lambda bi, i: (bi * nq + i, 0)),
            pl.BlockSpec((qb, PKV_W), lambda bi, i: (bi * nq + i, kv_col)),
            pl.BlockSpec((WINDOW, PKV_W), lambda bi, i: (jnp.maximum((bi * nq + i) * per - 1, 0), kv_col)),
            pl.BlockSpec((N_KV_A, GROUP_A * CHUNK, KEY_PAD), lambda bi, i: (0, 0, 0)),
            pl.BlockSpec((N_KV_A, GROUP_A * CHUNK, KEY_PAD), lambda bi, i: (0, 0, 0)),
        ],
        out_specs=pl.BlockSpec((qb, PQ_W), lambda bi, i: (bi * nq + i, 0)),
        out_shape=jax.ShapeDtypeStruct((b * s, PQ_W), BF16),
        scratch_shapes=[pltpu.VMEM((qb + KEY_PAD - CHUNK, PKV_W), BF16)],
        compiler_params=_cparams(("parallel", "parallel")),
        name="attn_prompt",
    )(qkv, qkv, qkv, bias, fill)


def _attn_sample_kernel(q_ref, kvf_ref, ck_ref, cv_ref, bias_ref, sink_ref, o_ref, nk_ref, nv_ref, *, l):
    k_all = jnp.concatenate([ck_ref[0], kvf_ref[:, 0:KA_W]], axis=0)
    v_all = jnp.concatenate([cv_ref[0], kvf_ref[:, KA_W:2 * KA_W]], axis=0)
    nk_ref[0] = k_all[l:]
    nv_ref[0] = v_all[l:]
    kb = k_all.astype(BF16)
    vb = v_all.astype(BF16)
    o_ref[...] = jnp.zeros_like(o_ref)
    for kv in range(N_KV_A):
        q4 = jnp.concatenate(
            [q_ref[:, (kv * GROUP_A + g) * HEAD_PAD:(kv * GROUP_A + g) * HEAD_PAD + HEAD_DIM_A]
             for g in range(GROUP_A)], axis=0)
        o4 = _attend(q4, kb[:, kv * HEAD_DIM_A:(kv + 1) * HEAD_DIM_A], vb[:, kv * HEAD_DIM_A:(kv + 1) * HEAD_DIM_A],
                     bias_ref[kv], sink_ref[kv])
        for g in range(GROUP_A):
            c0 = (kv * GROUP_A + g) * HEAD_PAD
            o_ref[:, c0:c0 + HEAD_DIM_A] = o4[g * l:(g + 1) * l].astype(BF16)


def _attn_sample(qkv, kvf, cache_k, cache_v, bias, sink, bd, l):
    cache_spec = pl.BlockSpec((1, WINDOW, KA_W), lambda bi: (bi, 0, 0))
    cache_shape = jax.ShapeDtypeStruct((bd, WINDOW, KA_W), F32)
    return pl.pallas_call(
        functools.partial(_attn_sample_kernel, l=l),
        grid=(bd,),
        in_specs=[
            pl.BlockSpec((l, A_KVF), lambda bi: (bi, 0)),
            pl.BlockSpec((l, 2 * KA_W), lambda bi: (bi, 0)),
            cache_spec, cache_spec,
            pl.BlockSpec((N_KV_A, GROUP_A * l, WINDOW + l), lambda bi: (0, 0, 0)),
            pl.BlockSpec((N_KV_A, GROUP_A * l, 1), lambda bi: (0, 0, 0)),
        ],
        out_specs=[pl.BlockSpec((l, PQ_W), lambda bi: (bi, 0)), cache_spec, cache_spec],
        out_shape=[jax.ShapeDtypeStruct((bd * l, PQ_W), BF16), cache_shape, cache_shape],
        compiler_params=_cparams(("parallel",)),
        name="attn_sample",
    )(qkv, kvf, cache_k, cache_v, bias, sink)


def _retention_kernel(q_ref, k_ref, v_ref, g_ref, s0_ref, di_ref, qd_ref, kd_ref, gc_ref, gain_ref,
                      o_ref, s_ref, *, n_chunks, c):
    @pl.when(pl.program_id(1) == 0)
    def _():
        s_ref[...] = s0_ref[...]

    def chunk(n, carry):
        r0 = pl.multiple_of(n * c, c)
        for h in range(N_HEADS_R):
            cols = slice(h * DK_R, (h + 1) * DK_R)
            qc = q_ref[pl.ds(r0, c), cols]
            kc = k_ref[pl.ds(r0, c), cols]
            vc = v_ref[pl.ds(r0, c), cols]
            state = s_ref[0, h]
            sc = lax.dot_general(qc, kc, (((1,), (1,)), ((), ())), preferred_element_type=F32) * di_ref[h]
            o = jnp.dot(sc.astype(BF16), vc, preferred_element_type=F32)
            q_dec = (qc.astype(F32) * qd_ref[h]).astype(BF16)
            o = o + jnp.dot(q_dec, state.astype(BF16), preferred_element_type=F32)
            k_dec = (kc.astype(F32) * kd_ref[h]).astype(BF16)
            s_ref[0, h] = gc_ref[h] * state + lax.dot_general(k_dec, vc, (((0,), (0,)), ((), ())),
                                                              preferred_element_type=F32)
            mu = jnp.mean(o, axis=-1, keepdims=True)
            d = o - mu
            var = jnp.mean(d * d, axis=-1, keepdims=True)
            y = d * lax.rsqrt(var + LN_EPS) * gain_ref[:, cols]
            o_ref[pl.ds(r0, c), cols] = (y * _silu(g_ref[pl.ds(r0, c), cols].astype(F32))).astype(BF16)
        return carry

    lax.fori_loop(0, n_chunks, chunk, 0, unroll=4 if n_chunks % 4 == 0 else 1)


def _retention(qkr, vg, s0, tabs, gain, b, s, c):
    di, qd, kd, gc = tabs
    st = min(1024, s)
    nt = s // st
    seq = lambda col: pl.BlockSpec((st, QR_W), lambda bi, j: (bi * nt + j, col))
    full = lambda a: pl.BlockSpec(a.shape, lambda bi, j: (0,) * a.ndim)
    st_spec = pl.BlockSpec((1, N_HEADS_R, DK_R, DK_R), lambda bi, j: (bi, 0, 0, 0))
    return pl.pallas_call(
        functools.partial(_retention_kernel, n_chunks=st // c, c=c),
        grid=(b, nt),
        in_specs=[seq(0), seq(1), seq(0), seq(1), st_spec, full(di), full(qd), full(kd), full(gc), full(gain)],
        out_specs=[seq(0), st_spec],
        out_shape=[jax.ShapeDtypeStruct((b * s, QR_W), BF16), jax.ShapeDtypeStruct((b, N_HEADS_R, DK_R, DK_R), F32)],
        compiler_params=_cparams(("parallel", "arbitrary")),
        name="retention",
    )(qkr, qkr, vg, vg, s0, di, qd, kd, gc, gain)


def _mix_kernel(x_ref, oa_ref, or_ref, g_ref, wa_ref, wb_ref, wo_ref, lg_ref, lb_ref, wrt_ref, x1_ref, st_ref, xp_ref):
    a = jnp.dot(oa_ref[...], wa_ref[...], preferred_element_type=F32)
    b = jnp.dot(or_ref[...], wb_ref[...], preferred_element_type=F32)
    merged = g_ref[:, 0:D_MODEL].astype(F32) * a + g_ref[:, D_MODEL:].astype(F32) * b
    y = jnp.dot(merged.astype(BF16), wo_ref[...], preferred_element_type=F32)
    x1 = _layer_norm(DN_ALPHA * x_ref[...] + y, lg_ref[...], lb_ref[...])
    x1_ref[...] = x1
    _store_packed(xp_ref, x1)
    logits = lax.dot_general(wrt_ref[...], x1.astype(BF16), (((1,), (1,)), ((), ())), preferred_element_type=F32)
    st_ref[...] = jax.nn.sigmoid(logits)


def _mix(x2d, oa, orr, gm, wa, wb, wo, lg, lb, wrt, tm):
    t = x2d.shape[0]
    row = lambda w: pl.BlockSpec((tm, w), lambda i: (i, 0))
    full = lambda a: pl.BlockSpec(a.shape, lambda i: (0,) * a.ndim)
    return pl.pallas_call(
        _mix_kernel,
        grid=(t // tm,),
        in_specs=[row(D_MODEL), row(PQ_W), row(QR_W), row(2 * D_MODEL),
                  full(wa), full(wb), full(wo), full(lg), full(lb), full(wrt)],
        out_specs=[row(D_MODEL), pl.BlockSpec((N_EXPERTS, tm), lambda i: (0, i)),
                   pl.BlockSpec((tm * PACK_SUB, 128), lambda i: (i, 0))],
        out_shape=[jax.ShapeDtypeStruct((t, D_MODEL), F32), jax.ShapeDtypeStruct((N_EXPERTS, t), F32),
                   jax.ShapeDtypeStruct((t * PACK_SUB, 128), U32)],
        compiler_params=_cparams(("parallel",)),
        name="mix",
    )(x2d, oa, orr, gm, wa, wb, wo, lg, lb, wrt)


def _route_kernel(s_ref, bias_ref, e_ref, w_ref, r_ref, cnt_ref, carry_ref):
    @pl.when(pl.program_id(0) == 0)
    def _():
        carry_ref[...] = jnp.zeros_like(carry_ref)

    s = s_ref[...]
    tl = s.shape[1]
    choice = s + bias_ref[...]
    row = lax.broadcasted_iota(I32, (N_EXPERTS, tl), 0)
    row_g = lax.broadcasted_iota(I32, (GROUP_SIZE, tl), 0)
    neg = -jnp.inf

    scores = []
    for g in range(N_GROUPS):
        blk = choice[g * GROUP_SIZE:(g + 1) * GROUP_SIZE]
        m1 = jnp.max(blk, axis=0, keepdims=True)
        i1 = jnp.min(jnp.where(blk == m1, row_g, GROUP_SIZE), axis=0, keepdims=True)
        m2 = jnp.max(jnp.where(row_g == i1, neg, blk), axis=0, keepdims=True)
        scores.append(m1 + m2)
    sc = jnp.concatenate(scores, axis=0)
    gi = lax.broadcasted_iota(I32, sc.shape, 0)
    grank = jnp.zeros(sc.shape, I32)
    for g in range(N_GROUPS):
        other = sc[g:g + 1]
        ahead = jnp.where(other > sc, 1, jnp.where(other == sc, jnp.where(gi > g, 1, 0), 0))
        grank = grank + ahead
    cm = jnp.concatenate(
        [jnp.where(grank[g:g + 1] < TOPK_GROUPS, choice[g * GROUP_SIZE:(g + 1) * GROUP_SIZE], neg)
         for g in range(N_GROUPS)], axis=0)

    experts, weights = [], []
    for _ in range(TOP_K):
        m = jnp.max(cm, axis=0, keepdims=True)
        idx = jnp.min(jnp.where(cm == m, row, N_EXPERTS), axis=0, keepdims=True)
        hit = row == idx
        weights.append(jnp.sum(jnp.where(hit, s, 0.0), axis=0, keepdims=True))
        cm = jnp.where(hit, neg, cm)
        experts.append(idx)
    e8 = jnp.concatenate(experts, axis=0)
    w8 = jnp.concatenate(weights, axis=0)
    e_ref[...] = e8
    w_ref[...] = w8 / jnp.sum(w8, axis=0, keepdims=True) * ROUTED_SCALE

    before = (lax.broadcasted_iota(I32, (tl, tl), 0) < lax.broadcasted_iota(I32, (tl, tl), 1))
    before = jnp.where(before, 1.0, 0.0).astype(BF16)
    carry = carry_ref[...]
    ranks = []
    for k in range(TOP_K):
        hit = row == experts[k]
        onehot = jnp.where(hit, 1.0, 0.0)
        prefix = jnp.dot(onehot.astype(BF16), before, preferred_element_type=F32)
        ranks.append(jnp.sum(jnp.where(hit, prefix + carry, 0.0), axis=0, keepdims=True))
        carry = carry + jnp.sum(onehot, axis=1, keepdims=True)
    r_ref[...] = jnp.concatenate(ranks, axis=0).astype(I32)
    carry_ref[...] = carry
    cnt_ref[...] = carry.astype(I32)


def _route(st, bias_col, tl):
    t = st.shape[1]
    tok = pl.BlockSpec((TOP_K, tl), lambda i: (0, i))
    return pl.pallas_call(
        _route_kernel,
        grid=(t // tl,),
        in_specs=[pl.BlockSpec((N_EXPERTS, tl), lambda i: (0, i)), pl.BlockSpec((N_EXPERTS, 1), lambda i: (0, 0))],
        out_specs=[tok, tok, tok, pl.BlockSpec((N_EXPERTS, 1), lambda i: (0, 0))],
        out_shape=[jax.ShapeDtypeStruct((TOP_K, t), I32), jax.ShapeDtypeStruct((TOP_K, t), F32),
                   jax.ShapeDtypeStruct((TOP_K, t), I32), jax.ShapeDtypeStruct((N_EXPERTS, 1), I32)],
        scratch_shapes=[pltpu.VMEM((N_EXPERTS, 1), F32)],
        compiler_params=_cparams(("arbitrary",)),
        name="route",
    )(st, bias_col)


def _dest_kernel(e_ref, r_ref, ps_ref, d_ref):
    tl = e_ref.shape[1]
    row = lax.broadcasted_iota(I32, (N_EXPERTS, tl), 0)
    ps = ps_ref[...]
    starts = [jnp.sum(jnp.where(row == e_ref[k:k + 1, :], ps, 0.0), axis=0, keepdims=True) for k in range(TOP_K)]
    d_ref[...] = jnp.concatenate(starts, axis=0).astype(I32) + r_ref[...]


def _dest(e8, r8, pstart_col, tl):
    t = e8.shape[1]
    tok = pl.BlockSpec((TOP_K, tl), lambda i: (0, i))
    return pl.pallas_call(
        _dest_kernel,
        grid=(t // tl,),
        in_specs=[tok, tok, pl.BlockSpec((N_EXPERTS, 1), lambda i: (0, 0))],
        out_specs=tok,
        out_shape=jax.ShapeDtypeStruct((TOP_K, t), I32),
        compiler_params=_cparams(("parallel",)),
        name="dest",
    )(e8, r8, pstart_col)


TOKEN_GROUP = 8


def _row_copy(src, src_tok, dst, dst_tok, sem):
    s0 = pl.multiple_of(src_tok * PACK_SUB, PACK_SUB)
    d0 = pl.multiple_of(dst_tok * PACK_SUB, PACK_SUB)
    return pltpu.make_async_copy(src.at[pl.ds(s0, PACK_SUB)], dst.at[pl.ds(d0, PACK_SUB)], sem)


def _wait_bytes_of(ref, sem):
    pltpu.make_async_copy(ref, ref, sem).wait()


def _dispatch_kernel(d_ref, x_ref, xs_hbm, sem, *, td):
    def start(g, carry):
        for j in range(TOKEN_GROUP):
            t = g * TOKEN_GROUP + j
            for k in range(TOP_K):
                _row_copy(x_ref, t, xs_hbm, d_ref[t * TOP_K + k], sem).start(priority=k % 2)
        return carry

    lax.fori_loop(0, td // TOKEN_GROUP, start, 0)
    for k in range(TOP_K):
        _wait_bytes_of(x_ref, sem)


def _dispatch(d8, xp, n_rows, td):
    t = d8.shape[0] // TOP_K
    return pl.pallas_call(
        functools.partial(_dispatch_kernel, td=td),
        grid=(t // td,),
        in_specs=[pl.BlockSpec((td * TOP_K,), lambda i: (i,), memory_space=pltpu.SMEM),
                  pl.BlockSpec((td * PACK_SUB, 128), lambda i: (i, 0))],
        out_specs=pl.BlockSpec(memory_space=pl.ANY),
        scratch_shapes=[pltpu.SemaphoreType.DMA(())],
        out_shape=jax.ShapeDtypeStruct((n_rows * PACK_SUB, 128), U32),
        compiler_params=_cparams(("arbitrary",)),
        name="dispatch",
    )(d8, xp)


def _moe_kernel(be_ref, nv_ref, x_ref, wg_ref, wu_ref, wd_ref, y_ref, wg_bf, wu_bf, wd_bf):
    j = pl.program_id(0)
    nv = nv_ref[j]

    @pl.when((j == 0) | (be_ref[j] != be_ref[jnp.maximum(j - 1, 0)]))
    def _():
        wg_bf[...] = wg_ref[0].astype(BF16)
        wu_bf[...] = wu_ref[0].astype(BF16)
        wd_bf[...] = wd_ref[0].astype(BF16)

    @pl.when(nv > 0)
    def _():
        hi, lo = _load_packed(x_ref, MOE_ROWS)
        valid = lax.broadcasted_iota(I32, hi.shape, 0) < nv
        xb = jnp.concatenate([jnp.where(valid, hi, 0.0).astype(BF16), jnp.where(valid, lo, 0.0).astype(BF16)], axis=1)
        g = jnp.dot(xb, wg_bf[...], preferred_element_type=F32)
        u = jnp.dot(xb, wu_bf[...], preferred_element_type=F32)
        h = (_silu(g) * u).astype(BF16)
        _store_packed(y_ref, jnp.dot(h, wd_bf[...], preferred_element_type=F32))

    @pl.when(nv == 0)
    def _():
        y_ref[...] = jnp.zeros_like(y_ref)


def _moe(blk_exp, blk_valid, xs, wg, wu, wd):
    n_blocks = blk_exp.shape[0]
    rows = pl.BlockSpec((MOE_ROWS * PACK_SUB, 128), lambda j, be, nv: (j, 0))
    return pl.pallas_call(
        _moe_kernel,
        grid_spec=pltpu.PrefetchScalarGridSpec(
            num_scalar_prefetch=2,
            grid=(n_blocks,),
            in_specs=[rows,
                      pl.BlockSpec((1, D_MODEL, D_EXPERT), lambda j, be, nv: (be[j], 0, 0)),
                      pl.BlockSpec((1, D_MODEL, D_EXPERT), lambda j, be, nv: (be[j], 0, 0)),
                      pl.BlockSpec((1, D_EXPERT, D_MODEL), lambda j, be, nv: (be[j], 0, 0))],
            out_specs=rows,
            scratch_shapes=[pltpu.VMEM((D_MODEL, D_EXPERT), BF16), pltpu.VMEM((D_MODEL, D_EXPERT), BF16),
                            pltpu.VMEM((D_EXPERT, D_MODEL), BF16)],
        ),
        out_shape=jax.ShapeDtypeStruct(xs.shape, U32),
        compiler_params=_cparams(("arbitrary",)),
        name="moe",
    )(blk_exp, blk_valid, xs, wg, wu, wd)


COMBINE_ROWS = 32


def _gather_rows(ys_hbm, d_ref, slot_buf, sem, t0, n):
    for j in range(n):
        t = t0 + j
        for k in range(TOP_K):
            _row_copy(ys_hbm, d_ref[t * TOP_K + k], slot_buf.at[k], t, sem).start(priority=k % 2)


def _final_kernel(d_ref, dnext_ref, x1_ref, w_ref, p_ref, wsg_ref, wsu_ref, wsd_ref, lg_ref, lb_ref,
                  wpp_ref, wpg_ref, ys_hbm, out_ref, gbuf, ybuf, sem, *, tf):
    i = pl.program_id(0)
    slot = i % 2

    @pl.when(i == 0)
    def _():
        def first(g, carry):
            _gather_rows(ys_hbm, d_ref, gbuf.at[0], sem.at[0], g * TOKEN_GROUP, TOKEN_GROUP)
            return carry

        lax.fori_loop(0, tf // TOKEN_GROUP, first, 0)

    x1 = x1_ref[...]
    xb = x1.astype(BF16)
    hs = _silu(jnp.dot(xb, wsg_ref[...], preferred_element_type=F32)) * jnp.dot(xb, wsu_ref[...],
                                                                               preferred_element_type=F32)
    y = jnp.dot(hs.astype(BF16), wsd_ref[...], preferred_element_type=F32)

    _wait_bytes_of(gbuf.at[slot], sem.at[slot])

    def combine(sb, carry):
        r0 = pl.multiple_of(sb * COMBINE_ROWS, COMBINE_ROWS)
        _gather_rows(ys_hbm, dnext_ref, gbuf.at[1 - slot], sem.at[1 - slot], r0, COMBINE_ROWS)
        y_hi = jnp.zeros((COMBINE_ROWS, PACK_W), F32)
        y_lo = jnp.zeros((COMBINE_ROWS, PACK_W), F32)
        for k in range(TOP_K):
            rows = gbuf.at[slot, k, pl.ds(pl.multiple_of(r0 * PACK_SUB, COMBINE_ROWS * PACK_SUB),
                                          COMBINE_ROWS * PACK_SUB)]
            hi, lo = _load_packed(rows, COMBINE_ROWS)
            wk = w_ref[pl.ds(r0, COMBINE_ROWS), k:k + 1]
            y_hi = y_hi + wk * hi
            y_lo = y_lo + wk * lo
        ybuf[pl.ds(r0, COMBINE_ROWS), 0:PACK_W] = y_hi
        ybuf[pl.ds(r0, COMBINE_ROWS), PACK_W:] = y_lo
        return carry

    lax.fori_loop(0, tf // COMBINE_ROWS, combine, 0)

    @pl.when(i == pl.num_programs(0) - 1)
    def _():
        _wait_bytes_of(gbuf.at[1 - slot], sem.at[1 - slot])

    y = y + ybuf[...]
    x2 = _layer_norm(DN_ALPHA * x1 + y, lg_ref[...], lb_ref[...])
    gate = jax.nn.sigmoid(jnp.dot(x2.astype(BF16), wpg_ref[...], preferred_element_type=F32))
    out_ref[...] = x2 + gate * jnp.dot(p_ref[...].astype(BF16), wpp_ref[...], preferred_element_type=F32)


def _final(d8, x1, w_tok, p2d, wsg, wsu, wsd, lg, lb, wpp, wpg, ys, tf):
    t = x1.shape[0]
    n_tiles = t // tf
    row = lambda w: pl.BlockSpec((tf, w), lambda i: (i, 0))
    full = lambda a: pl.BlockSpec(a.shape, lambda i: (0,) * a.ndim)
    return pl.pallas_call(
        functools.partial(_final_kernel, tf=tf),
        grid=(n_tiles,),
        in_specs=[pl.BlockSpec((tf * TOP_K,), lambda i: (i,), memory_space=pltpu.SMEM),
                  pl.BlockSpec((tf * TOP_K,), lambda i: (jnp.minimum(i + 1, n_tiles - 1),), memory_space=pltpu.SMEM),
                  row(D_MODEL), row(TOP_K), row(D_PLE),
                  full(wsg), full(wsu), full(wsd), full(lg), full(lb), full(wpp), full(wpg),
                  pl.BlockSpec(memory_space=pl.ANY)],
        out_specs=row(D_MODEL),
        scratch_shapes=[pltpu.VMEM((2, TOP_K, tf * PACK_SUB, 128), U32), pltpu.VMEM((tf, D_MODEL), F32),
                        pltpu.SemaphoreType.DMA((2,))],
        out_shape=jax.ShapeDtypeStruct((t, D_MODEL), F32),
        compiler_params=_cparams(("arbitrary",)),
        name="final",
    )(d8, d8, x1, w_tok, p2d, wsg, wsu, wsd, lg, lb, wpp, wpg, ys)


def _t5_bucket(rel):
    nb = N_BUCKETS // 2
    max_exact = nb // 2
    ret = jnp.where(rel > 0, nb, 0)
    n = jnp.abs(rel)
    nf = jnp.maximum(n, 1).astype(F32)
    large = max_exact + (jnp.log(nf / max_exact) / math.log(MAX_DIST / max_exact) * (nb - max_exact)).astype(I32)
    large = jnp.minimum(large, nb - 1)
    return ret + jnp.where(n < max_exact, n, large)


def _rel_bias(table, q_len, k_len):
    rel = (jnp.arange(k_len, dtype=I32)[None, :] - WINDOW) - jnp.arange(q_len, dtype=I32)[:, None]
    b = table[_t5_bucket(rel)]
    return jnp.transpose(b, (2, 0, 1)).reshape(N_KV_A, GROUP_A * q_len, k_len).astype(F32)


def _rotary_tables(pos):
    half = DK_R // 2
    inv = ROPE_BASE ** (-jnp.arange(half, dtype=F32) / half)
    ang = pos.astype(F32)[:, None] * inv[None, :]
    cos, sin = jnp.cos(ang), jnp.sin(ang)
    return jnp.concatenate([cos, cos], -1), jnp.concatenate([-sin, sin], -1)


def _decay_tables(c):
    log_gamma = jnp.log1p(-jnp.exp2(-5.0 - jnp.arange(N_HEADS_R, dtype=F32)))
    idx = jnp.arange(c, dtype=F32)
    di = jnp.exp(jnp.abs(idx[:, None] - idx[None, :])[None] * log_gamma[:, None, None])
    qd = jnp.exp((idx[None, :] + 1.0) * log_gamma[:, None])
    kd = jnp.exp((c - 1.0 - idx[None, :]) * log_gamma[:, None])
    gc = jnp.exp(c * log_gamma)
    bc = lambda a: jnp.broadcast_to(a[:, :, None], (N_HEADS_R, c, DK_R))
    return di, bc(qd), bc(kd), jnp.broadcast_to(gc[:, None, None], (N_HEADS_R, DK_R, DK_R))


def _pad_heads(w, n_heads):
    rows = w.shape[0]
    w = w.reshape(rows, n_heads, HEAD_DIM_A)
    w = jnp.pad(w, ((0, 0), (0, 0), (0, HEAD_PAD - HEAD_DIM_A)))
    return w.reshape(rows, n_heads * HEAD_PAD)


def _moe_ffn(x1, xp, st, p2d, wts):
    t = x1.shape[0]
    tile = min(256, t)
    e8, w8, r8, counts = _route(st, wts['router_bias'], tile)
    counts = counts[:, 0]
    padded = (counts + MOE_ROWS - 1) // MOE_ROWS * MOE_ROWS
    pad_end = jnp.cumsum(padded)
    pstart = (pad_end - padded).astype(I32)
    n_rows = t * TOP_K + N_EXPERTS * MOE_ROWS
    blk_start = jnp.arange(n_rows // MOE_ROWS, dtype=I32) * MOE_ROWS
    blk_exp = jnp.minimum(jnp.sum(blk_start[:, None] >= pad_end[None, :], axis=1), N_EXPERTS - 1).astype(I32)
    own = blk_exp[:, None] == jnp.arange(N_EXPERTS, dtype=I32)[None, :]
    blk_end = jnp.sum(jnp.where(own, (pstart + counts)[None, :], 0), axis=1)
    blk_valid = jnp.clip(blk_end - blk_start, 0, MOE_ROWS).astype(I32)
    d8 = _dest(e8, r8, pstart.astype(F32).reshape(N_EXPERTS, 1), tile).T.reshape(-1)
    xs = _dispatch(d8, xp, n_rows, tile)
    ys = _moe(blk_exp, blk_valid, xs, wts['w_exp_gate'], wts['w_exp_up'], wts['w_exp_down'])
    return _final(d8, x1, w8.T, p2d, wts['w_sh_gate'], wts['w_sh_up'], wts['w_sh_down'],
                  wts['ln2_g'], wts['ln2_b'], wts['w_ple_proj'], wts['w_ple_gate'], ys, tile)


def _post_mixers(x2d, oa, orr, gm, p2d, wts):
    t = x2d.shape[0]
    x1, st, xp = _mix(x2d, oa, orr, gm, wts['w_branch_attn'], wts['w_branch_ret'], wts['w_out'],
                      wts['ln1_g'], wts['ln1_b'], wts['w_router_t'], min(512, t))
    return _moe_ffn(x1, xp, st, p2d, wts)


def kernel(x_prompt, x_sample, cache_attn_k, cache_attn_v, state_retention, p_prompt, p_sample,
           w_in, attn_sinks, rel_bias_table, ret_gn_gain, w_branch_attn, w_branch_ret, w_out,
           ln1_g, ln1_b, w_router, router_bias, w_exp_gate, w_exp_up, w_exp_down,
           w_sh_gate, w_sh_up, w_sh_down, ln2_g, ln2_b, w_ple_proj, w_ple_gate):
    b, s, _ = x_prompt.shape
    bd, l, _ = x_sample.shape
    i = 0
    row = lambda a: a.reshape(1, -1).astype(F32)
    wts = dict(
        w_branch_attn=_pad_heads(w_branch_attn[i].astype(BF16).T, N_HEADS_A).T,
        w_branch_ret=w_branch_ret[i].astype(BF16),
        w_out=w_out[i].astype(BF16), ln1_g=row(ln1_g[i]), ln1_b=row(ln1_b[i]),
        w_router_t=w_router[i].T.astype(BF16), router_bias=router_bias[i].reshape(N_EXPERTS, 1).astype(F32),
        w_exp_gate=w_exp_gate[i], w_exp_up=w_exp_up[i], w_exp_down=w_exp_down[i],
        w_sh_gate=w_sh_gate[i].astype(BF16), w_sh_up=w_sh_up[i].astype(BF16), w_sh_down=w_sh_down[i].astype(BF16),
        ln2_g=row(ln2_g[i]), ln2_b=row(ln2_b[i]),
        w_ple_proj=w_ple_proj[i].astype(BF16), w_ple_gate=w_ple_gate[i].astype(BF16))
    w_in_bf = w_in[i].astype(BF16)
    w_in_bf = jnp.concatenate([_pad_heads(w_in_bf[:, :QA_W] * (HEAD_DIM_A ** -0.5), N_HEADS_A),
                               _pad_heads(w_in_bf[:, OFF_KV:OFF_QR], 2 * N_KV_A), w_in_bf[:, OFF_KV:]], axis=1)
    gain = row(ret_gn_gain[i])
    sinks = attn_sinks[i].astype(F32).reshape(N_KV_A, GROUP_A)

    xp = x_prompt.reshape(b * s, D_MODEL)
    cos_p, sin_p = _rotary_tables(jnp.arange(s, dtype=I32))
    qkv, kvf, qkr, vg, gm = _inproj(xp, w_in_bf, cos_p, sin_p, min(512, s))
    bias_p = jnp.pad(_rel_bias(rel_bias_table, CHUNK, WINDOW + CHUNK), ((0, 0), (0, 0), (0, KEY_PAD - WINDOW - CHUNK)))
    sink_p = jnp.repeat(sinks, CHUNK, axis=1)[..., None]
    fill_p = jnp.where(jnp.arange(KEY_PAD)[None, None, :] == WINDOW + CHUNK, sink_p, NEG_INF).astype(F32)
    oa = _attn_prompt(qkv, bias_p, fill_p, b, s)
    s0 = jnp.zeros((b, N_HEADS_R, DK_R, DK_R), F32)
    orr, ns_p = _retention(qkr, vg, s0, _decay_tables(CHUNK), gain, b, s, CHUNK)
    y_p = _post_mixers(xp, oa, orr, gm, p_prompt[i].reshape(b * s, D_PLE), wts).reshape(b, s, D_MODEL)
    kv_tail = kvf.reshape(b, s, 2, N_KV_A, HEAD_DIM_A)[:, s - WINDOW:]
    nk_p, nv_p = kv_tail[:, :, 0], kv_tail[:, :, 1]

    xs = x_sample.reshape(bd * l, D_MODEL)
    cos_s, sin_s = _rotary_tables(jnp.tile(PAST_LEN + jnp.arange(l, dtype=I32), bd))
    qkv, kvf, qkr, vg, gm = _inproj(xs, w_in_bf, cos_s, sin_s, bd * l)
    sink_s = jnp.repeat(sinks, l, axis=1)[..., None]
    oa, nk_s, nv_s = _attn_sample(qkv, kvf, cache_attn_k[i].reshape(bd, WINDOW, KA_W),
                                  cache_attn_v[i].reshape(bd, WINDOW, KA_W),
                                  _rel_bias(rel_bias_table, l, WINDOW + l), sink_s, bd, l)
    orr, ns_s = _retention(qkr, vg, state_retention[i].astype(F32), _decay_tables(l), gain, bd, l, l)
    y_s = _post_mixers(xs, oa, orr, gm, p_sample[i].reshape(bd * l, D_PLE), wts).reshape(bd, l, D_MODEL)
    shape_kv = (bd, WINDOW, N_KV_A, HEAD_DIM_A)
    return (y_p, y_s, nk_p[None], nv_p[None], ns_p[None],
            nk_s.reshape(shape_kv)[None], nv_s.reshape(shape_kv)[None], ns_s[None])
```

```python
import functools
import math

import jax
import jax.numpy as jnp
from jax import lax
from jax.experimental import pallas as pl
from jax.experimental.pallas import tpu as pltpu

F32 = jnp.float32
BF16 = jnp.bfloat16
I32 = jnp.int32

D_MODEL = 1024
CHUNK = 64
WINDOW = 128
N_HEADS_A = 8
N_KV_A = 2
GROUP_A = 4
HEAD_DIM_A = 64
N_BUCKETS = 32
MAX_DIST = 128
N_HEADS_R = 4
DK_R = 128
ROPE_BASE = 10000.0
N_EXPERTS = 256
TOP_K = 8
N_GROUPS = 8
GROUP_SIZE = N_EXPERTS // N_GROUPS
TOPK_GROUPS = 4
D_EXPERT = 256
ROUTED_SCALE = 2.5
D_PLE = 256
LN_EPS = 1e-5
NEG_INF = -1e30
PAST_LEN = 2048
DEPTH = 1
DN_ALPHA = (2 * DEPTH) ** 0.25

QA_W = N_HEADS_A * HEAD_DIM_A
KA_W = N_KV_A * HEAD_DIM_A
QR_W = N_HEADS_R * DK_R
OFF_KV = QA_W
OFF_QR = QA_W + 2 * KA_W
OFF_VR = OFF_QR + 2 * QR_W
OFF_GM = OFF_VR + 2 * QR_W
IN_W = OFF_GM + 2 * D_MODEL

HEAD_PAD = 128
PQ_W = N_HEADS_A * HEAD_PAD
PKV_W = 2 * N_KV_A * HEAD_PAD
A_KVF = PQ_W + PKV_W
A_QR = A_KVF + 2 * KA_W
A_VR = A_QR + 2 * QR_W
A_GM = A_VR + 2 * QR_W
A_W = A_GM + 2 * D_MODEL

MOE_ROWS = 512
VMEM_LIMIT = 56 * 1024 * 1024


def _cparams(sem, vmem=VMEM_LIMIT):
    return pltpu.CompilerParams(dimension_semantics=sem, vmem_limit_bytes=vmem)


def _layer_norm(h, g, b):
    mu = jnp.mean(h, axis=-1, keepdims=True)
    d = h - mu
    var = jnp.mean(d * d, axis=-1, keepdims=True)
    return d * lax.rsqrt(var + LN_EPS) * g + b


def _silu(x):
    return x * jax.nn.sigmoid(x)


U32 = jnp.uint32
PACK_W = D_MODEL // 2
PACK_SUB = PACK_W // 128


def _bf16_bits(x):
    return pltpu.bitcast(x.astype(BF16).astype(F32), U32)


def _store_packed(ref, x):
    n = x.shape[0]
    words = _bf16_bits(x[:, :PACK_W]) | (_bf16_bits(x[:, PACK_W:]) >> 16)
    for c in range(PACK_SUB):
        ref[pl.ds(c, n, stride=PACK_SUB), :] = words[:, c * 128:(c + 1) * 128]


def _load_packed(ref, n):
    words = jnp.concatenate([ref[pl.ds(c, n, stride=PACK_SUB), :] for c in range(PACK_SUB)], axis=1)
    hi = pltpu.bitcast(words & jnp.uint32(0xFFFF0000), F32)
    lo = pltpu.bitcast(words << 16, F32)
    return hi, lo


def _inproj_kernel(x_ref, w_ref, cos_ref, sin_ref, qkv_ref, kvf_ref, qkr_ref, vg_ref, gm_ref):
    xb = x_ref[...].astype(BF16)

    def mm(lo, hi):
        return jnp.dot(xb, w_ref[:, lo:hi], preferred_element_type=F32)

    qkv_ref[:, 0:PQ_W] = mm(0, PQ_W).astype(BF16)
    qkv_ref[:, PQ_W:A_KVF] = mm(PQ_W, A_KVF).astype(BF16)
    kvf_ref[...] = mm(A_KVF, A_QR)
    cos = cos_ref[...]
    sin = sin_ref[...]
    for part in range(2):
        z = mm(A_QR + part * QR_W, A_QR + (part + 1) * QR_W)
        for h in range(N_HEADS_R):
            zh = z[:, h * DK_R:(h + 1) * DK_R]
            r = zh * cos + pltpu.roll(zh, DK_R // 2, axis=1) * sin
            if part == 1:
                r = r * (DK_R ** -0.5)
            c0 = part * QR_W + h * DK_R
            qkr_ref[:, c0:c0 + DK_R] = r.astype(BF16)
    vg_ref[...] = mm(A_VR, A_GM).astype(BF16)
    for half in range(2):
        lo = A_GM + half * D_MODEL
        gm_ref[:, half * D_MODEL:(half + 1) * D_MODEL] = jax.nn.sigmoid(mm(lo, lo + D_MODEL)).astype(BF16)


def _inproj(x2d, w_bf, cos_tab, sin_tab, tm):
    t = x2d.shape[0]
    nper = cos_tab.shape[0] // tm
    row = lambda i: (i, 0)
    return pl.pallas_call(
        _inproj_kernel,
        grid=(t // tm,),
        in_specs=[
            pl.BlockSpec((tm, D_MODEL), row),
            pl.BlockSpec((D_MODEL, A_W), lambda i: (0, 0)),
            pl.BlockSpec((tm, DK_R), lambda i: (i % nper, 0)),
            pl.BlockSpec((tm, DK_R), lambda i: (i % nper, 0)),
        ],
        out_specs=[
            pl.BlockSpec((tm, A_KVF), row),
            pl.BlockSpec((tm, 2 * KA_W), row),
            pl.BlockSpec((tm, 2 * QR_W), row),
            pl.BlockSpec((tm, 2 * QR_W), row),
            pl.BlockSpec((tm, 2 * D_MODEL), row),
        ],
        out_shape=[
            jax.ShapeDtypeStruct((t, A_KVF), BF16),
            jax.ShapeDtypeStruct((t, 2 * KA_W), F32),
            jax.ShapeDtypeStruct((t, 2 * QR_W), BF16),
            jax.ShapeDtypeStruct((t, 2 * QR_W), BF16),
            jax.ShapeDtypeStruct((t, 2 * D_MODEL), BF16),
        ],
        compiler_params=_cparams(("parallel",)),
        name="inproj",
    )(x2d, w_bf, cos_tab, sin_tab)


def _attend(q4, k, v, bias, sink):
    s = lax.dot_general(q4, k, (((1,), (1,)), ((), ())), preferred_element_type=F32) + bias
    m = jnp.maximum(jnp.max(s, axis=-1, keepdims=True), sink)
    e = jnp.exp(s - m)
    p = e / (jnp.sum(e, axis=-1, keepdims=True) + jnp.exp(sink - m))
    return jnp.dot(p.astype(BF16), v, preferred_element_type=F32)


KEY_PAD = 256


def _attend_sink_column(q4, k, v, bias, fill, valid):
    s = lax.dot_general(q4, k, (((1,), (1,)), ((), ())), preferred_element_type=F32)
    s = jnp.where(valid, s + bias, fill)
    e = jnp.exp(s - jnp.max(s, axis=-1, keepdims=True)).astype(BF16)
    den = jnp.dot(e, jnp.ones((KEY_PAD, v.shape[1]), BF16), preferred_element_type=F32)
    return jnp.dot(e[:, 0:WINDOW + CHUNK], v, preferred_element_type=F32) / den


def _attn_prompt_kernel(q_ref, kvc_ref, kvp_ref, bias_ref, fill_ref, o_ref, kv_buf, *, n_chunks):
    i = pl.program_id(1)
    qb = n_chunks * CHUNK
    kv_buf[0:WINDOW, :] = kvp_ref[...]
    kv_buf[WINDOW:WINDOW + qb, :] = kvc_ref[...]
    kv_buf[WINDOW + qb:, :] = jnp.zeros((KEY_PAD - WINDOW - CHUNK, PKV_W), BF16)
    col = lax.broadcasted_iota(I32, (GROUP_A * CHUNK, KEY_PAD), 1)
    key_col = jnp.where(col < WINDOW + CHUNK, col, -1)

    def chunk(j, carry):
        r0 = pl.multiple_of(j * CHUNK, CHUNK)
        valid = key_col >= jnp.maximum(WINDOW - (i * n_chunks + j) * CHUNK, 0)
        for kv in range(N_KV_A):
            q4 = jnp.concatenate(
                [q_ref[pl.ds(r0, CHUNK), (kv * GROUP_A + g) * HEAD_PAD:(kv * GROUP_A + g + 1) * HEAD_PAD]
                 for g in range(GROUP_A)], axis=0)
            k = kv_buf[pl.ds(r0, KEY_PAD), kv * HEAD_PAD:(kv + 1) * HEAD_PAD]
            v = kv_buf[pl.ds(r0, WINDOW + CHUNK), (N_KV_A + kv) * HEAD_PAD:(N_KV_A + kv + 1) * HEAD_PAD]
            o4 = _attend_sink_column(q4, k, v, bias_ref[kv], fill_ref[kv], valid)
            for g in range(GROUP_A):
                c0 = (kv * GROUP_A + g) * HEAD_PAD
                o_ref[pl.ds(r0, CHUNK), c0:c0 + HEAD_PAD] = o4[g * CHUNK:(g + 1) * CHUNK].astype(BF16)
        return carry

    lax.fori_loop(0, n_chunks, chunk, 0, unroll=2 if n_chunks % 2 == 0 else 1)


def _attn_prompt(qkv, bias, fill, b, s):
    qb = min(512, s)
    n_chunks = qb // CHUNK
    nq = s // qb
    per = qb // WINDOW
    kv_col = PQ_W // PKV_W
    return pl.pallas_call(
        functools.partial(_attn_prompt_kernel, n_chunks=n_chunks),
        grid=(b, nq),
        in_specs=[
            pl.BlockSpec((qb, PQ_W), lambda bi, i: (bi * nq + i, 0)),
            pl.BlockSpec((qb, PKV_W), lambda bi, i: (bi * nq + i, kv_col)),
            pl.BlockSpec((WINDOW, PKV_W), lambda bi, i: (jnp.maximum((bi * nq + i) * per - 1, 0), kv_col)),
            pl.BlockSpec((N_KV_A, GROUP_A * CHUNK, KEY_PAD), lambda bi, i: (0, 0, 0)),
            pl.BlockSpec((N_KV_A, GROUP_A * CHUNK, KEY_PAD), lambda bi, i: (0, 0, 0)),
        ],
        out_specs=pl.BlockSpec((qb, PQ_W), lambda bi, i: (bi * nq + i, 0)),
        out_shape=jax.ShapeDtypeStruct((b * s, PQ_W), BF16),
        scratch_shapes=[pltpu.VMEM((qb + KEY_PAD - CHUNK, PKV_W), BF16)],
        compiler_params=_cparams(("parallel", "parallel")),
        name="attn_prompt",
    )(qkv, qkv, qkv, bias, fill)


def _attn_sample_kernel(q_ref, kvf_ref, ck_ref, cv_ref, bias_ref, sink_ref, o_ref, nk_ref, nv_ref, *, l):
    k_all = jnp.concatenate([ck_ref[0], kvf_ref[:, 0:KA_W]], axis=0)
    v_all = jnp.concatenate([cv_ref[0], kvf_ref[:, KA_W:2 * KA_W]], axis=0)
    nk_ref[0] = k_all[l:]
    nv_ref[0] = v_all[l:]
    kb = k_all.astype(BF16)
    vb = v_all.astype(BF16)
    o_ref[...] = jnp.zeros_like(o_ref)
    for kv in range(N_KV_A):
        q4 = jnp.concatenate(
            [q_ref[:, (kv * GROUP_A + g) * HEAD_PAD:(kv * GROUP_A + g) * HEAD_PAD + HEAD_DIM_A]
             for g in range(GROUP_A)], axis=0)
        o4 = _attend(q4, kb[:, kv * HEAD_DIM_A:(kv + 1) * HEAD_DIM_A], vb[:, kv * HEAD_DIM_A:(kv + 1) * HEAD_DIM_A],
                     bias_ref[kv], sink_ref[kv])
        for g in range(GROUP_A):
            c0 = (kv * GROUP_A + g) * HEAD_PAD
            o_ref[:, c0:c0 + HEAD_DIM_A] = o4[g * l:(g + 1) * l].astype(BF16)


def _attn_sample(qkv, kvf, cache_k, cache_v, bias, sink, bd, l):
    cache_spec = pl.BlockSpec((1, WINDOW, KA_W), lambda bi: (bi, 0, 0))
    cache_shape = jax.ShapeDtypeStruct((bd, WINDOW, KA_W), F32)
    return pl.pallas_call(
        functools.partial(_attn_sample_kernel, l=l),
        grid=(bd,),
        in_specs=[
            pl.BlockSpec((l, A_KVF), lambda bi: (bi, 0)),
            pl.BlockSpec((l, 2 * KA_W), lambda bi: (bi, 0)),
            cache_spec, cache_spec,
            pl.BlockSpec((N_KV_A, GROUP_A * l, WINDOW + l), lambda bi: (0, 0, 0)),
            pl.BlockSpec((N_KV_A, GROUP_A * l, 1), lambda bi: (0, 0, 0)),
        ],
        out_specs=[pl.BlockSpec((l, PQ_W), lambda bi: (bi, 0)), cache_spec, cache_spec],
        out_shape=[jax.ShapeDtypeStruct((bd * l, PQ_W), BF16), cache_shape, cache_shape],
        compiler_params=_cparams(("parallel",)),
        name="attn_sample",
    )(qkv, kvf, cache_k, cache_v, bias, sink)


def _retention_kernel(q_ref, k_ref, v_ref, g_ref, s0_ref, di_ref, qd_ref, kd_ref, gc_ref, gain_ref,
                      o_ref, s_ref, *, n_chunks, c):
    @pl.when(pl.program_id(1) == 0)
    def _():
        s_ref[...] = s0_ref[...]

    def chunk(n, carry):
        r0 = pl.multiple_of(n * c, c)
        for h in range(N_HEADS_R):
            cols = slice(h * DK_R, (h + 1) * DK_R)
            qc = q_ref[pl.ds(r0, c), cols]
            kc = k_ref[pl.ds(r0, c), cols]
            vc = v_ref[pl.ds(r0, c), cols]
            state = s_ref[0, h]
            sc = lax.dot_general(qc, kc, (((1,), (1,)), ((), ())), preferred_element_type=F32) * di_ref[h]
            o = jnp.dot(sc.astype(BF16), vc, preferred_element_type=F32)
            q_dec = (qc.astype(F32) * qd_ref[h]).astype(BF16)
            o = o + jnp.dot(q_dec, state.astype(BF16), preferred_element_type=F32)
            k_dec = (kc.astype(F32) * kd_ref[h]).astype(BF16)
            s_ref[0, h] = gc_ref[h] * state + lax.dot_general(k_dec, vc, (((0,), (0,)), ((), ())),
                                                              preferred_element_type=F32)
            mu = jnp.mean(o, axis=-1, keepdims=True)
            d = o - mu
            var = jnp.mean(d * d, axis=-1, keepdims=True)
            y = d * lax.rsqrt(var + LN_EPS) * gain_ref[:, cols]
            o_ref[pl.ds(r0, c), cols] = (y * _silu(g_ref[pl.ds(r0, c), cols].astype(F32))).astype(BF16)
        return carry

    lax.fori_loop(0, n_chunks, chunk, 0, unroll=4 if n_chunks % 4 == 0 else 1)


def _retention(qkr, vg, s0, tabs, gain, b, s, c):
    di, qd, kd, gc = tabs
    st = min(1024, s)
    nt = s // st
    seq = lambda col: pl.BlockSpec((st, QR_W), lambda bi, j: (bi * nt + j, col))
    full = lambda a: pl.BlockSpec(a.shape, lambda bi, j: (0,) * a.ndim)
    st_spec = pl.BlockSpec((1, N_HEADS_R, DK_R, DK_R), lambda bi, j: (bi, 0, 0, 0))
    return pl.pallas_call(
        functools.partial(_retention_kernel, n_chunks=st // c, c=c),
        grid=(b, nt),
        in_specs=[seq(0), seq(1), seq(0), seq(1), st_spec, full(di), full(qd), full(kd), full(gc), full(gain)],
        out_specs=[seq(0), st_spec],
        out_shape=[jax.ShapeDtypeStruct((b * s, QR_W), BF16), jax.ShapeDtypeStruct((b, N_HEADS_R, DK_R, DK_R), F32)],
        compiler_params=_cparams(("parallel", "arbitrary")),
        name="retention",
    )(qkr, qkr, vg, vg, s0, di, qd, kd, gc, gain)


def _mix_kernel(x_ref, oa_ref, or_ref, g_ref, wa_ref, wb_ref, wo_ref, lg_ref, lb_ref, wrt_ref, x1_ref, st_ref, xp_ref):
    a = jnp.dot(oa_ref[...], wa_ref[...], preferred_element_type=F32)
    b = jnp.dot(or_ref[...], wb_ref[...], preferred_element_type=F32)
    merged = g_ref[:, 0:D_MODEL].astype(F32) * a + g_ref[:, D_MODEL:].astype(F32) * b
    y = jnp.dot(merged.astype(BF16), wo_ref[...], preferred_element_type=F32)
    x1 = _layer_norm(DN_ALPHA * x_ref[...] + y, lg_ref[...], lb_ref[...])
    x1_ref[...] = x1
    _store_packed(xp_ref, x1)
    logits = lax.dot_general(wrt_ref[...], x1.astype(BF16), (((1,), (1,)), ((), ())), preferred_element_type=F32)
    st_ref[...] = jax.nn.sigmoid(logits)


def _mix(x2d, oa, orr, gm, wa, wb, wo, lg, lb, wrt, tm):
    t = x2d.shape[0]
    row = lambda w: pl.BlockSpec((tm, w), lambda i: (i, 0))
    full = lambda a: pl.BlockSpec(a.shape, lambda i: (0,) * a.ndim)
    return pl.pallas_call(
        _mix_kernel,
        grid=(t // tm,),
        in_specs=[row(D_MODEL), row(PQ_W), row(QR_W), row(2 * D_MODEL),
                  full(wa), full(wb), full(wo), full(lg), full(lb), full(wrt)],
        out_specs=[row(D_MODEL), pl.BlockSpec((N_EXPERTS, tm), lambda i: (0, i)),
                   pl.BlockSpec((tm * PACK_SUB, 128), lambda i: (i, 0))],
        out_shape=[jax.ShapeDtypeStruct((t, D_MODEL), F32), jax.ShapeDtypeStruct((N_EXPERTS, t), F32),
                   jax.ShapeDtypeStruct((t * PACK_SUB, 128), U32)],
        compiler_params=_cparams(("parallel",)),
        name="mix",
    )(x2d, oa, orr, gm, wa, wb, wo, lg, lb, wrt)


def _route_kernel(s_ref, bias_ref, cnt0_ref, e_ref, w_ref, r_ref, cnt_ref, carry_ref):
    @pl.when(pl.program_id(0) == 0)
    def _():
        carry_ref[...] = cnt0_ref[...].astype(F32)

    s = s_ref[...]
    tl = s.shape[1]
    choice = s + bias_ref[...]
    row = lax.broadcasted_iota(I32, (N_EXPERTS, tl), 0)
    row_g = lax.broadcasted_iota(I32, (GROUP_SIZE, tl), 0)
    neg = -jnp.inf

    scores = []
    for g in range(N_GROUPS):
        blk = choice[g * GROUP_SIZE:(g + 1) * GROUP_SIZE]
        m1 = jnp.max(blk, axis=0, keepdims=True)
        i1 = jnp.min(jnp.where(blk == m1, row_g, GROUP_SIZE), axis=0, keepdims=True)
        m2 = jnp.max(jnp.where(row_g == i1, neg, blk), axis=0, keepdims=True)
        scores.append(m1 + m2)
    sc = jnp.concatenate(scores, axis=0)
    gi = lax.broadcasted_iota(I32, sc.shape, 0)
    grank = jnp.zeros(sc.shape, I32)
    for g in range(N_GROUPS):
        other = sc[g:g + 1]
        ahead = jnp.where(other > sc, 1, jnp.where(other == sc, jnp.where(gi > g, 1, 0), 0))
        grank = grank + ahead
    cm = jnp.concatenate(
        [jnp.where(grank[g:g + 1] < TOPK_GROUPS, choice[g * GROUP_SIZE:(g + 1) * GROUP_SIZE], neg)
         for g in range(N_GROUPS)], axis=0)

    experts, weights = [], []
    for _ in range(TOP_K):
        m = jnp.max(cm, axis=0, keepdims=True)
        idx = jnp.min(jnp.where(cm == m, row, N_EXPERTS), axis=0, keepdims=True)
        hit = row == idx
        weights.append(jnp.sum(jnp.where(hit, s, 0.0), axis=0, keepdims=True))
        cm = jnp.where(hit, neg, cm)
        experts.append(idx)
    e8 = jnp.concatenate(experts, axis=0)
    w8 = jnp.concatenate(weights, axis=0)
    e_ref[...] = e8
    w_ref[...] = w8 / jnp.sum(w8, axis=0, keepdims=True) * ROUTED_SCALE

    before = (lax.broadcasted_iota(I32, (tl, tl), 0) < lax.broadcasted_iota(I32, (tl, tl), 1))
    before = jnp.where(before, 1.0, 0.0).astype(BF16)
    carry = carry_ref[...]
    ranks = []
    for k in range(TOP_K):
        hit = row == experts[k]
        onehot = jnp.where(hit, 1.0, 0.0)
        prefix = jnp.dot(onehot.astype(BF16), before, preferred_element_type=F32)
        ranks.append(jnp.sum(jnp.where(hit, prefix + carry, 0.0), axis=0, keepdims=True))
        carry = carry + jnp.sum(onehot, axis=1, keepdims=True)
    r_ref[...] = jnp.concatenate(ranks, axis=0).astype(I32)
    carry_ref[...] = carry
    cnt_ref[...] = carry.astype(I32)


def _route(st, bias_col, counts0, tl):
    t = st.shape[1]
    tok = pl.BlockSpec((TOP_K, tl), lambda i: (0, i))
    return pl.pallas_call(
        _route_kernel,
        grid=(t // tl,),
        in_specs=[pl.BlockSpec((N_EXPERTS, tl), lambda i: (0, i)), pl.BlockSpec((N_EXPERTS, 1), lambda i: (0, 0)),
                  pl.BlockSpec((N_EXPERTS, 1), lambda i: (0, 0))],
        out_specs=[tok, tok, tok, pl.BlockSpec((N_EXPERTS, 1), lambda i: (0, 0))],
        out_shape=[jax.ShapeDtypeStruct((TOP_K, t), I32), jax.ShapeDtypeStruct((TOP_K, t), F32),
                   jax.ShapeDtypeStruct((TOP_K, t), I32), jax.ShapeDtypeStruct((N_EXPERTS, 1), I32)],
        scratch_shapes=[pltpu.VMEM((N_EXPERTS, 1), F32)],
        compiler_params=_cparams(("arbitrary",)),
        name="route",
    )(st, bias_col, counts0)


def _dest_kernel(e_ref, r_ref, ps_ref, d_ref):
    tl = e_ref.shape[1]
    row = lax.broadcasted_iota(I32, (N_EXPERTS, tl), 0)
    ps = ps_ref[...]
    starts = [jnp.sum(jnp.where(row == e_ref[k:k + 1, :], ps, 0.0), axis=0, keepdims=True) for k in range(TOP_K)]
    d_ref[...] = jnp.concatenate(starts, axis=0).astype(I32) + r_ref[...]


def _dest(e8, r8, pstart_col, tl):
    t = e8.shape[1]
    tok = pl.BlockSpec((TOP_K, tl), lambda i: (0, i))
    return pl.pallas_call(
        _dest_kernel,
        grid=(t // tl,),
        in_specs=[tok, tok, pl.BlockSpec((N_EXPERTS, 1), lambda i: (0, 0))],
        out_specs=tok,
        out_shape=jax.ShapeDtypeStruct((TOP_K, t), I32),
        compiler_params=_cparams(("parallel",)),
        name="dest",
    )(e8, r8, pstart_col)


TOKEN_GROUP = 8


def _row_copy(src, src_tok, dst, dst_tok, sem):
    s0 = pl.multiple_of(src_tok * PACK_SUB, PACK_SUB)
    d0 = pl.multiple_of(dst_tok * PACK_SUB, PACK_SUB)
    return pltpu.make_async_copy(src.at[pl.ds(s0, PACK_SUB)], dst.at[pl.ds(d0, PACK_SUB)], sem)


def _wait_bytes_of(ref, sem):
    pltpu.make_async_copy(ref, ref, sem).wait()


def _dispatch_kernel(d_ref, x_ref, *refs, td):
    xs_hbm, sem = refs[-2:]
    def start(g, carry):
        for j in range(TOKEN_GROUP):
            t = g * TOKEN_GROUP + j
            for k in range(TOP_K):
                _row_copy(x_ref, t, xs_hbm, d_ref[t * TOP_K + k], sem).start(priority=k % 2)
        return carry

    lax.fori_loop(0, td // TOKEN_GROUP, start, 0)
    for k in range(TOP_K):
        _wait_bytes_of(x_ref, sem)


def _dispatch(d8, xp, n_rows, td, xs_prev=None):
    t = d8.shape[0] // TOP_K
    in_specs = [pl.BlockSpec((td * TOP_K,), lambda i: (i,), memory_space=pltpu.SMEM),
                pl.BlockSpec((td * PACK_SUB, 128), lambda i: (i, 0))]
    args = [d8, xp]
    if xs_prev is not None:
        in_specs.append(pl.BlockSpec(memory_space=pl.ANY))
        args.append(xs_prev)
    return pl.pallas_call(
        functools.partial(_dispatch_kernel, td=td),
        grid=(t // td,),
        in_specs=in_specs,
        out_specs=pl.BlockSpec(memory_space=pl.ANY),
        scratch_shapes=[pltpu.SemaphoreType.DMA(())],
        out_shape=jax.ShapeDtypeStruct((n_rows * PACK_SUB, 128), U32),
        input_output_aliases={} if xs_prev is None else {2: 0},
        compiler_params=_cparams(("arbitrary",)),
        name="dispatch",
    )(*args)


def _moe_kernel(be_ref, nv_ref, x_ref, wg_ref, wu_ref, wd_ref, y_ref, wg_bf, wu_bf, wd_bf):
    j = pl.program_id(0)
    nv = nv_ref[j]

    @pl.when((j == 0) | (be_ref[j] != be_ref[jnp.maximum(j - 1, 0)]))
    def _():
        wg_bf[...] = wg_ref[0].astype(BF16)
        wu_bf[...] = wu_ref[0].astype(BF16)
        wd_bf[...] = wd_ref[0].astype(BF16)

    @pl.when(nv > 0)
    def _():
        hi, lo = _load_packed(x_ref, MOE_ROWS)
        valid = lax.broadcasted_iota(I32, hi.shape, 0) < nv
        xb = jnp.concatenate([jnp.where(valid, hi, 0.0).astype(BF16), jnp.where(valid, lo, 0.0).astype(BF16)], axis=1)
        g = jnp.dot(xb, wg_bf[...], preferred_element_type=F32)
        u = jnp.dot(xb, wu_bf[...], preferred_element_type=F32)
        h = (_silu(g) * u).astype(BF16)
        _store_packed(y_ref, jnp.dot(h, wd_bf[...], preferred_element_type=F32))

    @pl.when(nv == 0)
    def _():
        y_ref[...] = jnp.zeros_like(y_ref)


def _moe(blk_exp, blk_valid, xs, wg, wu, wd):
    n_blocks = blk_exp.shape[0]
    rows = pl.BlockSpec((MOE_ROWS * PACK_SUB, 128), lambda j, be, nv: (j, 0))
    return pl.pallas_call(
        _moe_kernel,
        grid_spec=pltpu.PrefetchScalarGridSpec(
            num_scalar_prefetch=2,
            grid=(n_blocks,),
            in_specs=[rows,
                      pl.BlockSpec((1, D_MODEL, D_EXPERT), lambda j, be, nv: (be[j], 0, 0)),
                      pl.BlockSpec((1, D_MODEL, D_EXPERT), lambda j, be, nv: (be[j], 0, 0)),
                      pl.BlockSpec((1, D_EXPERT, D_MODEL), lambda j, be, nv: (be[j], 0, 0))],
            out_specs=rows,
            scratch_shapes=[pltpu.VMEM((D_MODEL, D_EXPERT), BF16), pltpu.VMEM((D_MODEL, D_EXPERT), BF16),
                            pltpu.VMEM((D_EXPERT, D_MODEL), BF16)],
        ),
        out_shape=jax.ShapeDtypeStruct(xs.shape, U32),
        compiler_params=_cparams(("arbitrary",)),
        name="moe",
    )(blk_exp, blk_valid, xs, wg, wu, wd)


COMBINE_ROWS = 32


def _gather_rows(ys_hbm, d_ref, slot_buf, sem, t0, n):
    for j in range(n):
        t = t0 + j
        for k in range(TOP_K):
            _row_copy(ys_hbm, d_ref[t * TOP_K + k], slot_buf.at[k], t, sem).start(priority=k % 2)


def _final_kernel(d_ref, dnext_ref, x1_ref, w_ref, p_ref, wsg_ref, wsu_ref, wsd_ref, lg_ref, lb_ref,
                  wpp_ref, wpg_ref, ys_hbm, out_ref, gbuf, ybuf, sem, *, tf):
    i = pl.program_id(0)
    slot = i % 2

    @pl.when(i == 0)
    def _():
        def first(g, carry):
            _gather_rows(ys_hbm, d_ref, gbuf.at[0], sem.at[0], g * TOKEN_GROUP, TOKEN_GROUP)
            return carry

        lax.fori_loop(0, tf // TOKEN_GROUP, first, 0)

    x1 = x1_ref[...]
    xb = x1.astype(BF16)
    hs = _silu(jnp.dot(xb, wsg_ref[...], preferred_element_type=F32)) * jnp.dot(xb, wsu_ref[...],
                                                                               preferred_element_type=F32)
    y = jnp.dot(hs.astype(BF16), wsd_ref[...], preferred_element_type=F32)

    _wait_bytes_of(gbuf.at[slot], sem.at[slot])

    def combine(sb, carry):
        r0 = pl.multiple_of(sb * COMBINE_ROWS, COMBINE_ROWS)
        _gather_rows(ys_hbm, dnext_ref, gbuf.at[1 - slot], sem.at[1 - slot], r0, COMBINE_ROWS)
        y_hi = jnp.zeros((COMBINE_ROWS, PACK_W), F32)
        y_lo = jnp.zeros((COMBINE_ROWS, PACK_W), F32)
        for k in range(TOP_K):
            rows = gbuf.at[slot, k, pl.ds(pl.multiple_of(r0 * PACK_SUB, COMBINE_ROWS * PACK_SUB),
                                          COMBINE_ROWS * PACK_SUB)]
            hi, lo = _load_packed(rows, COMBINE_ROWS)
            wk = w_ref[pl.ds(r0, COMBINE_ROWS), k:k + 1]
            y_hi = y_hi + wk * hi
            y_lo = y_lo + wk * lo
        ybuf[pl.ds(r0, COMBINE_ROWS), 0:PACK_W] = y_hi
        ybuf[pl.ds(r0, COMBINE_ROWS), PACK_W:] = y_lo
        return carry

    lax.fori_loop(0, tf // COMBINE_ROWS, combine, 0)

    @pl.when(i == pl.num_programs(0) - 1)
    def _():
        _wait_bytes_of(gbuf.at[1 - slot], sem.at[1 - slot])

    y = y + ybuf[...]
    x2 = _layer_norm(DN_ALPHA * x1 + y, lg_ref[...], lb_ref[...])
    gate = jax.nn.sigmoid(jnp.dot(x2.astype(BF16), wpg_ref[...], preferred_element_type=F32))
    out_ref[...] = x2 + gate * jnp.dot(p_ref[...].astype(BF16), wpp_ref[...], preferred_element_type=F32)


def _final(d8, x1, w_tok, p2d, wsg, wsu, wsd, lg, lb, wpp, wpg, ys, tf):
    t = x1.shape[0]
    n_tiles = t // tf
    row = lambda w: pl.BlockSpec((tf, w), lambda i: (i, 0))
    full = lambda a: pl.BlockSpec(a.shape, lambda i: (0,) * a.ndim)
    return pl.pallas_call(
        functools.partial(_final_kernel, tf=tf),
        grid=(n_tiles,),
        in_specs=[pl.BlockSpec((tf * TOP_K,), lambda i: (i,), memory_space=pltpu.SMEM),
                  pl.BlockSpec((tf * TOP_K,), lambda i: (jnp.minimum(i + 1, n_tiles - 1),), memory_space=pltpu.SMEM),
                  row(D_MODEL), row(TOP_K), row(D_PLE),
                  full(wsg), full(wsu), full(wsd), full(lg), full(lb), full(wpp), full(wpg),
                  pl.BlockSpec(memory_space=pl.ANY)],
        out_specs=row(D_MODEL),
        scratch_shapes=[pltpu.VMEM((2, TOP_K, tf * PACK_SUB, 128), U32), pltpu.VMEM((tf, D_MODEL), F32),
                        pltpu.SemaphoreType.DMA((2,))],
        out_shape=jax.ShapeDtypeStruct((t, D_MODEL), F32),
        compiler_params=_cparams(("arbitrary",)),
        name="final",
    )(d8, d8, x1, w_tok, p2d, wsg, wsu, wsd, lg, lb, wpp, wpg, ys)


def _t5_bucket(rel):
    nb = N_BUCKETS // 2
    max_exact = nb // 2
    ret = jnp.where(rel > 0, nb, 0)
    n = jnp.abs(rel)
    nf = jnp.maximum(n, 1).astype(F32)
    large = max_exact + (jnp.log(nf / max_exact) / math.log(MAX_DIST / max_exact) * (nb - max_exact)).astype(I32)
    large = jnp.minimum(large, nb - 1)
    return ret + jnp.where(n < max_exact, n, large)


def _rel_bias(table, q_len, k_len):
    rel = (jnp.arange(k_len, dtype=I32)[None, :] - WINDOW) - jnp.arange(q_len, dtype=I32)[:, None]
    b = table[_t5_bucket(rel)]
    return jnp.transpose(b, (2, 0, 1)).reshape(N_KV_A, GROUP_A * q_len, k_len).astype(F32)


def _rotary_tables(pos):
    half = DK_R // 2
    inv = ROPE_BASE ** (-jnp.arange(half, dtype=F32) / half)
    ang = pos.astype(F32)[:, None] * inv[None, :]
    cos, sin = jnp.cos(ang), jnp.sin(ang)
    return jnp.concatenate([cos, cos], -1), jnp.concatenate([-sin, sin], -1)


def _decay_tables(c):
    log_gamma = jnp.log1p(-jnp.exp2(-5.0 - jnp.arange(N_HEADS_R, dtype=F32)))
    idx = jnp.arange(c, dtype=F32)
    di = jnp.exp(jnp.abs(idx[:, None] - idx[None, :])[None] * log_gamma[:, None, None])
    qd = jnp.exp((idx[None, :] + 1.0) * log_gamma[:, None])
    kd = jnp.exp((c - 1.0 - idx[None, :]) * log_gamma[:, None])
    gc = jnp.exp(c * log_gamma)
    bc = lambda a: jnp.broadcast_to(a[:, :, None], (N_HEADS_R, c, DK_R))
    return di, bc(qd), bc(kd), jnp.broadcast_to(gc[:, None, None], (N_HEADS_R, DK_R, DK_R))


def _pad_heads(w, n_heads):
    rows = w.shape[0]
    w = w.reshape(rows, n_heads, HEAD_DIM_A)
    w = jnp.pad(w, ((0, 0), (0, 0), (0, HEAD_PAD - HEAD_DIM_A)))
    return w.reshape(rows, n_heads * HEAD_PAD)


def _moe_ffn(groups, wts):
    tiles = [min(256, g[0].shape[0]) for g in groups]
    counts = jnp.zeros((N_EXPERTS, 1), I32)
    routed = []
    for (x1, xp, st, p2d), tile in zip(groups, tiles):
        e8, w8, r8, counts = _route(st, wts['router_bias'], counts, tile)
        routed.append((e8, w8, r8))
    counts = counts[:, 0]
    padded = (counts + MOE_ROWS - 1) // MOE_ROWS * MOE_ROWS
    pad_end = jnp.cumsum(padded)
    pstart = (pad_end - padded).astype(I32)
    n_rows = sum(g[0].shape[0] for g in groups) * TOP_K + N_EXPERTS * MOE_ROWS
    blk_start = jnp.arange(n_rows // MOE_ROWS, dtype=I32) * MOE_ROWS
    blk_exp = jnp.minimum(jnp.sum(blk_start[:, None] >= pad_end[None, :], axis=1), N_EXPERTS - 1).astype(I32)
    own = blk_exp[:, None] == jnp.arange(N_EXPERTS, dtype=I32)[None, :]
    blk_end = jnp.sum(jnp.where(own, (pstart + counts)[None, :], 0), axis=1)
    blk_valid = jnp.clip(blk_end - blk_start, 0, MOE_ROWS).astype(I32)
    pstart_col = pstart.astype(F32).reshape(N_EXPERTS, 1)
    dests, xs = [], None
    for (x1, xp, st, p2d), (e8, w8, r8), tile in zip(groups, routed, tiles):
        d8 = _dest(e8, r8, pstart_col, tile).T.reshape(-1)
        xs = _dispatch(d8, xp, n_rows, tile, xs)
        dests.append(d8)
    ys = _moe(blk_exp, blk_valid, xs, wts['w_exp_gate'], wts['w_exp_up'], wts['w_exp_down'])
    return [_final(d8, x1, w8.T, p2d, wts['w_sh_gate'], wts['w_sh_up'], wts['w_sh_down'],
                   wts['ln2_g'], wts['ln2_b'], wts['w_ple_proj'], wts['w_ple_gate'], ys, tile)
            for (x1, xp, st, p2d), (e8, w8, r8), d8, tile in zip(groups, routed, dests, tiles)]


def _mix_group(x2d, oa, orr, gm, p2d, wts):
    x1, st, xp = _mix(x2d, oa, orr, gm, wts['w_branch_attn'], wts['w_branch_ret'], wts['w_out'],
                      wts['ln1_g'], wts['ln1_b'], wts['w_router_t'], min(512, x2d.shape[0]))
    return x1, xp, st, p2d


def kernel(x_prompt, x_sample, cache_attn_k, cache_attn_v, state_retention, p_prompt, p_sample,
           w_in, attn_sinks, rel_bias_table, ret_gn_gain, w_branch_attn, w_branch_ret, w_out,
           ln1_g, ln1_b, w_router, router_bias, w_exp_gate, w_exp_up, w_exp_down,
           w_sh_gate, w_sh_up, w_sh_down, ln2_g, ln2_b, w_ple_proj, w_ple_gate):
    b, s, _ = x_prompt.shape
    bd, l, _ = x_sample.shape
    i = 0
    row = lambda a: a.reshape(1, -1).astype(F32)
    wts = dict(
        w_branch_attn=_pad_heads(w_branch_attn[i].astype(BF16).T, N_HEADS_A).T,
        w_branch_ret=w_branch_ret[i].astype(BF16),
        w_out=w_out[i].astype(BF16), ln1_g=row(ln1_g[i]), ln1_b=row(ln1_b[i]),
        w_router_t=w_router[i].T.astype(BF16), router_bias=router_bias[i].reshape(N_EXPERTS, 1).astype(F32),
        w_exp_gate=w_exp_gate[i], w_exp_up=w_exp_up[i], w_exp_down=w_exp_down[i],
        w_sh_gate=w_sh_gate[i].astype(BF16), w_sh_up=w_sh_up[i].astype(BF16), w_sh_down=w_sh_down[i].astype(BF16),
        ln2_g=row(ln2_g[i]), ln2_b=row(ln2_b[i]),
        w_ple_proj=w_ple_proj[i].astype(BF16), w_ple_gate=w_ple_gate[i].astype(BF16))
    w_in_bf = w_in[i].astype(BF16)
    w_in_bf = jnp.concatenate([_pad_heads(w_in_bf[:, :QA_W] * (HEAD_DIM_A ** -0.5), N_HEADS_A),
                               _pad_heads(w_in_bf[:, OFF_KV:OFF_QR], 2 * N_KV_A), w_in_bf[:, OFF_KV:]], axis=1)
    gain = row(ret_gn_gain[i])
    sinks = attn_sinks[i].astype(F32).reshape(N_KV_A, GROUP_A)

    xp = x_prompt.reshape(b * s, D_MODEL)
    cos_p, sin_p = _rotary_tables(jnp.arange(s, dtype=I32))
    qkv, kvf, qkr, vg, gm = _inproj(xp, w_in_bf, cos_p, sin_p, min(512, s))
    bias_p = jnp.pad(_rel_bias(rel_bias_table, CHUNK, WINDOW + CHUNK), ((0, 0), (0, 0), (0, KEY_PAD - WINDOW - CHUNK)))
    sink_p = jnp.repeat(sinks, CHUNK, axis=1)[..., None]
    fill_p = jnp.where(jnp.arange(KEY_PAD)[None, None, :] == WINDOW + CHUNK, sink_p, NEG_INF).astype(F32)
    oa = _attn_prompt(qkv, bias_p, fill_p, b, s)
    s0 = jnp.zeros((b, N_HEADS_R, DK_R, DK_R), F32)
    orr, ns_p = _retention(qkr, vg, s0, _decay_tables(CHUNK), gain, b, s, CHUNK)
    group_p = _mix_group(xp, oa, orr, gm, p_prompt[i].reshape(b * s, D_PLE), wts)
    kv_tail = kvf.reshape(b, s, 2, N_KV_A, HEAD_DIM_A)[:, s - WINDOW:]
    nk_p, nv_p = kv_tail[:, :, 0], kv_tail[:, :, 1]

    xs = x_sample.reshape(bd * l, D_MODEL)
    cos_s, sin_s = _rotary_tables(jnp.tile(PAST_LEN + jnp.arange(l, dtype=I32), bd))
    qkv, kvf, qkr, vg, gm = _inproj(xs, w_in_bf, cos_s, sin_s, bd * l)
    sink_s = jnp.repeat(sinks, l, axis=1)[..., None]
    oa, nk_s, nv_s = _attn_sample(qkv, kvf, cache_attn_k[i].reshape(bd, WINDOW, KA_W),
                                  cache_attn_v[i].reshape(bd, WINDOW, KA_W),
                                  _rel_bias(rel_bias_table, l, WINDOW + l), sink_s, bd, l)
    orr, ns_s = _retention(qkr, vg, state_retention[i].astype(F32), _decay_tables(l), gain, bd, l, l)
    group_s = _mix_group(xs, oa, orr, gm, p_sample[i].reshape(bd * l, D_PLE), wts)
    y_p, y_s = _moe_ffn([group_p, group_s], wts)
    y_p, y_s = y_p.reshape(b, s, D_MODEL), y_s.reshape(bd, l, D_MODEL)
    shape_kv = (bd, WINDOW, N_KV_A, HEAD_DIM_A)
    return (y_p, y_s, nk_p[None], nv_p[None], ns_p[None],
            nk_s.reshape(shape_kv)[None], nv_s.reshape(shape_kv)[None], ns_s[None])
```

```python
import functools
import math

import jax
import jax.numpy as jnp
from jax import lax
from jax.experimental import pallas as pl
from jax.experimental.pallas import tpu as pltpu

F32 = jnp.float32
BF16 = jnp.bfloat16
I32 = jnp.int32

D_MODEL = 1024
CHUNK = 64
WINDOW = 128
N_HEADS_A = 8
N_KV_A = 2
GROUP_A = 4
HEAD_DIM_A = 64
N_BUCKETS = 32
MAX_DIST = 128
N_HEADS_R = 4
DK_R = 128
ROPE_BASE = 10000.0
N_EXPERTS = 256
TOP_K = 8
N_GROUPS = 8
GROUP_SIZE = N_EXPERTS // N_GROUPS
TOPK_GROUPS = 4
D_EXPERT = 256
ROUTED_SCALE = 2.5
D_PLE = 256
LN_EPS = 1e-5
NEG_INF = -1e30
PAST_LEN = 2048
DEPTH = 1
DN_ALPHA = (2 * DEPTH) ** 0.25

QA_W = N_HEADS_A * HEAD_DIM_A
KA_W = N_KV_A * HEAD_DIM_A
QR_W = N_HEADS_R * DK_R
OFF_KV = QA_W
OFF_QR = QA_W + 2 * KA_W
OFF_VR = OFF_QR + 2 * QR_W
OFF_GM = OFF_VR + 2 * QR_W
IN_W = OFF_GM + 2 * D_MODEL

HEAD_PAD = 128
PQ_W = N_HEADS_A * HEAD_PAD
PKV_W = 2 * N_KV_A * HEAD_PAD
A_KVF = PQ_W + PKV_W
A_QR = A_KVF + 2 * KA_W
A_VR = A_QR + 2 * QR_W
A_GM = A_VR + 2 * QR_W
A_W = A_GM + 2 * D_MODEL

MOE_ROWS = 512
VMEM_LIMIT = 56 * 1024 * 1024


def _cparams(sem, vmem=VMEM_LIMIT):
    return pltpu.CompilerParams(dimension_semantics=sem, vmem_limit_bytes=vmem)


def _layer_norm(h, g, b):
    mu = jnp.mean(h, axis=-1, keepdims=True)
    d = h - mu
    var = jnp.mean(d * d, axis=-1, keepdims=True)
    return d * lax.rsqrt(var + LN_EPS) * g + b


def _silu(x):
    return x * jax.nn.sigmoid(x)


U32 = jnp.uint32
PACK_W = D_MODEL // 2
PACK_SUB = PACK_W // 128


def _bf16_bits(x):
    return pltpu.bitcast(x.astype(BF16).astype(F32), U32)


def _store_packed(ref, x):
    n = x.shape[0]
    words = _bf16_bits(x[:, :PACK_W]) | (_bf16_bits(x[:, PACK_W:]) >> 16)
    for c in range(PACK_SUB):
        ref[pl.ds(c, n, stride=PACK_SUB), :] = words[:, c * 128:(c + 1) * 128]


def _load_packed(ref, n):
    words = jnp.concatenate([ref[pl.ds(c, n, stride=PACK_SUB), :] for c in range(PACK_SUB)], axis=1)
    hi = pltpu.bitcast(words & jnp.uint32(0xFFFF0000), F32)
    lo = pltpu.bitcast(words << 16, F32)
    return hi, lo


def _inproj_kernel(x_ref, w_ref, cos_ref, sin_ref, qkv_ref, kvf_ref, qkr_ref, vg_ref, gm_ref):
    xb = x_ref[...].astype(BF16)

    def mm(lo, hi):
        return jnp.dot(xb, w_ref[:, lo:hi], preferred_element_type=F32)

    qkv_ref[:, 0:PQ_W] = mm(0, PQ_W).astype(BF16)
    qkv_ref[:, PQ_W:A_KVF] = mm(PQ_W, A_KVF).astype(BF16)
    kvf_ref[...] = mm(A_KVF, A_QR)
    cos = cos_ref[...]
    sin = sin_ref[...]
    for part in range(2):
        z = mm(A_QR + part * QR_W, A_QR + (part + 1) * QR_W)
        for h in range(N_HEADS_R):
            zh = z[:, h * DK_R:(h + 1) * DK_R]
            r = zh * cos + pltpu.roll(zh, DK_R // 2, axis=1) * sin
            if part == 1:
                r = r * (DK_R ** -0.5)
            c0 = part * QR_W + h * DK_R
            qkr_ref[:, c0:c0 + DK_R] = r.astype(BF16)
    vg_ref[...] = mm(A_VR, A_GM).astype(BF16)
    for half in range(2):
        lo = A_GM + half * D_MODEL
        gm_ref[:, half * D_MODEL:(half + 1) * D_MODEL] = jax.nn.sigmoid(mm(lo, lo + D_MODEL)).astype(BF16)


def _inproj(x2d, w_bf, cos_tab, sin_tab, tm):
    t = x2d.shape[0]
    nper = cos_tab.shape[0] // tm
    row = lambda i: (i, 0)
    return pl.pallas_call(
        _inproj_kernel,
        grid=(t // tm,),
        in_specs=[
            pl.BlockSpec((tm, D_MODEL), row),
            pl.BlockSpec((D_MODEL, A_W), lambda i: (0, 0)),
            pl.BlockSpec((tm, DK_R), lambda i: (i % nper, 0)),
            pl.BlockSpec((tm, DK_R), lambda i: (i % nper, 0)),
        ],
        out_specs=[
            pl.BlockSpec((tm, A_KVF), row),
            pl.BlockSpec((tm, 2 * KA_W), row),
            pl.BlockSpec((tm, 2 * QR_W), row),
            pl.BlockSpec((tm, 2 * QR_W), row),
            pl.BlockSpec((tm, 2 * D_MODEL), row),
        ],
        out_shape=[
            jax.ShapeDtypeStruct((t, A_KVF), BF16),
            jax.ShapeDtypeStruct((t, 2 * KA_W), F32),
            jax.ShapeDtypeStruct((t, 2 * QR_W), BF16),
            jax.ShapeDtypeStruct((t, 2 * QR_W), BF16),
            jax.ShapeDtypeStruct((t, 2 * D_MODEL), BF16),
        ],
        compiler_params=_cparams(("parallel",)),
        name="inproj",
    )(x2d, w_bf, cos_tab, sin_tab)


def _attend(q4, k, v, bias, sink):
    s = lax.dot_general(q4, k, (((1,), (1,)), ((), ())), preferred_element_type=F32) + bias
    m = jnp.maximum(jnp.max(s, axis=-1, keepdims=True), sink)
    e = jnp.exp(s - m)
    p = e / (jnp.sum(e, axis=-1, keepdims=True) + jnp.exp(sink - m))
    return jnp.dot(p.astype(BF16), v, preferred_element_type=F32)


KEY_PAD = 256


def _attend_sink_column(q4, k, v, bias, fill, valid):
    s = lax.dot_general(q4, k, (((1,), (1,)), ((), ())), preferred_element_type=F32)
    s = jnp.where(valid, s + bias, fill)
    e = jnp.exp(s - jnp.max(s, axis=-1, keepdims=True)).astype(BF16)
    den = jnp.dot(e, jnp.ones((KEY_PAD, v.shape[1]), BF16), preferred_element_type=F32)
    return jnp.dot(e[:, 0:WINDOW + CHUNK], v, preferred_element_type=F32) / den


def _attn_prompt_kernel(q_ref, kvc_ref, kvp_ref, bias_ref, fill_ref, o_ref, kv_buf, *, n_chunks):
    i = pl.program_id(1)
    qb = n_chunks * CHUNK
    kv_buf[0:WINDOW, :] = kvp_ref[...]
    kv_buf[WINDOW:WINDOW + qb, :] = kvc_ref[...]
    kv_buf[WINDOW + qb:, :] = jnp.zeros((KEY_PAD - WINDOW - CHUNK, PKV_W), BF16)
    col = lax.broadcasted_iota(I32, (GROUP_A * CHUNK, KEY_PAD), 1)
    key_col = jnp.where(col < WINDOW + CHUNK, col, -1)

    def chunk(j, carry):
        r0 = pl.multiple_of(j * CHUNK, CHUNK)
        valid = key_col >= jnp.maximum(WINDOW - (i * n_chunks + j) * CHUNK, 0)
        for kv in range(N_KV_A):
            q4 = jnp.concatenate(
                [q_ref[pl.ds(r0, CHUNK), (kv * GROUP_A + g) * HEAD_PAD:(kv * GROUP_A + g + 1) * HEAD_PAD]
                 for g in range(GROUP_A)], axis=0)
            k = kv_buf[pl.ds(r0, KEY_PAD), kv * HEAD_PAD:(kv + 1) * HEAD_PAD]
            v = kv_buf[pl.ds(r0, WINDOW + CHUNK), (N_KV_A + kv) * HEAD_PAD:(N_KV_A + kv + 1) * HEAD_PAD]
            o4 = _attend_sink_column(q4, k, v, bias_ref[kv], fill_ref[kv], valid)
            for g in range(GROUP_A):
                c0 = (kv * GROUP_A + g) * HEAD_PAD
                o_ref[pl.ds(r0, CHUNK), c0:c0 + HEAD_PAD] = o4[g * CHUNK:(g + 1) * CHUNK].astype(BF16)
        return carry

    lax.fori_loop(0, n_chunks, chunk, 0, unroll=2 if n_chunks % 2 == 0 else 1)


def _attn_prompt(qkv, bias, fill, b, s):
    qb = min(512, s)
    n_chunks = qb // CHUNK
    nq = s // qb
    per = qb // WINDOW
    kv_col = PQ_W // PKV_W
    return pl.pallas_call(
        functools.partial(_attn_prompt_kernel, n_chunks=n_chunks),
        grid=(b, nq),
        in_specs=[
            pl.BlockSpec((qb, PQ_W), lambda bi, i: (bi * nq + i, 0)),
            pl.BlockSpec((qb, PKV_W), lambda bi, i: (bi * nq + i, kv_col)),
            pl.BlockSpec((WINDOW, PKV_W), lambda bi, i: (jnp.maximum((bi * nq + i) * per - 1, 0), kv_col)),
            pl.BlockSpec((N_KV_A, GROUP_A * CHUNK, KEY_PAD), lambda bi, i: (0, 0, 0)),
            pl.BlockSpec((N_KV_A, GROUP_A * CHUNK, KEY_PAD), lambda bi, i: (0, 0, 0)),
        ],
        out_specs=pl.BlockSpec((qb, PQ_W), lambda bi, i: (bi * nq + i, 0)),
        out_shape=jax.ShapeDtypeStruct((b * s, PQ_W), BF16),
        scratch_shapes=[pltpu.VMEM((qb + KEY_PAD - CHUNK, PKV_W), BF16)],
        compiler_params=_cparams(("parallel", "parallel")),
        name="attn_prompt",
    )(qkv, qkv, qkv, bias, fill)


def _attn_sample_kernel(q_ref, kvf_ref, ck_ref, cv_ref, bias_ref, sink_ref, o_ref, nk_ref, nv_ref, *, l):
    k_all = jnp.concatenate([ck_ref[0], kvf_ref[:, 0:KA_W]], axis=0)
    v_all = jnp.concatenate([cv_ref[0], kvf_ref[:, KA_W:2 * KA_W]], axis=0)
    nk_ref[0] = k_all[l:]
    nv_ref[0] = v_all[l:]
    kb = k_all.astype(BF16)
    vb = v_all.astype(BF16)
    o_ref[...] = jnp.zeros_like(o_ref)
    for kv in range(N_KV_A):
        q4 = jnp.concatenate(
            [q_ref[:, (kv * GROUP_A + g) * HEAD_PAD:(kv * GROUP_A + g) * HEAD_PAD + HEAD_DIM_A]
             for g in range(GROUP_A)], axis=0)
        o4 = _attend(q4, kb[:, kv * HEAD_DIM_A:(kv + 1) * HEAD_DIM_A], vb[:, kv * HEAD_DIM_A:(kv + 1) * HEAD_DIM_A],
                     bias_ref[kv], sink_ref[kv])
        for g in range(GROUP_A):
            c0 = (kv * GROUP_A + g) * HEAD_PAD
            o_ref[:, c0:c0 + HEAD_DIM_A] = o4[g * l:(g + 1) * l].astype(BF16)


def _attn_sample(qkv, kvf, cache_k, cache_v, bias, sink, bd, l):
    cache_spec = pl.BlockSpec((1, WINDOW, KA_W), lambda bi: (bi, 0, 0))
    cache_shape = jax.ShapeDtypeStruct((bd, WINDOW, KA_W), F32)
    return pl.pallas_call(
        functools.partial(_attn_sample_kernel, l=l),
        grid=(bd,),
        in_specs=[
            pl.BlockSpec((l, A_KVF), lambda bi: (bi, 0)),
            pl.BlockSpec((l, 2 * KA_W), lambda bi: (bi, 0)),
            cache_spec, cache_spec,
            pl.BlockSpec((N_KV_A, GROUP_A * l, WINDOW + l), lambda bi: (0, 0, 0)),
            pl.BlockSpec((N_KV_A, GROUP_A * l, 1), lambda bi: (0, 0, 0)),
        ],
        out_specs=[pl.BlockSpec((l, PQ_W), lambda bi: (bi, 0)), cache_spec, cache_spec],
        out_shape=[jax.ShapeDtypeStruct((bd * l, PQ_W), BF16), cache_shape, cache_shape],
        compiler_params=_cparams(("parallel",)),
        name="attn_sample",
    )(qkv, kvf, cache_k, cache_v, bias, sink)


def _retention_kernel(q_ref, k_ref, v_ref, g_ref, s0_ref, di_ref, qd_ref, kd_ref, gc_ref, gain_ref,
                      o_ref, s_ref, *, n_chunks, c):
    @pl.when(pl.program_id(1) == 0)
    def _():
        s_ref[...] = s0_ref[...]

    def chunk(n, carry):
        r0 = pl.multiple_of(n * c, c)
        for h in range(N_HEADS_R):
            cols = slice(h * DK_R, (h + 1) * DK_R)
            qc = q_ref[pl.ds(r0, c), cols]
            kc = k_ref[pl.ds(r0, c), cols]
            vc = v_ref[pl.ds(r0, c), cols]
            state = s_ref[0, h]
            sc = lax.dot_general(qc, kc, (((1,), (1,)), ((), ())), preferred_element_type=F32) * di_ref[h]
            o = jnp.dot(sc.astype(BF16), vc, preferred_element_type=F32)
            q_dec = (qc.astype(F32) * qd_ref[h]).astype(BF16)
            o = o + jnp.dot(q_dec, state.astype(BF16), preferred_element_type=F32)
            k_dec = (kc.astype(F32) * kd_ref[h]).astype(BF16)
            s_ref[0, h] = gc_ref[h] * state + lax.dot_general(k_dec, vc, (((0,), (0,)), ((), ())),
                                                              preferred_element_type=F32)
            mu = jnp.mean(o, axis=-1, keepdims=True)
            d = o - mu
            var = jnp.mean(d * d, axis=-1, keepdims=True)
            y = d * lax.rsqrt(var + LN_EPS) * gain_ref[:, cols]
            o_ref[pl.ds(r0, c), cols] = (y * _silu(g_ref[pl.ds(r0, c), cols].astype(F32))).astype(BF16)
        return carry

    lax.fori_loop(0, n_chunks, chunk, 0, unroll=4 if n_chunks % 4 == 0 else 1)


def _retention(qkr, vg, s0, tabs, gain, b, s, c):
    di, qd, kd, gc = tabs
    st = min(1024, s)
    nt = s // st
    seq = lambda col: pl.BlockSpec((st, QR_W), lambda bi, j: (bi * nt + j, col))
    full = lambda a: pl.BlockSpec(a.shape, lambda bi, j: (0,) * a.ndim)
    st_spec = pl.BlockSpec((1, N_HEADS_R, DK_R, DK_R), lambda bi, j: (bi, 0, 0, 0))
    return pl.pallas_call(
        functools.partial(_retention_kernel, n_chunks=st // c, c=c),
        grid=(b, nt),
        in_specs=[seq(0), seq(1), seq(0), seq(1), st_spec, full(di), full(qd), full(kd), full(gc), full(gain)],
        out_specs=[seq(0), st_spec],
        out_shape=[jax.ShapeDtypeStruct((b * s, QR_W), BF16), jax.ShapeDtypeStruct((b, N_HEADS_R, DK_R, DK_R), F32)],
        compiler_params=_cparams(("parallel", "arbitrary")),
        name="retention",
    )(qkr, qkr, vg, vg, s0, di, qd, kd, gc, gain)


def _mix_kernel(x_ref, oa_ref, or_ref, g_ref, wa_ref, wb_ref, wo_ref, lg_ref, lb_ref, wrt_ref, x1_ref, st_ref, xp_ref):
    a = jnp.dot(oa_ref[...], wa_ref[...], preferred_element_type=F32)
    b = jnp.dot(or_ref[...], wb_ref[...], preferred_element_type=F32)
    merged = g_ref[:, 0:D_MODEL].astype(F32) * a + g_ref[:, D_MODEL:].astype(F32) * b
    y = jnp.dot(merged.astype(BF16), wo_ref[...], preferred_element_type=F32)
    x1 = _layer_norm(DN_ALPHA * x_ref[...] + y, lg_ref[...], lb_ref[...])
    x1_ref[...] = x1
    _store_packed(xp_ref, x1)
    logits = lax.dot_general(wrt_ref[...], x1.astype(BF16), (((1,), (1,)), ((), ())), preferred_element_type=F32)
    st_ref[...] = jax.nn.sigmoid(logits)


def _mix(x2d, oa, orr, gm, wa, wb, wo, lg, lb, wrt, tm):
    t = x2d.shape[0]
    row = lambda w: pl.BlockSpec((tm, w), lambda i: (i, 0))
    full = lambda a: pl.BlockSpec(a.shape, lambda i: (0,) * a.ndim)
    return pl.pallas_call(
        _mix_kernel,
        grid=(t // tm,),
        in_specs=[row(D_MODEL), row(PQ_W), row(QR_W), row(2 * D_MODEL),
                  full(wa), full(wb), full(wo), full(lg), full(lb), full(wrt)],
        out_specs=[row(D_MODEL), pl.BlockSpec((N_EXPERTS, tm), lambda i: (0, i)),
                   pl.BlockSpec((tm * PACK_SUB, 128), lambda i: (i, 0))],
        out_shape=[jax.ShapeDtypeStruct((t, D_MODEL), F32), jax.ShapeDtypeStruct((N_EXPERTS, t), F32),
                   jax.ShapeDtypeStruct((t * PACK_SUB, 128), U32)],
        compiler_params=_cparams(("parallel",)),
        name="mix",
    )(x2d, oa, orr, gm, wa, wb, wo, lg, lb, wrt)


def _route_kernel(s_ref, bias_ref, cnt0_ref, e_ref, w_ref, r_ref, cnt_ref, carry_ref):
    @pl.when(pl.program_id(0) == 0)
    def _():
        carry_ref[...] = cnt0_ref[...].astype(F32)

    s = s_ref[...]
    tl = s.shape[1]
    choice = s + bias_ref[...]
    row = lax.broadcasted_iota(I32, (N_EXPERTS, tl), 0)
    row_g = lax.broadcasted_iota(I32, (GROUP_SIZE, tl), 0)
    neg = -jnp.inf

    scores = []
    for g in range(N_GROUPS):
        blk = choice[g * GROUP_SIZE:(g + 1) * GROUP_SIZE]
        m1 = jnp.max(blk, axis=0, keepdims=True)
        i1 = jnp.min(jnp.where(blk == m1, row_g, GROUP_SIZE), axis=0, keepdims=True)
        m2 = jnp.max(jnp.where(row_g == i1, neg, blk), axis=0, keepdims=True)
        scores.append(m1 + m2)
    sc = jnp.concatenate(scores, axis=0)
    gi = lax.broadcasted_iota(I32, sc.shape, 0)
    grank = jnp.zeros(sc.shape, I32)
    for g in range(N_GROUPS):
        other = sc[g:g + 1]
        ahead = jnp.where(other > sc, 1, jnp.where(other == sc, jnp.where(gi > g, 1, 0), 0))
        grank = grank + ahead
    cm = jnp.concatenate(
        [jnp.where(grank[g:g + 1] < TOPK_GROUPS, choice[g * GROUP_SIZE:(g + 1) * GROUP_SIZE], neg)
         for g in range(N_GROUPS)], axis=0)

    experts, weights = [], []
    for _ in range(TOP_K):
        m = jnp.max(cm, axis=0, keepdims=True)
        idx = jnp.min(jnp.where(cm == m, row, N_EXPERTS), axis=0, keepdims=True)
        hit = row == idx
        weights.append(jnp.sum(jnp.where(hit, s, 0.0), axis=0, keepdims=True))
        cm = jnp.where(hit, neg, cm)
        experts.append(idx)
    e8 = jnp.concatenate(experts, axis=0)
    w8 = jnp.concatenate(weights, axis=0)
    e_ref[...] = e8
    w_ref[...] = w8 / jnp.sum(w8, axis=0, keepdims=True) * ROUTED_SCALE

    before = (lax.broadcasted_iota(I32, (tl, tl), 0) < lax.broadcasted_iota(I32, (tl, tl), 1))
    before = jnp.where(before, 1.0, 0.0).astype(BF16)
    carry = carry_ref[...]
    ranks = []
    for k in range(TOP_K):
        hit = row == experts[k]
        onehot = jnp.where(hit, 1.0, 0.0)
        prefix = jnp.dot(onehot.astype(BF16), before, preferred_element_type=F32)
        ranks.append(jnp.sum(jnp.where(hit, prefix + carry, 0.0), axis=0, keepdims=True))
        carry = carry + jnp.sum(onehot, axis=1, keepdims=True)
    r_ref[...] = jnp.concatenate(ranks, axis=0).astype(I32)
    carry_ref[...] = carry
    cnt_ref[...] = carry.astype(I32)


def _route(st, bias_col, counts0, tl):
    t = st.shape[1]
    tok = pl.BlockSpec((TOP_K, tl), lambda i: (0, i))
    return pl.pallas_call(
        _route_kernel,
        grid=(t // tl,),
        in_specs=[pl.BlockSpec((N_EXPERTS, tl), lambda i: (0, i)), pl.BlockSpec((N_EXPERTS, 1), lambda i: (0, 0)),
                  pl.BlockSpec((N_EXPERTS, 1), lambda i: (0, 0))],
        out_specs=[tok, tok, tok, pl.BlockSpec((N_EXPERTS, 1), lambda i: (0, 0))],
        out_shape=[jax.ShapeDtypeStruct((TOP_K, t), I32), jax.ShapeDtypeStruct((TOP_K, t), F32),
                   jax.ShapeDtypeStruct((TOP_K, t), I32), jax.ShapeDtypeStruct((N_EXPERTS, 1), I32)],
        scratch_shapes=[pltpu.VMEM((N_EXPERTS, 1), F32)],
        compiler_params=_cparams(("arbitrary",)),
        name="route",
    )(st, bias_col, counts0)


def _dest_kernel(e_ref, r_ref, ps_ref, d_ref):
    tl = e_ref.shape[1]
    row = lax.broadcasted_iota(I32, (N_EXPERTS, tl), 0)
    ps = ps_ref[...]
    starts = [jnp.sum(jnp.where(row == e_ref[k:k + 1, :], ps, 0.0), axis=0, keepdims=True) for k in range(TOP_K)]
    d_ref[...] = jnp.concatenate(starts, axis=0).astype(I32) + r_ref[...]


def _dest(e8, r8, pstart_col, tl):
    t = e8.shape[1]
    tok = pl.BlockSpec((TOP_K, tl), lambda i: (0, i))
    return pl.pallas_call(
        _dest_kernel,
        grid=(t // tl,),
        in_specs=[tok, tok, pl.BlockSpec((N_EXPERTS, 1), lambda i: (0, 0))],
        out_specs=tok,
        out_shape=jax.ShapeDtypeStruct((TOP_K, t), I32),
        compiler_params=_cparams(("parallel",)),
        name="dest",
    )(e8, r8, pstart_col)


TOKEN_GROUP = 8


def _row_copy(src, src_tok, dst, dst_tok, sem):
    s0 = pl.multiple_of(src_tok * PACK_SUB, PACK_SUB)
    d0 = pl.multiple_of(dst_tok * PACK_SUB, PACK_SUB)
    return pltpu.make_async_copy(src.at[pl.ds(s0, PACK_SUB)], dst.at[pl.ds(d0, PACK_SUB)], sem)


def _wait_bytes_of(ref, sem):
    pltpu.make_async_copy(ref, ref, sem).wait()


def _dispatch_kernel(d_ref, x_ref, *refs, td):
    xs_hbm, sem = refs[-2:]
    def start(g, carry):
        for j in range(TOKEN_GROUP):
            t = g * TOKEN_GROUP + j
            for k in range(TOP_K):
                _row_copy(x_ref, t, xs_hbm, d_ref[t * TOP_K + k], sem).start(priority=k % 2)
        return carry

    lax.fori_loop(0, td // TOKEN_GROUP, start, 0)
    for k in range(TOP_K):
        _wait_bytes_of(x_ref, sem)


def _dispatch(d8, xp, n_rows, td, xs_prev=None):
    t = d8.shape[0] // TOP_K
    in_specs = [pl.BlockSpec((td * TOP_K,), lambda i: (i,), memory_space=pltpu.SMEM),
                pl.BlockSpec((td * PACK_SUB, 128), lambda i: (i, 0))]
    args = [d8, xp]
    if xs_prev is not None:
        in_specs.append(pl.BlockSpec(memory_space=pl.ANY))
        args.append(xs_prev)
    return pl.pallas_call(
        functools.partial(_dispatch_kernel, td=td),
        grid=(t // td,),
        in_specs=in_specs,
        out_specs=pl.BlockSpec(memory_space=pl.ANY),
        scratch_shapes=[pltpu.SemaphoreType.DMA(())],
        out_shape=jax.ShapeDtypeStruct((n_rows * PACK_SUB, 128), U32),
        input_output_aliases={} if xs_prev is None else {2: 0},
        compiler_params=_cparams(("arbitrary",)),
        name="dispatch",
    )(*args)


X_RING = 3


def _moe_kernel(be_ref, nv_ref, x_hbm, wg_ref, wu_ref, wd_ref, y_ref, x_ring, x_sem, wg_bf, wu_bf, wd_bf):
    j = pl.program_id(0)
    n = pl.num_programs(0)
    nv = nv_ref[j]
    blk_rows = MOE_ROWS * PACK_SUB

    def x_copy(blk):
        r0 = pl.multiple_of(blk * blk_rows, blk_rows)
        slot = blk % X_RING
        return pltpu.make_async_copy(x_hbm.at[pl.ds(r0, blk_rows)], x_ring.at[slot], x_sem.at[slot])

    def start_if_real(blk):
        @pl.when((blk < n) & (nv_ref[jnp.minimum(blk, n - 1)] > 0))
        def _():
            x_copy(blk).start()

    @pl.when(j == 0)
    def _():
        for blk in range(X_RING - 1):
            start_if_real(jnp.int32(blk))

    start_if_real(j + X_RING - 1)

    @pl.when((j == 0) | (be_ref[j] != be_ref[jnp.maximum(j - 1, 0)]))
    def _():
        wg_bf[...] = wg_ref[0].astype(BF16)
        wu_bf[...] = wu_ref[0].astype(BF16)
        wd_bf[...] = wd_ref[0].astype(BF16)

    @pl.when(nv > 0)
    def _():
        x_copy(j).wait()
        hi, lo = _load_packed(x_ring.at[j % X_RING], MOE_ROWS)
        valid = lax.broadcasted_iota(I32, hi.shape, 0) < nv
        xb = jnp.concatenate([jnp.where(valid, hi, 0.0).astype(BF16), jnp.where(valid, lo, 0.0).astype(BF16)], axis=1)
        g = jnp.dot(xb, wg_bf[...], preferred_element_type=F32)
        u = jnp.dot(xb, wu_bf[...], preferred_element_type=F32)
        h = (_silu(g) * u).astype(BF16)
        _store_packed(y_ref, jnp.dot(h, wd_bf[...], preferred_element_type=F32))

    @pl.when(nv == 0)
    def _():
        y_ref[...] = jnp.zeros_like(y_ref)


def _moe(blk_exp, blk_valid, xs, wg, wu, wd):
    n_blocks = blk_exp.shape[0]
    rows = pl.BlockSpec((MOE_ROWS * PACK_SUB, 128), lambda j, be, nv: (j, 0))
    return pl.pallas_call(
        _moe_kernel,
        grid_spec=pltpu.PrefetchScalarGridSpec(
            num_scalar_prefetch=2,
            grid=(n_blocks,),
            in_specs=[pl.BlockSpec(memory_space=pl.ANY),
                      pl.BlockSpec((1, D_MODEL, D_EXPERT), lambda j, be, nv: (be[j], 0, 0)),
                      pl.BlockSpec((1, D_MODEL, D_EXPERT), lambda j, be, nv: (be[j], 0, 0)),
                      pl.BlockSpec((1, D_EXPERT, D_MODEL), lambda j, be, nv: (be[j], 0, 0))],
            out_specs=rows,
            scratch_shapes=[pltpu.VMEM((X_RING, MOE_ROWS * PACK_SUB, 128), U32), pltpu.SemaphoreType.DMA((X_RING,)),
                            pltpu.VMEM((D_MODEL, D_EXPERT), BF16), pltpu.VMEM((D_MODEL, D_EXPERT), BF16),
                            pltpu.VMEM((D_EXPERT, D_MODEL), BF16)],
        ),
        out_shape=jax.ShapeDtypeStruct(xs.shape, U32),
        compiler_params=_cparams(("arbitrary",)),
        name="moe",
    )(blk_exp, blk_valid, xs, wg, wu, wd)


COMBINE_ROWS = 32


def _gather_rows(ys_hbm, d_ref, slot_buf, sem, t0, n):
    for j in range(n):
        t = t0 + j
        for k in range(TOP_K):
            _row_copy(ys_hbm, d_ref[t * TOP_K + k], slot_buf.at[k], t, sem).start(priority=k % 2)


def _final_kernel(d_ref, dnext_ref, x1_ref, w_ref, p_ref, wsg_ref, wsu_ref, wsd_ref, lg_ref, lb_ref,
                  wpp_ref, wpg_ref, ys_hbm, out_ref, gbuf, ybuf, sem, *, tf):
    i = pl.program_id(0)
    slot = i % 2

    @pl.when(i == 0)
    def _():
        def first(g, carry):
            _gather_rows(ys_hbm, d_ref, gbuf.at[0], sem.at[0], g * TOKEN_GROUP, TOKEN_GROUP)
            return carry

        lax.fori_loop(0, tf // TOKEN_GROUP, first, 0)

    x1 = x1_ref[...]
    xb = x1.astype(BF16)
    hs = _silu(jnp.dot(xb, wsg_ref[...], preferred_element_type=F32)) * jnp.dot(xb, wsu_ref[...],
                                                                               preferred_element_type=F32)
    y = jnp.dot(hs.astype(BF16), wsd_ref[...], preferred_element_type=F32)

    _wait_bytes_of(gbuf.at[slot], sem.at[slot])

    def combine(sb, carry):
        r0 = pl.multiple_of(sb * COMBINE_ROWS, COMBINE_ROWS)
        _gather_rows(ys_hbm, dnext_ref, gbuf.at[1 - slot], sem.at[1 - slot], r0, COMBINE_ROWS)
        y_hi = jnp.zeros((COMBINE_ROWS, PACK_W), F32)
        y_lo = jnp.zeros((COMBINE_ROWS, PACK_W), F32)
        for k in range(TOP_K):
            rows = gbuf.at[slot, k, pl.ds(pl.multiple_of(r0 * PACK_SUB, COMBINE_ROWS * PACK_SUB),
                                          COMBINE_ROWS * PACK_SUB)]
            hi, lo = _load_packed(rows, COMBINE_ROWS)
            wk = w_ref[pl.ds(r0, COMBINE_ROWS), k:k + 1]
            y_hi = y_hi + wk * hi
            y_lo = y_lo + wk * lo
        ybuf[pl.ds(r0, COMBINE_ROWS), 0:PACK_W] = y_hi
        ybuf[pl.ds(r0, COMBINE_ROWS), PACK_W:] = y_lo
        return carry

    lax.fori_loop(0, tf // COMBINE_ROWS, combine, 0)

    @pl.when(i == pl.num_programs(0) - 1)
    def _():
        _wait_bytes_of(gbuf.at[1 - slot], sem.at[1 - slot])

    y = y + ybuf[...]
    x2 = _layer_norm(DN_ALPHA * x1 + y, lg_ref[...], lb_ref[...])
    gate = jax.nn.sigmoid(jnp.dot(x2.astype(BF16), wpg_ref[...], preferred_element_type=F32))
    out_ref[...] = x2 + gate * jnp.dot(p_ref[...].astype(BF16), wpp_ref[...], preferred_element_type=F32)


def _final(d8, x1, w_tok, p2d, wsg, wsu, wsd, lg, lb, wpp, wpg, ys, tf):
    t = x1.shape[0]
    n_tiles = t // tf
    row = lambda w: pl.BlockSpec((tf, w), lambda i: (i, 0))
    full = lambda a: pl.BlockSpec(a.shape, lambda i: (0,) * a.ndim)
    return pl.pallas_call(
        functools.partial(_final_kernel, tf=tf),
        grid=(n_tiles,),
        in_specs=[pl.BlockSpec((tf * TOP_K,), lambda i: (i,), memory_space=pltpu.SMEM),
                  pl.BlockSpec((tf * TOP_K,), lambda i: (jnp.minimum(i + 1, n_tiles - 1),), memory_space=pltpu.SMEM),
                  row(D_MODEL), row(TOP_K), row(D_PLE),
                  full(wsg), full(wsu), full(wsd), full(lg), full(lb), full(wpp), full(wpg),
                  pl.BlockSpec(memory_space=pl.ANY)],
        out_specs=row(D_MODEL),
        scratch_shapes=[pltpu.VMEM((2, TOP_K, tf * PACK_SUB, 128), U32), pltpu.VMEM((tf, D_MODEL), F32),
                        pltpu.SemaphoreType.DMA((2,))],
        out_shape=jax.ShapeDtypeStruct((t, D_MODEL), F32),
        compiler_params=_cparams(("arbitrary",)),
        name="final",
    )(d8, d8, x1, w_tok, p2d, wsg, wsu, wsd, lg, lb, wpp, wpg, ys)


def _t5_bucket(rel):
    nb = N_BUCKETS // 2
    max_exact = nb // 2
    ret = jnp.where(rel > 0, nb, 0)
    n = jnp.abs(rel)
    nf = jnp.maximum(n, 1).astype(F32)
    large = max_exact + (jnp.log(nf / max_exact) / math.log(MAX_DIST / max_exact) * (nb - max_exact)).astype(I32)
    large = jnp.minimum(large, nb - 1)
    return ret + jnp.where(n < max_exact, n, large)


def _rel_bias(table, q_len, k_len):
    rel = (jnp.arange(k_len, dtype=I32)[None, :] - WINDOW) - jnp.arange(q_len, dtype=I32)[:, None]
    b = table[_t5_bucket(rel)]
    return jnp.transpose(b, (2, 0, 1)).reshape(N_KV_A, GROUP_A * q_len, k_len).astype(F32)


def _rotary_tables(pos):
    half = DK_R // 2
    inv = ROPE_BASE ** (-jnp.arange(half, dtype=F32) / half)
    ang = pos.astype(F32)[:, None] * inv[None, :]
    cos, sin = jnp.cos(ang), jnp.sin(ang)
    return jnp.concatenate([cos, cos], -1), jnp.concatenate([-sin, sin], -1)


def _decay_tables(c):
    log_gamma = jnp.log1p(-jnp.exp2(-5.0 - jnp.arange(N_HEADS_R, dtype=F32)))
    idx = jnp.arange(c, dtype=F32)
    di = jnp.exp(jnp.abs(idx[:, None] - idx[None, :])[None] * log_gamma[:, None, None])
    qd = jnp.exp((idx[None, :] + 1.0) * log_gamma[:, None])
    kd = jnp.exp((c - 1.0 - idx[None, :]) * log_gamma[:, None])
    gc = jnp.exp(c * log_gamma)
    bc = lambda a: jnp.broadcast_to(a[:, :, None], (N_HEADS_R, c, DK_R))
    return di, bc(qd), bc(kd), jnp.broadcast_to(gc[:, None, None], (N_HEADS_R, DK_R, DK_R))


def _pad_heads(w, n_heads):
    rows = w.shape[0]
    w = w.reshape(rows, n_heads, HEAD_DIM_A)
    w = jnp.pad(w, ((0, 0), (0, 0), (0, HEAD_PAD - HEAD_DIM_A)))
    return w.reshape(rows, n_heads * HEAD_PAD)


def _moe_ffn(groups, wts):
    tiles = [min(256, g[0].shape[0]) for g in groups]
    counts = jnp.zeros((N_EXPERTS, 1), I32)
    routed = []
    for (x1, xp, st, p2d), tile in zip(groups, tiles):
        e8, w8, r8, counts = _route(st, wts['router_bias'], counts, tile)
        routed.append((e8, w8, r8))
    counts = counts[:, 0]
    padded = (counts + MOE_ROWS - 1) // MOE_ROWS * MOE_ROWS
    pad_end = jnp.cumsum(padded)
    pstart = (pad_end - padded).astype(I32)
    n_rows = sum(g[0].shape[0] for g in groups) * TOP_K + N_EXPERTS * MOE_ROWS
    blk_start = jnp.arange(n_rows // MOE_ROWS, dtype=I32) * MOE_ROWS
    blk_exp = jnp.minimum(jnp.sum(blk_start[:, None] >= pad_end[None, :], axis=1), N_EXPERTS - 1).astype(I32)
    own = blk_exp[:, None] == jnp.arange(N_EXPERTS, dtype=I32)[None, :]
    blk_end = jnp.sum(jnp.where(own, (pstart + counts)[None, :], 0), axis=1)
    blk_valid = jnp.clip(blk_end - blk_start, 0, MOE_ROWS).astype(I32)
    pstart_col = pstart.astype(F32).reshape(N_EXPERTS, 1)
    dests, xs = [], None
    for (x1, xp, st, p2d), (e8, w8, r8), tile in zip(groups, routed, tiles):
        d8 = _dest(e8, r8, pstart_col, tile).T.reshape(-1)
        xs = _dispatch(d8, xp, n_rows, tile, xs)
        dests.append(d8)
    ys = _moe(blk_exp, blk_valid, xs, wts['w_exp_gate'], wts['w_exp_up'], wts['w_exp_down'])
    return [_final(d8, x1, w8.T, p2d, wts['w_sh_gate'], wts['w_sh_up'], wts['w_sh_down'],
                   wts['ln2_g'], wts['ln2_b'], wts['w_ple_proj'], wts['w_ple_gate'], ys, tile)
            for (x1, xp, st, p2d), (e8, w8, r8), d8, tile in zip(groups, routed, dests, tiles)]


def _mix_group(x2d, oa, orr, gm, p2d, wts):
    x1, st, xp = _mix(x2d, oa, orr, gm, wts['w_branch_attn'], wts['w_branch_ret'], wts['w_out'],
                      wts['ln1_g'], wts['ln1_b'], wts['w_router_t'], min(512, x2d.shape[0]))
    return x1, xp, st, p2d


def kernel(x_prompt, x_sample, cache_attn_k, cache_attn_v, state_retention, p_prompt, p_sample,
           w_in, attn_sinks, rel_bias_table, ret_gn_gain, w_branch_attn, w_branch_ret, w_out,
           ln1_g, ln1_b, w_router, router_bias, w_exp_gate, w_exp_up, w_exp_down,
           w_sh_gate, w_sh_up, w_sh_down, ln2_g, ln2_b, w_ple_proj, w_ple_gate):
    b, s, _ = x_prompt.shape
    bd, l, _ = x_sample.shape
    i = 0
    row = lambda a: a.reshape(1, -1).astype(F32)
    wts = dict(
        w_branch_attn=_pad_heads(w_branch_attn[i].astype(BF16).T, N_HEADS_A).T,
        w_branch_ret=w_branch_ret[i].astype(BF16),
        w_out=w_out[i].astype(BF16), ln1_g=row(ln1_g[i]), ln1_b=row(ln1_b[i]),
        w_router_t=w_router[i].T.astype(BF16), router_bias=router_bias[i].reshape(N_EXPERTS, 1).astype(F32),
        w_exp_gate=w_exp_gate[i], w_exp_up=w_exp_up[i], w_exp_down=w_exp_down[i],
        w_sh_gate=w_sh_gate[i].astype(BF16), w_sh_up=w_sh_up[i].astype(BF16), w_sh_down=w_sh_down[i].astype(BF16),
        ln2_g=row(ln2_g[i]), ln2_b=row(ln2_b[i]),
        w_ple_proj=w_ple_proj[i].astype(BF16), w_ple_gate=w_ple_gate[i].astype(BF16))
    w_in_bf = w_in[i].astype(BF16)
    w_in_bf = jnp.concatenate([_pad_heads(w_in_bf[:, :QA_W] * (HEAD_DIM_A ** -0.5), N_HEADS_A),
                               _pad_heads(w_in_bf[:, OFF_KV:OFF_QR], 2 * N_KV_A), w_in_bf[:, OFF_KV:]], axis=1)
    gain = row(ret_gn_gain[i])
    sinks = attn_sinks[i].astype(F32).reshape(N_KV_A, GROUP_A)

    xp = x_prompt.reshape(b * s, D_MODEL)
    cos_p, sin_p = _rotary_tables(jnp.arange(s, dtype=I32))
    qkv, kvf, qkr, vg, gm = _inproj(xp, w_in_bf, cos_p, sin_p, min(512, s))
    bias_p = jnp.pad(_rel_bias(rel_bias_table, CHUNK, WINDOW + CHUNK), ((0, 0), (0, 0), (0, KEY_PAD - WINDOW - CHUNK)))
    sink_p = jnp.repeat(sinks, CHUNK, axis=1)[..., None]
    fill_p = jnp.where(jnp.arange(KEY_PAD)[None, None, :] == WINDOW + CHUNK, sink_p, NEG_INF).astype(F32)
    oa = _attn_prompt(qkv, bias_p, fill_p, b, s)
    s0 = jnp.zeros((b, N_HEADS_R, DK_R, DK_R), F32)
    orr, ns_p = _retention(qkr, vg, s0, _decay_tables(CHUNK), gain, b, s, CHUNK)
    group_p = _mix_group(xp, oa, orr, gm, p_prompt[i].reshape(b * s, D_PLE), wts)
    kv_tail = kvf.reshape(b, s, 2, N_KV_A, HEAD_DIM_A)[:, s - WINDOW:]
    nk_p, nv_p = kv_tail[:, :, 0], kv_tail[:, :, 1]

    xs = x_sample.reshape(bd * l, D_MODEL)
    cos_s, sin_s = _rotary_tables(jnp.tile(PAST_LEN + jnp.arange(l, dtype=I32), bd))
    qkv, kvf, qkr, vg, gm = _inproj(xs, w_in_bf, cos_s, sin_s, bd * l)
    sink_s = jnp.repeat(sinks, l, axis=1)[..., None]
    oa, nk_s, nv_s = _attn_sample(qkv, kvf, cache_attn_k[i].reshape(bd, WINDOW, KA_W),
                                  cache_attn_v[i].reshape(bd, WINDOW, KA_W),
                                  _rel_bias(rel_bias_table, l, WINDOW + l), sink_s, bd, l)
    orr, ns_s = _retention(qkr, vg, state_retention[i].astype(F32), _decay_tables(l), gain, bd, l, l)
    group_s = _mix_group(xs, oa, orr, gm, p_sample[i].reshape(bd * l, D_PLE), wts)
    y_p, y_s = _moe_ffn([group_p, group_s], wts)
    y_p, y_s = y_p.reshape(b, s, D_MODEL), y_s.reshape(bd, l, D_MODEL)
    shape_kv = (bd, WINDOW, N_KV_A, HEAD_DIM_A)
    return (y_p, y_s, nk_p[None], nv_p[None], ns_p[None],
            nk_s.reshape(shape_kv)[None], nv_s.reshape(shape_kv)[None], ns_s[None])
```

```python
import functools
import math

import jax
import jax.numpy as jnp
from jax import lax
from jax.experimental import pallas as pl
from jax.experimental.pallas import tpu as pltpu

F32 = jnp.float32
BF16 = jnp.bfloat16
I32 = jnp.int32

D_MODEL = 1024
CHUNK = 64
WINDOW = 128
N_HEADS_A = 8
N_KV_A = 2
GROUP_A = 4
HEAD_DIM_A = 64
N_BUCKETS = 32
MAX_DIST = 128
N_HEADS_R = 4
DK_R = 128
ROPE_BASE = 10000.0
N_EXPERTS = 256
TOP_K = 8
N_GROUPS = 8
GROUP_SIZE = N_EXPERTS // N_GROUPS
TOPK_GROUPS = 4
D_EXPERT = 256
ROUTED_SCALE = 2.5
D_PLE = 256
LN_EPS = 1e-5
NEG_INF = -1e30
PAST_LEN = 2048
DEPTH = 1
DN_ALPHA = (2 * DEPTH) ** 0.25

QA_W = N_HEADS_A * HEAD_DIM_A
KA_W = N_KV_A * HEAD_DIM_A
QR_W = N_HEADS_R * DK_R
OFF_KV = QA_W
OFF_QR = QA_W + 2 * KA_W
OFF_VR = OFF_QR + 2 * QR_W
OFF_GM = OFF_VR + 2 * QR_W
IN_W = OFF_GM + 2 * D_MODEL

HEAD_PAD = 128
PQ_W = N_HEADS_A * HEAD_PAD
PKV_W = 2 * N_KV_A * HEAD_PAD
A_KVF = PQ_W + PKV_W
A_QR = A_KVF + 2 * KA_W
A_VR = A_QR + 2 * QR_W
A_GM = A_VR + 2 * QR_W
A_W = A_GM + 2 * D_MODEL

MOE_ROWS = 512
VMEM_LIMIT = 56 * 1024 * 1024


def _cparams(sem, vmem=VMEM_LIMIT):
    return pltpu.CompilerParams(dimension_semantics=sem, vmem_limit_bytes=vmem)


def _layer_norm(h, g, b):
    mu = jnp.mean(h, axis=-1, keepdims=True)
    d = h - mu
    var = jnp.mean(d * d, axis=-1, keepdims=True)
    return d * lax.rsqrt(var + LN_EPS) * g + b


def _silu(x):
    return x * jax.nn.sigmoid(x)


U32 = jnp.uint32
PACK_W = D_MODEL // 2
PACK_SUB = PACK_W // 128


def _bf16_bits(x):
    return pltpu.bitcast(x.astype(BF16).astype(F32), U32)


def _store_packed(ref, x):
    n = x.shape[0]
    words = _bf16_bits(x[:, :PACK_W]) | (_bf16_bits(x[:, PACK_W:]) >> 16)
    for c in range(PACK_SUB):
        ref[pl.ds(c, n, stride=PACK_SUB), :] = words[:, c * 128:(c + 1) * 128]


def _load_packed(ref, n):
    words = jnp.concatenate([ref[pl.ds(c, n, stride=PACK_SUB), :] for c in range(PACK_SUB)], axis=1)
    hi = pltpu.bitcast(words & jnp.uint32(0xFFFF0000), F32)
    lo = pltpu.bitcast(words << 16, F32)
    return hi, lo


def _inproj_kernel(x_ref, w_ref, cos_ref, sin_ref, qkv_ref, kvf_ref, qkr_ref, vg_ref, gm_ref):
    xb = x_ref[...].astype(BF16)

    def mm(lo, hi):
        return jnp.dot(xb, w_ref[:, lo:hi], preferred_element_type=F32)

    qkv_ref[:, 0:PQ_W] = mm(0, PQ_W).astype(BF16)
    qkv_ref[:, PQ_W:A_KVF] = mm(PQ_W, A_KVF).astype(BF16)
    kvf_ref[...] = mm(A_KVF, A_QR)
    cos = cos_ref[...]
    sin = sin_ref[...]
    for part in range(2):
        z = mm(A_QR + part * QR_W, A_QR + (part + 1) * QR_W)
        for h in range(N_HEADS_R):
            zh = z[:, h * DK_R:(h + 1) * DK_R]
            r = zh * cos + pltpu.roll(zh, DK_R // 2, axis=1) * sin
            if part == 1:
                r = r * (DK_R ** -0.5)
            c0 = part * QR_W + h * DK_R
            qkr_ref[:, c0:c0 + DK_R] = r.astype(BF16)
    vg_ref[...] = mm(A_VR, A_GM).astype(BF16)
    for half in range(2):
        lo = A_GM + half * D_MODEL
        gm_ref[:, half * D_MODEL:(half + 1) * D_MODEL] = jax.nn.sigmoid(mm(lo, lo + D_MODEL)).astype(BF16)


def _inproj(x2d, w_bf, cos_tab, sin_tab, tm):
    t = x2d.shape[0]
    nper = cos_tab.shape[0] // tm
    row = lambda i: (i, 0)
    return pl.pallas_call(
        _inproj_kernel,
        grid=(t // tm,),
        in_specs=[
            pl.BlockSpec((tm, D_MODEL), row),
            pl.BlockSpec((D_MODEL, A_W), lambda i: (0, 0)),
            pl.BlockSpec((tm, DK_R), lambda i: (i % nper, 0)),
            pl.BlockSpec((tm, DK_R), lambda i: (i % nper, 0)),
        ],
        out_specs=[
            pl.BlockSpec((tm, A_KVF), row),
            pl.BlockSpec((tm, 2 * KA_W), row),
            pl.BlockSpec((tm, 2 * QR_W), row),
            pl.BlockSpec((tm, 2 * QR_W), row),
            pl.BlockSpec((tm, 2 * D_MODEL), row),
        ],
        out_shape=[
            jax.ShapeDtypeStruct((t, A_KVF), BF16),
            jax.ShapeDtypeStruct((t, 2 * KA_W), F32),
            jax.ShapeDtypeStruct((t, 2 * QR_W), BF16),
            jax.ShapeDtypeStruct((t, 2 * QR_W), BF16),
            jax.ShapeDtypeStruct((t, 2 * D_MODEL), BF16),
        ],
        compiler_params=_cparams(("parallel",)),
        name="inproj",
    )(x2d, w_bf, cos_tab, sin_tab)


def _attend(q4, k, v, bias, sink):
    s = lax.dot_general(q4, k, (((1,), (1,)), ((), ())), preferred_element_type=F32) + bias
    m = jnp.maximum(jnp.max(s, axis=-1, keepdims=True), sink)
    e = jnp.exp(s - m)
    p = e / (jnp.sum(e, axis=-1, keepdims=True) + jnp.exp(sink - m))
    return jnp.dot(p.astype(BF16), v, preferred_element_type=F32)


KEY_PAD = 256


def _attend_sink_column(q4, k, v, bias, fill, valid):
    s = lax.dot_general(q4, k, (((1,), (1,)), ((), ())), preferred_element_type=F32)
    s = jnp.where(valid, s + bias, fill)
    e = jnp.exp(s - jnp.max(s, axis=-1, keepdims=True)).astype(BF16)
    dv = v.shape[1]
    ones = jnp.ones((KEY_PAD, dv), BF16)
    v_rows = jnp.concatenate([v, jnp.zeros((KEY_PAD - WINDOW - CHUNK, dv), BF16)], axis=0)
    both = jnp.dot(e, jnp.concatenate([v_rows, ones], axis=1), preferred_element_type=F32)
    return both[:, 0:dv] / both[:, dv:]


def _attn_prompt_kernel(q_ref, kvc_ref, kvp_ref, bias_ref, fill_ref, o_ref, kv_buf, *, n_chunks):
    i = pl.program_id(1)
    qb = n_chunks * CHUNK
    kv_buf[0:WINDOW, :] = kvp_ref[...]
    kv_buf[WINDOW:WINDOW + qb, :] = kvc_ref[...]
    kv_buf[WINDOW + qb:, :] = jnp.zeros((KEY_PAD - WINDOW - CHUNK, PKV_W), BF16)
    col = lax.broadcasted_iota(I32, (GROUP_A * CHUNK, KEY_PAD), 1)
    key_col = jnp.where(col < WINDOW + CHUNK, col, -1)

    def chunk(j, carry):
        r0 = pl.multiple_of(j * CHUNK, CHUNK)
        valid = key_col >= jnp.maximum(WINDOW - (i * n_chunks + j) * CHUNK, 0)
        for kv in range(N_KV_A):
            q4 = jnp.concatenate(
                [q_ref[pl.ds(r0, CHUNK), (kv * GROUP_A + g) * HEAD_PAD:(kv * GROUP_A + g + 1) * HEAD_PAD]
                 for g in range(GROUP_A)], axis=0)
            k = kv_buf[pl.ds(r0, KEY_PAD), kv * HEAD_PAD:(kv + 1) * HEAD_PAD]
            v = kv_buf[pl.ds(r0, WINDOW + CHUNK), (N_KV_A + kv) * HEAD_PAD:(N_KV_A + kv + 1) * HEAD_PAD]
            o4 = _attend_sink_column(q4, k, v, bias_ref[kv], fill_ref[kv], valid)
            for g in range(GROUP_A):
                c0 = (kv * GROUP_A + g) * HEAD_PAD
                o_ref[pl.ds(r0, CHUNK), c0:c0 + HEAD_PAD] = o4[g * CHUNK:(g + 1) * CHUNK].astype(BF16)
        return carry

    lax.fori_loop(0, n_chunks, chunk, 0, unroll=8 if n_chunks % 8 == 0 else 1)


def _attn_prompt(qkv, bias, fill, b, s):
    qb = min(512, s)
    n_chunks = qb // CHUNK
    nq = s // qb
    per = qb // WINDOW
    kv_col = PQ_W // PKV_W
    return pl.pallas_call(
        functools.partial(_attn_prompt_kernel, n_chunks=n_chunks),
        grid=(b, nq),
        in_specs=[
            pl.BlockSpec((qb, PQ_W), lambda bi, i: (bi * nq + i, 0)),
            pl.BlockSpec((qb, PKV_W), lambda bi, i: (bi * nq + i, kv_col)),
            pl.BlockSpec((WINDOW, PKV_W), lambda bi, i: (jnp.maximum((bi * nq + i) * per - 1, 0), kv_col)),
            pl.BlockSpec((N_KV_A, GROUP_A * CHUNK, KEY_PAD), lambda bi, i: (0, 0, 0)),
            pl.BlockSpec((N_KV_A, GROUP_A * CHUNK, KEY_PAD), lambda bi, i: (0, 0, 0)),
        ],
        out_specs=pl.BlockSpec((qb, PQ_W), lambda bi, i: (bi * nq + i, 0)),
        out_shape=jax.ShapeDtypeStruct((b * s, PQ_W), BF16),
        scratch_shapes=[pltpu.VMEM((qb + KEY_PAD - CHUNK, PKV_W), BF16)],
        compiler_params=_cparams(("parallel", "parallel")),
        name="attn_prompt",
    )(qkv, qkv, qkv, bias, fill)


def _attn_sample_kernel(q_ref, kvf_ref, ck_ref, cv_ref, bias_ref, sink_ref, o_ref, nk_ref, nv_ref, *, l):
    k_all = jnp.concatenate([ck_ref[0], kvf_ref[:, 0:KA_W]], axis=0)
    v_all = jnp.concatenate([cv_ref[0], kvf_ref[:, KA_W:2 * KA_W]], axis=0)
    nk_ref[0] = k_all[l:]
    nv_ref[0] = v_all[l:]
    kb = k_all.astype(BF16)
    vb = v_all.astype(BF16)
    o_ref[...] = jnp.zeros_like(o_ref)
    for kv in range(N_KV_A):
        q4 = jnp.concatenate(
            [q_ref[:, (kv * GROUP_A + g) * HEAD_PAD:(kv * GROUP_A + g) * HEAD_PAD + HEAD_DIM_A]
             for g in range(GROUP_A)], axis=0)
        o4 = _attend(q4, kb[:, kv * HEAD_DIM_A:(kv + 1) * HEAD_DIM_A], vb[:, kv * HEAD_DIM_A:(kv + 1) * HEAD_DIM_A],
                     bias_ref[kv], sink_ref[kv])
        for g in range(GROUP_A):
            c0 = (kv * GROUP_A + g) * HEAD_PAD
            o_ref[:, c0:c0 + HEAD_DIM_A] = o4[g * l:(g + 1) * l].astype(BF16)


def _attn_sample(qkv, kvf, cache_k, cache_v, bias, sink, bd, l):
    cache_spec = pl.BlockSpec((1, WINDOW, KA_W), lambda bi: (bi, 0, 0))
    cache_shape = jax.ShapeDtypeStruct((bd, WINDOW, KA_W), F32)
    return pl.pallas_call(
        functools.partial(_attn_sample_kernel, l=l),
        grid=(bd,),
        in_specs=[
            pl.BlockSpec((l, A_KVF), lambda bi: (bi, 0)),
            pl.BlockSpec((l, 2 * KA_W), lambda bi: (bi, 0)),
            cache_spec, cache_spec,
            pl.BlockSpec((N_KV_A, GROUP_A * l, WINDOW + l), lambda bi: (0, 0, 0)),
            pl.BlockSpec((N_KV_A, GROUP_A * l, 1), lambda bi: (0, 0, 0)),
        ],
        out_specs=[pl.BlockSpec((l, PQ_W), lambda bi: (bi, 0)), cache_spec, cache_spec],
        out_shape=[jax.ShapeDtypeStruct((bd * l, PQ_W), BF16), cache_shape, cache_shape],
        compiler_params=_cparams(("parallel",)),
        name="attn_sample",
    )(qkv, kvf, cache_k, cache_v, bias, sink)


def _retention_kernel(q_ref, k_ref, v_ref, g_ref, s0_ref, di_ref, qd_ref, kd_ref, gc_ref, gain_ref,
                      o_ref, s_ref, *, n_chunks, c):
    @pl.when(pl.program_id(1) == 0)
    def _():
        s_ref[...] = s0_ref[...]

    def chunk(n, carry):
        r0 = pl.multiple_of(n * c, c)
        for h in range(N_HEADS_R):
            cols = slice(h * DK_R, (h + 1) * DK_R)
            qc = q_ref[pl.ds(r0, c), cols]
            kc = k_ref[pl.ds(r0, c), cols]
            vc = v_ref[pl.ds(r0, c), cols]
            state = s_ref[0, h]
            sc = lax.dot_general(qc, kc, (((1,), (1,)), ((), ())), preferred_element_type=F32) * di_ref[h]
            o = jnp.dot(sc.astype(BF16), vc, preferred_element_type=F32)
            q_dec = (qc.astype(F32) * qd_ref[h]).astype(BF16)
            o = o + jnp.dot(q_dec, state.astype(BF16), preferred_element_type=F32)
            k_dec = (kc.astype(F32) * kd_ref[h]).astype(BF16)
            s_ref[0, h] = gc_ref[h] * state + lax.dot_general(k_dec, vc, (((0,), (0,)), ((), ())),
                                                              preferred_element_type=F32)
            mu = jnp.mean(o, axis=-1, keepdims=True)
            d = o - mu
            var = jnp.mean(d * d, axis=-1, keepdims=True)
            y = d * lax.rsqrt(var + LN_EPS) * gain_ref[:, cols]
            o_ref[pl.ds(r0, c), cols] = (y * _silu(g_ref[pl.ds(r0, c), cols].astype(F32))).astype(BF16)
        return carry

    lax.fori_loop(0, n_chunks, chunk, 0, unroll=4 if n_chunks % 4 == 0 else 1)


def _retention(qkr, vg, s0, tabs, gain, b, s, c):
    di, qd, kd, gc = tabs
    st = min(1024, s)
    nt = s // st
    seq = lambda col: pl.BlockSpec((st, QR_W), lambda bi, j: (bi * nt + j, col))
    full = lambda a: pl.BlockSpec(a.shape, lambda bi, j: (0,) * a.ndim)
    st_spec = pl.BlockSpec((1, N_HEADS_R, DK_R, DK_R), lambda bi, j: (bi, 0, 0, 0))
    return pl.pallas_call(
        functools.partial(_retention_kernel, n_chunks=st // c, c=c),
        grid=(b, nt),
        in_specs=[seq(0), seq(1), seq(0), seq(1), st_spec, full(di), full(qd), full(kd), full(gc), full(gain)],
        out_specs=[seq(0), st_spec],
        out_shape=[jax.ShapeDtypeStruct((b * s, QR_W), BF16), jax.ShapeDtypeStruct((b, N_HEADS_R, DK_R, DK_R), F32)],
        compiler_params=_cparams(("parallel", "arbitrary")),
        name="retention",
    )(qkr, qkr, vg, vg, s0, di, qd, kd, gc, gain)


def _mix_kernel(x_ref, oa_ref, or_ref, g_ref, wa_ref, wb_ref, wo_ref, lg_ref, lb_ref, wrt_ref, x1_ref, st_ref, xp_ref):
    a = jnp.dot(oa_ref[...], wa_ref[...], preferred_element_type=F32)
    b = jnp.dot(or_ref[...], wb_ref[...], preferred_element_type=F32)
    merged = g_ref[:, 0:D_MODEL].astype(F32) * a + g_ref[:, D_MODEL:].astype(F32) * b
    y = jnp.dot(merged.astype(BF16), wo_ref[...], preferred_element_type=F32)
    x1 = _layer_norm(DN_ALPHA * x_ref[...] + y, lg_ref[...], lb_ref[...])
    x1_ref[...] = x1
    _store_packed(xp_ref, x1)
    logits = lax.dot_general(wrt_ref[...], x1.astype(BF16), (((1,), (1,)), ((), ())), preferred_element_type=F32)
    st_ref[...] = jax.nn.sigmoid(logits)


def _mix(x2d, oa, orr, gm, wa, wb, wo, lg, lb, wrt, tm):
    t = x2d.shape[0]
    row = lambda w: pl.BlockSpec((tm, w), lambda i: (i, 0))
    full = lambda a: pl.BlockSpec(a.shape, lambda i: (0,) * a.ndim)
    return pl.pallas_call(
        _mix_kernel,
        grid=(t // tm,),
        in_specs=[row(D_MODEL), row(PQ_W), row(QR_W), row(2 * D_MODEL),
                  full(wa), full(wb), full(wo), full(lg), full(lb), full(wrt)],
        out_specs=[row(D_MODEL), pl.BlockSpec((N_EXPERTS, tm), lambda i: (0, i)),
                   pl.BlockSpec((tm * PACK_SUB, 128), lambda i: (i, 0))],
        out_shape=[jax.ShapeDtypeStruct((t, D_MODEL), F32), jax.ShapeDtypeStruct((N_EXPERTS, t), F32),
                   jax.ShapeDtypeStruct((t * PACK_SUB, 128), U32)],
        compiler_params=_cparams(("parallel",)),
        name="mix",
    )(x2d, oa, orr, gm, wa, wb, wo, lg, lb, wrt)


def _route_kernel(s_ref, bias_ref, cnt0_ref, e_ref, w_ref, r_ref, cnt_ref, carry_ref):
    @pl.when(pl.program_id(0) == 0)
    def _():
        carry_ref[...] = cnt0_ref[...].astype(F32)

    s = s_ref[...]
    tl = s.shape[1]
    choice = s + bias_ref[...]
    row = lax.broadcasted_iota(I32, (N_EXPERTS, tl), 0)
    row_g = lax.broadcasted_iota(I32, (GROUP_SIZE, tl), 0)
    neg = -jnp.inf

    scores = []
    for g in range(N_GROUPS):
        blk = choice[g * GROUP_SIZE:(g + 1) * GROUP_SIZE]
        m1 = jnp.max(blk, axis=0, keepdims=True)
        i1 = jnp.min(jnp.where(blk == m1, row_g, GROUP_SIZE), axis=0, keepdims=True)
        m2 = jnp.max(jnp.where(row_g == i1, neg, blk), axis=0, keepdims=True)
        scores.append(m1 + m2)
    sc = jnp.concatenate(scores, axis=0)
    gi = lax.broadcasted_iota(I32, sc.shape, 0)
    grank = jnp.zeros(sc.shape, I32)
    for g in range(N_GROUPS):
        other = sc[g:g + 1]
        ahead = jnp.where(other > sc, 1, jnp.where(other == sc, jnp.where(gi > g, 1, 0), 0))
        grank = grank + ahead
    cm = jnp.concatenate(
        [jnp.where(grank[g:g + 1] < TOPK_GROUPS, choice[g * GROUP_SIZE:(g + 1) * GROUP_SIZE], neg)
         for g in range(N_GROUPS)], axis=0)

    experts, weights = [], []
    for _ in range(TOP_K):
        m = jnp.max(cm, axis=0, keepdims=True)
        idx = jnp.min(jnp.where(cm == m, row, N_EXPERTS), axis=0, keepdims=True)
        hit = row == idx
        weights.append(jnp.sum(jnp.where(hit, s, 0.0), axis=0, keepdims=True))
        cm = jnp.where(hit, neg, cm)
        experts.append(idx)
    e8 = jnp.concatenate(experts, axis=0)
    w8 = jnp.concatenate(weights, axis=0)
    e_ref[...] = e8
    w_ref[...] = w8 / jnp.sum(w8, axis=0, keepdims=True) * ROUTED_SCALE

    before = (lax.broadcasted_iota(I32, (tl, tl), 0) < lax.broadcasted_iota(I32, (tl, tl), 1))
    before = jnp.where(before, 1.0, 0.0).astype(BF16)
    carry = carry_ref[...]
    ranks = []
    for k in range(TOP_K):
        hit = row == experts[k]
        onehot = jnp.where(hit, 1.0, 0.0)
        prefix = jnp.dot(onehot.astype(BF16), before, preferred_element_type=F32)
        ranks.append(jnp.sum(jnp.where(hit, prefix + carry, 0.0), axis=0, keepdims=True))
        carry = carry + jnp.sum(onehot, axis=1, keepdims=True)
    r_ref[...] = jnp.concatenate(ranks, axis=0).astype(I32)
    carry_ref[...] = carry
    cnt_ref[...] = carry.astype(I32)


def _route(st, bias_col, counts0, tl):
    t = st.shape[1]
    tok = pl.BlockSpec((TOP_K, tl), lambda i: (0, i))
    return pl.pallas_call(
        _route_kernel,
        grid=(t // tl,),
        in_specs=[pl.BlockSpec((N_EXPERTS, tl), lambda i: (0, i)), pl.BlockSpec((N_EXPERTS, 1), lambda i: (0, 0)),
                  pl.BlockSpec((N_EXPERTS, 1), lambda i: (0, 0))],
        out_specs=[tok, tok, tok, pl.BlockSpec((N_EXPERTS, 1), lambda i: (0, 0))],
        out_shape=[jax.ShapeDtypeStruct((TOP_K, t), I32), jax.ShapeDtypeStruct((TOP_K, t), F32),
                   jax.ShapeDtypeStruct((TOP_K, t), I32), jax.ShapeDtypeStruct((N_EXPERTS, 1), I32)],
        scratch_shapes=[pltpu.VMEM((N_EXPERTS, 1), F32)],
        compiler_params=_cparams(("arbitrary",)),
        name="route",
    )(st, bias_col, counts0)


def _dest_kernel(e_ref, r_ref, ps_ref, d_ref):
    tl = e_ref.shape[1]
    row = lax.broadcasted_iota(I32, (N_EXPERTS, tl), 0)
    ps = ps_ref[...]
    starts = [jnp.sum(jnp.where(row == e_ref[k:k + 1, :], ps, 0.0), axis=0, keepdims=True) for k in range(TOP_K)]
    d_ref[...] = jnp.concatenate(starts, axis=0).astype(I32) + r_ref[...]


def _dest(e8, r8, pstart_col, tl):
    t = e8.shape[1]
    tok = pl.BlockSpec((TOP_K, tl), lambda i: (0, i))
    return pl.pallas_call(
        _dest_kernel,
        grid=(t // tl,),
        in_specs=[tok, tok, pl.BlockSpec((N_EXPERTS, 1), lambda i: (0, 0))],
        out_specs=tok,
        out_shape=jax.ShapeDtypeStruct((TOP_K, t), I32),
        compiler_params=_cparams(("parallel",)),
        name="dest",
    )(e8, r8, pstart_col)


TOKEN_GROUP = 8


def _row_copy(src, src_tok, dst, dst_tok, sem):
    s0 = pl.multiple_of(src_tok * PACK_SUB, PACK_SUB)
    d0 = pl.multiple_of(dst_tok * PACK_SUB, PACK_SUB)
    return pltpu.make_async_copy(src.at[pl.ds(s0, PACK_SUB)], dst.at[pl.ds(d0, PACK_SUB)], sem)


def _wait_bytes_of(ref, sem):
    pltpu.make_async_copy(ref, ref, sem).wait()


def _dispatch_kernel(d_ref, x_ref, *refs, td):
    xs_hbm, sem = refs[-2:]
    def start(g, carry):
        for j in range(TOKEN_GROUP):
            t = g * TOKEN_GROUP + j
            for k in range(TOP_K):
                _row_copy(x_ref, t, xs_hbm, d_ref[t * TOP_K + k], sem).start(priority=k % 2)
        return carry

    lax.fori_loop(0, td // TOKEN_GROUP, start, 0)
    for k in range(TOP_K):
        _wait_bytes_of(x_ref, sem)


def _dispatch(d8, xp, n_rows, td, xs_prev=None):
    t = d8.shape[0] // TOP_K
    in_specs = [pl.BlockSpec((td * TOP_K,), lambda i: (i,), memory_space=pltpu.SMEM),
                pl.BlockSpec((td * PACK_SUB, 128), lambda i: (i, 0))]
    args = [d8, xp]
    if xs_prev is not None:
        in_specs.append(pl.BlockSpec(memory_space=pl.ANY))
        args.append(xs_prev)
    return pl.pallas_call(
        functools.partial(_dispatch_kernel, td=td),
        grid=(t // td,),
        in_specs=in_specs,
        out_specs=pl.BlockSpec(memory_space=pl.ANY),
        scratch_shapes=[pltpu.SemaphoreType.DMA(())],
        out_shape=jax.ShapeDtypeStruct((n_rows * PACK_SUB, 128), U32),
        input_output_aliases={} if xs_prev is None else {2: 0},
        compiler_params=_cparams(("arbitrary",)),
        name="dispatch",
    )(*args)


X_RING = 3


def _moe_kernel(be_ref, nv_ref, x_hbm, wg_ref, wu_ref, wd_ref, y_ref, x_ring, x_sem, wg_bf, wu_bf, wd_bf):
    j = pl.program_id(0)
    n = pl.num_programs(0)
    nv = nv_ref[j]
    blk_rows = MOE_ROWS * PACK_SUB

    def x_copy(blk):
        r0 = pl.multiple_of(blk * blk_rows, blk_rows)
        slot = blk % X_RING
        return pltpu.make_async_copy(x_hbm.at[pl.ds(r0, blk_rows)], x_ring.at[slot], x_sem.at[slot])

    def start_if_real(blk):
        @pl.when((blk < n) & (nv_ref[jnp.minimum(blk, n - 1)] > 0))
        def _():
            x_copy(blk).start()

    @pl.when(j == 0)
    def _():
        for blk in range(X_RING - 1):
            start_if_real(jnp.int32(blk))

    start_if_real(j + X_RING - 1)

    @pl.when((j == 0) | (be_ref[j] != be_ref[jnp.maximum(j - 1, 0)]))
    def _():
        wg_bf[...] = wg_ref[0].astype(BF16)
        wu_bf[...] = wu_ref[0].astype(BF16)
        wd_bf[...] = wd_ref[0].astype(BF16)

    @pl.when(nv > 0)
    def _():
        x_copy(j).wait()
        hi, lo = _load_packed(x_ring.at[j % X_RING], MOE_ROWS)
        valid = lax.broadcasted_iota(I32, hi.shape, 0) < nv
        xb = jnp.concatenate([jnp.where(valid, hi, 0.0).astype(BF16), jnp.where(valid, lo, 0.0).astype(BF16)], axis=1)
        g = jnp.dot(xb, wg_bf[...], preferred_element_type=F32)
        u = jnp.dot(xb, wu_bf[...], preferred_element_type=F32)
        h = (_silu(g) * u).astype(BF16)
        _store_packed(y_ref, jnp.dot(h, wd_bf[...], preferred_element_type=F32))

    @pl.when(nv == 0)
    def _():
        y_ref[...] = jnp.zeros_like(y_ref)


def _moe(blk_exp, blk_valid, xs, wg, wu, wd):
    n_blocks = blk_exp.shape[0]
    rows = pl.BlockSpec((MOE_ROWS * PACK_SUB, 128), lambda j, be, nv: (j, 0))
    return pl.pallas_call(
        _moe_kernel,
        grid_spec=pltpu.PrefetchScalarGridSpec(
            num_scalar_prefetch=2,
            grid=(n_blocks,),
            in_specs=[pl.BlockSpec(memory_space=pl.ANY),
                      pl.BlockSpec((1, D_MODEL, D_EXPERT), lambda j, be, nv: (be[j], 0, 0)),
                      pl.BlockSpec((1, D_MODEL, D_EXPERT), lambda j, be, nv: (be[j], 0, 0)),
                      pl.BlockSpec((1, D_EXPERT, D_MODEL), lambda j, be, nv: (be[j], 0, 0))],
            out_specs=rows,
            scratch_shapes=[pltpu.VMEM((X_RING, MOE_ROWS * PACK_SUB, 128), U32), pltpu.SemaphoreType.DMA((X_RING,)),
                            pltpu.VMEM((D_MODEL, D_EXPERT), BF16), pltpu.VMEM((D_MODEL, D_EXPERT), BF16),
                            pltpu.VMEM((D_EXPERT, D_MODEL), BF16)],
        ),
        out_shape=jax.ShapeDtypeStruct(xs.shape, U32),
        compiler_params=_cparams(("arbitrary",)),
        name="moe",
    )(blk_exp, blk_valid, xs, wg, wu, wd)


COMBINE_ROWS = 32


def _gather_rows(ys_hbm, d_ref, slot_buf, sem, t0, n):
    for j in range(n):
        t = t0 + j
        for k in range(TOP_K):
            _row_copy(ys_hbm, d_ref[t * TOP_K + k], slot_buf.at[k], t, sem).start(priority=k % 2)


def _final_kernel(d_ref, dnext_ref, x1_ref, w_ref, p_ref, wsg_ref, wsu_ref, wsd_ref, lg_ref, lb_ref,
                  wpp_ref, wpg_ref, ys_hbm, out_ref, gbuf, ybuf, sem, *, tf):
    i = pl.program_id(0)
    slot = i % 2

    @pl.when(i == 0)
    def _():
        def first(g, carry):
            _gather_rows(ys_hbm, d_ref, gbuf.at[0], sem.at[0], g * TOKEN_GROUP, TOKEN_GROUP)
            return carry

        lax.fori_loop(0, tf // TOKEN_GROUP, first, 0)

    x1 = x1_ref[...]
    xb = x1.astype(BF16)
    hs = _silu(jnp.dot(xb, wsg_ref[...], preferred_element_type=F32)) * jnp.dot(xb, wsu_ref[...],
                                                                               preferred_element_type=F32)
    y = jnp.dot(hs.astype(BF16), wsd_ref[...], preferred_element_type=F32)

    _wait_bytes_of(gbuf.at[slot], sem.at[slot])

    def combine(sb, carry):
        r0 = pl.multiple_of(sb * COMBINE_ROWS, COMBINE_ROWS)
        _gather_rows(ys_hbm, dnext_ref, gbuf.at[1 - slot], sem.at[1 - slot], r0, COMBINE_ROWS)
        y_hi = jnp.zeros((COMBINE_ROWS, PACK_W), F32)
        y_lo = jnp.zeros((COMBINE_ROWS, PACK_W), F32)
        for k in range(TOP_K):
            rows = gbuf.at[slot, k, pl.ds(pl.multiple_of(r0 * PACK_SUB, COMBINE_ROWS * PACK_SUB),
                                          COMBINE_ROWS * PACK_SUB)]
            hi, lo = _load_packed(rows, COMBINE_ROWS)
            wk = w_ref[pl.ds(r0, COMBINE_ROWS), k:k + 1]
            y_hi = y_hi + wk * hi
            y_lo = y_lo + wk * lo
        ybuf[pl.ds(r0, COMBINE_ROWS), 0:PACK_W] = y_hi
        ybuf[pl.ds(r0, COMBINE_ROWS), PACK_W:] = y_lo
        return carry

    lax.fori_loop(0, tf // COMBINE_ROWS, combine, 0)

    @pl.when(i == pl.num_programs(0) - 1)
    def _():
        _wait_bytes_of(gbuf.at[1 - slot], sem.at[1 - slot])

    y = y + ybuf[...]
    x2 = _layer_norm(DN_ALPHA * x1 + y, lg_ref[...], lb_ref[...])
    gate = jax.nn.sigmoid(jnp.dot(x2.astype(BF16), wpg_ref[...], preferred_element_type=F32))
    out_ref[...] = x2 + gate * jnp.dot(p_ref[...].astype(BF16), wpp_ref[...], preferred_element_type=F32)


def _final(d8, x1, w_tok, p2d, wsg, wsu, wsd, lg, lb, wpp, wpg, ys, tf):
    t = x1.shape[0]
    n_tiles = t // tf
    row = lambda w: pl.BlockSpec((tf, w), lambda i: (i, 0))
    full = lambda a: pl.BlockSpec(a.shape, lambda i: (0,) * a.ndim)
    return pl.pallas_call(
        functools.partial(_final_kernel, tf=tf),
        grid=(n_tiles,),
        in_specs=[pl.BlockSpec((tf * TOP_K,), lambda i: (i,), memory_space=pltpu.SMEM),
                  pl.BlockSpec((tf * TOP_K,), lambda i: (jnp.minimum(i + 1, n_tiles - 1),), memory_space=pltpu.SMEM),
                  row(D_MODEL), row(TOP_K), row(D_PLE),
                  full(wsg), full(wsu), full(wsd), full(lg), full(lb), full(wpp), full(wpg),
                  pl.BlockSpec(memory_space=pl.ANY)],
        out_specs=row(D_MODEL),
        scratch_shapes=[pltpu.VMEM((2, TOP_K, tf * PACK_SUB, 128), U32), pltpu.VMEM((tf, D_MODEL), F32),
                        pltpu.SemaphoreType.DMA((2,))],
        out_shape=jax.ShapeDtypeStruct((t, D_MODEL), F32),
        compiler_params=_cparams(("arbitrary",)),
        name="final",
    )(d8, d8, x1, w_tok, p2d, wsg, wsu, wsd, lg, lb, wpp, wpg, ys)


def _t5_bucket(rel):
    nb = N_BUCKETS // 2
    max_exact = nb // 2
    ret = jnp.where(rel > 0, nb, 0)
    n = jnp.abs(rel)
    nf = jnp.maximum(n, 1).astype(F32)
    large = max_exact + (jnp.log(nf / max_exact) / math.log(MAX_DIST / max_exact) * (nb - max_exact)).astype(I32)
    large = jnp.minimum(large, nb - 1)
    return ret + jnp.where(n < max_exact, n, large)


def _rel_bias(table, q_len, k_len):
    rel = (jnp.arange(k_len, dtype=I32)[None, :] - WINDOW) - jnp.arange(q_len, dtype=I32)[:, None]
    b = table[_t5_bucket(rel)]
    return jnp.transpose(b, (2, 0, 1)).reshape(N_KV_A, GROUP_A * q_len, k_len).astype(F32)


def _rotary_tables(pos):
    half = DK_R // 2
    inv = ROPE_BASE ** (-jnp.arange(half, dtype=F32) / half)
    ang = pos.astype(F32)[:, None] * inv[None, :]
    cos, sin = jnp.cos(ang), jnp.sin(ang)
    return jnp.concatenate([cos, cos], -1), jnp.concatenate([-sin, sin], -1)


def _decay_tables(c):
    log_gamma = jnp.log1p(-jnp.exp2(-5.0 - jnp.arange(N_HEADS_R, dtype=F32)))
    idx = jnp.arange(c, dtype=F32)
    di = jnp.exp(jnp.abs(idx[:, None] - idx[None, :])[None] * log_gamma[:, None, None])
    qd = jnp.exp((idx[None, :] + 1.0) * log_gamma[:, None])
    kd = jnp.exp((c - 1.0 - idx[None, :]) * log_gamma[:, None])
    gc = jnp.exp(c * log_gamma)
    bc = lambda a: jnp.broadcast_to(a[:, :, None], (N_HEADS_R, c, DK_R))
    return di, bc(qd), bc(kd), jnp.broadcast_to(gc[:, None, None], (N_HEADS_R, DK_R, DK_R))


def _pad_heads(w, n_heads):
    rows = w.shape[0]
    w = w.reshape(rows, n_heads, HEAD_DIM_A)
    w = jnp.pad(w, ((0, 0), (0, 0), (0, HEAD_PAD - HEAD_DIM_A)))
    return w.reshape(rows, n_heads * HEAD_PAD)


def _moe_ffn(groups, wts):
    tiles = [min(256, g[0].shape[0]) for g in groups]
    counts = jnp.zeros((N_EXPERTS, 1), I32)
    routed = []
    for (x1, xp, st, p2d), tile in zip(groups, tiles):
        e8, w8, r8, counts = _route(st, wts['router_bias'], counts, tile)
        routed.append((e8, w8, r8))
    counts = counts[:, 0]
    padded = (counts + MOE_ROWS - 1) // MOE_ROWS * MOE_ROWS
    pad_end = jnp.cumsum(padded)
    pstart = (pad_end - padded).astype(I32)
    n_rows = sum(g[0].shape[0] for g in groups) * TOP_K + N_EXPERTS * MOE_ROWS
    blk_start = jnp.arange(n_rows // MOE_ROWS, dtype=I32) * MOE_ROWS
    blk_exp = jnp.minimum(jnp.sum(blk_start[:, None] >= pad_end[None, :], axis=1), N_EXPERTS - 1).astype(I32)
    own = blk_exp[:, None] == jnp.arange(N_EXPERTS, dtype=I32)[None, :]
    blk_end = jnp.sum(jnp.where(own, (pstart + counts)[None, :], 0), axis=1)
    blk_valid = jnp.clip(blk_end - blk_start, 0, MOE_ROWS).astype(I32)
    pstart_col = pstart.astype(F32).reshape(N_EXPERTS, 1)
    dests, xs = [], None
    for (x1, xp, st, p2d), (e8, w8, r8), tile in zip(groups, routed, tiles):
        d8 = _dest(e8, r8, pstart_col, tile).T.reshape(-1)
        xs = _dispatch(d8, xp, n_rows, tile, xs)
        dests.append(d8)
    ys = _moe(blk_exp, blk_valid, xs, wts['w_exp_gate'], wts['w_exp_up'], wts['w_exp_down'])
    return [_final(d8, x1, w8.T, p2d, wts['w_sh_gate'], wts['w_sh_up'], wts['w_sh_down'],
                   wts['ln2_g'], wts['ln2_b'], wts['w_ple_proj'], wts['w_ple_gate'], ys, tile)
            for (x1, xp, st, p2d), (e8, w8, r8), d8, tile in zip(groups, routed, dests, tiles)]


def _mix_group(x2d, oa, orr, gm, p2d, wts):
    x1, st, xp = _mix(x2d, oa, orr, gm, wts['w_branch_attn'], wts['w_branch_ret'], wts['w_out'],
                      wts['ln1_g'], wts['ln1_b'], wts['w_router_t'], min(512, x2d.shape[0]))
    return x1, xp, st, p2d


def kernel(x_prompt, x_sample, cache_attn_k, cache_attn_v, state_retention, p_prompt, p_sample,
           w_in, attn_sinks, rel_bias_table, ret_gn_gain, w_branch_attn, w_branch_ret, w_out,
           ln1_g, ln1_b, w_router, router_bias, w_exp_gate, w_exp_up, w_exp_down,
           w_sh_gate, w_sh_up, w_sh_down, ln2_g, ln2_b, w_ple_proj, w_ple_gate):
    b, s, _ = x_prompt.shape
    bd, l, _ = x_sample.shape
    i = 0
    row = lambda a: a.reshape(1, -1).astype(F32)
    wts = dict(
        w_branch_attn=_pad_heads(w_branch_attn[i].astype(BF16).T, N_HEADS_A).T,
        w_branch_ret=w_branch_ret[i].astype(BF16),
        w_out=w_out[i].astype(BF16), ln1_g=row(ln1_g[i]), ln1_b=row(ln1_b[i]),
        w_router_t=w_router[i].T.astype(BF16), router_bias=router_bias[i].reshape(N_EXPERTS, 1).astype(F32),
        w_exp_gate=w_exp_gate[i], w_exp_up=w_exp_up[i], w_exp_down=w_exp_down[i],
        w_sh_gate=w_sh_gate[i].astype(BF16), w_sh_up=w_sh_up[i].astype(BF16), w_sh_down=w_sh_down[i].astype(BF16),
        ln2_g=row(ln2_g[i]), ln2_b=row(ln2_b[i]),
        w_ple_proj=w_ple_proj[i].astype(BF16), w_ple_gate=w_ple_gate[i].astype(BF16))
    w_in_bf = w_in[i].astype(BF16)
    w_in_bf = jnp.concatenate([_pad_heads(w_in_bf[:, :QA_W] * (HEAD_DIM_A ** -0.5), N_HEADS_A),
                               _pad_heads(w_in_bf[:, OFF_KV:OFF_QR], 2 * N_KV_A), w_in_bf[:, OFF_KV:]], axis=1)
    gain = row(ret_gn_gain[i])
    sinks = attn_sinks[i].astype(F32).reshape(N_KV_A, GROUP_A)

    xp = x_prompt.reshape(b * s, D_MODEL)
    cos_p, sin_p = _rotary_tables(jnp.arange(s, dtype=I32))
    qkv, kvf, qkr, vg, gm = _inproj(xp, w_in_bf, cos_p, sin_p, min(512, s))
    bias_p = jnp.pad(_rel_bias(rel_bias_table, CHUNK, WINDOW + CHUNK), ((0, 0), (0, 0), (0, KEY_PAD - WINDOW - CHUNK)))
    sink_p = jnp.repeat(sinks, CHUNK, axis=1)[..., None]
    fill_p = jnp.where(jnp.arange(KEY_PAD)[None, None, :] == WINDOW + CHUNK, sink_p, NEG_INF).astype(F32)
    oa = _attn_prompt(qkv, bias_p, fill_p, b, s)
    s0 = jnp.zeros((b, N_HEADS_R, DK_R, DK_R), F32)
    orr, ns_p = _retention(qkr, vg, s0, _decay_tables(CHUNK), gain, b, s, CHUNK)
    group_p = _mix_group(xp, oa, orr, gm, p_prompt[i].reshape(b * s, D_PLE), wts)
    kv_tail = kvf.reshape(b, s, 2, N_KV_A, HEAD_DIM_A)[:, s - WINDOW:]
    nk_p, nv_p = kv_tail[:, :, 0], kv_tail[:, :, 1]

    xs = x_sample.reshape(bd * l, D_MODEL)
    cos_s, sin_s = _rotary_tables(jnp.tile(PAST_LEN + jnp.arange(l, dtype=I32), bd))
    qkv, kvf, qkr, vg, gm = _inproj(xs, w_in_bf, cos_s, sin_s, bd * l)
    sink_s = jnp.repeat(sinks, l, axis=1)[..., None]
    oa, nk_s, nv_s = _attn_sample(qkv, kvf, cache_attn_k[i].reshape(bd, WINDOW, KA_W),
                                  cache_attn_v[i].reshape(bd, WINDOW, KA_W),
                                  _rel_bias(rel_bias_table, l, WINDOW + l), sink_s, bd, l)
    orr, ns_s = _retention(qkr, vg, state_retention[i].astype(F32), _decay_tables(l), gain, bd, l, l)
    group_s = _mix_group(xs, oa, orr, gm, p_sample[i].reshape(bd * l, D_PLE), wts)
    y_p, y_s = _moe_ffn([group_p, group_s], wts)
    y_p, y_s = y_p.reshape(b, s, D_MODEL), y_s.reshape(bd, l, D_MODEL)
    shape_kv = (bd, WINDOW, N_KV_A, HEAD_DIM_A)
    return (y_p, y_s, nk_p[None], nv_p[None], ns_p[None],
            nk_s.reshape(shape_kv)[None], nv_s.reshape(shape_kv)[None], ns_s[None])
```

```python
import functools
import math

import jax
import jax.numpy as jnp
from jax import lax
from jax.experimental import pallas as pl
from jax.experimental.pallas import tpu as pltpu

F32 = jnp.float32
BF16 = jnp.bfloat16
I32 = jnp.int32

D_MODEL = 1024
CHUNK = 64
WINDOW = 128
N_HEADS_A = 8
N_KV_A = 2
GROUP_A = 4
HEAD_DIM_A = 64
N_BUCKETS = 32
MAX_DIST = 128
N_HEADS_R = 4
DK_R = 128
ROPE_BASE = 10000.0
N_EXPERTS = 256
TOP_K = 8
N_GROUPS = 8
GROUP_SIZE = N_EXPERTS // N_GROUPS
TOPK_GROUPS = 4
D_EXPERT = 256
ROUTED_SCALE = 2.5
D_PLE = 256
LN_EPS = 1e-5
NEG_INF = -1e30
PAST_LEN = 2048
DEPTH = 1
DN_ALPHA = (2 * DEPTH) ** 0.25

QA_W = N_HEADS_A * HEAD_DIM_A
KA_W = N_KV_A * HEAD_DIM_A
QR_W = N_HEADS_R * DK_R
OFF_KV = QA_W
OFF_QR = QA_W + 2 * KA_W
OFF_VR = OFF_QR + 2 * QR_W
OFF_GM = OFF_VR + 2 * QR_W
IN_W = OFF_GM + 2 * D_MODEL

HEAD_PAD = 128
PQ_W = N_HEADS_A * HEAD_PAD
PKV_W = 2 * N_KV_A * HEAD_PAD
A_KVF = PQ_W + PKV_W
A_QR = A_KVF + 2 * KA_W
A_VR = A_QR + 2 * QR_W
A_GM = A_VR + 2 * QR_W
A_W = A_GM + 2 * D_MODEL

MOE_ROWS = 512
VMEM_LIMIT = 56 * 1024 * 1024


def _cparams(sem, vmem=VMEM_LIMIT):
    return pltpu.CompilerParams(dimension_semantics=sem, vmem_limit_bytes=vmem)


def _layer_norm(h, g, b):
    mu = jnp.mean(h, axis=-1, keepdims=True)
    d = h - mu
    var = jnp.mean(d * d, axis=-1, keepdims=True)
    return d * lax.rsqrt(var + LN_EPS) * g + b


def _silu(x):
    return x * jax.nn.sigmoid(x)


U32 = jnp.uint32
PACK_W = D_MODEL // 2
PACK_SUB = PACK_W // 128


def _bf16_bits(x):
    return pltpu.bitcast(x.astype(BF16).astype(F32), U32)


def _store_packed(ref, x):
    n = x.shape[0]
    words = _bf16_bits(x[:, :PACK_W]) | (_bf16_bits(x[:, PACK_W:]) >> 16)
    for c in range(PACK_SUB):
        ref[pl.ds(c, n, stride=PACK_SUB), :] = words[:, c * 128:(c + 1) * 128]


def _load_packed(ref, n):
    words = jnp.concatenate([ref[pl.ds(c, n, stride=PACK_SUB), :] for c in range(PACK_SUB)], axis=1)
    hi = pltpu.bitcast(words & jnp.uint32(0xFFFF0000), F32)
    lo = pltpu.bitcast(words << 16, F32)
    return hi, lo


def _inproj_kernel(x_ref, w_ref, cos_ref, sin_ref, qkv_ref, kvf_ref, qkr_ref, vg_ref, gm_ref):
    xb = x_ref[...].astype(BF16)

    def mm(lo, hi):
        return jnp.dot(xb, w_ref[:, lo:hi], preferred_element_type=F32)

    qkv_ref[:, 0:PQ_W] = mm(0, PQ_W).astype(BF16)
    qkv_ref[:, PQ_W:A_KVF] = mm(PQ_W, A_KVF).astype(BF16)
    kvf_ref[...] = mm(A_KVF, A_QR)
    cos = cos_ref[...]
    sin = sin_ref[...]
    for part in range(2):
        z = mm(A_QR + part * QR_W, A_QR + (part + 1) * QR_W)
        for h in range(N_HEADS_R):
            zh = z[:, h * DK_R:(h + 1) * DK_R]
            r = zh * cos + pltpu.roll(zh, DK_R // 2, axis=1) * sin
            if part == 1:
                r = r * (DK_R ** -0.5)
            c0 = part * QR_W + h * DK_R
            qkr_ref[:, c0:c0 + DK_R] = r.astype(BF16)
    vg_ref[...] = mm(A_VR, A_GM).astype(BF16)
    for half in range(2):
        lo = A_GM + half * D_MODEL
        gm_ref[:, half * D_MODEL:(half + 1) * D_MODEL] = jax.nn.sigmoid(mm(lo, lo + D_MODEL)).astype(BF16)


def _inproj(x2d, w_bf, cos_tab, sin_tab, tm):
    t = x2d.shape[0]
    nper = cos_tab.shape[0] // tm
    row = lambda i: (i, 0)
    return pl.pallas_call(
        _inproj_kernel,
        grid=(t // tm,),
        in_specs=[
            pl.BlockSpec((tm, D_MODEL), row),
            pl.BlockSpec((D_MODEL, A_W), lambda i: (0, 0)),
            pl.BlockSpec((tm, DK_R), lambda i: (i % nper, 0)),
            pl.BlockSpec((tm, DK_R), lambda i: (i % nper, 0)),
        ],
        out_specs=[
            pl.BlockSpec((tm, A_KVF), row),
            pl.BlockSpec((tm, 2 * KA_W), row),
            pl.BlockSpec((tm, 2 * QR_W), row),
            pl.BlockSpec((tm, 2 * QR_W), row),
            pl.BlockSpec((tm, 2 * D_MODEL), row),
        ],
        out_shape=[
            jax.ShapeDtypeStruct((t, A_KVF), BF16),
            jax.ShapeDtypeStruct((t, 2 * KA_W), F32),
            jax.ShapeDtypeStruct((t, 2 * QR_W), BF16),
            jax.ShapeDtypeStruct((t, 2 * QR_W), BF16),
            jax.ShapeDtypeStruct((t, 2 * D_MODEL), BF16),
        ],
        compiler_params=_cparams(("parallel",)),
        name="inproj",
    )(x2d, w_bf, cos_tab, sin_tab)


def _attend(q4, k, v, bias, sink):
    s = lax.dot_general(q4, k, (((1,), (1,)), ((), ())), preferred_element_type=F32) + bias
    m = jnp.maximum(jnp.max(s, axis=-1, keepdims=True), sink)
    e = jnp.exp(s - m)
    p = e / (jnp.sum(e, axis=-1, keepdims=True) + jnp.exp(sink - m))
    return jnp.dot(p.astype(BF16), v, preferred_element_type=F32)


KEY_PAD = 256


def _attend_sink_column(q4, k, v, bias, fill, valid):
    s = lax.dot_general(q4, k, (((1,), (1,)), ((), ())), preferred_element_type=F32)
    s = jnp.where(valid, s + bias, fill)
    e = jnp.exp(s - jnp.max(s, axis=-1, keepdims=True)).astype(BF16)
    dv = v.shape[1]
    ones = jnp.ones((KEY_PAD, dv), BF16)
    v_rows = jnp.concatenate([v, jnp.zeros((KEY_PAD - WINDOW - CHUNK, dv), BF16)], axis=0)
    both = jnp.dot(e, jnp.concatenate([v_rows, ones], axis=1), preferred_element_type=F32)
    return both[:, 0:dv] / both[:, dv:]


def _attn_prompt_kernel(q_ref, kvc_ref, kvp_ref, bias_ref, fill_ref, o_ref, kv_buf, *, n_chunks):
    i = pl.program_id(1)
    qb = n_chunks * CHUNK
    kv_buf[0:WINDOW, :] = kvp_ref[...]
    kv_buf[WINDOW:WINDOW + qb, :] = kvc_ref[...]
    kv_buf[WINDOW + qb:, :] = jnp.zeros((KEY_PAD - WINDOW - CHUNK, PKV_W), BF16)
    col = lax.broadcasted_iota(I32, (GROUP_A * CHUNK, KEY_PAD), 1)
    key_col = jnp.where(col < WINDOW + CHUNK, col, -1)

    def chunk(j, carry):
        r0 = pl.multiple_of(j * CHUNK, CHUNK)
        valid = key_col >= jnp.maximum(WINDOW - (i * n_chunks + j) * CHUNK, 0)
        for kv in range(N_KV_A):
            q4 = jnp.concatenate(
                [q_ref[pl.ds(r0, CHUNK), (kv * GROUP_A + g) * HEAD_PAD:(kv * GROUP_A + g + 1) * HEAD_PAD]
                 for g in range(GROUP_A)], axis=0)
            k = kv_buf[pl.ds(r0, KEY_PAD), kv * HEAD_PAD:(kv + 1) * HEAD_PAD]
            v = kv_buf[pl.ds(r0, WINDOW + CHUNK), (N_KV_A + kv) * HEAD_PAD:(N_KV_A + kv + 1) * HEAD_PAD]
            o4 = _attend_sink_column(q4, k, v, bias_ref[kv], fill_ref[kv], valid)
            for g in range(GROUP_A):
                c0 = (kv * GROUP_A + g) * HEAD_PAD
                o_ref[pl.ds(r0, CHUNK), c0:c0 + HEAD_PAD] = o4[g * CHUNK:(g + 1) * CHUNK].astype(BF16)
        return carry

    lax.fori_loop(0, n_chunks, chunk, 0, unroll=8 if n_chunks % 8 == 0 else 1)


def _attn_prompt(qkv, bias, fill, b, s):
    qb = min(512, s)
    n_chunks = qb // CHUNK
    nq = s // qb
    per = qb // WINDOW
    kv_col = PQ_W // PKV_W
    return pl.pallas_call(
        functools.partial(_attn_prompt_kernel, n_chunks=n_chunks),
        grid=(b, nq),
        in_specs=[
            pl.BlockSpec((qb, PQ_W), lambda bi, i: (bi * nq + i, 0)),
            pl.BlockSpec((qb, PKV_W), lambda bi, i: (bi * nq + i, kv_col)),
            pl.BlockSpec((WINDOW, PKV_W), lambda bi, i: (jnp.maximum((bi * nq + i) * per - 1, 0), kv_col)),
            pl.BlockSpec((N_KV_A, GROUP_A * CHUNK, KEY_PAD), lambda bi, i: (0, 0, 0)),
            pl.BlockSpec((N_KV_A, GROUP_A * CHUNK, KEY_PAD), lambda bi, i: (0, 0, 0)),
        ],
        out_specs=pl.BlockSpec((qb, PQ_W), lambda bi, i: (bi * nq + i, 0)),
        out_shape=jax.ShapeDtypeStruct((b * s, PQ_W), BF16),
        scratch_shapes=[pltpu.VMEM((qb + KEY_PAD - CHUNK, PKV_W), BF16)],
        compiler_params=_cparams(("parallel", "parallel")),
        name="attn_prompt",
    )(qkv, qkv, qkv, bias, fill)


def _attn_sample_kernel(q_ref, kvf_ref, ck_ref, cv_ref, bias_ref, sink_ref, o_ref, nk_ref, nv_ref, *, l):
    k_all = jnp.concatenate([ck_ref[0], kvf_ref[:, 0:KA_W]], axis=0)
    v_all = jnp.concatenate([cv_ref[0], kvf_ref[:, KA_W:2 * KA_W]], axis=0)
    nk_ref[0] = k_all[l:]
    nv_ref[0] = v_all[l:]
    kb = k_all.astype(BF16)
    vb = v_all.astype(BF16)
    o_ref[...] = jnp.zeros_like(o_ref)
    for kv in range(N_KV_A):
        q4 = jnp.concatenate(
            [q_ref[:, (kv * GROUP_A + g) * HEAD_PAD:(kv * GROUP_A + g) * HEAD_PAD + HEAD_DIM_A]
             for g in range(GROUP_A)], axis=0)
        o4 = _attend(q4, kb[:, kv * HEAD_DIM_A:(kv + 1) * HEAD_DIM_A], vb[:, kv * HEAD_DIM_A:(kv + 1) * HEAD_DIM_A],
                     bias_ref[kv], sink_ref[kv])
        for g in range(GROUP_A):
            c0 = (kv * GROUP_A + g) * HEAD_PAD
            o_ref[:, c0:c0 + HEAD_DIM_A] = o4[g * l:(g + 1) * l].astype(BF16)


def _attn_sample(qkv, kvf, cache_k, cache_v, bias, sink, bd, l):
    cache_spec = pl.BlockSpec((1, WINDOW, KA_W), lambda bi: (bi, 0, 0))
    cache_shape = jax.ShapeDtypeStruct((bd, WINDOW, KA_W), F32)
    return pl.pallas_call(
        functools.partial(_attn_sample_kernel, l=l),
        grid=(bd,),
        in_specs=[
            pl.BlockSpec((l, A_KVF), lambda bi: (bi, 0)),
            pl.BlockSpec((l, 2 * KA_W), lambda bi: (bi, 0)),
            cache_spec, cache_spec,
            pl.BlockSpec((N_KV_A, GROUP_A * l, WINDOW + l), lambda bi: (0, 0, 0)),
            pl.BlockSpec((N_KV_A, GROUP_A * l, 1), lambda bi: (0, 0, 0)),
        ],
        out_specs=[pl.BlockSpec((l, PQ_W), lambda bi: (bi, 0)), cache_spec, cache_spec],
        out_shape=[jax.ShapeDtypeStruct((bd * l, PQ_W), BF16), cache_shape, cache_shape],
        compiler_params=_cparams(("parallel",)),
        name="attn_sample",
    )(qkv, kvf, cache_k, cache_v, bias, sink)


def _retention_kernel(q_ref, k_ref, v_ref, g_ref, s0_ref, di_ref, qd_ref, kd_ref, gc_ref, gain_ref,
                      o_ref, s_ref, *, n_chunks, c):
    @pl.when(pl.program_id(1) == 0)
    def _():
        s_ref[...] = s0_ref[...]

    def chunk(n, carry):
        r0 = pl.multiple_of(n * c, c)
        for h in range(N_HEADS_R):
            cols = slice(h * DK_R, (h + 1) * DK_R)
            qc = q_ref[pl.ds(r0, c), cols]
            kc = k_ref[pl.ds(r0, c), cols]
            vc = v_ref[pl.ds(r0, c), cols]
            state = s_ref[0, h]
            sc = lax.dot_general(qc, kc, (((1,), (1,)), ((), ())), preferred_element_type=F32) * di_ref[h]
            o = jnp.dot(sc.astype(BF16), vc, preferred_element_type=F32)
            q_dec = (qc.astype(F32) * qd_ref[h]).astype(BF16)
            o = o + jnp.dot(q_dec, state.astype(BF16), preferred_element_type=F32)
            k_dec = (kc.astype(F32) * kd_ref[h]).astype(BF16)
            s_ref[0, h] = gc_ref[h] * state + lax.dot_general(k_dec, vc, (((0,), (0,)), ((), ())),
                                                              preferred_element_type=F32)
            mu = jnp.mean(o, axis=-1, keepdims=True)
            d = o - mu
            var = jnp.mean(d * d, axis=-1, keepdims=True)
            y = d * lax.rsqrt(var + LN_EPS) * gain_ref[:, cols]
            o_ref[pl.ds(r0, c), cols] = (y * _silu(g_ref[pl.ds(r0, c), cols].astype(F32))).astype(BF16)
        return carry

    lax.fori_loop(0, n_chunks, chunk, 0, unroll=4 if n_chunks % 4 == 0 else 1)


def _retention(qkr, vg, s0, tabs, gain, b, s, c):
    di, qd, kd, gc = tabs
    st = min(1024, s)
    nt = s // st
    seq = lambda col: pl.BlockSpec((st, QR_W), lambda bi, j: (bi * nt + j, col))
    full = lambda a: pl.BlockSpec(a.shape, lambda bi, j: (0,) * a.ndim)
    st_spec = pl.BlockSpec((1, N_HEADS_R, DK_R, DK_R), lambda bi, j: (bi, 0, 0, 0))
    return pl.pallas_call(
        functools.partial(_retention_kernel, n_chunks=st // c, c=c),
        grid=(b, nt),
        in_specs=[seq(0), seq(1), seq(0), seq(1), st_spec, full(di), full(qd), full(kd), full(gc), full(gain)],
        out_specs=[seq(0), st_spec],
        out_shape=[jax.ShapeDtypeStruct((b * s, QR_W), BF16), jax.ShapeDtypeStruct((b, N_HEADS_R, DK_R, DK_R), F32)],
        compiler_params=_cparams(("parallel", "arbitrary")),
        name="retention",
    )(qkr, qkr, vg, vg, s0, di, qd, kd, gc, gain)


def _mix_kernel(x_ref, oa_ref, or_ref, g_ref, wa_ref, wb_ref, wo_ref, lg_ref, lb_ref, wrt_ref, x1_ref, st_ref, xp_ref):
    a = jnp.dot(oa_ref[...], wa_ref[...], preferred_element_type=F32)
    b = jnp.dot(or_ref[...], wb_ref[...], preferred_element_type=F32)
    merged = g_ref[:, 0:D_MODEL].astype(F32) * a + g_ref[:, D_MODEL:].astype(F32) * b
    y = jnp.dot(merged.astype(BF16), wo_ref[...], preferred_element_type=F32)
    x1 = _layer_norm(DN_ALPHA * x_ref[...] + y, lg_ref[...], lb_ref[...])
    x1_ref[...] = x1
    _store_packed(xp_ref, x1)
    logits = lax.dot_general(wrt_ref[...], x1.astype(BF16), (((1,), (1,)), ((), ())), preferred_element_type=F32)
    st_ref[...] = jax.nn.sigmoid(logits)


def _mix(x2d, oa, orr, gm, wa, wb, wo, lg, lb, wrt, tm):
    t = x2d.shape[0]
    row = lambda w: pl.BlockSpec((tm, w), lambda i: (i, 0))
    full = lambda a: pl.BlockSpec(a.shape, lambda i: (0,) * a.ndim)
    return pl.pallas_call(
        _mix_kernel,
        grid=(t // tm,),
        in_specs=[row(D_MODEL), row(PQ_W), row(QR_W), row(2 * D_MODEL),
                  full(wa), full(wb), full(wo), full(lg), full(lb), full(wrt)],
        out_specs=[row(D_MODEL), pl.BlockSpec((N_EXPERTS, tm), lambda i: (0, i)),
                   pl.BlockSpec((tm * PACK_SUB, 128), lambda i: (i, 0))],
        out_shape=[jax.ShapeDtypeStruct((t, D_MODEL), F32), jax.ShapeDtypeStruct((N_EXPERTS, t), F32),
                   jax.ShapeDtypeStruct((t * PACK_SUB, 128), U32)],
        compiler_params=_cparams(("parallel",)),
        name="mix",
    )(x2d, oa, orr, gm, wa, wb, wo, lg, lb, wrt)


def _route_kernel(s_ref, bias_ref, cnt0_ref, e_ref, w_ref, r_ref, cnt_ref, carry_ref):
    @pl.when(pl.program_id(0) == 0)
    def _():
        carry_ref[...] = cnt0_ref[...].astype(F32)

    s = s_ref[...]
    tl = s.shape[1]
    choice = s + bias_ref[...]
    row = lax.broadcasted_iota(I32, (N_EXPERTS, tl), 0)
    row_g = lax.broadcasted_iota(I32, (GROUP_SIZE, tl), 0)
    neg = -jnp.inf

    scores = []
    for g in range(N_GROUPS):
        blk = choice[g * GROUP_SIZE:(g + 1) * GROUP_SIZE]
        m1 = jnp.max(blk, axis=0, keepdims=True)
        i1 = jnp.min(jnp.where(blk == m1, row_g, GROUP_SIZE), axis=0, keepdims=True)
        m2 = jnp.max(jnp.where(row_g == i1, neg, blk), axis=0, keepdims=True)
        scores.append(m1 + m2)
    sc = jnp.concatenate(scores, axis=0)
    gi = lax.broadcasted_iota(I32, sc.shape, 0)
    grank = jnp.zeros(sc.shape, I32)
    for g in range(N_GROUPS):
        other = sc[g:g + 1]
        ahead = jnp.where(other > sc, 1, jnp.where(other == sc, jnp.where(gi > g, 1, 0), 0))
        grank = grank + ahead
    cm = jnp.concatenate(
        [jnp.where(grank[g:g + 1] < TOPK_GROUPS, choice[g * GROUP_SIZE:(g + 1) * GROUP_SIZE], neg)
         for g in range(N_GROUPS)], axis=0)

    experts, weights = [], []
    for _ in range(TOP_K):
        m = jnp.max(cm, axis=0, keepdims=True)
        idx = jnp.min(jnp.where(cm == m, row, N_EXPERTS), axis=0, keepdims=True)
        hit = row == idx
        weights.append(jnp.sum(jnp.where(hit, s, 0.0), axis=0, keepdims=True))
        cm = jnp.where(hit, neg, cm)
        experts.append(idx)
    e8 = jnp.concatenate(experts, axis=0)
    w8 = jnp.concatenate(weights, axis=0)
    e_ref[...] = e8
    w_ref[...] = w8 / jnp.sum(w8, axis=0, keepdims=True) * ROUTED_SCALE

    before = (lax.broadcasted_iota(I32, (tl, tl), 0) < lax.broadcasted_iota(I32, (tl, tl), 1))
    before = jnp.where(before, 1.0, 0.0).astype(BF16)
    carry = carry_ref[...]
    ranks = []
    for k in range(TOP_K):
        hit = row == experts[k]
        onehot = jnp.where(hit, 1.0, 0.0)
        prefix = jnp.dot(onehot.astype(BF16), before, preferred_element_type=F32)
        ranks.append(jnp.sum(jnp.where(hit, prefix + carry, 0.0), axis=0, keepdims=True))
        carry = carry + jnp.sum(onehot, axis=1, keepdims=True)
    r_ref[...] = jnp.concatenate(ranks, axis=0).astype(I32)
    carry_ref[...] = carry
    cnt_ref[...] = carry.astype(I32)


def _route(st, bias_col, counts0, tl):
    t = st.shape[1]
    tok = pl.BlockSpec((TOP_K, tl), lambda i: (0, i))
    return pl.pallas_call(
        _route_kernel,
        grid=(t // tl,),
        in_specs=[pl.BlockSpec((N_EXPERTS, tl), lambda i: (0, i)), pl.BlockSpec((N_EXPERTS, 1), lambda i: (0, 0)),
                  pl.BlockSpec((N_EXPERTS, 1), lambda i: (0, 0))],
        out_specs=[tok, tok, tok, pl.BlockSpec((N_EXPERTS, 1), lambda i: (0, 0))],
        out_shape=[jax.ShapeDtypeStruct((TOP_K, t), I32), jax.ShapeDtypeStruct((TOP_K, t), F32),
                   jax.ShapeDtypeStruct((TOP_K, t), I32), jax.ShapeDtypeStruct((N_EXPERTS, 1), I32)],
        scratch_shapes=[pltpu.VMEM((N_EXPERTS, 1), F32)],
        compiler_params=_cparams(("arbitrary",)),
        name="route",
    )(st, bias_col, counts0)


def _dest_kernel(e_ref, r_ref, ps_ref, d_ref):
    tl = e_ref.shape[1]
    row = lax.broadcasted_iota(I32, (N_EXPERTS, tl), 0)
    ps = ps_ref[...]
    starts = [jnp.sum(jnp.where(row == e_ref[k:k + 1, :], ps, 0.0), axis=0, keepdims=True) for k in range(TOP_K)]
    d_ref[...] = jnp.concatenate(starts, axis=0).astype(I32) + r_ref[...]


def _dest(e8, r8, pstart_col, tl):
    t = e8.shape[1]
    tok = pl.BlockSpec((TOP_K, tl), lambda i: (0, i))
    return pl.pallas_call(
        _dest_kernel,
        grid=(t // tl,),
        in_specs=[tok, tok, pl.BlockSpec((N_EXPERTS, 1), lambda i: (0, 0))],
        out_specs=tok,
        out_shape=jax.ShapeDtypeStruct((TOP_K, t), I32),
        compiler_params=_cparams(("parallel",)),
        name="dest",
    )(e8, r8, pstart_col)


TOKEN_GROUP = 8


def _row_copy(src, src_tok, dst, dst_tok, sem):
    s0 = pl.multiple_of(src_tok * PACK_SUB, PACK_SUB)
    d0 = pl.multiple_of(dst_tok * PACK_SUB, PACK_SUB)
    return pltpu.make_async_copy(src.at[pl.ds(s0, PACK_SUB)], dst.at[pl.ds(d0, PACK_SUB)], sem)


def _wait_bytes_of(ref, sem):
    pltpu.make_async_copy(ref, ref, sem).wait()


def _dispatch_kernel(d_ref, x_ref, wsg_ref, wsu_ref, wsd_ref, *refs, td):
    xs_hbm, sh_ref, sem = refs[-3:]

    def start(g, carry):
        for j in range(TOKEN_GROUP):
            t = g * TOKEN_GROUP + j
            for k in range(TOP_K):
                _row_copy(x_ref, t, xs_hbm, d_ref[t * TOP_K + k], sem).start(priority=k % 2)
        return carry

    lax.fori_loop(0, td // TOKEN_GROUP, start, 0)

    hi, lo = _load_packed(x_ref, td)
    xb = jnp.concatenate([hi.astype(BF16), lo.astype(BF16)], axis=1)
    hs = _silu(jnp.dot(xb, wsg_ref[...], preferred_element_type=F32)) * jnp.dot(xb, wsu_ref[...],
                                                                               preferred_element_type=F32)
    sh_ref[...] = jnp.dot(hs.astype(BF16), wsd_ref[...], preferred_element_type=F32)

    for k in range(TOP_K):
        _wait_bytes_of(x_ref, sem)


def _dispatch(d8, xp, wsg, wsu, wsd, n_rows, td, xs_prev=None):
    t = d8.shape[0] // TOP_K
    full = lambda a: pl.BlockSpec(a.shape, lambda i: (0,) * a.ndim)
    in_specs = [pl.BlockSpec((td * TOP_K,), lambda i: (i,), memory_space=pltpu.SMEM),
                pl.BlockSpec((td * PACK_SUB, 128), lambda i: (i, 0)), full(wsg), full(wsu), full(wsd)]
    args = [d8, xp, wsg, wsu, wsd]
    if xs_prev is not None:
        in_specs.append(pl.BlockSpec(memory_space=pl.ANY))
        args.append(xs_prev)
    return pl.pallas_call(
        functools.partial(_dispatch_kernel, td=td),
        grid=(t // td,),
        in_specs=in_specs,
        out_specs=[pl.BlockSpec(memory_space=pl.ANY), pl.BlockSpec((td, D_MODEL), lambda i: (i, 0))],
        scratch_shapes=[pltpu.SemaphoreType.DMA(())],
        out_shape=[jax.ShapeDtypeStruct((n_rows * PACK_SUB, 128), U32), jax.ShapeDtypeStruct((t, D_MODEL), F32)],
        input_output_aliases={} if xs_prev is None else {5: 0},
        compiler_params=_cparams(("arbitrary",)),
        name="dispatch",
    )(*args)


X_RING = 3


def _moe_kernel(be_ref, nv_ref, x_hbm, wg_ref, wu_ref, wd_ref, y_ref, x_ring, x_sem, wg_bf, wu_bf, wd_bf):
    j = pl.program_id(0)
    n = pl.num_programs(0)
    nv = nv_ref[j]
    blk_rows = MOE_ROWS * PACK_SUB

    def x_copy(blk):
        r0 = pl.multiple_of(blk * blk_rows, blk_rows)
        slot = blk % X_RING
        return pltpu.make_async_copy(x_hbm.at[pl.ds(r0, blk_rows)], x_ring.at[slot], x_sem.at[slot])

    def start_if_real(blk):
        @pl.when((blk < n) & (nv_ref[jnp.minimum(blk, n - 1)] > 0))
        def _():
            x_copy(blk).start()

    @pl.when(j == 0)
    def _():
        for blk in range(X_RING - 1):
            start_if_real(jnp.int32(blk))

    start_if_real(j + X_RING - 1)

    @pl.when((j == 0) | (be_ref[j] != be_ref[jnp.maximum(j - 1, 0)]))
    def _():
        wg_bf[...] = wg_ref[0].astype(BF16)
        wu_bf[...] = wu_ref[0].astype(BF16)
        wd_bf[...] = wd_ref[0].astype(BF16)

    @pl.when(nv > 0)
    def _():
        x_copy(j).wait()
        hi, lo = _load_packed(x_ring.at[j % X_RING], MOE_ROWS)
        valid = lax.broadcasted_iota(I32, hi.shape, 0) < nv
        xb = jnp.concatenate([jnp.where(valid, hi, 0.0).astype(BF16), jnp.where(valid, lo, 0.0).astype(BF16)], axis=1)
        g = jnp.dot(xb, wg_bf[...], preferred_element_type=F32)
        u = jnp.dot(xb, wu_bf[...], preferred_element_type=F32)
        h = (_silu(g) * u).astype(BF16)
        _store_packed(y_ref, jnp.dot(h, wd_bf[...], preferred_element_type=F32))

    @pl.when(nv == 0)
    def _():
        y_ref[...] = jnp.zeros_like(y_ref)


def _moe(blk_exp, blk_valid, xs, wg, wu, wd):
    n_blocks = blk_exp.shape[0]
    rows = pl.BlockSpec((MOE_ROWS * PACK_SUB, 128), lambda j, be, nv: (j, 0))
    return pl.pallas_call(
        _moe_kernel,
        grid_spec=pltpu.PrefetchScalarGridSpec(
            num_scalar_prefetch=2,
            grid=(n_blocks,),
            in_specs=[pl.BlockSpec(memory_space=pl.ANY),
                      pl.BlockSpec((1, D_MODEL, D_EXPERT), lambda j, be, nv: (be[j], 0, 0)),
                      pl.BlockSpec((1, D_MODEL, D_EXPERT), lambda j, be, nv: (be[j], 0, 0)),
                      pl.BlockSpec((1, D_EXPERT, D_MODEL), lambda j, be, nv: (be[j], 0, 0))],
            out_specs=rows,
            scratch_shapes=[pltpu.VMEM((X_RING, MOE_ROWS * PACK_SUB, 128), U32), pltpu.SemaphoreType.DMA((X_RING,)),
                            pltpu.VMEM((D_MODEL, D_EXPERT), BF16), pltpu.VMEM((D_MODEL, D_EXPERT), BF16),
                            pltpu.VMEM((D_EXPERT, D_MODEL), BF16)],
        ),
        out_shape=jax.ShapeDtypeStruct(xs.shape, U32),
        compiler_params=_cparams(("arbitrary",)),
        name="moe",
    )(blk_exp, blk_valid, xs, wg, wu, wd)


COMBINE_ROWS = 32


def _gather_rows(ys_hbm, d_ref, slot_buf, sem, t0, n):
    for j in range(n):
        t = t0 + j
        for k in range(TOP_K):
            _row_copy(ys_hbm, d_ref[t * TOP_K + k], slot_buf.at[k], t, sem).start(priority=k % 2)


def _final_kernel(d_ref, dnext_ref, x1_ref, w_ref, p_ref, sh_ref, lg_ref, lb_ref,
                  wpp_ref, wpg_ref, ys_hbm, out_ref, gbuf, ybuf, sem, *, tf):
    i = pl.program_id(0)
    slot = i % 2

    @pl.when(i == 0)
    def _():
        def first(g, carry):
            _gather_rows(ys_hbm, d_ref, gbuf.at[0], sem.at[0], g * TOKEN_GROUP, TOKEN_GROUP)
            return carry

        lax.fori_loop(0, tf // TOKEN_GROUP, first, 0)

    _wait_bytes_of(gbuf.at[slot], sem.at[slot])

    def combine(sb, carry):
        r0 = pl.multiple_of(sb * COMBINE_ROWS, COMBINE_ROWS)
        _gather_rows(ys_hbm, dnext_ref, gbuf.at[1 - slot], sem.at[1 - slot], r0, COMBINE_ROWS)
        y_hi = jnp.zeros((COMBINE_ROWS, PACK_W), F32)
        y_lo = jnp.zeros((COMBINE_ROWS, PACK_W), F32)
        for k in range(TOP_K):
            rows = gbuf.at[slot, k, pl.ds(pl.multiple_of(r0 * PACK_SUB, COMBINE_ROWS * PACK_SUB),
                                          COMBINE_ROWS * PACK_SUB)]
            hi, lo = _load_packed(rows, COMBINE_ROWS)
            wk = w_ref[pl.ds(r0, COMBINE_ROWS), k:k + 1]
            y_hi = y_hi + wk * hi
            y_lo = y_lo + wk * lo
        ybuf[pl.ds(r0, COMBINE_ROWS), 0:PACK_W] = y_hi
        ybuf[pl.ds(r0, COMBINE_ROWS), PACK_W:] = y_lo
        return carry

    lax.fori_loop(0, tf // COMBINE_ROWS, combine, 0)

    @pl.when(i == pl.num_programs(0) - 1)
    def _():
        _wait_bytes_of(gbuf.at[1 - slot], sem.at[1 - slot])

    y = sh_ref[...] + ybuf[...]
    x2 = _layer_norm(DN_ALPHA * x1_ref[...] + y, lg_ref[...], lb_ref[...])
    gate = jax.nn.sigmoid(jnp.dot(x2.astype(BF16), wpg_ref[...], preferred_element_type=F32))
    out_ref[...] = x2 + gate * jnp.dot(p_ref[...].astype(BF16), wpp_ref[...], preferred_element_type=F32)


def _final(d8, x1, w_tok, p2d, sh, lg, lb, wpp, wpg, ys, tf):
    t = x1.shape[0]
    n_tiles = t // tf
    row = lambda w: pl.BlockSpec((tf, w), lambda i: (i, 0))
    full = lambda a: pl.BlockSpec(a.shape, lambda i: (0,) * a.ndim)
    return pl.pallas_call(
        functools.partial(_final_kernel, tf=tf),
        grid=(n_tiles,),
        in_specs=[pl.BlockSpec((tf * TOP_K,), lambda i: (i,), memory_space=pltpu.SMEM),
                  pl.BlockSpec((tf * TOP_K,), lambda i: (jnp.minimum(i + 1, n_tiles - 1),), memory_space=pltpu.SMEM),
                  row(D_MODEL), row(TOP_K), row(D_PLE), row(D_MODEL),
                  full(lg), full(lb), full(wpp), full(wpg),
                  pl.BlockSpec(memory_space=pl.ANY)],
        out_specs=row(D_MODEL),
        scratch_shapes=[pltpu.VMEM((2, TOP_K, tf * PACK_SUB, 128), U32), pltpu.VMEM((tf, D_MODEL), F32),
                        pltpu.SemaphoreType.DMA((2,))],
        out_shape=jax.ShapeDtypeStruct((t, D_MODEL), F32),
        compiler_params=_cparams(("arbitrary",)),
        name="final",
    )(d8, d8, x1, w_tok, p2d, sh, lg, lb, wpp, wpg, ys)


def _t5_bucket(rel):
    nb = N_BUCKETS // 2
    max_exact = nb // 2
    ret = jnp.where(rel > 0, nb, 0)
    n = jnp.abs(rel)
    nf = jnp.maximum(n, 1).astype(F32)
    large = max_exact + (jnp.log(nf / max_exact) / math.log(MAX_DIST / max_exact) * (nb - max_exact)).astype(I32)
    large = jnp.minimum(large, nb - 1)
    return ret + jnp.where(n < max_exact, n, large)


def _rel_bias(table, q_len, k_len):
    rel = (jnp.arange(k_len, dtype=I32)[None, :] - WINDOW) - jnp.arange(q_len, dtype=I32)[:, None]
    b = table[_t5_bucket(rel)]
    return jnp.transpose(b, (2, 0, 1)).reshape(N_KV_A, GROUP_A * q_len, k_len).astype(F32)


def _rotary_tables(pos):
    half = DK_R // 2
    inv = ROPE_BASE ** (-jnp.arange(half, dtype=F32) / half)
    ang = pos.astype(F32)[:, None] * inv[None, :]
    cos, sin = jnp.cos(ang), jnp.sin(ang)
    return jnp.concatenate([cos, cos], -1), jnp.concatenate([-sin, sin], -1)


def _decay_tables(c):
    log_gamma = jnp.log1p(-jnp.exp2(-5.0 - jnp.arange(N_HEADS_R, dtype=F32)))
    idx = jnp.arange(c, dtype=F32)
    di = jnp.exp(jnp.abs(idx[:, None] - idx[None, :])[None] * log_gamma[:, None, None])
    qd = jnp.exp((idx[None, :] + 1.0) * log_gamma[:, None])
    kd = jnp.exp((c - 1.0 - idx[None, :]) * log_gamma[:, None])
    gc = jnp.exp(c * log_gamma)
    bc = lambda a: jnp.broadcast_to(a[:, :, None], (N_HEADS_R, c, DK_R))
    return di, bc(qd), bc(kd), jnp.broadcast_to(gc[:, None, None], (N_HEADS_R, DK_R, DK_R))


def _pad_heads(w, n_heads):
    rows = w.shape[0]
    w = w.reshape(rows, n_heads, HEAD_DIM_A)
    w = jnp.pad(w, ((0, 0), (0, 0), (0, HEAD_PAD - HEAD_DIM_A)))
    return w.reshape(rows, n_heads * HEAD_PAD)


def _moe_ffn(groups, wts):
    tiles = [min(256, g[0].shape[0]) for g in groups]
    counts = jnp.zeros((N_EXPERTS, 1), I32)
    routed = []
    for (x1, xp, st, p2d), tile in zip(groups, tiles):
        e8, w8, r8, counts = _route(st, wts['router_bias'], counts, tile)
        routed.append((e8, w8, r8))
    counts = counts[:, 0]
    padded = (counts + MOE_ROWS - 1) // MOE_ROWS * MOE_ROWS
    pad_end = jnp.cumsum(padded)
    pstart = (pad_end - padded).astype(I32)
    n_rows = sum(g[0].shape[0] for g in groups) * TOP_K + N_EXPERTS * MOE_ROWS
    blk_start = jnp.arange(n_rows // MOE_ROWS, dtype=I32) * MOE_ROWS
    blk_exp = jnp.minimum(jnp.sum(blk_start[:, None] >= pad_end[None, :], axis=1), N_EXPERTS - 1).astype(I32)
    own = blk_exp[:, None] == jnp.arange(N_EXPERTS, dtype=I32)[None, :]
    blk_end = jnp.sum(jnp.where(own, (pstart + counts)[None, :], 0), axis=1)
    blk_valid = jnp.clip(blk_end - blk_start, 0, MOE_ROWS).astype(I32)
    pstart_col = pstart.astype(F32).reshape(N_EXPERTS, 1)
    dests, shared, xs = [], [], None
    for (x1, xp, st, p2d), (e8, w8, r8), tile in zip(groups, routed, tiles):
        d8 = _dest(e8, r8, pstart_col, tile).T.reshape(-1)
        xs, sh = _dispatch(d8, xp, wts['w_sh_gate'], wts['w_sh_up'], wts['w_sh_down'], n_rows, tile, xs)
        dests.append(d8)
        shared.append(sh)
    ys = _moe(blk_exp, blk_valid, xs, wts['w_exp_gate'], wts['w_exp_up'], wts['w_exp_down'])
    return [_final(d8, x1, w8.T, p2d, sh, wts['ln2_g'], wts['ln2_b'], wts['w_ple_proj'], wts['w_ple_gate'], ys, tile)
            for (x1, xp, st, p2d), (e8, w8, r8), d8, sh, tile in zip(groups, routed, dests, shared, tiles)]


def _mix_group(x2d, oa, orr, gm, p2d, wts):
    x1, st, xp = _mix(x2d, oa, orr, gm, wts['w_branch_attn'], wts['w_branch_ret'], wts['w_out'],
                      wts['ln1_g'], wts['ln1_b'], wts['w_router_t'], min(512, x2d.shape[0]))
    return x1, xp, st, p2d


def kernel(x_prompt, x_sample, cache_attn_k, cache_attn_v, state_retention, p_prompt, p_sample,
           w_in, attn_sinks, rel_bias_table, ret_gn_gain, w_branch_attn, w_branch_ret, w_out,
           ln1_g, ln1_b, w_router, router_bias, w_exp_gate, w_exp_up, w_exp_down,
           w_sh_gate, w_sh_up, w_sh_down, ln2_g, ln2_b, w_ple_proj, w_ple_gate):
    b, s, _ = x_prompt.shape
    bd, l, _ = x_sample.shape
    i = 0
    row = lambda a: a.reshape(1, -1).astype(F32)
    wts = dict(
        w_branch_attn=_pad_heads(w_branch_attn[i].astype(BF16).T, N_HEADS_A).T,
        w_branch_ret=w_branch_ret[i].astype(BF16),
        w_out=w_out[i].astype(BF16), ln1_g=row(ln1_g[i]), ln1_b=row(ln1_b[i]),
        w_router_t=w_router[i].T.astype(BF16), router_bias=router_bias[i].reshape(N_EXPERTS, 1).astype(F32),
        w_exp_gate=w_exp_gate[i], w_exp_up=w_exp_up[i], w_exp_down=w_exp_down[i],
        w_sh_gate=w_sh_gate[i].astype(BF16), w_sh_up=w_sh_up[i].astype(BF16), w_sh_down=w_sh_down[i].astype(BF16),
        ln2_g=row(ln2_g[i]), ln2_b=row(ln2_b[i]),
        w_ple_proj=w_ple_proj[i].astype(BF16), w_ple_gate=w_ple_gate[i].astype(BF16))
    w_in_bf = w_in[i].astype(BF16)
    w_in_bf = jnp.concatenate([_pad_heads(w_in_bf[:, :QA_W] * (HEAD_DIM_A ** -0.5), N_HEADS_A),
                               _pad_heads(w_in_bf[:, OFF_KV:OFF_QR], 2 * N_KV_A), w_in_bf[:, OFF_KV:]], axis=1)
    gain = row(ret_gn_gain[i])
    sinks = attn_sinks[i].astype(F32).reshape(N_KV_A, GROUP_A)

    xp = x_prompt.reshape(b * s, D_MODEL)
    cos_p, sin_p = _rotary_tables(jnp.arange(s, dtype=I32))
    qkv, kvf, qkr, vg, gm = _inproj(xp, w_in_bf, cos_p, sin_p, min(512, s))
    bias_p = jnp.pad(_rel_bias(rel_bias_table, CHUNK, WINDOW + CHUNK), ((0, 0), (0, 0), (0, KEY_PAD - WINDOW - CHUNK)))
    sink_p = jnp.repeat(sinks, CHUNK, axis=1)[..., None]
    fill_p = jnp.where(jnp.arange(KEY_PAD)[None, None, :] == WINDOW + CHUNK, sink_p, NEG_INF).astype(F32)
    oa = _attn_prompt(qkv, bias_p, fill_p, b, s)
    s0 = jnp.zeros((b, N_HEADS_R, DK_R, DK_R), F32)
    orr, ns_p = _retention(qkr, vg, s0, _decay_tables(CHUNK), gain, b, s, CHUNK)
    group_p = _mix_group(xp, oa, orr, gm, p_prompt[i].reshape(b * s, D_PLE), wts)
    kv_tail = kvf.reshape(b, s, 2, N_KV_A, HEAD_DIM_A)[:, s - WINDOW:]
    nk_p, nv_p = kv_tail[:, :, 0], kv_tail[:, :, 1]

    xs = x_sample.reshape(bd * l, D_MODEL)
    cos_s, sin_s = _rotary_tables(jnp.tile(PAST_LEN + jnp.arange(l, dtype=I32), bd))
    qkv, kvf, qkr, vg, gm = _inproj(xs, w_in_bf, cos_s, sin_s, bd * l)
    sink_s = jnp.repeat(sinks, l, axis=1)[..., None]
    oa, nk_s, nv_s = _attn_sample(qkv, kvf, cache_attn_k[i].reshape(bd, WINDOW, KA_W),
                                  cache_attn_v[i].reshape(bd, WINDOW, KA_W),
                                  _rel_bias(rel_bias_table, l, WINDOW + l), sink_s, bd, l)
    orr, ns_s = _retention(qkr, vg, state_retention[i].astype(F32), _decay_tables(l), gain, bd, l, l)
    group_s = _mix_group(xs, oa, orr, gm, p_sample[i].reshape(bd * l, D_PLE), wts)
    y_p, y_s = _moe_ffn([group_p, group_s], wts)
    y_p, y_s = y_p.reshape(b, s, D_MODEL), y_s.reshape(bd, l, D_MODEL)
    shape_kv = (bd, WINDOW, N_KV_A, HEAD_DIM_A)
    return (y_p, y_s, nk_p[None], nv_p[None], ns_p[None],
            nk_s.reshape(shape_kv)[None], nv_s.reshape(shape_kv)[None], ns_s[None])
```

```python
import functools
import math

import jax
import jax.numpy as jnp
from jax import lax
from jax.experimental import pallas as pl
from jax.experimental.pallas import tpu as pltpu

F32 = jnp.float32
BF16 = jnp.bfloat16
I32 = jnp.int32

D_MODEL = 1024
CHUNK = 64
WINDOW = 128
N_HEADS_A = 8
N_KV_A = 2
GROUP_A = 4
HEAD_DIM_A = 64
N_BUCKETS = 32
MAX_DIST = 128
N_HEADS_R = 4
DK_R = 128
ROPE_BASE = 10000.0
N_EXPERTS = 256
TOP_K = 8
N_GROUPS = 8
GROUP_SIZE = N_EXPERTS // N_GROUPS
TOPK_GROUPS = 4
D_EXPERT = 256
ROUTED_SCALE = 2.5
D_PLE = 256
LN_EPS = 1e-5
NEG_INF = -1e30
PAST_LEN = 2048
DEPTH = 1
DN_ALPHA = (2 * DEPTH) ** 0.25

QA_W = N_HEADS_A * HEAD_DIM_A
KA_W = N_KV_A * HEAD_DIM_A
QR_W = N_HEADS_R * DK_R
OFF_KV = QA_W
OFF_QR = QA_W + 2 * KA_W
OFF_VR = OFF_QR + 2 * QR_W
OFF_GM = OFF_VR + 2 * QR_W
IN_W = OFF_GM + 2 * D_MODEL

HEAD_PAD = 128
PQ_W = N_HEADS_A * HEAD_PAD
PKV_W = 2 * N_KV_A * HEAD_PAD
A_KVF = PQ_W + PKV_W
A_QR = A_KVF + 2 * KA_W
A_VR = A_QR + 2 * QR_W
A_GM = A_VR + 2 * QR_W
A_W = A_GM + 2 * D_MODEL

MOE_ROWS = 512
VMEM_LIMIT = 56 * 1024 * 1024


def _cparams(sem, vmem=VMEM_LIMIT):
    return pltpu.CompilerParams(dimension_semantics=sem, vmem_limit_bytes=vmem)


def _layer_norm(h, g, b):
    mu = jnp.mean(h, axis=-1, keepdims=True)
    d = h - mu
    var = jnp.mean(d * d, axis=-1, keepdims=True)
    return d * lax.rsqrt(var + LN_EPS) * g + b


def _silu(x):
    return x * jax.nn.sigmoid(x)


U32 = jnp.uint32
PACK_W = D_MODEL // 2
PACK_SUB = PACK_W // 128


def _bf16_bits(x):
    return pltpu.bitcast(x.astype(BF16).astype(F32), U32)


def _store_packed(ref, x):
    n = x.shape[0]
    words = _bf16_bits(x[:, :PACK_W]) | (_bf16_bits(x[:, PACK_W:]) >> 16)
    for c in range(PACK_SUB):
        ref[pl.ds(c, n, stride=PACK_SUB), :] = words[:, c * 128:(c + 1) * 128]


def _load_packed(ref, n):
    words = jnp.concatenate([ref[pl.ds(c, n, stride=PACK_SUB), :] for c in range(PACK_SUB)], axis=1)
    hi = pltpu.bitcast(words & jnp.uint32(0xFFFF0000), F32)
    lo = pltpu.bitcast(words << 16, F32)
    return hi, lo


def _inproj_kernel(x_ref, w_ref, cos_ref, sin_ref, qkv_ref, kvf_ref, qkr_ref, vg_ref, gm_ref):
    xb = x_ref[...].astype(BF16)

    def mm(lo, hi):
        return jnp.dot(xb, w_ref[:, lo:hi], preferred_element_type=F32)

    qkv_ref[:, 0:PQ_W] = mm(0, PQ_W).astype(BF16)
    qkv_ref[:, PQ_W:A_KVF] = mm(PQ_W, A_KVF).astype(BF16)
    kvf_ref[...] = mm(A_KVF, A_QR)
    cos = cos_ref[...]
    sin = sin_ref[...]
    for part in range(2):
        z = mm(A_QR + part * QR_W, A_QR + (part + 1) * QR_W)
        for h in range(N_HEADS_R):
            zh = z[:, h * DK_R:(h + 1) * DK_R]
            r = zh * cos + pltpu.roll(zh, DK_R // 2, axis=1) * sin
            if part == 1:
                r = r * (DK_R ** -0.5)
            c0 = part * QR_W + h * DK_R
            qkr_ref[:, c0:c0 + DK_R] = r.astype(BF16)
    vg_ref[...] = mm(A_VR, A_GM).astype(BF16)
    for half in range(2):
        lo = A_GM + half * D_MODEL
        gm_ref[:, half * D_MODEL:(half + 1) * D_MODEL] = jax.nn.sigmoid(mm(lo, lo + D_MODEL)).astype(BF16)


def _inproj(x2d, w_bf, cos_tab, sin_tab, tm):
    t = x2d.shape[0]
    nper = cos_tab.shape[0] // tm
    row = lambda i: (i, 0)
    return pl.pallas_call(
        _inproj_kernel,
        grid=(t // tm,),
        in_specs=[
            pl.BlockSpec((tm, D_MODEL), row),
            pl.BlockSpec((D_MODEL, A_W), lambda i: (0, 0)),
            pl.BlockSpec((tm, DK_R), lambda i: (i % nper, 0)),
            pl.BlockSpec((tm, DK_R), lambda i: (i % nper, 0)),
        ],
        out_specs=[
            pl.BlockSpec((tm, A_KVF), row),
            pl.BlockSpec((tm, 2 * KA_W), row),
            pl.BlockSpec((tm, 2 * QR_W), row),
            pl.BlockSpec((tm, 2 * QR_W), row),
            pl.BlockSpec((tm, 2 * D_MODEL), row),
        ],
        out_shape=[
            jax.ShapeDtypeStruct((t, A_KVF), BF16),
            jax.ShapeDtypeStruct((t, 2 * KA_W), F32),
            jax.ShapeDtypeStruct((t, 2 * QR_W), BF16),
            jax.ShapeDtypeStruct((t, 2 * QR_W), BF16),
            jax.ShapeDtypeStruct((t, 2 * D_MODEL), BF16),
        ],
        compiler_params=_cparams(("parallel",)),
        name="inproj",
    )(x2d, w_bf, cos_tab, sin_tab)


def _attend(q4, k, v, bias, sink):
    s = lax.dot_general(q4, k, (((1,), (1,)), ((), ())), preferred_element_type=F32) + bias
    m = jnp.maximum(jnp.max(s, axis=-1, keepdims=True), sink)
    e = jnp.exp(s - m)
    p = e / (jnp.sum(e, axis=-1, keepdims=True) + jnp.exp(sink - m))
    return jnp.dot(p.astype(BF16), v, preferred_element_type=F32)


KEY_PAD = 256


def _attend_sink_column(q4, k, v, bias, fill, valid):
    s = lax.dot_general(q4, k, (((1,), (1,)), ((), ())), preferred_element_type=F32)
    s = jnp.where(valid, s + bias, fill)
    e = jnp.exp(s - jnp.max(s, axis=-1, keepdims=True)).astype(BF16)
    dv = v.shape[1]
    ones = jnp.ones((KEY_PAD, dv), BF16)
    v_rows = jnp.concatenate([v, jnp.zeros((KEY_PAD - WINDOW - CHUNK, dv), BF16)], axis=0)
    both = jnp.dot(e, jnp.concatenate([v_rows, ones], axis=1), preferred_element_type=F32)
    return both[:, 0:dv] / both[:, dv:]


def _attn_prompt_kernel(q_ref, kvc_ref, kvp_ref, bias_ref, fill_ref, o_ref, kv_buf, *, n_chunks):
    i = pl.program_id(1)
    qb = n_chunks * CHUNK
    kv_buf[0:WINDOW, :] = kvp_ref[...]
    kv_buf[WINDOW:WINDOW + qb, :] = kvc_ref[...]
    kv_buf[WINDOW + qb:, :] = jnp.zeros((KEY_PAD - WINDOW - CHUNK, PKV_W), BF16)
    col = lax.broadcasted_iota(I32, (GROUP_A * CHUNK, KEY_PAD), 1)
    key_col = jnp.where(col < WINDOW + CHUNK, col, -1)

    def chunk(j, carry):
        r0 = pl.multiple_of(j * CHUNK, CHUNK)
        valid = key_col >= jnp.maximum(WINDOW - (i * n_chunks + j) * CHUNK, 0)
        for kv in range(N_KV_A):
            q4 = jnp.concatenate(
                [q_ref[pl.ds(r0, CHUNK), (kv * GROUP_A + g) * HEAD_PAD:(kv * GROUP_A + g + 1) * HEAD_PAD]
                 for g in range(GROUP_A)], axis=0)
            k = kv_buf[pl.ds(r0, KEY_PAD), kv * HEAD_PAD:(kv + 1) * HEAD_PAD]
            v = kv_buf[pl.ds(r0, WINDOW + CHUNK), (N_KV_A + kv) * HEAD_PAD:(N_KV_A + kv + 1) * HEAD_PAD]
            o4 = _attend_sink_column(q4, k, v, bias_ref[kv], fill_ref[kv], valid)
            for g in range(GROUP_A):
                c0 = (kv * GROUP_A + g) * HEAD_PAD
                o_ref[pl.ds(r0, CHUNK), c0:c0 + HEAD_PAD] = o4[g * CHUNK:(g + 1) * CHUNK].astype(BF16)
        return carry

    lax.fori_loop(0, n_chunks, chunk, 0, unroll=8 if n_chunks % 8 == 0 else 1)


def _attn_prompt(qkv, bias, fill, b, s):
    qb = min(512, s)
    n_chunks = qb // CHUNK
    nq = s // qb
    per = qb // WINDOW
    kv_col = PQ_W // PKV_W
    return pl.pallas_call(
        functools.partial(_attn_prompt_kernel, n_chunks=n_chunks),
        grid=(b, nq),
        in_specs=[
            pl.BlockSpec((qb, PQ_W), lambda bi, i: (bi * nq + i, 0)),
            pl.BlockSpec((qb, PKV_W), lambda bi, i: (bi * nq + i, kv_col)),
            pl.BlockSpec((WINDOW, PKV_W), lambda bi, i: (jnp.maximum((bi * nq + i) * per - 1, 0), kv_col)),
            pl.BlockSpec((N_KV_A, GROUP_A * CHUNK, KEY_PAD), lambda bi, i: (0, 0, 0)),
            pl.BlockSpec((N_KV_A, GROUP_A * CHUNK, KEY_PAD), lambda bi, i: (0, 0, 0)),
        ],
        out_specs=pl.BlockSpec((qb, PQ_W), lambda bi, i: (bi * nq + i, 0)),
        out_shape=jax.ShapeDtypeStruct((b * s, PQ_W), BF16),
        scratch_shapes=[pltpu.VMEM((qb + KEY_PAD - CHUNK, PKV_W), BF16)],
        compiler_params=_cparams(("parallel", "parallel")),
        name="attn_prompt",
    )(qkv, qkv, qkv, bias, fill)


def _attn_sample_kernel(q_ref, kvf_ref, ck_ref, cv_ref, bias_ref, sink_ref, o_ref, nk_ref, nv_ref, *, l):
    k_all = jnp.concatenate([ck_ref[0], kvf_ref[:, 0:KA_W]], axis=0)
    v_all = jnp.concatenate([cv_ref[0], kvf_ref[:, KA_W:2 * KA_W]], axis=0)
    nk_ref[0] = k_all[l:]
    nv_ref[0] = v_all[l:]
    kb = k_all.astype(BF16)
    vb = v_all.astype(BF16)
    o_ref[...] = jnp.zeros_like(o_ref)
    for kv in range(N_KV_A):
        q4 = jnp.concatenate(
            [q_ref[:, (kv * GROUP_A + g) * HEAD_PAD:(kv * GROUP_A + g) * HEAD_PAD + HEAD_DIM_A]
             for g in range(GROUP_A)], axis=0)
        o4 = _attend(q4, kb[:, kv * HEAD_DIM_A:(kv + 1) * HEAD_DIM_A], vb[:, kv * HEAD_DIM_A:(kv + 1) * HEAD_DIM_A],
                     bias_ref[kv], sink_ref[kv])
        for g in range(GROUP_A):
            c0 = (kv * GROUP_A + g) * HEAD_PAD
            o_ref[:, c0:c0 + HEAD_DIM_A] = o4[g * l:(g + 1) * l].astype(BF16)


def _attn_sample(qkv, kvf, cache_k, cache_v, bias, sink, bd, l):
    cache_spec = pl.BlockSpec((1, WINDOW, KA_W), lambda bi: (bi, 0, 0))
    cache_shape = jax.ShapeDtypeStruct((bd, WINDOW, KA_W), F32)
    return pl.pallas_call(
        functools.partial(_attn_sample_kernel, l=l),
        grid=(bd,),
        in_specs=[
            pl.BlockSpec((l, A_KVF), lambda bi: (bi, 0)),
            pl.BlockSpec((l, 2 * KA_W), lambda bi: (bi, 0)),
            cache_spec, cache_spec,
            pl.BlockSpec((N_KV_A, GROUP_A * l, WINDOW + l), lambda bi: (0, 0, 0)),
            pl.BlockSpec((N_KV_A, GROUP_A * l, 1), lambda bi: (0, 0, 0)),
        ],
        out_specs=[pl.BlockSpec((l, PQ_W), lambda bi: (bi, 0)), cache_spec, cache_spec],
        out_shape=[jax.ShapeDtypeStruct((bd * l, PQ_W), BF16), cache_shape, cache_shape],
        compiler_params=_cparams(("parallel",)),
        name="attn_sample",
    )(qkv, kvf, cache_k, cache_v, bias, sink)


def _retention_kernel(q_ref, k_ref, v_ref, g_ref, s0_ref, di_ref, qd_ref, kd_ref, gc_ref, gain_ref,
                      o_ref, s_ref, *, n_chunks, c):
    @pl.when(pl.program_id(1) == 0)
    def _():
        s_ref[...] = s0_ref[...]

    def chunk(n, carry):
        r0 = pl.multiple_of(n * c, c)
        for h in range(N_HEADS_R):
            cols = slice(h * DK_R, (h + 1) * DK_R)
            qc = q_ref[pl.ds(r0, c), cols]
            kc = k_ref[pl.ds(r0, c), cols]
            vc = v_ref[pl.ds(r0, c), cols]
            state = s_ref[0, h]
            sc = lax.dot_general(qc, kc, (((1,), (1,)), ((), ())), preferred_element_type=F32) * di_ref[h]
            o = jnp.dot(sc.astype(BF16), vc, preferred_element_type=F32)
            q_dec = (qc.astype(F32) * qd_ref[h]).astype(BF16)
            o = o + jnp.dot(q_dec, state.astype(BF16), preferred_element_type=F32)
            k_dec = (kc.astype(F32) * kd_ref[h]).astype(BF16)
            s_ref[0, h] = gc_ref[h] * state + lax.dot_general(k_dec, vc, (((0,), (0,)), ((), ())),
                                                              preferred_element_type=F32)
            mu = jnp.mean(o, axis=-1, keepdims=True)
            d = o - mu
            var = jnp.mean(d * d, axis=-1, keepdims=True)
            y = d * lax.rsqrt(var + LN_EPS) * gain_ref[:, cols]
            o_ref[pl.ds(r0, c), cols] = (y * _silu(g_ref[pl.ds(r0, c), cols].astype(F32))).astype(BF16)
        return carry

    lax.fori_loop(0, n_chunks, chunk, 0, unroll=4 if n_chunks % 4 == 0 else 1)


def _retention(qkr, vg, s0, tabs, gain, b, s, c):
    di, qd, kd, gc = tabs
    st = min(1024, s)
    nt = s // st
    seq = lambda col: pl.BlockSpec((st, QR_W), lambda bi, j: (bi * nt + j, col))
    full = lambda a: pl.BlockSpec(a.shape, lambda bi, j: (0,) * a.ndim)
    st_spec = pl.BlockSpec((1, N_HEADS_R, DK_R, DK_R), lambda bi, j: (bi, 0, 0, 0))
    return pl.pallas_call(
        functools.partial(_retention_kernel, n_chunks=st // c, c=c),
        grid=(b, nt),
        in_specs=[seq(0), seq(1), seq(0), seq(1), st_spec, full(di), full(qd), full(kd), full(gc), full(gain)],
        out_specs=[seq(0), st_spec],
        out_shape=[jax.ShapeDtypeStruct((b * s, QR_W), BF16), jax.ShapeDtypeStruct((b, N_HEADS_R, DK_R, DK_R), F32)],
        compiler_params=_cparams(("parallel", "arbitrary")),
        name="retention",
    )(qkr, qkr, vg, vg, s0, di, qd, kd, gc, gain)


def _mix_kernel(x_ref, oa_ref, or_ref, g_ref, wa_ref, wb_ref, wo_ref, lg_ref, lb_ref, wrt_ref, x1_ref, st_ref, xp_ref):
    a = jnp.dot(oa_ref[...], wa_ref[...], preferred_element_type=F32)
    b = jnp.dot(or_ref[...], wb_ref[...], preferred_element_type=F32)
    merged = g_ref[:, 0:D_MODEL].astype(F32) * a + g_ref[:, D_MODEL:].astype(F32) * b
    y = jnp.dot(merged.astype(BF16), wo_ref[...], preferred_element_type=F32)
    x1 = _layer_norm(DN_ALPHA * x_ref[...] + y, lg_ref[...], lb_ref[...])
    x1_ref[...] = x1
    _store_packed(xp_ref, x1)
    logits = lax.dot_general(wrt_ref[...], x1.astype(BF16), (((1,), (1,)), ((), ())), preferred_element_type=F32)
    st_ref[...] = jax.nn.sigmoid(logits)


def _mix(x2d, oa, orr, gm, wa, wb, wo, lg, lb, wrt, tm):
    t = x2d.shape[0]
    row = lambda w: pl.BlockSpec((tm, w), lambda i: (i, 0))
    full = lambda a: pl.BlockSpec(a.shape, lambda i: (0,) * a.ndim)
    return pl.pallas_call(
        _mix_kernel,
        grid=(t // tm,),
        in_specs=[row(D_MODEL), row(PQ_W), row(QR_W), row(2 * D_MODEL),
                  full(wa), full(wb), full(wo), full(lg), full(lb), full(wrt)],
        out_specs=[row(D_MODEL), pl.BlockSpec((N_EXPERTS, tm), lambda i: (0, i)),
                   pl.BlockSpec((tm * PACK_SUB, 128), lambda i: (i, 0))],
        out_shape=[jax.ShapeDtypeStruct((t, D_MODEL), F32), jax.ShapeDtypeStruct((N_EXPERTS, t), F32),
                   jax.ShapeDtypeStruct((t * PACK_SUB, 128), U32)],
        compiler_params=_cparams(("parallel",)),
        name="mix",
    )(x2d, oa, orr, gm, wa, wb, wo, lg, lb, wrt)


def _route_kernel(s_ref, bias_ref, cnt0_ref, e_ref, w_ref, r_ref, cnt_ref, carry_ref):
    @pl.when(pl.program_id(0) == 0)
    def _():
        carry_ref[...] = cnt0_ref[...].astype(F32)

    s = s_ref[...]
    tl = s.shape[1]
    choice = s + bias_ref[...]
    row = lax.broadcasted_iota(I32, (N_EXPERTS, tl), 0)
    row_g = lax.broadcasted_iota(I32, (GROUP_SIZE, tl), 0)
    neg = -jnp.inf

    scores = []
    for g in range(N_GROUPS):
        blk = choice[g * GROUP_SIZE:(g + 1) * GROUP_SIZE]
        m1 = jnp.max(blk, axis=0, keepdims=True)
        i1 = jnp.min(jnp.where(blk == m1, row_g, GROUP_SIZE), axis=0, keepdims=True)
        m2 = jnp.max(jnp.where(row_g == i1, neg, blk), axis=0, keepdims=True)
        scores.append(m1 + m2)
    sc = jnp.concatenate(scores, axis=0)
    gi = lax.broadcasted_iota(I32, sc.shape, 0)
    grank = jnp.zeros(sc.shape, I32)
    for g in range(N_GROUPS):
        other = sc[g:g + 1]
        ahead = jnp.where(other > sc, 1, jnp.where(other == sc, jnp.where(gi > g, 1, 0), 0))
        grank = grank + ahead
    cm = jnp.concatenate(
        [jnp.where(grank[g:g + 1] < TOPK_GROUPS, choice[g * GROUP_SIZE:(g + 1) * GROUP_SIZE], neg)
         for g in range(N_GROUPS)], axis=0)

    experts, weights = [], []
    for _ in range(TOP_K):
        m = jnp.max(cm, axis=0, keepdims=True)
        idx = jnp.min(jnp.where(cm == m, row, N_EXPERTS), axis=0, keepdims=True)
        hit = row == idx
        weights.append(jnp.sum(jnp.where(hit, s, 0.0), axis=0, keepdims=True))
        cm = jnp.where(hit, neg, cm)
        experts.append(idx)
    e8 = jnp.concatenate(experts, axis=0)
    w8 = jnp.concatenate(weights, axis=0)
    e_ref[...] = e8
    w_ref[...] = w8 / jnp.sum(w8, axis=0, keepdims=True) * ROUTED_SCALE

    before = (lax.broadcasted_iota(I32, (tl, tl), 0) < lax.broadcasted_iota(I32, (tl, tl), 1))
    before = jnp.where(before, 1.0, 0.0).astype(BF16)
    carry = carry_ref[...]
    ranks = []
    for k in range(TOP_K):
        hit = row == experts[k]
        onehot = jnp.where(hit, 1.0, 0.0)
        prefix = jnp.dot(onehot.astype(BF16), before, preferred_element_type=F32)
        ranks.append(jnp.sum(jnp.where(hit, prefix + carry, 0.0), axis=0, keepdims=True))
        carry = carry + jnp.sum(onehot, axis=1, keepdims=True)
    r_ref[...] = jnp.concatenate(ranks, axis=0).astype(I32)
    carry_ref[...] = carry
    cnt_ref[...] = carry.astype(I32)


def _route(st, bias_col, counts0, tl):
    t = st.shape[1]
    tok = pl.BlockSpec((TOP_K, tl), lambda i: (0, i))
    return pl.pallas_call(
        _route_kernel,
        grid=(t // tl,),
        in_specs=[pl.BlockSpec((N_EXPERTS, tl), lambda i: (0, i)), pl.BlockSpec((N_EXPERTS, 1), lambda i: (0, 0)),
                  pl.BlockSpec((N_EXPERTS, 1), lambda i: (0, 0))],
        out_specs=[tok, tok, tok, pl.BlockSpec((N_EXPERTS, 1), lambda i: (0, 0))],
        out_shape=[jax.ShapeDtypeStruct((TOP_K, t), I32), jax.ShapeDtypeStruct((TOP_K, t), F32),
                   jax.ShapeDtypeStruct((TOP_K, t), I32), jax.ShapeDtypeStruct((N_EXPERTS, 1), I32)],
        scratch_shapes=[pltpu.VMEM((N_EXPERTS, 1), F32)],
        compiler_params=_cparams(("arbitrary",)),
        name="route",
    )(st, bias_col, counts0)


def _dest_kernel(e_ref, r_ref, ps_ref, d_ref):
    tl = e_ref.shape[1]
    row = lax.broadcasted_iota(I32, (N_EXPERTS, tl), 0)
    ps = ps_ref[...]
    starts = [jnp.sum(jnp.where(row == e_ref[k:k + 1, :], ps, 0.0), axis=0, keepdims=True) for k in range(TOP_K)]
    d_ref[...] = jnp.concatenate(starts, axis=0).astype(I32) + r_ref[...]


def _dest(e8, r8, pstart_col, tl):
    t = e8.shape[1]
    tok = pl.BlockSpec((TOP_K, tl), lambda i: (0, i))
    return pl.pallas_call(
        _dest_kernel,
        grid=(t // tl,),
        in_specs=[tok, tok, pl.BlockSpec((N_EXPERTS, 1), lambda i: (0, 0))],
        out_specs=tok,
        out_shape=jax.ShapeDtypeStruct((TOP_K, t), I32),
        compiler_params=_cparams(("parallel",)),
        name="dest",
    )(e8, r8, pstart_col)


TOKEN_GROUP = 8


def _row_copy(src, src_tok, dst, dst_tok, sem):
    s0 = pl.multiple_of(src_tok * PACK_SUB, PACK_SUB)
    d0 = pl.multiple_of(dst_tok * PACK_SUB, PACK_SUB)
    return pltpu.make_async_copy(src.at[pl.ds(s0, PACK_SUB)], dst.at[pl.ds(d0, PACK_SUB)], sem)


def _wait_bytes_of(ref, sem):
    pltpu.make_async_copy(ref, ref, sem).wait()


def _dispatch_kernel(d_ref, x_ref, wsg_ref, wsu_ref, wsd_ref, *refs, td):
    xs_hbm, sh_ref, sem = refs[-3:]

    def start(g, carry):
        for j in range(TOKEN_GROUP):
            t = g * TOKEN_GROUP + j
            for k in range(TOP_K):
                _row_copy(x_ref, t, xs_hbm, d_ref[t * TOP_K + k], sem).start(priority=k % 2)
        return carry

    lax.fori_loop(0, td // TOKEN_GROUP, start, 0)

    hi, lo = _load_packed(x_ref, td)
    xb = jnp.concatenate([hi.astype(BF16), lo.astype(BF16)], axis=1)
    hs = _silu(jnp.dot(xb, wsg_ref[...], preferred_element_type=F32)) * jnp.dot(xb, wsu_ref[...],
                                                                               preferred_element_type=F32)
    sh_ref[...] = jnp.dot(hs.astype(BF16), wsd_ref[...], preferred_element_type=F32)

    for k in range(TOP_K):
        _wait_bytes_of(x_ref, sem)


def _dispatch(d8, xp, wsg, wsu, wsd, n_rows, td, xs_prev=None):
    t = d8.shape[0] // TOP_K
    full = lambda a: pl.BlockSpec(a.shape, lambda i: (0,) * a.ndim)
    in_specs = [pl.BlockSpec((td * TOP_K,), lambda i: (i,), memory_space=pltpu.SMEM),
                pl.BlockSpec((td * PACK_SUB, 128), lambda i: (i, 0)), full(wsg), full(wsu), full(wsd)]
    args = [d8, xp, wsg, wsu, wsd]
    if xs_prev is not None:
        in_specs.append(pl.BlockSpec(memory_space=pl.ANY))
        args.append(xs_prev)
    return pl.pallas_call(
        functools.partial(_dispatch_kernel, td=td),
        grid=(t // td,),
        in_specs=in_specs,
        out_specs=[pl.BlockSpec(memory_space=pl.ANY), pl.BlockSpec((td, D_MODEL), lambda i: (i, 0))],
        scratch_shapes=[pltpu.SemaphoreType.DMA(())],
        out_shape=[jax.ShapeDtypeStruct((n_rows * PACK_SUB, 128), U32), jax.ShapeDtypeStruct((t, D_MODEL), F32)],
        input_output_aliases={} if xs_prev is None else {5: 0},
        compiler_params=_cparams(("arbitrary",)),
        name="dispatch",
    )(*args)


X_RING = 3


def _moe_kernel(be_ref, nv_ref, x_hbm, wg_ref, wu_ref, wd_ref, y_ref, x_ring, x_sem, wg_bf, wu_bf, wd_bf):
    j = pl.program_id(0)
    n = pl.num_programs(0)
    nv = nv_ref[j]
    blk_rows = MOE_ROWS * PACK_SUB

    def x_copy(blk):
        r0 = pl.multiple_of(blk * blk_rows, blk_rows)
        slot = blk % X_RING
        return pltpu.make_async_copy(x_hbm.at[pl.ds(r0, blk_rows)], x_ring.at[slot], x_sem.at[slot])

    def start_if_real(blk):
        @pl.when((blk < n) & (nv_ref[jnp.minimum(blk, n - 1)] > 0))
        def _():
            x_copy(blk).start()

    @pl.when(j == 0)
    def _():
        for blk in range(X_RING - 1):
            start_if_real(jnp.int32(blk))

    start_if_real(j + X_RING - 1)

    @pl.when((j == 0) | (be_ref[j] != be_ref[jnp.maximum(j - 1, 0)]))
    def _():
        wg_bf[...] = wg_ref[0].astype(BF16)
        wu_bf[...] = wu_ref[0].astype(BF16)
        wd_bf[...] = wd_ref[0].astype(BF16)

    @pl.when(nv > 0)
    def _():
        x_copy(j).wait()
        hi, lo = _load_packed(x_ring.at[j % X_RING], MOE_ROWS)
        valid = lax.broadcasted_iota(I32, hi.shape, 0) < nv
        xb = jnp.concatenate([jnp.where(valid, hi, 0.0).astype(BF16), jnp.where(valid, lo, 0.0).astype(BF16)], axis=1)
        g = jnp.dot(xb, wg_bf[...], preferred_element_type=F32)
        u = jnp.dot(xb, wu_bf[...], preferred_element_type=F32)
        h = (_silu(g) * u).astype(BF16)
        _store_packed(y_ref, jnp.dot(h, wd_bf[...], preferred_element_type=F32))

    @pl.when(nv == 0)
    def _():
        y_ref[...] = jnp.zeros_like(y_ref)


def _moe(blk_exp, blk_valid, xs, wg, wu, wd):
    n_blocks = blk_exp.shape[0]
    rows = pl.BlockSpec((MOE_ROWS * PACK_SUB, 128), lambda j, be, nv: (j, 0))
    return pl.pallas_call(
        _moe_kernel,
        grid_spec=pltpu.PrefetchScalarGridSpec(
            num_scalar_prefetch=2,
            grid=(n_blocks,),
            in_specs=[pl.BlockSpec(memory_space=pl.ANY),
                      pl.BlockSpec((1, D_MODEL, D_EXPERT), lambda j, be, nv: (be[j], 0, 0)),
                      pl.BlockSpec((1, D_MODEL, D_EXPERT), lambda j, be, nv: (be[j], 0, 0)),
                      pl.BlockSpec((1, D_EXPERT, D_MODEL), lambda j, be, nv: (be[j], 0, 0))],
            out_specs=rows,
            scratch_shapes=[pltpu.VMEM((X_RING, MOE_ROWS * PACK_SUB, 128), U32), pltpu.SemaphoreType.DMA((X_RING,)),
                            pltpu.VMEM((D_MODEL, D_EXPERT), BF16), pltpu.VMEM((D_MODEL, D_EXPERT), BF16),
                            pltpu.VMEM((D_EXPERT, D_MODEL), BF16)],
        ),
        out_shape=jax.ShapeDtypeStruct(xs.shape, U32),
        compiler_params=_cparams(("arbitrary",)),
        name="moe",
    )(blk_exp, blk_valid, xs, wg, wu, wd)


COMBINE_ROWS = 32


def _gather_rows(ys_hbm, d_ref, slot_buf, sem, t0, n):
    for j in range(n):
        t = t0 + j
        for k in range(TOP_K):
            _row_copy(ys_hbm, d_ref[t * TOP_K + k], slot_buf.at[k], t, sem).start(priority=k % 2)


def _final_kernel(d_ref, dnext_ref, x1_ref, w_ref, p_ref, sh_ref, lg_ref, lb_ref,
                  wpp_ref, wpg_ref, ys_hbm, out_ref, gbuf, ybuf, sem, *, tf):
    i = pl.program_id(0)
    slot = i % 2

    @pl.when(i == 0)
    def _():
        def first(g, carry):
            _gather_rows(ys_hbm, d_ref, gbuf.at[0], sem.at[0], g * TOKEN_GROUP, TOKEN_GROUP)
            return carry

        lax.fori_loop(0, tf // TOKEN_GROUP, first, 0)

    _wait_bytes_of(gbuf.at[slot], sem.at[slot])

    def combine(sb, carry):
        r0 = pl.multiple_of(sb * COMBINE_ROWS, COMBINE_ROWS)
        _gather_rows(ys_hbm, dnext_ref, gbuf.at[1 - slot], sem.at[1 - slot], r0, COMBINE_ROWS)
        y_hi = jnp.zeros((COMBINE_ROWS, PACK_W), F32)
        y_lo = jnp.zeros((COMBINE_ROWS, PACK_W), F32)
        for k in range(TOP_K):
            rows = gbuf.at[slot, k, pl.ds(pl.multiple_of(r0 * PACK_SUB, COMBINE_ROWS * PACK_SUB),
                                          COMBINE_ROWS * PACK_SUB)]
            hi, lo = _load_packed(rows, COMBINE_ROWS)
            wk = w_ref[pl.ds(r0, COMBINE_ROWS), k:k + 1]
            y_hi = y_hi + wk * hi
            y_lo = y_lo + wk * lo
        ybuf[pl.ds(r0, COMBINE_ROWS), 0:PACK_W] = y_hi
        ybuf[pl.ds(r0, COMBINE_ROWS), PACK_W:] = y_lo
        return carry

    lax.fori_loop(0, tf // COMBINE_ROWS, combine, 0)

    @pl.when(i == pl.num_programs(0) - 1)
    def _():
        _wait_bytes_of(gbuf.at[1 - slot], sem.at[1 - slot])

    y = sh_ref[...] + ybuf[...]
    x2 = _layer_norm(DN_ALPHA * x1_ref[...] + y, lg_ref[...], lb_ref[...])
    gate = jax.nn.sigmoid(jnp.dot(x2.astype(BF16), wpg_ref[...], preferred_element_type=F32))
    out_ref[...] = x2 + gate * jnp.dot(p_ref[...].astype(BF16), wpp_ref[...], preferred_element_type=F32)


def _final(d8, x1, w_tok, p2d, sh, lg, lb, wpp, wpg, ys, tf):
    t = x1.shape[0]
    n_tiles = t // tf
    row = lambda w: pl.BlockSpec((tf, w), lambda i: (i, 0))
    full = lambda a: pl.BlockSpec(a.shape, lambda i: (0,) * a.ndim)
    return pl.pallas_call(
        functools.partial(_final_kernel, tf=tf),
        grid=(n_tiles,),
        in_specs=[pl.BlockSpec((tf * TOP_K,), lambda i: (i,), memory_space=pltpu.SMEM),
                  pl.BlockSpec((tf * TOP_K,), lambda i: (jnp.minimum(i + 1, n_tiles - 1),), memory_space=pltpu.SMEM),
                  row(D_MODEL), row(TOP_K), row(D_PLE), row(D_MODEL),
                  full(lg), full(lb), full(wpp), full(wpg),
                  pl.BlockSpec(memory_space=pl.ANY)],
        out_specs=row(D_MODEL),
        scratch_shapes=[pltpu.VMEM((2, TOP_K, tf * PACK_SUB, 128), U32), pltpu.VMEM((tf, D_MODEL), F32),
                        pltpu.SemaphoreType.DMA((2,))],
        out_shape=jax.ShapeDtypeStruct((t, D_MODEL), F32),
        compiler_params=_cparams(("arbitrary",)),
        name="final",
    )(d8, d8, x1, w_tok, p2d, sh, lg, lb, wpp, wpg, ys)


def _t5_bucket(rel):
    nb = N_BUCKETS // 2
    max_exact = nb // 2
    ret = jnp.where(rel > 0, nb, 0)
    n = jnp.abs(rel)
    nf = jnp.maximum(n, 1).astype(F32)
    large = max_exact + (jnp.log(nf / max_exact) / math.log(MAX_DIST / max_exact) * (nb - max_exact)).astype(I32)
    large = jnp.minimum(large, nb - 1)
    return ret + jnp.where(n < max_exact, n, large)


def _rel_bias(table, q_len, k_len):
    rel = (jnp.arange(k_len, dtype=I32)[None, :] - WINDOW) - jnp.arange(q_len, dtype=I32)[:, None]
    hit = _t5_bucket(rel)[:, :, None, None] == jnp.arange(N_BUCKETS, dtype=I32)[None, None, :, None]
    b = jnp.sum(jnp.where(hit, table.astype(F32)[None, None], 0.0), axis=2)
    return jnp.transpose(b, (2, 0, 1)).reshape(N_KV_A, GROUP_A * q_len, k_len).astype(F32)


def _rotary_tables(pos):
    half = DK_R // 2
    inv = ROPE_BASE ** (-jnp.arange(half, dtype=F32) / half)
    ang = pos.astype(F32)[:, None] * inv[None, :]
    cos, sin = jnp.cos(ang), jnp.sin(ang)
    return jnp.concatenate([cos, cos], -1), jnp.concatenate([-sin, sin], -1)


def _decay_tables(c):
    log_gamma = jnp.log1p(-jnp.exp2(-5.0 - jnp.arange(N_HEADS_R, dtype=F32)))
    idx = jnp.arange(c, dtype=F32)
    di = jnp.exp(jnp.abs(idx[:, None] - idx[None, :])[None] * log_gamma[:, None, None])
    qd = jnp.exp((idx[None, :] + 1.0) * log_gamma[:, None])
    kd = jnp.exp((c - 1.0 - idx[None, :]) * log_gamma[:, None])
    gc = jnp.exp(c * log_gamma)
    bc = lambda a: jnp.broadcast_to(a[:, :, None], (N_HEADS_R, c, DK_R))
    return di, bc(qd), bc(kd), jnp.broadcast_to(gc[:, None, None], (N_HEADS_R, DK_R, DK_R))


def _pad_heads(w, n_heads):
    rows = w.shape[0]
    w = w.reshape(rows, n_heads, HEAD_DIM_A)
    w = jnp.pad(w, ((0, 0), (0, 0), (0, HEAD_PAD - HEAD_DIM_A)))
    return w.reshape(rows, n_heads * HEAD_PAD)


def _moe_ffn(groups, wts):
    tiles = [min(256, g[0].shape[0]) for g in groups]
    counts = jnp.zeros((N_EXPERTS, 1), I32)
    routed = []
    for (x1, xp, st, p2d), tile in zip(groups, tiles):
        e8, w8, r8, counts = _route(st, wts['router_bias'], counts, min(2 * tile, st.shape[1]))
        routed.append((e8, w8, r8))
    counts = counts[:, 0]
    padded = (counts + MOE_ROWS - 1) // MOE_ROWS * MOE_ROWS
    pad_end = jnp.cumsum(padded)
    pstart = (pad_end - padded).astype(I32)
    n_rows = sum(g[0].shape[0] for g in groups) * TOP_K + N_EXPERTS * MOE_ROWS
    blk_start = jnp.arange(n_rows // MOE_ROWS, dtype=I32) * MOE_ROWS
    blk_exp = jnp.minimum(jnp.sum(blk_start[:, None] >= pad_end[None, :], axis=1), N_EXPERTS - 1).astype(I32)
    own = blk_exp[:, None] == jnp.arange(N_EXPERTS, dtype=I32)[None, :]
    blk_end = jnp.sum(jnp.where(own, (pstart + counts)[None, :], 0), axis=1)
    blk_valid = jnp.clip(blk_end - blk_start, 0, MOE_ROWS).astype(I32)
    pstart_col = pstart.astype(F32).reshape(N_EXPERTS, 1)
    dests, shared, xs = [], [], None
    for (x1, xp, st, p2d), (e8, w8, r8), tile in zip(groups, routed, tiles):
        d8 = _dest(e8, r8, pstart_col, tile).T.reshape(-1)
        xs, sh = _dispatch(d8, xp, wts['w_sh_gate'], wts['w_sh_up'], wts['w_sh_down'], n_rows, tile, xs)
        dests.append(d8)
        shared.append(sh)
    ys = _moe(blk_exp, blk_valid, xs, wts['w_exp_gate'], wts['w_exp_up'], wts['w_exp_down'])
    return [_final(d8, x1, w8.T, p2d, sh, wts['ln2_g'], wts['ln2_b'], wts['w_ple_proj'], wts['w_ple_gate'], ys, tile)
            for (x1, xp, st, p2d), (e8, w8, r8), d8, sh, tile in zip(groups, routed, dests, shared, tiles)]


def _mix_group(x2d, oa, orr, gm, p2d, wts):
    x1, st, xp = _mix(x2d, oa, orr, gm, wts['w_branch_attn'], wts['w_branch_ret'], wts['w_out'],
                      wts['ln1_g'], wts['ln1_b'], wts['w_router_t'], min(512, x2d.shape[0]))
    return x1, xp, st, p2d


def kernel(x_prompt, x_sample, cache_attn_k, cache_attn_v, state_retention, p_prompt, p_sample,
           w_in, attn_sinks, rel_bias_table, ret_gn_gain, w_branch_attn, w_branch_ret, w_out,
           ln1_g, ln1_b, w_router, router_bias, w_exp_gate, w_exp_up, w_exp_down,
           w_sh_gate, w_sh_up, w_sh_down, ln2_g, ln2_b, w_ple_proj, w_ple_gate):
    b, s, _ = x_prompt.shape
    bd, l, _ = x_sample.shape
    i = 0
    row = lambda a: a.reshape(1, -1).astype(F32)
    wts = dict(
        w_branch_attn=_pad_heads(w_branch_attn[i].astype(BF16).T, N_HEADS_A).T,
        w_branch_ret=w_branch_ret[i].astype(BF16),
        w_out=w_out[i].astype(BF16), ln1_g=row(ln1_g[i]), ln1_b=row(ln1_b[i]),
        w_router_t=w_router[i].T.astype(BF16), router_bias=router_bias[i].reshape(N_EXPERTS, 1).astype(F32),
        w_exp_gate=w_exp_gate[i], w_exp_up=w_exp_up[i], w_exp_down=w_exp_down[i],
        w_sh_gate=w_sh_gate[i].astype(BF16), w_sh_up=w_sh_up[i].astype(BF16), w_sh_down=w_sh_down[i].astype(BF16),
        ln2_g=row(ln2_g[i]), ln2_b=row(ln2_b[i]),
        w_ple_proj=w_ple_proj[i].astype(BF16), w_ple_gate=w_ple_gate[i].astype(BF16))
    w_in_bf = w_in[i].astype(BF16)
    w_in_bf = jnp.concatenate([_pad_heads(w_in_bf[:, :QA_W] * (HEAD_DIM_A ** -0.5), N_HEADS_A),
                               _pad_heads(w_in_bf[:, OFF_KV:OFF_QR], 2 * N_KV_A), w_in_bf[:, OFF_KV:]], axis=1)
    gain = row(ret_gn_gain[i])
    sinks = attn_sinks[i].astype(F32).reshape(N_KV_A, GROUP_A)

    xp = x_prompt.reshape(b * s, D_MODEL)
    cos_p, sin_p = _rotary_tables(jnp.arange(s, dtype=I32))
    qkv, kvf, qkr, vg, gm = _inproj(xp, w_in_bf, cos_p, sin_p, min(512, s))
    bias_p = jnp.pad(_rel_bias(rel_bias_table, CHUNK, WINDOW + CHUNK), ((0, 0), (0, 0), (0, KEY_PAD - WINDOW - CHUNK)))
    sink_p = jnp.repeat(sinks, CHUNK, axis=1)[..., None]
    fill_p = jnp.where(jnp.arange(KEY_PAD)[None, None, :] == WINDOW + CHUNK, sink_p, NEG_INF).astype(F32)
    oa = _attn_prompt(qkv, bias_p, fill_p, b, s)
    s0 = jnp.zeros((b, N_HEADS_R, DK_R, DK_R), F32)
    orr, ns_p = _retention(qkr, vg, s0, _decay_tables(CHUNK), gain, b, s, CHUNK)
    group_p = _mix_group(xp, oa, orr, gm, p_prompt[i].reshape(b * s, D_PLE), wts)
    kv_tail = kvf.reshape(b, s, 2, N_KV_A, HEAD_DIM_A)[:, s - WINDOW:]
    nk_p, nv_p = kv_tail[:, :, 0], kv_tail[:, :, 1]

    xs = x_sample.reshape(bd * l, D_MODEL)
    cos_s, sin_s = _rotary_tables(jnp.tile(PAST_LEN + jnp.arange(l, dtype=I32), bd))
    qkv, kvf, qkr, vg, gm = _inproj(xs, w_in_bf, cos_s, sin_s, bd * l)
    sink_s = jnp.repeat(sinks, l, axis=1)[..., None]
    oa, nk_s, nv_s = _attn_sample(qkv, kvf, cache_attn_k[i].reshape(bd, WINDOW, KA_W),
                                  cache_attn_v[i].reshape(bd, WINDOW, KA_W),
                                  _rel_bias(rel_bias_table, l, WINDOW + l), sink_s, bd, l)
    orr, ns_s = _retention(qkr, vg, state_retention[i].astype(F32), _decay_tables(l), gain, bd, l, l)
    group_s = _mix_group(xs, oa, orr, gm, p_sample[i].reshape(bd * l, D_PLE), wts)
    y_p, y_s = _moe_ffn([group_p, group_s], wts)
    y_p, y_s = y_p.reshape(b, s, D_MODEL), y_s.reshape(bd, l, D_MODEL)
    shape_kv = (bd, WINDOW, N_KV_A, HEAD_DIM_A)
    return (y_p, y_s, nk_p[None], nv_p[None], ns_p[None],
            nk_s.reshape(shape_kv)[None], nv_s.reshape(shape_kv)[None], ns_s[None])
```

```python
import functools
import math

import jax
import jax.numpy as jnp
from jax import lax
from jax.experimental import pallas as pl
from jax.experimental.pallas import tpu as pltpu

F32 = jnp.float32
BF16 = jnp.bfloat16
I32 = jnp.int32

D_MODEL = 1024
CHUNK = 64
WINDOW = 128
N_HEADS_A = 8
N_KV_A = 2
GROUP_A = 4
HEAD_DIM_A = 64
N_BUCKETS = 32
MAX_DIST = 128
N_HEADS_R = 4
DK_R = 128
ROPE_BASE = 10000.0
N_EXPERTS = 256
TOP_K = 8
N_GROUPS = 8
GROUP_SIZE = N_EXPERTS // N_GROUPS
TOPK_GROUPS = 4
D_EXPERT = 256
ROUTED_SCALE = 2.5
D_PLE = 256
LN_EPS = 1e-5
NEG_INF = -1e30
PAST_LEN = 2048
DEPTH = 1
DN_ALPHA = (2 * DEPTH) ** 0.25

QA_W = N_HEADS_A * HEAD_DIM_A
KA_W = N_KV_A * HEAD_DIM_A
QR_W = N_HEADS_R * DK_R
OFF_KV = QA_W
OFF_QR = QA_W + 2 * KA_W
OFF_VR = OFF_QR + 2 * QR_W
OFF_GM = OFF_VR + 2 * QR_W
IN_W = OFF_GM + 2 * D_MODEL

HEAD_PAD = 128
PQ_W = N_HEADS_A * HEAD_PAD
PKV_W = 2 * N_KV_A * HEAD_PAD
A_KVF = PQ_W + PKV_W
A_QR = A_KVF + 2 * KA_W
A_VR = A_QR + 2 * QR_W
A_GM = A_VR + 2 * QR_W
A_W = A_GM + 2 * D_MODEL

MOE_ROWS = 512
TOKEN_TILE = 512
VMEM_LIMIT = 56 * 1024 * 1024


def _cparams(sem, vmem=VMEM_LIMIT):
    return pltpu.CompilerParams(dimension_semantics=sem, vmem_limit_bytes=vmem)


def _layer_norm(h, g, b):
    mu = jnp.mean(h, axis=-1, keepdims=True)
    d = h - mu
    var = jnp.mean(d * d, axis=-1, keepdims=True)
    return d * lax.rsqrt(var + LN_EPS) * g + b


def _silu(x):
    return x * jax.nn.sigmoid(x)


U32 = jnp.uint32
PACK_W = D_MODEL // 2
PACK_SUB = PACK_W // 128


def _bf16_bits(x):
    return pltpu.bitcast(x.astype(BF16).astype(F32), U32)


def _store_packed(ref, x):
    n = x.shape[0]
    words = _bf16_bits(x[:, :PACK_W]) | (_bf16_bits(x[:, PACK_W:]) >> 16)
    for c in range(PACK_SUB):
        ref[pl.ds(c, n, stride=PACK_SUB), :] = words[:, c * 128:(c + 1) * 128]


def _load_packed(ref, n):
    words = jnp.concatenate([ref[pl.ds(c, n, stride=PACK_SUB), :] for c in range(PACK_SUB)], axis=1)
    hi = pltpu.bitcast(words & jnp.uint32(0xFFFF0000), F32)
    lo = pltpu.bitcast(words << 16, F32)
    return hi, lo


def _inproj_kernel(x_ref, w_ref, cos_ref, sin_ref, qkv_ref, kvf_ref, qkr_ref, vg_ref, gm_ref):
    xb = x_ref[...].astype(BF16)

    def mm(lo, hi):
        return jnp.dot(xb, w_ref[:, lo:hi], preferred_element_type=F32)

    qkv_ref[:, 0:PQ_W] = mm(0, PQ_W).astype(BF16)
    qkv_ref[:, PQ_W:A_KVF] = mm(PQ_W, A_KVF).astype(BF16)
    kvf_ref[...] = mm(A_KVF, A_QR)
    cos = cos_ref[...]
    sin = sin_ref[...]
    for part in range(2):
        z = mm(A_QR + part * QR_W, A_QR + (part + 1) * QR_W)
        for h in range(N_HEADS_R):
            zh = z[:, h * DK_R:(h + 1) * DK_R]
            r = zh * cos + pltpu.roll(zh, DK_R // 2, axis=1) * sin
            if part == 1:
                r = r * (DK_R ** -0.5)
            c0 = part * QR_W + h * DK_R
            qkr_ref[:, c0:c0 + DK_R] = r.astype(BF16)
    vg_ref[...] = mm(A_VR, A_GM).astype(BF16)
    for half in range(2):
        lo = A_GM + half * D_MODEL
        gm_ref[:, half * D_MODEL:(half + 1) * D_MODEL] = jax.nn.sigmoid(mm(lo, lo + D_MODEL)).astype(BF16)


def _inproj(x2d, w_bf, cos_tab, sin_tab, tm):
    t = x2d.shape[0]
    nper = cos_tab.shape[0] // tm
    row = lambda i: (i, 0)
    return pl.pallas_call(
        _inproj_kernel,
        grid=(t // tm,),
        in_specs=[
            pl.BlockSpec((tm, D_MODEL), row),
            pl.BlockSpec((D_MODEL, A_W), lambda i: (0, 0)),
            pl.BlockSpec((tm, DK_R), lambda i: (i % nper, 0)),
            pl.BlockSpec((tm, DK_R), lambda i: (i % nper, 0)),
        ],
        out_specs=[
            pl.BlockSpec((tm, A_KVF), row),
            pl.BlockSpec((tm, 2 * KA_W), row),
            pl.BlockSpec((tm, 2 * QR_W), row),
            pl.BlockSpec((tm, 2 * QR_W), row),
            pl.BlockSpec((tm, 2 * D_MODEL), row),
        ],
        out_shape=[
            jax.ShapeDtypeStruct((t, A_KVF), BF16),
            jax.ShapeDtypeStruct((t, 2 * KA_W), F32),
            jax.ShapeDtypeStruct((t, 2 * QR_W), BF16),
            jax.ShapeDtypeStruct((t, 2 * QR_W), BF16),
            jax.ShapeDtypeStruct((t, 2 * D_MODEL), BF16),
        ],
        compiler_params=_cparams(("parallel",)),
        name="inproj",
    )(x2d, w_bf, cos_tab, sin_tab)


def _attend(q4, k, v, bias, sink):
    s = lax.dot_general(q4, k, (((1,), (1,)), ((), ())), preferred_element_type=F32) + bias
    m = jnp.maximum(jnp.max(s, axis=-1, keepdims=True), sink)
    e = jnp.exp(s - m)
    p = e / (jnp.sum(e, axis=-1, keepdims=True) + jnp.exp(sink - m))
    return jnp.dot(p.astype(BF16), v, preferred_element_type=F32)


KEY_PAD = 256


def _attend_sink_column(q4, k, v, bias, fill, valid):
    s = lax.dot_general(q4, k, (((1,), (1,)), ((), ())), preferred_element_type=F32)
    s = jnp.where(valid, s + bias, fill)
    e = jnp.exp(s - jnp.max(s, axis=-1, keepdims=True)).astype(BF16)
    dv = v.shape[1]
    ones = jnp.ones((KEY_PAD, dv), BF16)
    v_rows = jnp.concatenate([v, jnp.zeros((KEY_PAD - WINDOW - CHUNK, dv), BF16)], axis=0)
    both = jnp.dot(e, jnp.concatenate([v_rows, ones], axis=1), preferred_element_type=F32)
    return both[:, 0:dv] / both[:, dv:]


def _attn_prompt_kernel(q_ref, kvc_ref, kvp_ref, bias_ref, fill_ref, o_ref, kv_buf, *, n_chunks):
    i = pl.program_id(1)
    qb = n_chunks * CHUNK
    kv_buf[0:WINDOW, :] = kvp_ref[...]
    kv_buf[WINDOW:WINDOW + qb, :] = kvc_ref[...]
    kv_buf[WINDOW + qb:, :] = jnp.zeros((KEY_PAD - WINDOW - CHUNK, PKV_W), BF16)
    col = lax.broadcasted_iota(I32, (GROUP_A * CHUNK, KEY_PAD), 1)
    key_col = jnp.where(col < WINDOW + CHUNK, col, -1)

    def chunk(j, carry):
        r0 = pl.multiple_of(j * CHUNK, CHUNK)
        valid = key_col >= jnp.maximum(WINDOW - (i * n_chunks + j) * CHUNK, 0)
        for kv in range(N_KV_A):
            q4 = jnp.concatenate(
                [q_ref[pl.ds(r0, CHUNK), (kv * GROUP_A + g) * HEAD_PAD:(kv * GROUP_A + g + 1) * HEAD_PAD]
                 for g in range(GROUP_A)], axis=0)
            k = kv_buf[pl.ds(r0, KEY_PAD), kv * HEAD_PAD:(kv + 1) * HEAD_PAD]
            v = kv_buf[pl.ds(r0, WINDOW + CHUNK), (N_KV_A + kv) * HEAD_PAD:(N_KV_A + kv + 1) * HEAD_PAD]
            o4 = _attend_sink_column(q4, k, v, bias_ref[kv], fill_ref[kv], valid)
            for g in range(GROUP_A):
                c0 = (kv * GROUP_A + g) * HEAD_PAD
                o_ref[pl.ds(r0, CHUNK), c0:c0 + HEAD_PAD] = o4[g * CHUNK:(g + 1) * CHUNK].astype(BF16)
        return carry

    lax.fori_loop(0, n_chunks, chunk, 0, unroll=8 if n_chunks % 8 == 0 else 1)


def _attn_prompt(qkv, bias, fill, b, s):
    qb = min(512, s)
    n_chunks = qb // CHUNK
    nq = s // qb
    per = qb // WINDOW
    kv_col = PQ_W // PKV_W
    return pl.pallas_call(
        functools.partial(_attn_prompt_kernel, n_chunks=n_chunks),
        grid=(b, nq),
        in_specs=[
            pl.BlockSpec((qb, PQ_W), lambda bi, i: (bi * nq + i, 0)),
            pl.BlockSpec((qb, PKV_W), lambda bi, i: (bi * nq + i, kv_col)),
            pl.BlockSpec((WINDOW, PKV_W), lambda bi, i: (jnp.maximum((bi * nq + i) * per - 1, 0), kv_col)),
            pl.BlockSpec((N_KV_A, GROUP_A * CHUNK, KEY_PAD), lambda bi, i: (0, 0, 0)),
            pl.BlockSpec((N_KV_A, GROUP_A * CHUNK, KEY_PAD), lambda bi, i: (0, 0, 0)),
        ],
        out_specs=pl.BlockSpec((qb, PQ_W), lambda bi, i: (bi * nq + i, 0)),
        out_shape=jax.ShapeDtypeStruct((b * s, PQ_W), BF16),
        scratch_shapes=[pltpu.VMEM((qb + KEY_PAD - CHUNK, PKV_W), BF16)],
        compiler_params=_cparams(("parallel", "parallel")),
        name="attn_prompt",
    )(qkv, qkv, qkv, bias, fill)


def _attn_sample_kernel(q_ref, kvf_ref, ck_ref, cv_ref, bias_ref, sink_ref, o_ref, nk_ref, nv_ref, *, l):
    k_all = jnp.concatenate([ck_ref[0], kvf_ref[:, 0:KA_W]], axis=0)
    v_all = jnp.concatenate([cv_ref[0], kvf_ref[:, KA_W:2 * KA_W]], axis=0)
    nk_ref[0] = k_all[l:]
    nv_ref[0] = v_all[l:]
    kb = k_all.astype(BF16)
    vb = v_all.astype(BF16)
    o_ref[...] = jnp.zeros_like(o_ref)
    for kv in range(N_KV_A):
        q4 = jnp.concatenate(
            [q_ref[:, (kv * GROUP_A + g) * HEAD_PAD:(kv * GROUP_A + g) * HEAD_PAD + HEAD_DIM_A]
             for g in range(GROUP_A)], axis=0)
        o4 = _attend(q4, kb[:, kv * HEAD_DIM_A:(kv + 1) * HEAD_DIM_A], vb[:, kv * HEAD_DIM_A:(kv + 1) * HEAD_DIM_A],
                     bias_ref[kv], sink_ref[kv])
        for g in range(GROUP_A):
            c0 = (kv * GROUP_A + g) * HEAD_PAD
            o_ref[:, c0:c0 + HEAD_DIM_A] = o4[g * l:(g + 1) * l].astype(BF16)


def _attn_sample(qkv, kvf, cache_k, cache_v, bias, sink, bd, l):
    cache_spec = pl.BlockSpec((1, WINDOW, KA_W), lambda bi: (bi, 0, 0))
    cache_shape = jax.ShapeDtypeStruct((bd, WINDOW, KA_W), F32)
    return pl.pallas_call(
        functools.partial(_attn_sample_kernel, l=l),
        grid=(bd,),
        in_specs=[
            pl.BlockSpec((l, A_KVF), lambda bi: (bi, 0)),
            pl.BlockSpec((l, 2 * KA_W), lambda bi: (bi, 0)),
            cache_spec, cache_spec,
            pl.BlockSpec((N_KV_A, GROUP_A * l, WINDOW + l), lambda bi: (0, 0, 0)),
            pl.BlockSpec((N_KV_A, GROUP_A * l, 1), lambda bi: (0, 0, 0)),
        ],
        out_specs=[pl.BlockSpec((l, PQ_W), lambda bi: (bi, 0)), cache_spec, cache_spec],
        out_shape=[jax.ShapeDtypeStruct((bd * l, PQ_W), BF16), cache_shape, cache_shape],
        compiler_params=_cparams(("parallel",)),
        name="attn_sample",
    )(qkv, kvf, cache_k, cache_v, bias, sink)


def _retention_kernel(q_ref, k_ref, v_ref, g_ref, s0_ref, di_ref, qd_ref, kd_ref, gc_ref, gain_ref,
                      o_ref, s_ref, *, n_chunks, c):
    @pl.when(pl.program_id(1) == 0)
    def _():
        s_ref[...] = s0_ref[...]

    def chunk(n, carry):
        r0 = pl.multiple_of(n * c, c)
        for h in range(N_HEADS_R):
            cols = slice(h * DK_R, (h + 1) * DK_R)
            qc = q_ref[pl.ds(r0, c), cols]
            kc = k_ref[pl.ds(r0, c), cols]
            vc = v_ref[pl.ds(r0, c), cols]
            state = s_ref[0, h]
            sc = lax.dot_general(qc, kc, (((1,), (1,)), ((), ())), preferred_element_type=F32) * di_ref[h]
            o = jnp.dot(sc.astype(BF16), vc, preferred_element_type=F32)
            q_dec = (qc.astype(F32) * qd_ref[h]).astype(BF16)
            o = o + jnp.dot(q_dec, state.astype(BF16), preferred_element_type=F32)
            k_dec = (kc.astype(F32) * kd_ref[h]).astype(BF16)
            s_ref[0, h] = gc_ref[h] * state + lax.dot_general(k_dec, vc, (((0,), (0,)), ((), ())),
                                                              preferred_element_type=F32)
            mu = jnp.mean(o, axis=-1, keepdims=True)
            d = o - mu
            var = jnp.mean(d * d, axis=-1, keepdims=True)
            y = d * lax.rsqrt(var + LN_EPS) * gain_ref[:, cols]
            o_ref[pl.ds(r0, c), cols] = (y * _silu(g_ref[pl.ds(r0, c), cols].astype(F32))).astype(BF16)
        return carry

    lax.fori_loop(0, n_chunks, chunk, 0, unroll=4 if n_chunks % 4 == 0 else 1)


def _retention(qkr, vg, s0, tabs, gain, b, s, c):
    di, qd, kd, gc = tabs
    st = min(1024, s)
    nt = s // st
    seq = lambda col: pl.BlockSpec((st, QR_W), lambda bi, j: (bi * nt + j, col))
    full = lambda a: pl.BlockSpec(a.shape, lambda bi, j: (0,) * a.ndim)
    st_spec = pl.BlockSpec((1, N_HEADS_R, DK_R, DK_R), lambda bi, j: (bi, 0, 0, 0))
    return pl.pallas_call(
        functools.partial(_retention_kernel, n_chunks=st // c, c=c),
        grid=(b, nt),
        in_specs=[seq(0), seq(1), seq(0), seq(1), st_spec, full(di), full(qd), full(kd), full(gc), full(gain)],
        out_specs=[seq(0), st_spec],
        out_shape=[jax.ShapeDtypeStruct((b * s, QR_W), BF16), jax.ShapeDtypeStruct((b, N_HEADS_R, DK_R, DK_R), F32)],
        compiler_params=_cparams(("parallel", "arbitrary")),
        name="retention",
    )(qkr, qkr, vg, vg, s0, di, qd, kd, gc, gain)


def _mix_kernel(x_ref, oa_ref, or_ref, g_ref, wa_ref, wb_ref, wo_ref, lg_ref, lb_ref, wrt_ref, x1_ref, st_ref, xp_ref):
    a = jnp.dot(oa_ref[...], wa_ref[...], preferred_element_type=F32)
    b = jnp.dot(or_ref[...], wb_ref[...], preferred_element_type=F32)
    merged = g_ref[:, 0:D_MODEL].astype(F32) * a + g_ref[:, D_MODEL:].astype(F32) * b
    y = jnp.dot(merged.astype(BF16), wo_ref[...], preferred_element_type=F32)
    x1 = _layer_norm(DN_ALPHA * x_ref[...] + y, lg_ref[...], lb_ref[...])
    x1_ref[...] = x1
    _store_packed(xp_ref, x1)
    logits = lax.dot_general(wrt_ref[...], x1.astype(BF16), (((1,), (1,)), ((), ())), preferred_element_type=F32)
    st_ref[...] = jax.nn.sigmoid(logits)


def _mix(x2d, oa, orr, gm, wa, wb, wo, lg, lb, wrt, tm):
    t = x2d.shape[0]
    row = lambda w: pl.BlockSpec((tm, w), lambda i: (i, 0))
    full = lambda a: pl.BlockSpec(a.shape, lambda i: (0,) * a.ndim)
    return pl.pallas_call(
        _mix_kernel,
        grid=(t // tm,),
        in_specs=[row(D_MODEL), row(PQ_W), row(QR_W), row(2 * D_MODEL),
                  full(wa), full(wb), full(wo), full(lg), full(lb), full(wrt)],
        out_specs=[row(D_MODEL), pl.BlockSpec((N_EXPERTS, tm), lambda i: (0, i)),
                   pl.BlockSpec((tm * PACK_SUB, 128), lambda i: (i, 0))],
        out_shape=[jax.ShapeDtypeStruct((t, D_MODEL), F32), jax.ShapeDtypeStruct((N_EXPERTS, t), F32),
                   jax.ShapeDtypeStruct((t * PACK_SUB, 128), U32)],
        compiler_params=_cparams(("parallel",)),
        name="mix",
    )(x2d, oa, orr, gm, wa, wb, wo, lg, lb, wrt)


def _route_kernel(s_ref, bias_ref, cnt0_ref, e_ref, w_ref, r_ref, cnt_ref, carry_ref):
    @pl.when(pl.program_id(0) == 0)
    def _():
        carry_ref[...] = cnt0_ref[...].astype(F32)

    s = s_ref[...]
    tl = s.shape[1]
    choice = s + bias_ref[...]
    row = lax.broadcasted_iota(I32, (N_EXPERTS, tl), 0)
    row_g = lax.broadcasted_iota(I32, (GROUP_SIZE, tl), 0)
    neg = -jnp.inf

    scores = []
    for g in range(N_GROUPS):
        blk = choice[g * GROUP_SIZE:(g + 1) * GROUP_SIZE]
        m1 = jnp.max(blk, axis=0, keepdims=True)
        i1 = jnp.min(jnp.where(blk == m1, row_g, GROUP_SIZE), axis=0, keepdims=True)
        m2 = jnp.max(jnp.where(row_g == i1, neg, blk), axis=0, keepdims=True)
        scores.append(m1 + m2)
    sc = jnp.concatenate(scores, axis=0)
    gi = lax.broadcasted_iota(I32, sc.shape, 0)
    grank = jnp.zeros(sc.shape, I32)
    for g in range(N_GROUPS):
        other = sc[g:g + 1]
        ahead = jnp.where(other > sc, 1, jnp.where(other == sc, jnp.where(gi > g, 1, 0), 0))
        grank = grank + ahead
    cm = jnp.concatenate(
        [jnp.where(grank[g:g + 1] < TOPK_GROUPS, choice[g * GROUP_SIZE:(g + 1) * GROUP_SIZE], neg)
         for g in range(N_GROUPS)], axis=0)

    experts, weights = [], []
    for _ in range(TOP_K):
        m = jnp.max(cm, axis=0, keepdims=True)
        idx = jnp.min(jnp.where(cm == m, row, N_EXPERTS), axis=0, keepdims=True)
        hit = row == idx
        weights.append(jnp.sum(jnp.where(hit, s, 0.0), axis=0, keepdims=True))
        cm = jnp.where(hit, neg, cm)
        experts.append(idx)
    e8 = jnp.concatenate(experts, axis=0)
    w8 = jnp.concatenate(weights, axis=0)
    e_ref[...] = e8
    w_ref[...] = w8 / jnp.sum(w8, axis=0, keepdims=True) * ROUTED_SCALE

    before = (lax.broadcasted_iota(I32, (tl, tl), 0) < lax.broadcasted_iota(I32, (tl, tl), 1))
    before = jnp.where(before, 1.0, 0.0).astype(BF16)
    carry = carry_ref[...]
    ranks = []
    for k in range(TOP_K):
        hit = row == experts[k]
        onehot = jnp.where(hit, 1.0, 0.0)
        prefix = jnp.dot(onehot.astype(BF16), before, preferred_element_type=F32)
        ranks.append(jnp.sum(jnp.where(hit, prefix + carry, 0.0), axis=0, keepdims=True))
        carry = carry + jnp.sum(onehot, axis=1, keepdims=True)
    r_ref[...] = jnp.concatenate(ranks, axis=0).astype(I32)
    carry_ref[...] = carry
    cnt_ref[...] = carry.astype(I32)


def _route(st, bias_col, counts0, tl):
    t = st.shape[1]
    tok = pl.BlockSpec((TOP_K, tl), lambda i: (0, i))
    return pl.pallas_call(
        _route_kernel,
        grid=(t // tl,),
        in_specs=[pl.BlockSpec((N_EXPERTS, tl), lambda i: (0, i)), pl.BlockSpec((N_EXPERTS, 1), lambda i: (0, 0)),
                  pl.BlockSpec((N_EXPERTS, 1), lambda i: (0, 0))],
        out_specs=[tok, tok, tok, pl.BlockSpec((N_EXPERTS, 1), lambda i: (0, 0))],
        out_shape=[jax.ShapeDtypeStruct((TOP_K, t), I32), jax.ShapeDtypeStruct((TOP_K, t), F32),
                   jax.ShapeDtypeStruct((TOP_K, t), I32), jax.ShapeDtypeStruct((N_EXPERTS, 1), I32)],
        scratch_shapes=[pltpu.VMEM((N_EXPERTS, 1), F32)],
        compiler_params=_cparams(("arbitrary",)),
        name="route",
    )(st, bias_col, counts0)


def _dest_kernel(e_ref, r_ref, ps_ref, d_ref):
    tl = e_ref.shape[1]
    row = lax.broadcasted_iota(I32, (N_EXPERTS, tl), 0)
    ps = ps_ref[...]
    starts = [jnp.sum(jnp.where(row == e_ref[k:k + 1, :], ps, 0.0), axis=0, keepdims=True) for k in range(TOP_K)]
    d_ref[...] = jnp.concatenate(starts, axis=0).astype(I32) + r_ref[...]


def _dest(e8, r8, pstart_col, tl):
    t = e8.shape[1]
    tok = pl.BlockSpec((TOP_K, tl), lambda i: (0, i))
    return pl.pallas_call(
        _dest_kernel,
        grid=(t // tl,),
        in_specs=[tok, tok, pl.BlockSpec((N_EXPERTS, 1), lambda i: (0, 0))],
        out_specs=tok,
        out_shape=jax.ShapeDtypeStruct((TOP_K, t), I32),
        compiler_params=_cparams(("parallel",)),
        name="dest",
    )(e8, r8, pstart_col)


TOKEN_GROUP = 8


def _row_copy(src, src_tok, dst, dst_tok, sem):
    s0 = pl.multiple_of(src_tok * PACK_SUB, PACK_SUB)
    d0 = pl.multiple_of(dst_tok * PACK_SUB, PACK_SUB)
    return pltpu.make_async_copy(src.at[pl.ds(s0, PACK_SUB)], dst.at[pl.ds(d0, PACK_SUB)], sem)


def _wait_bytes_of(ref, sem):
    pltpu.make_async_copy(ref, ref, sem).wait()


def _dispatch_kernel(d_ref, x_ref, wsg_ref, wsu_ref, wsd_ref, *refs, td):
    xs_hbm, sh_ref, sem = refs[-3:]

    def start(g, carry):
        for j in range(TOKEN_GROUP):
            t = g * TOKEN_GROUP + j
            for k in range(TOP_K):
                _row_copy(x_ref, t, xs_hbm, d_ref[t * TOP_K + k], sem).start(priority=k % 2)
        return carry

    lax.fori_loop(0, td // TOKEN_GROUP, start, 0)

    hi, lo = _load_packed(x_ref, td)
    xb = jnp.concatenate([hi.astype(BF16), lo.astype(BF16)], axis=1)
    hs = _silu(jnp.dot(xb, wsg_ref[...], preferred_element_type=F32)) * jnp.dot(xb, wsu_ref[...],
                                                                               preferred_element_type=F32)
    sh_ref[...] = jnp.dot(hs.astype(BF16), wsd_ref[...], preferred_element_type=F32)

    for k in range(TOP_K):
        _wait_bytes_of(x_ref, sem)


def _dispatch(d8, xp, wsg, wsu, wsd, n_rows, td, xs_prev=None):
    t = d8.shape[0] // TOP_K
    full = lambda a: pl.BlockSpec(a.shape, lambda i: (0,) * a.ndim)
    in_specs = [pl.BlockSpec((td * TOP_K,), lambda i: (i,), memory_space=pltpu.SMEM),
                pl.BlockSpec((td * PACK_SUB, 128), lambda i: (i, 0)), full(wsg), full(wsu), full(wsd)]
    args = [d8, xp, wsg, wsu, wsd]
    if xs_prev is not None:
        in_specs.append(pl.BlockSpec(memory_space=pl.ANY))
        args.append(xs_prev)
    return pl.pallas_call(
        functools.partial(_dispatch_kernel, td=td),
        grid=(t // td,),
        in_specs=in_specs,
        out_specs=[pl.BlockSpec(memory_space=pl.ANY), pl.BlockSpec((td, D_MODEL), lambda i: (i, 0))],
        scratch_shapes=[pltpu.SemaphoreType.DMA(())],
        out_shape=[jax.ShapeDtypeStruct((n_rows * PACK_SUB, 128), U32), jax.ShapeDtypeStruct((t, D_MODEL), F32)],
        input_output_aliases={} if xs_prev is None else {5: 0},
        compiler_params=_cparams(("arbitrary",)),
        name="dispatch",
    )(*args)


X_RING = 3


def _moe_kernel(be_ref, nv_ref, x_hbm, wg_ref, wu_ref, wd_ref, y_ref, x_ring, x_sem, wg_bf, wu_bf, wd_bf):
    j = pl.program_id(0)
    n = pl.num_programs(0)
    nv = nv_ref[j]
    blk_rows = MOE_ROWS * PACK_SUB

    def x_copy(blk):
        r0 = pl.multiple_of(blk * blk_rows, blk_rows)
        slot = blk % X_RING
        return pltpu.make_async_copy(x_hbm.at[pl.ds(r0, blk_rows)], x_ring.at[slot], x_sem.at[slot])

    def start_if_real(blk):
        @pl.when((blk < n) & (nv_ref[jnp.minimum(blk, n - 1)] > 0))
        def _():
            x_copy(blk).start()

    @pl.when(j == 0)
    def _():
        for blk in range(X_RING - 1):
            start_if_real(jnp.int32(blk))

    start_if_real(j + X_RING - 1)

    @pl.when((j == 0) | (be_ref[j] != be_ref[jnp.maximum(j - 1, 0)]))
    def _():
        wg_bf[...] = wg_ref[0].astype(BF16)
        wu_bf[...] = wu_ref[0].astype(BF16)
        wd_bf[...] = wd_ref[0].astype(BF16)

    @pl.when(nv > 0)
    def _():
        x_copy(j).wait()
        hi, lo = _load_packed(x_ring.at[j % X_RING], MOE_ROWS)
        valid = lax.broadcasted_iota(I32, hi.shape, 0) < nv
        xb = jnp.concatenate([jnp.where(valid, hi, 0.0).astype(BF16), jnp.where(valid, lo, 0.0).astype(BF16)], axis=1)
        g = jnp.dot(xb, wg_bf[...], preferred_element_type=F32)
        u = jnp.dot(xb, wu_bf[...], preferred_element_type=F32)
        h = (_silu(g) * u).astype(BF16)
        _store_packed(y_ref, jnp.dot(h, wd_bf[...], preferred_element_type=F32))

    @pl.when(nv == 0)
    def _():
        y_ref[...] = jnp.zeros_like(y_ref)


def _moe(blk_exp, blk_valid, xs, wg, wu, wd):
    n_blocks = blk_exp.shape[0]
    rows = pl.BlockSpec((MOE_ROWS * PACK_SUB, 128), lambda j, be, nv: (j, 0))
    return pl.pallas_call(
        _moe_kernel,
        grid_spec=pltpu.PrefetchScalarGridSpec(
            num_scalar_prefetch=2,
            grid=(n_blocks,),
            in_specs=[pl.BlockSpec(memory_space=pl.ANY),
                      pl.BlockSpec((1, D_MODEL, D_EXPERT), lambda j, be, nv: (be[j], 0, 0)),
                      pl.BlockSpec((1, D_MODEL, D_EXPERT), lambda j, be, nv: (be[j], 0, 0)),
                      pl.BlockSpec((1, D_EXPERT, D_MODEL), lambda j, be, nv: (be[j], 0, 0))],
            out_specs=rows,
            scratch_shapes=[pltpu.VMEM((X_RING, MOE_ROWS * PACK_SUB, 128), U32), pltpu.SemaphoreType.DMA((X_RING,)),
                            pltpu.VMEM((D_MODEL, D_EXPERT), BF16), pltpu.VMEM((D_MODEL, D_EXPERT), BF16),
                            pltpu.VMEM((D_EXPERT, D_MODEL), BF16)],
        ),
        out_shape=jax.ShapeDtypeStruct(xs.shape, U32),
        compiler_params=_cparams(("arbitrary",)),
        name="moe",
    )(blk_exp, blk_valid, xs, wg, wu, wd)


COMBINE_ROWS = 32


def _gather_rows(ys_hbm, d_ref, slot_buf, sem, t0, n):
    for j in range(n):
        t = t0 + j
        for k in range(TOP_K):
            _row_copy(ys_hbm, d_ref[t * TOP_K + k], slot_buf.at[k], t, sem).start(priority=k % 2)


def _final_kernel(d_ref, dnext_ref, x1_ref, w_ref, p_ref, sh_ref, lg_ref, lb_ref,
                  wpp_ref, wpg_ref, ys_hbm, out_ref, gbuf, ybuf, sem, *, tf):
    i = pl.program_id(0)
    slot = i % 2

    @pl.when(i == 0)
    def _():
        def first(g, carry):
            _gather_rows(ys_hbm, d_ref, gbuf.at[0], sem.at[0], g * TOKEN_GROUP, TOKEN_GROUP)
            return carry

        lax.fori_loop(0, tf // TOKEN_GROUP, first, 0)

    _wait_bytes_of(gbuf.at[slot], sem.at[slot])

    def combine(sb, carry):
        r0 = pl.multiple_of(sb * COMBINE_ROWS, COMBINE_ROWS)
        _gather_rows(ys_hbm, dnext_ref, gbuf.at[1 - slot], sem.at[1 - slot], r0, COMBINE_ROWS)
        y_hi = jnp.zeros((COMBINE_ROWS, PACK_W), F32)
        y_lo = jnp.zeros((COMBINE_ROWS, PACK_W), F32)
        for k in range(TOP_K):
            rows = gbuf.at[slot, k, pl.ds(pl.multiple_of(r0 * PACK_SUB, COMBINE_ROWS * PACK_SUB),
                                          COMBINE_ROWS * PACK_SUB)]
            hi, lo = _load_packed(rows, COMBINE_ROWS)
            wk = w_ref[pl.ds(r0, COMBINE_ROWS), k:k + 1]
            y_hi = y_hi + wk * hi
            y_lo = y_lo + wk * lo
        ybuf[pl.ds(r0, COMBINE_ROWS), 0:PACK_W] = y_hi
        ybuf[pl.ds(r0, COMBINE_ROWS), PACK_W:] = y_lo
        return carry

    lax.fori_loop(0, tf // COMBINE_ROWS, combine, 0)

    @pl.when(i == pl.num_programs(0) - 1)
    def _():
        _wait_bytes_of(gbuf.at[1 - slot], sem.at[1 - slot])

    y = sh_ref[...] + ybuf[...]
    x2 = _layer_norm(DN_ALPHA * x1_ref[...] + y, lg_ref[...], lb_ref[...])
    gate = jax.nn.sigmoid(jnp.dot(x2.astype(BF16), wpg_ref[...], preferred_element_type=F32))
    out_ref[...] = x2 + gate * jnp.dot(p_ref[...].astype(BF16), wpp_ref[...], preferred_element_type=F32)


def _final(d8, x1, w_tok, p2d, sh, lg, lb, wpp, wpg, ys, tf):
    t = x1.shape[0]
    n_tiles = t // tf
    row = lambda w: pl.BlockSpec((tf, w), lambda i: (i, 0))
    full = lambda a: pl.BlockSpec(a.shape, lambda i: (0,) * a.ndim)
    return pl.pallas_call(
        functools.partial(_final_kernel, tf=tf),
        grid=(n_tiles,),
        in_specs=[pl.BlockSpec((tf * TOP_K,), lambda i: (i,), memory_space=pltpu.SMEM),
                  pl.BlockSpec((tf * TOP_K,), lambda i: (jnp.minimum(i + 1, n_tiles - 1),), memory_space=pltpu.SMEM),
                  row(D_MODEL), row(TOP_K), row(D_PLE), row(D_MODEL),
                  full(lg), full(lb), full(wpp), full(wpg),
                  pl.BlockSpec(memory_space=pl.ANY)],
        out_specs=row(D_MODEL),
        scratch_shapes=[pltpu.VMEM((2, TOP_K, tf * PACK_SUB, 128), U32), pltpu.VMEM((tf, D_MODEL), F32),
                        pltpu.SemaphoreType.DMA((2,))],
        out_shape=jax.ShapeDtypeStruct((t, D_MODEL), F32),
        compiler_params=_cparams(("arbitrary",)),
        name="final",
    )(d8, d8, x1, w_tok, p2d, sh, lg, lb, wpp, wpg, ys)


def _t5_bucket(rel):
    nb = N_BUCKETS // 2
    max_exact = nb // 2
    ret = jnp.where(rel > 0, nb, 0)
    n = jnp.abs(rel)
    nf = jnp.maximum(n, 1).astype(F32)
    large = max_exact + (jnp.log(nf / max_exact) / math.log(MAX_DIST / max_exact) * (nb - max_exact)).astype(I32)
    large = jnp.minimum(large, nb - 1)
    return ret + jnp.where(n < max_exact, n, large)


def _rel_bias(table, q_len, k_len):
    rel = (jnp.arange(k_len, dtype=I32)[None, :] - WINDOW) - jnp.arange(q_len, dtype=I32)[:, None]
    hit = _t5_bucket(rel)[:, :, None, None] == jnp.arange(N_BUCKETS, dtype=I32)[None, None, :, None]
    b = jnp.sum(jnp.where(hit, table.astype(F32)[None, None], 0.0), axis=2)
    return jnp.transpose(b, (2, 0, 1)).reshape(N_KV_A, GROUP_A * q_len, k_len).astype(F32)


def _rotary_tables(pos):
    half = DK_R // 2
    inv = ROPE_BASE ** (-jnp.arange(half, dtype=F32) / half)
    ang = pos.astype(F32)[:, None] * inv[None, :]
    cos, sin = jnp.cos(ang), jnp.sin(ang)
    return jnp.concatenate([cos, cos], -1), jnp.concatenate([-sin, sin], -1)


def _decay_tables(c):
    log_gamma = jnp.log1p(-jnp.exp2(-5.0 - jnp.arange(N_HEADS_R, dtype=F32)))
    idx = jnp.arange(c, dtype=F32)
    di = jnp.exp(jnp.abs(idx[:, None] - idx[None, :])[None] * log_gamma[:, None, None])
    qd = jnp.exp((idx[None, :] + 1.0) * log_gamma[:, None])
    kd = jnp.exp((c - 1.0 - idx[None, :]) * log_gamma[:, None])
    gc = jnp.exp(c * log_gamma)
    bc = lambda a: jnp.broadcast_to(a[:, :, None], (N_HEADS_R, c, DK_R))
    return di, bc(qd), bc(kd), jnp.broadcast_to(gc[:, None, None], (N_HEADS_R, DK_R, DK_R))


def _pad_heads(w, n_heads):
    rows = w.shape[0]
    w = w.reshape(rows, n_heads, HEAD_DIM_A)
    w = jnp.pad(w, ((0, 0), (0, 0), (0, HEAD_PAD - HEAD_DIM_A)))
    return w.reshape(rows, n_heads * HEAD_PAD)


def _moe_ffn(groups, wts):
    tiles = [min(TOKEN_TILE, g[0].shape[0]) for g in groups]
    counts = jnp.zeros((N_EXPERTS, 1), I32)
    routed = []
    for (x1, xp, st, p2d), tile in zip(groups, tiles):
        e8, w8, r8, counts = _route(st, wts['router_bias'], counts, tile)
        routed.append((e8, w8, r8))
    counts = counts[:, 0]
    padded = (counts + MOE_ROWS - 1) // MOE_ROWS * MOE_ROWS
    pad_end = jnp.cumsum(padded)
    pstart = (pad_end - padded).astype(I32)
    n_rows = sum(g[0].shape[0] for g in groups) * TOP_K + N_EXPERTS * MOE_ROWS
    blk_start = jnp.arange(n_rows // MOE_ROWS, dtype=I32) * MOE_ROWS
    blk_exp = jnp.minimum(jnp.sum(blk_start[:, None] >= pad_end[None, :], axis=1), N_EXPERTS - 1).astype(I32)
    own = blk_exp[:, None] == jnp.arange(N_EXPERTS, dtype=I32)[None, :]
    blk_end = jnp.sum(jnp.where(own, (pstart + counts)[None, :], 0), axis=1)
    blk_valid = jnp.clip(blk_end - blk_start, 0, MOE_ROWS).astype(I32)
    pstart_col = pstart.astype(F32).reshape(N_EXPERTS, 1)
    dests, shared, xs = [], [], None
    for (x1, xp, st, p2d), (e8, w8, r8), tile in zip(groups, routed, tiles):
        d8 = _dest(e8, r8, pstart_col, tile).T.reshape(-1)
        xs, sh = _dispatch(d8, xp, wts['w_sh_gate'], wts['w_sh_up'], wts['w_sh_down'], n_rows, tile, xs)
        dests.append(d8)
        shared.append(sh)
    ys = _moe(blk_exp, blk_valid, xs, wts['w_exp_gate'], wts['w_exp_up'], wts['w_exp_down'])
    return [_final(d8, x1, w8.T, p2d, sh, wts['ln2_g'], wts['ln2_b'], wts['w_ple_proj'], wts['w_ple_gate'], ys, tile)
            for (x1, xp, st, p2d), (e8, w8, r8), d8, sh, tile in zip(groups, routed, dests, shared, tiles)]


def _mix_group(x2d, oa, orr, gm, p2d, wts):
    x1, st, xp = _mix(x2d, oa, orr, gm, wts['w_branch_attn'], wts['w_branch_ret'], wts['w_out'],
                      wts['ln1_g'], wts['ln1_b'], wts['w_router_t'], min(512, x2d.shape[0]))
    return x1, xp, st, p2d


def kernel(x_prompt, x_sample, cache_attn_k, cache_attn_v, state_retention, p_prompt, p_sample,
           w_in, attn_sinks, rel_bias_table, ret_gn_gain, w_branch_attn, w_branch_ret, w_out,
           ln1_g, ln1_b, w_router, router_bias, w_exp_gate, w_exp_up, w_exp_down,
           w_sh_gate, w_sh_up, w_sh_down, ln2_g, ln2_b, w_ple_proj, w_ple_gate):
    b, s, _ = x_prompt.shape
    bd, l, _ = x_sample.shape
    i = 0
    row = lambda a: a.reshape(1, -1).astype(F32)
    wts = dict(
        w_branch_attn=_pad_heads(w_branch_attn[i].astype(BF16).T, N_HEADS_A).T,
        w_branch_ret=w_branch_ret[i].astype(BF16),
        w_out=w_out[i].astype(BF16), ln1_g=row(ln1_g[i]), ln1_b=row(ln1_b[i]),
        w_router_t=w_router[i].T.astype(BF16), router_bias=router_bias[i].reshape(N_EXPERTS, 1).astype(F32),
        w_exp_gate=w_exp_gate[i], w_exp_up=w_exp_up[i], w_exp_down=w_exp_down[i],
        w_sh_gate=w_sh_gate[i].astype(BF16), w_sh_up=w_sh_up[i].astype(BF16), w_sh_down=w_sh_down[i].astype(BF16),
        ln2_g=row(ln2_g[i]), ln2_b=row(ln2_b[i]),
        w_ple_proj=w_ple_proj[i].astype(BF16), w_ple_gate=w_ple_gate[i].astype(BF16))
    w_in_bf = w_in[i].astype(BF16)
    w_in_bf = jnp.concatenate([_pad_heads(w_in_bf[:, :QA_W] * (HEAD_DIM_A ** -0.5), N_HEADS_A),
                               _pad_heads(w_in_bf[:, OFF_KV:OFF_QR], 2 * N_KV_A), w_in_bf[:, OFF_KV:]], axis=1)
    gain = row(ret_gn_gain[i])
    sinks = attn_sinks[i].astype(F32).reshape(N_KV_A, GROUP_A)

    xp = x_prompt.reshape(b * s, D_MODEL)
    cos_p, sin_p = _rotary_tables(jnp.arange(s, dtype=I32))
    qkv, kvf, qkr, vg, gm = _inproj(xp, w_in_bf, cos_p, sin_p, min(512, s))
    bias_p = jnp.pad(_rel_bias(rel_bias_table, CHUNK, WINDOW + CHUNK), ((0, 0), (0, 0), (0, KEY_PAD - WINDOW - CHUNK)))
    sink_p = jnp.repeat(sinks, CHUNK, axis=1)[..., None]
    fill_p = jnp.where(jnp.arange(KEY_PAD)[None, None, :] == WINDOW + CHUNK, sink_p, NEG_INF).astype(F32)
    oa = _attn_prompt(qkv, bias_p, fill_p, b, s)
    s0 = jnp.zeros((b, N_HEADS_R, DK_R, DK_R), F32)
    orr, ns_p = _retention(qkr, vg, s0, _decay_tables(CHUNK), gain, b, s, CHUNK)
    group_p = _mix_group(xp, oa, orr, gm, p_prompt[i].reshape(b * s, D_PLE), wts)
    kv_tail = kvf.reshape(b, s, 2, N_KV_A, HEAD_DIM_A)[:, s - WINDOW:]
    nk_p, nv_p = kv_tail[:, :, 0], kv_tail[:, :, 1]

    xs = x_sample.reshape(bd * l, D_MODEL)
    cos_s, sin_s = _rotary_tables(jnp.tile(PAST_LEN + jnp.arange(l, dtype=I32), bd))
    qkv, kvf, qkr, vg, gm = _inproj(xs, w_in_bf, cos_s, sin_s, bd * l)
    sink_s = jnp.repeat(sinks, l, axis=1)[..., None]
    oa, nk_s, nv_s = _attn_sample(qkv, kvf, cache_attn_k[i].reshape(bd, WINDOW, KA_W),
                                  cache_attn_v[i].reshape(bd, WINDOW, KA_W),
                                  _rel_bias(rel_bias_table, l, WINDOW + l), sink_s, bd, l)
    orr, ns_s = _retention(qkr, vg, state_retention[i].astype(F32), _decay_tables(l), gain, bd, l, l)
    group_s = _mix_group(xs, oa, orr, gm, p_sample[i].reshape(bd * l, D_PLE), wts)
    y_p, y_s = _moe_ffn([group_p, group_s], wts)
    y_p, y_s = y_p.reshape(b, s, D_MODEL), y_s.reshape(bd, l, D_MODEL)
    shape_kv = (bd, WINDOW, N_KV_A, HEAD_DIM_A)
    return (y_p, y_s, nk_p[None], nv_p[None], ns_p[None],
            nk_s.reshape(shape_kv)[None], nv_s.reshape(shape_kv)[None], ns_s[None])
```

```python
import functools
import math

import jax
import jax.numpy as jnp
from jax import lax
from jax.experimental import pallas as pl
from jax.experimental.pallas import tpu as pltpu

F32 = jnp.float32
BF16 = jnp.bfloat16
I32 = jnp.int32

D_MODEL = 1024
CHUNK = 64
WINDOW = 128
N_HEADS_A = 8
N_KV_A = 2
GROUP_A = 4
HEAD_DIM_A = 64
N_BUCKETS = 32
MAX_DIST = 128
N_HEADS_R = 4
DK_R = 128
ROPE_BASE = 10000.0
N_EXPERTS = 256
TOP_K = 8
N_GROUPS = 8
GROUP_SIZE = N_EXPERTS // N_GROUPS
TOPK_GROUPS = 4
D_EXPERT = 256
ROUTED_SCALE = 2.5
D_PLE = 256
LN_EPS = 1e-5
NEG_INF = -1e30
PAST_LEN = 2048
DEPTH = 1
DN_ALPHA = (2 * DEPTH) ** 0.25

QA_W = N_HEADS_A * HEAD_DIM_A
KA_W = N_KV_A * HEAD_DIM_A
QR_W = N_HEADS_R * DK_R
OFF_KV = QA_W
OFF_QR = QA_W + 2 * KA_W
OFF_VR = OFF_QR + 2 * QR_W
OFF_GM = OFF_VR + 2 * QR_W
IN_W = OFF_GM + 2 * D_MODEL

HEAD_PAD = 128
PQ_W = N_HEADS_A * HEAD_PAD
PKV_W = 2 * N_KV_A * HEAD_PAD
A_KVF = PQ_W + PKV_W
A_QR = A_KVF + 2 * KA_W
A_VR = A_QR + 2 * QR_W
A_GM = A_VR + 2 * QR_W
A_W = A_GM + 2 * D_MODEL

MOE_ROWS = 512
TOKEN_TILE = 512
COMBINE_TILE = 256
VMEM_LIMIT = 56 * 1024 * 1024


def _cparams(sem, vmem=VMEM_LIMIT):
    return pltpu.CompilerParams(dimension_semantics=sem, vmem_limit_bytes=vmem)


def _layer_norm(h, g, b):
    mu = jnp.mean(h, axis=-1, keepdims=True)
    d = h - mu
    var = jnp.mean(d * d, axis=-1, keepdims=True)
    return d * lax.rsqrt(var + LN_EPS) * g + b


def _silu(x):
    return x * jax.nn.sigmoid(x)


U32 = jnp.uint32
PACK_W = D_MODEL // 2
PACK_SUB = PACK_W // 128


def _bf16_bits(x):
    return pltpu.bitcast(x.astype(BF16).astype(F32), U32)


def _store_packed(ref, x):
    n = x.shape[0]
    words = _bf16_bits(x[:, :PACK_W]) | (_bf16_bits(x[:, PACK_W:]) >> 16)
    for c in range(PACK_SUB):
        ref[pl.ds(c, n, stride=PACK_SUB), :] = words[:, c * 128:(c + 1) * 128]


def _load_packed(ref, n):
    words = jnp.concatenate([ref[pl.ds(c, n, stride=PACK_SUB), :] for c in range(PACK_SUB)], axis=1)
    hi = pltpu.bitcast(words & jnp.uint32(0xFFFF0000), F32)
    lo = pltpu.bitcast(words << 16, F32)
    return hi, lo


def _inproj_kernel(x_ref, w_ref, cos_ref, sin_ref, qkv_ref, kvf_ref, qkr_ref, vg_ref, gm_ref):
    xb = x_ref[...].astype(BF16)

    def mm(lo, hi):
        return jnp.dot(xb, w_ref[:, lo:hi], preferred_element_type=F32)

    qkv_ref[:, 0:PQ_W] = mm(0, PQ_W).astype(BF16)
    qkv_ref[:, PQ_W:A_KVF] = mm(PQ_W, A_KVF).astype(BF16)
    kvf_ref[...] = mm(A_KVF, A_QR)
    cos = cos_ref[...]
    sin = sin_ref[...]
    for part in range(2):
        z = mm(A_QR + part * QR_W, A_QR + (part + 1) * QR_W)
        for h in range(N_HEADS_R):
            zh = z[:, h * DK_R:(h + 1) * DK_R]
            r = zh * cos + pltpu.roll(zh, DK_R // 2, axis=1) * sin
            if part == 1:
                r = r * (DK_R ** -0.5)
            c0 = part * QR_W + h * DK_R
            qkr_ref[:, c0:c0 + DK_R] = r.astype(BF16)
    vg_ref[...] = mm(A_VR, A_GM).astype(BF16)
    for half in range(2):
        lo = A_GM + half * D_MODEL
        gm_ref[:, half * D_MODEL:(half + 1) * D_MODEL] = jax.nn.sigmoid(mm(lo, lo + D_MODEL)).astype(BF16)


def _inproj(x2d, w_bf, cos_tab, sin_tab, tm):
    t = x2d.shape[0]
    nper = cos_tab.shape[0] // tm
    row = lambda i: (i, 0)
    return pl.pallas_call(
        _inproj_kernel,
        grid=(t // tm,),
        in_specs=[
            pl.BlockSpec((tm, D_MODEL), row),
            pl.BlockSpec((D_MODEL, A_W), lambda i: (0, 0)),
            pl.BlockSpec((tm, DK_R), lambda i: (i % nper, 0)),
            pl.BlockSpec((tm, DK_R), lambda i: (i % nper, 0)),
        ],
        out_specs=[
            pl.BlockSpec((tm, A_KVF), row),
            pl.BlockSpec((tm, 2 * KA_W), row),
            pl.BlockSpec((tm, 2 * QR_W), row),
            pl.BlockSpec((tm, 2 * QR_W), row),
            pl.BlockSpec((tm, 2 * D_MODEL), row),
        ],
        out_shape=[
            jax.ShapeDtypeStruct((t, A_KVF), BF16),
            jax.ShapeDtypeStruct((t, 2 * KA_W), F32),
            jax.ShapeDtypeStruct((t, 2 * QR_W), BF16),
            jax.ShapeDtypeStruct((t, 2 * QR_W), BF16),
            jax.ShapeDtypeStruct((t, 2 * D_MODEL), BF16),
        ],
        compiler_params=_cparams(("parallel",)),
        name="inproj",
    )(x2d, w_bf, cos_tab, sin_tab)


def _attend(q4, k, v, bias, sink):
    s = lax.dot_general(q4, k, (((1,), (1,)), ((), ())), preferred_element_type=F32) + bias
    m = jnp.maximum(jnp.max(s, axis=-1, keepdims=True), sink)
    e = jnp.exp(s - m)
    p = e / (jnp.sum(e, axis=-1, keepdims=True) + jnp.exp(sink - m))
    return jnp.dot(p.astype(BF16), v, preferred_element_type=F32)


KEY_PAD = 256


def _attend_sink_column(q4, k, v, bias, fill, valid):
    s = lax.dot_general(q4, k, (((1,), (1,)), ((), ())), preferred_element_type=F32)
    s = jnp.where(valid, s + bias, fill)
    e = jnp.exp(s - jnp.max(s, axis=-1, keepdims=True)).astype(BF16)
    dv = v.shape[1]
    ones = jnp.ones((KEY_PAD, dv), BF16)
    v_rows = jnp.concatenate([v, jnp.zeros((KEY_PAD - WINDOW - CHUNK, dv), BF16)], axis=0)
    both = jnp.dot(e, jnp.concatenate([v_rows, ones], axis=1), preferred_element_type=F32)
    return both[:, 0:dv] / both[:, dv:]


def _attn_prompt_kernel(q_ref, kvc_ref, kvp_ref, bias_ref, fill_ref, o_ref, kv_buf, *, n_chunks):
    i = pl.program_id(1)
    qb = n_chunks * CHUNK
    kv_buf[0:WINDOW, :] = kvp_ref[...]
    kv_buf[WINDOW:WINDOW + qb, :] = kvc_ref[...]
    kv_buf[WINDOW + qb:, :] = jnp.zeros((KEY_PAD - WINDOW - CHUNK, PKV_W), BF16)
    col = lax.broadcasted_iota(I32, (GROUP_A * CHUNK, KEY_PAD), 1)
    key_col = jnp.where(col < WINDOW + CHUNK, col, -1)

    def chunk(j, carry):
        r0 = pl.multiple_of(j * CHUNK, CHUNK)
        valid = key_col >= jnp.maximum(WINDOW - (i * n_chunks + j) * CHUNK, 0)
        for kv in range(N_KV_A):
            q4 = jnp.concatenate(
                [q_ref[pl.ds(r0, CHUNK), (kv * GROUP_A + g) * HEAD_PAD:(kv * GROUP_A + g + 1) * HEAD_PAD]
                 for g in range(GROUP_A)], axis=0)
            k = kv_buf[pl.ds(r0, KEY_PAD), kv * HEAD_PAD:(kv + 1) * HEAD_PAD]
            v = kv_buf[pl.ds(r0, WINDOW + CHUNK), (N_KV_A + kv) * HEAD_PAD:(N_KV_A + kv + 1) * HEAD_PAD]
            o4 = _attend_sink_column(q4, k, v, bias_ref[kv], fill_ref[kv], valid)
            for g in range(GROUP_A):
                c0 = (kv * GROUP_A + g) * HEAD_PAD
                o_ref[pl.ds(r0, CHUNK), c0:c0 + HEAD_PAD] = o4[g * CHUNK:(g + 1) * CHUNK].astype(BF16)
        return carry

    lax.fori_loop(0, n_chunks, chunk, 0, unroll=8 if n_chunks % 8 == 0 else 1)


def _attn_prompt(qkv, bias, fill, b, s):
    qb = min(512, s)
    n_chunks = qb // CHUNK
    nq = s // qb
    per = qb // WINDOW
    kv_col = PQ_W // PKV_W
    return pl.pallas_call(
        functools.partial(_attn_prompt_kernel, n_chunks=n_chunks),
        grid=(b, nq),
        in_specs=[
            pl.BlockSpec((qb, PQ_W), lambda bi, i: (bi * nq + i, 0)),
            pl.BlockSpec((qb, PKV_W), lambda bi, i: (bi * nq + i, kv_col)),
            pl.BlockSpec((WINDOW, PKV_W), lambda bi, i: (jnp.maximum((bi * nq + i) * per - 1, 0), kv_col)),
            pl.BlockSpec((N_KV_A, GROUP_A * CHUNK, KEY_PAD), lambda bi, i: (0, 0, 0)),
            pl.BlockSpec((N_KV_A, GROUP_A * CHUNK, KEY_PAD), lambda bi, i: (0, 0, 0)),
        ],
        out_specs=pl.BlockSpec((qb, PQ_W), lambda bi, i: (bi * nq + i, 0)),
        out_shape=jax.ShapeDtypeStruct((b * s, PQ_W), BF16),
        scratch_shapes=[pltpu.VMEM((qb + KEY_PAD - CHUNK, PKV_W), BF16)],
        compiler_params=_cparams(("parallel", "parallel")),
        name="attn_prompt",
    )(qkv, qkv, qkv, bias, fill)


def _attn_sample_kernel(q_ref, kvf_ref, ck_ref, cv_ref, bias_ref, sink_ref, o_ref, nk_ref, nv_ref, *, l):
    k_all = jnp.concatenate([ck_ref[0], kvf_ref[:, 0:KA_W]], axis=0)
    v_all = jnp.concatenate([cv_ref[0], kvf_ref[:, KA_W:2 * KA_W]], axis=0)
    nk_ref[0] = k_all[l:]
    nv_ref[0] = v_all[l:]
    kb = k_all.astype(BF16)
    vb = v_all.astype(BF16)
    o_ref[...] = jnp.zeros_like(o_ref)
    for kv in range(N_KV_A):
        q4 = jnp.concatenate(
            [q_ref[:, (kv * GROUP_A + g) * HEAD_PAD:(kv * GROUP_A + g) * HEAD_PAD + HEAD_DIM_A]
             for g in range(GROUP_A)], axis=0)
        o4 = _attend(q4, kb[:, kv * HEAD_DIM_A:(kv + 1) * HEAD_DIM_A], vb[:, kv * HEAD_DIM_A:(kv + 1) * HEAD_DIM_A],
                     bias_ref[kv], sink_ref[kv])
        for g in range(GROUP_A):
            c0 = (kv * GROUP_A + g) * HEAD_PAD
            o_ref[:, c0:c0 + HEAD_DIM_A] = o4[g * l:(g + 1) * l].astype(BF16)


def _attn_sample(qkv, kvf, cache_k, cache_v, bias, sink, bd, l):
    cache_spec = pl.BlockSpec((1, WINDOW, KA_W), lambda bi: (bi, 0, 0))
    cache_shape = jax.ShapeDtypeStruct((bd, WINDOW, KA_W), F32)
    return pl.pallas_call(
        functools.partial(_attn_sample_kernel, l=l),
        grid=(bd,),
        in_specs=[
            pl.BlockSpec((l, A_KVF), lambda bi: (bi, 0)),
            pl.BlockSpec((l, 2 * KA_W), lambda bi: (bi, 0)),
            cache_spec, cache_spec,
            pl.BlockSpec((N_KV_A, GROUP_A * l, WINDOW + l), lambda bi: (0, 0, 0)),
            pl.BlockSpec((N_KV_A, GROUP_A * l, 1), lambda bi: (0, 0, 0)),
        ],
        out_specs=[pl.BlockSpec((l, PQ_W), lambda bi: (bi, 0)), cache_spec, cache_spec],
        out_shape=[jax.ShapeDtypeStruct((bd * l, PQ_W), BF16), cache_shape, cache_shape],
        compiler_params=_cparams(("parallel",)),
        name="attn_sample",
    )(qkv, kvf, cache_k, cache_v, bias, sink)


def _retention_kernel(q_ref, k_ref, v_ref, g_ref, s0_ref, di_ref, qd_ref, kd_ref, gc_ref, gain_ref,
                      o_ref, s_ref, *, n_chunks, c):
    @pl.when(pl.program_id(1) == 0)
    def _():
        s_ref[...] = s0_ref[...]

    def chunk(n, carry):
        r0 = pl.multiple_of(n * c, c)
        for h in range(N_HEADS_R):
            cols = slice(h * DK_R, (h + 1) * DK_R)
            qc = q_ref[pl.ds(r0, c), cols]
            kc = k_ref[pl.ds(r0, c), cols]
            vc = v_ref[pl.ds(r0, c), cols]
            state = s_ref[0, h]
            sc = lax.dot_general(qc, kc, (((1,), (1,)), ((), ())), preferred_element_type=F32) * di_ref[h]
            o = jnp.dot(sc.astype(BF16), vc, preferred_element_type=F32)
            q_dec = (qc.astype(F32) * qd_ref[h]).astype(BF16)
            o = o + jnp.dot(q_dec, state.astype(BF16), preferred_element_type=F32)
            k_dec = (kc.astype(F32) * kd_ref[h]).astype(BF16)
            s_ref[0, h] = gc_ref[h] * state + lax.dot_general(k_dec, vc, (((0,), (0,)), ((), ())),
                                                              preferred_element_type=F32)
            mu = jnp.mean(o, axis=-1, keepdims=True)
            d = o - mu
            var = jnp.mean(d * d, axis=-1, keepdims=True)
            y = d * lax.rsqrt(var + LN_EPS) * gain_ref[:, cols]
            o_ref[pl.ds(r0, c), cols] = (y * _silu(g_ref[pl.ds(r0, c), cols].astype(F32))).astype(BF16)
        return carry

    lax.fori_loop(0, n_chunks, chunk, 0, unroll=4 if n_chunks % 4 == 0 else 1)


def _retention(qkr, vg, s0, tabs, gain, b, s, c):
    di, qd, kd, gc = tabs
    st = min(1024, s)
    nt = s // st
    seq = lambda col: pl.BlockSpec((st, QR_W), lambda bi, j: (bi * nt + j, col))
    full = lambda a: pl.BlockSpec(a.shape, lambda bi, j: (0,) * a.ndim)
    st_spec = pl.BlockSpec((1, N_HEADS_R, DK_R, DK_R), lambda bi, j: (bi, 0, 0, 0))
    return pl.pallas_call(
        functools.partial(_retention_kernel, n_chunks=st // c, c=c),
        grid=(b, nt),
        in_specs=[seq(0), seq(1), seq(0), seq(1), st_spec, full(di), full(qd), full(kd), full(gc), full(gain)],
        out_specs=[seq(0), st_spec],
        out_shape=[jax.ShapeDtypeStruct((b * s, QR_W), BF16), jax.ShapeDtypeStruct((b, N_HEADS_R, DK_R, DK_R), F32)],
        compiler_params=_cparams(("parallel", "arbitrary")),
        name="retention",
    )(qkr, qkr, vg, vg, s0, di, qd, kd, gc, gain)


def _mix_kernel(x_ref, oa_ref, or_ref, g_ref, wa_ref, wb_ref, wo_ref, lg_ref, lb_ref, wrt_ref, x1_ref, st_ref, xp_ref):
    a = jnp.dot(oa_ref[...], wa_ref[...], preferred_element_type=F32)
    b = jnp.dot(or_ref[...], wb_ref[...], preferred_element_type=F32)
    merged = g_ref[:, 0:D_MODEL].astype(F32) * a + g_ref[:, D_MODEL:].astype(F32) * b
    y = jnp.dot(merged.astype(BF16), wo_ref[...], preferred_element_type=F32)
    x1 = _layer_norm(DN_ALPHA * x_ref[...] + y, lg_ref[...], lb_ref[...])
    x1_ref[...] = x1
    _store_packed(xp_ref, x1)
    logits = lax.dot_general(wrt_ref[...], x1.astype(BF16), (((1,), (1,)), ((), ())), preferred_element_type=F32)
    st_ref[...] = jax.nn.sigmoid(logits)


def _mix(x2d, oa, orr, gm, wa, wb, wo, lg, lb, wrt, tm):
    t = x2d.shape[0]
    row = lambda w: pl.BlockSpec((tm, w), lambda i: (i, 0))
    full = lambda a: pl.BlockSpec(a.shape, lambda i: (0,) * a.ndim)
    return pl.pallas_call(
        _mix_kernel,
        grid=(t // tm,),
        in_specs=[row(D_MODEL), row(PQ_W), row(QR_W), row(2 * D_MODEL),
                  full(wa), full(wb), full(wo), full(lg), full(lb), full(wrt)],
        out_specs=[row(D_MODEL), pl.BlockSpec((N_EXPERTS, tm), lambda i: (0, i)),
                   pl.BlockSpec((tm * PACK_SUB, 128), lambda i: (i, 0))],
        out_shape=[jax.ShapeDtypeStruct((t, D_MODEL), F32), jax.ShapeDtypeStruct((N_EXPERTS, t), F32),
                   jax.ShapeDtypeStruct((t * PACK_SUB, 128), U32)],
        compiler_params=_cparams(("parallel",)),
        name="mix",
    )(x2d, oa, orr, gm, wa, wb, wo, lg, lb, wrt)


def _route_kernel(s_ref, bias_ref, cnt0_ref, e_ref, w_ref, r_ref, cnt_ref, carry_ref):
    @pl.when(pl.program_id(0) == 0)
    def _():
        carry_ref[...] = cnt0_ref[...].astype(F32)

    s = s_ref[...]
    tl = s.shape[1]
    choice = s + bias_ref[...]
    row = lax.broadcasted_iota(I32, (N_EXPERTS, tl), 0)
    row_g = lax.broadcasted_iota(I32, (GROUP_SIZE, tl), 0)
    neg = -jnp.inf

    scores = []
    for g in range(N_GROUPS):
        blk = choice[g * GROUP_SIZE:(g + 1) * GROUP_SIZE]
        m1 = jnp.max(blk, axis=0, keepdims=True)
        i1 = jnp.min(jnp.where(blk == m1, row_g, GROUP_SIZE), axis=0, keepdims=True)
        m2 = jnp.max(jnp.where(row_g == i1, neg, blk), axis=0, keepdims=True)
        scores.append(m1 + m2)
    sc = jnp.concatenate(scores, axis=0)
    gi = lax.broadcasted_iota(I32, sc.shape, 0)
    grank = jnp.zeros(sc.shape, I32)
    for g in range(N_GROUPS):
        other = sc[g:g + 1]
        ahead = jnp.where(other > sc, 1, jnp.where(other == sc, jnp.where(gi > g, 1, 0), 0))
        grank = grank + ahead
    cm = jnp.concatenate(
        [jnp.where(grank[g:g + 1] < TOPK_GROUPS, choice[g * GROUP_SIZE:(g + 1) * GROUP_SIZE], neg)
         for g in range(N_GROUPS)], axis=0)

    experts, weights = [], []
    for _ in range(TOP_K):
        m = jnp.max(cm, axis=0, keepdims=True)
        idx = jnp.min(jnp.where(cm == m, row, N_EXPERTS), axis=0, keepdims=True)
        hit = row == idx
        weights.append(jnp.sum(jnp.where(hit, s, 0.0), axis=0, keepdims=True))
        cm = jnp.where(hit, neg, cm)
        experts.append(idx)
    e8 = jnp.concatenate(experts, axis=0)
    w8 = jnp.concatenate(weights, axis=0)
    e_ref[...] = e8
    w_ref[...] = w8 / jnp.sum(w8, axis=0, keepdims=True) * ROUTED_SCALE

    before = (lax.broadcasted_iota(I32, (tl, tl), 0) < lax.broadcasted_iota(I32, (tl, tl), 1))
    before = jnp.where(before, 1.0, 0.0).astype(BF16)
    carry = carry_ref[...]
    ranks = []
    for k in range(TOP_K):
        hit = row == experts[k]
        onehot = jnp.where(hit, 1.0, 0.0)
        prefix = jnp.dot(onehot.astype(BF16), before, preferred_element_type=F32)
        ranks.append(jnp.sum(jnp.where(hit, prefix + carry, 0.0), axis=0, keepdims=True))
        carry = carry + jnp.sum(onehot, axis=1, keepdims=True)
    r_ref[...] = jnp.concatenate(ranks, axis=0).astype(I32)
    carry_ref[...] = carry
    cnt_ref[...] = carry.astype(I32)


def _route(st, bias_col, counts0, tl):
    t = st.shape[1]
    tok = pl.BlockSpec((TOP_K, tl), lambda i: (0, i))
    return pl.pallas_call(
        _route_kernel,
        grid=(t // tl,),
        in_specs=[pl.BlockSpec((N_EXPERTS, tl), lambda i: (0, i)), pl.BlockSpec((N_EXPERTS, 1), lambda i: (0, 0)),
                  pl.BlockSpec((N_EXPERTS, 1), lambda i: (0, 0))],
        out_specs=[tok, tok, tok, pl.BlockSpec((N_EXPERTS, 1), lambda i: (0, 0))],
        out_shape=[jax.ShapeDtypeStruct((TOP_K, t), I32), jax.ShapeDtypeStruct((TOP_K, t), F32),
                   jax.ShapeDtypeStruct((TOP_K, t), I32), jax.ShapeDtypeStruct((N_EXPERTS, 1), I32)],
        scratch_shapes=[pltpu.VMEM((N_EXPERTS, 1), F32)],
        compiler_params=_cparams(("arbitrary",)),
        name="route",
    )(st, bias_col, counts0)


def _dest_kernel(e_ref, r_ref, ps_ref, d_ref):
    tl = e_ref.shape[1]
    row = lax.broadcasted_iota(I32, (N_EXPERTS, tl), 0)
    ps = ps_ref[...]
    starts = [jnp.sum(jnp.where(row == e_ref[k:k + 1, :], ps, 0.0), axis=0, keepdims=True) for k in range(TOP_K)]
    d_ref[...] = jnp.concatenate(starts, axis=0).astype(I32) + r_ref[...]


def _dest(e8, r8, pstart_col, tl):
    t = e8.shape[1]
    tok = pl.BlockSpec((TOP_K, tl), lambda i: (0, i))
    return pl.pallas_call(
        _dest_kernel,
        grid=(t // tl,),
        in_specs=[tok, tok, pl.BlockSpec((N_EXPERTS, 1), lambda i: (0, 0))],
        out_specs=tok,
        out_shape=jax.ShapeDtypeStruct((TOP_K, t), I32),
        compiler_params=_cparams(("parallel",)),
        name="dest",
    )(e8, r8, pstart_col)


TOKEN_GROUP = 8


def _row_copy(src, src_tok, dst, dst_tok, sem):
    s0 = pl.multiple_of(src_tok * PACK_SUB, PACK_SUB)
    d0 = pl.multiple_of(dst_tok * PACK_SUB, PACK_SUB)
    return pltpu.make_async_copy(src.at[pl.ds(s0, PACK_SUB)], dst.at[pl.ds(d0, PACK_SUB)], sem)


def _wait_bytes_of(ref, sem):
    pltpu.make_async_copy(ref, ref, sem).wait()


def _dispatch_kernel(d_ref, x_ref, wsg_ref, wsu_ref, wsd_ref, *refs, td):
    xs_hbm, sh_ref, sem = refs[-3:]

    def start(g, carry):
        for j in range(TOKEN_GROUP):
            t = g * TOKEN_GROUP + j
            for k in range(TOP_K):
                _row_copy(x_ref, t, xs_hbm, d_ref[t * TOP_K + k], sem).start(priority=k % 2)
        return carry

    lax.fori_loop(0, td // TOKEN_GROUP, start, 0)

    hi, lo = _load_packed(x_ref, td)
    xb = jnp.concatenate([hi.astype(BF16), lo.astype(BF16)], axis=1)
    hs = _silu(jnp.dot(xb, wsg_ref[...], preferred_element_type=F32)) * jnp.dot(xb, wsu_ref[...],
                                                                               preferred_element_type=F32)
    sh_ref[...] = jnp.dot(hs.astype(BF16), wsd_ref[...], preferred_element_type=F32)

    for k in range(TOP_K):
        _wait_bytes_of(x_ref, sem)


def _dispatch(d8, xp, wsg, wsu, wsd, n_rows, td, xs_prev=None):
    t = d8.shape[0] // TOP_K
    full = lambda a: pl.BlockSpec(a.shape, lambda i: (0,) * a.ndim)
    in_specs = [pl.BlockSpec((td * TOP_K,), lambda i: (i,), memory_space=pltpu.SMEM),
                pl.BlockSpec((td * PACK_SUB, 128), lambda i: (i, 0)), full(wsg), full(wsu), full(wsd)]
    args = [d8, xp, wsg, wsu, wsd]
    if xs_prev is not None:
        in_specs.append(pl.BlockSpec(memory_space=pl.ANY))
        args.append(xs_prev)
    return pl.pallas_call(
        functools.partial(_dispatch_kernel, td=td),
        grid=(t // td,),
        in_specs=in_specs,
        out_specs=[pl.BlockSpec(memory_space=pl.ANY), pl.BlockSpec((td, D_MODEL), lambda i: (i, 0))],
        scratch_shapes=[pltpu.SemaphoreType.DMA(())],
        out_shape=[jax.ShapeDtypeStruct((n_rows * PACK_SUB, 128), U32), jax.ShapeDtypeStruct((t, D_MODEL), F32)],
        input_output_aliases={} if xs_prev is None else {5: 0},
        compiler_params=_cparams(("arbitrary",)),
        name="dispatch",
    )(*args)


X_RING = 3


def _moe_kernel(be_ref, nv_ref, x_hbm, wg_ref, wu_ref, wd_ref, y_ref, x_ring, x_sem, wg_bf, wu_bf, wd_bf):
    j = pl.program_id(0)
    n = pl.num_programs(0)
    nv = nv_ref[j]
    blk_rows = MOE_ROWS * PACK_SUB

    def x_copy(blk):
        r0 = pl.multiple_of(blk * blk_rows, blk_rows)
        slot = blk % X_RING
        return pltpu.make_async_copy(x_hbm.at[pl.ds(r0, blk_rows)], x_ring.at[slot], x_sem.at[slot])

    def start_if_real(blk):
        @pl.when((blk < n) & (nv_ref[jnp.minimum(blk, n - 1)] > 0))
        def _():
            x_copy(blk).start()

    @pl.when(j == 0)
    def _():
        for blk in range(X_RING - 1):
            start_if_real(jnp.int32(blk))

    start_if_real(j + X_RING - 1)

    @pl.when((j == 0) | (be_ref[j] != be_ref[jnp.maximum(j - 1, 0)]))
    def _():
        wg_bf[...] = wg_ref[0].astype(BF16)
        wu_bf[...] = wu_ref[0].astype(BF16)
        wd_bf[...] = wd_ref[0].astype(BF16)

    @pl.when(nv > 0)
    def _():
        x_copy(j).wait()
        hi, lo = _load_packed(x_ring.at[j % X_RING], MOE_ROWS)
        valid = lax.broadcasted_iota(I32, hi.shape, 0) < nv
        xb = jnp.concatenate([jnp.where(valid, hi, 0.0).astype(BF16), jnp.where(valid, lo, 0.0).astype(BF16)], axis=1)
        g = jnp.dot(xb, wg_bf[...], preferred_element_type=F32)
        u = jnp.dot(xb, wu_bf[...], preferred_element_type=F32)
        h = (_silu(g) * u).astype(BF16)
        _store_packed(y_ref, jnp.dot(h, wd_bf[...], preferred_element_type=F32))

    @pl.when(nv == 0)
    def _():
        y_ref[...] = jnp.zeros_like(y_ref)


def _moe(blk_exp, blk_valid, xs, wg, wu, wd):
    n_blocks = blk_exp.shape[0]
    rows = pl.BlockSpec((MOE_ROWS * PACK_SUB, 128), lambda j, be, nv: (j, 0))
    return pl.pallas_call(
        _moe_kernel,
        grid_spec=pltpu.PrefetchScalarGridSpec(
            num_scalar_prefetch=2,
            grid=(n_blocks,),
            in_specs=[pl.BlockSpec(memory_space=pl.ANY),
                      pl.BlockSpec((1, D_MODEL, D_EXPERT), lambda j, be, nv: (be[j], 0, 0)),
                      pl.BlockSpec((1, D_MODEL, D_EXPERT), lambda j, be, nv: (be[j], 0, 0)),
                      pl.BlockSpec((1, D_EXPERT, D_MODEL), lambda j, be, nv: (be[j], 0, 0))],
            out_specs=rows,
            scratch_shapes=[pltpu.VMEM((X_RING, MOE_ROWS * PACK_SUB, 128), U32), pltpu.SemaphoreType.DMA((X_RING,)),
                            pltpu.VMEM((D_MODEL, D_EXPERT), BF16), pltpu.VMEM((D_MODEL, D_EXPERT), BF16),
                            pltpu.VMEM((D_EXPERT, D_MODEL), BF16)],
        ),
        out_shape=jax.ShapeDtypeStruct(xs.shape, U32),
        compiler_params=_cparams(("arbitrary",)),
        name="moe",
    )(blk_exp, blk_valid, xs, wg, wu, wd)


COMBINE_ROWS = 32


def _gather_rows(ys_hbm, d_ref, slot_buf, sem, t0, n):
    for j in range(n):
        t = t0 + j
        for k in range(TOP_K):
            _row_copy(ys_hbm, d_ref[t * TOP_K + k], slot_buf.at[k], t, sem).start(priority=k % 2)


def _final_kernel(d_ref, dnext_ref, x1_ref, w_ref, p_ref, sh_ref, lg_ref, lb_ref,
                  wpp_ref, wpg_ref, ys_hbm, out_ref, gbuf, ybuf, sem, *, tf):
    i = pl.program_id(0)
    slot = i % 2

    @pl.when(i == 0)
    def _():
        def first(g, carry):
            _gather_rows(ys_hbm, d_ref, gbuf.at[0], sem.at[0], g * TOKEN_GROUP, TOKEN_GROUP)
            return carry

        lax.fori_loop(0, tf // TOKEN_GROUP, first, 0)

    _wait_bytes_of(gbuf.at[slot], sem.at[slot])

    def combine(sb, carry):
        r0 = pl.multiple_of(sb * COMBINE_ROWS, COMBINE_ROWS)
        _gather_rows(ys_hbm, dnext_ref, gbuf.at[1 - slot], sem.at[1 - slot], r0, COMBINE_ROWS)
        y_hi = jnp.zeros((COMBINE_ROWS, PACK_W), F32)
        y_lo = jnp.zeros((COMBINE_ROWS, PACK_W), F32)
        for k in range(TOP_K):
            rows = gbuf.at[slot, k, pl.ds(pl.multiple_of(r0 * PACK_SUB, COMBINE_ROWS * PACK_SUB),
                                          COMBINE_ROWS * PACK_SUB)]
            hi, lo = _load_packed(rows, COMBINE_ROWS)
            wk = w_ref[pl.ds(r0, COMBINE_ROWS), k:k + 1]
            y_hi = y_hi + wk * hi
            y_lo = y_lo + wk * lo
        ybuf[pl.ds(r0, COMBINE_ROWS), 0:PACK_W] = y_hi
        ybuf[pl.ds(r0, COMBINE_ROWS), PACK_W:] = y_lo
        return carry

    lax.fori_loop(0, tf // COMBINE_ROWS, combine, 0)

    @pl.when(i == pl.num_programs(0) - 1)
    def _():
        _wait_bytes_of(gbuf.at[1 - slot], sem.at[1 - slot])

    y = sh_ref[...] + ybuf[...]
    x2 = _layer_norm(DN_ALPHA * x1_ref[...] + y, lg_ref[...], lb_ref[...])
    gate = jax.nn.sigmoid(jnp.dot(x2.astype(BF16), wpg_ref[...], preferred_element_type=F32))
    out_ref[...] = x2 + gate * jnp.dot(p_ref[...].astype(BF16), wpp_ref[...], preferred_element_type=F32)


def _final(d8, x1, w_tok, p2d, sh, lg, lb, wpp, wpg, ys, tf):
    t = x1.shape[0]
    n_tiles = t // tf
    row = lambda w: pl.BlockSpec((tf, w), lambda i: (i, 0))
    full = lambda a: pl.BlockSpec(a.shape, lambda i: (0,) * a.ndim)
    return pl.pallas_call(
        functools.partial(_final_kernel, tf=tf),
        grid=(n_tiles,),
        in_specs=[pl.BlockSpec((tf * TOP_K,), lambda i: (i,), memory_space=pltpu.SMEM),
                  pl.BlockSpec((tf * TOP_K,), lambda i: (jnp.minimum(i + 1, n_tiles - 1),), memory_space=pltpu.SMEM),
                  row(D_MODEL), row(TOP_K), row(D_PLE), row(D_MODEL),
                  full(lg), full(lb), full(wpp), full(wpg),
                  pl.BlockSpec(memory_space=pl.ANY)],
        out_specs=row(D_MODEL),
        scratch_shapes=[pltpu.VMEM((2, TOP_K, tf * PACK_SUB, 128), U32), pltpu.VMEM((tf, D_MODEL), F32),
                        pltpu.SemaphoreType.DMA((2,))],
        out_shape=jax.ShapeDtypeStruct((t, D_MODEL), F32),
        compiler_params=_cparams(("arbitrary",)),
        name="final",
    )(d8, d8, x1, w_tok, p2d, sh, lg, lb, wpp, wpg, ys)


def _t5_bucket(rel):
    nb = N_BUCKETS // 2
    max_exact = nb // 2
    ret = jnp.where(rel > 0, nb, 0)
    n = jnp.abs(rel)
    nf = jnp.maximum(n, 1).astype(F32)
    large = max_exact + (jnp.log(nf / max_exact) / math.log(MAX_DIST / max_exact) * (nb - max_exact)).astype(I32)
    large = jnp.minimum(large, nb - 1)
    return ret + jnp.where(n < max_exact, n, large)


def _rel_bias(table, q_len, k_len):
    rel = (jnp.arange(k_len, dtype=I32)[None, :] - WINDOW) - jnp.arange(q_len, dtype=I32)[:, None]
    hit = _t5_bucket(rel)[:, :, None, None] == jnp.arange(N_BUCKETS, dtype=I32)[None, None, :, None]
    b = jnp.sum(jnp.where(hit, table.astype(F32)[None, None], 0.0), axis=2)
    return jnp.transpose(b, (2, 0, 1)).reshape(N_KV_A, GROUP_A * q_len, k_len).astype(F32)


def _rotary_tables(pos):
    half = DK_R // 2
    inv = ROPE_BASE ** (-jnp.arange(half, dtype=F32) / half)
    ang = pos.astype(F32)[:, None] * inv[None, :]
    cos, sin = jnp.cos(ang), jnp.sin(ang)
    return jnp.concatenate([cos, cos], -1), jnp.concatenate([-sin, sin], -1)


def _decay_tables(c):
    log_gamma = jnp.log1p(-jnp.exp2(-5.0 - jnp.arange(N_HEADS_R, dtype=F32)))
    idx = jnp.arange(c, dtype=F32)
    di = jnp.exp(jnp.abs(idx[:, None] - idx[None, :])[None] * log_gamma[:, None, None])
    qd = jnp.exp((idx[None, :] + 1.0) * log_gamma[:, None])
    kd = jnp.exp((c - 1.0 - idx[None, :]) * log_gamma[:, None])
    gc = jnp.exp(c * log_gamma)
    bc = lambda a: jnp.broadcast_to(a[:, :, None], (N_HEADS_R, c, DK_R))
    return di, bc(qd), bc(kd), jnp.broadcast_to(gc[:, None, None], (N_HEADS_R, DK_R, DK_R))


def _pad_heads(w, n_heads):
    rows = w.shape[0]
    w = w.reshape(rows, n_heads, HEAD_DIM_A)
    w = jnp.pad(w, ((0, 0), (0, 0), (0, HEAD_PAD - HEAD_DIM_A)))
    return w.reshape(rows, n_heads * HEAD_PAD)


def _moe_ffn(groups, wts):
    tiles = [min(TOKEN_TILE, g[0].shape[0]) for g in groups]
    counts = jnp.zeros((N_EXPERTS, 1), I32)
    routed = []
    for (x1, xp, st, p2d), tile in zip(groups, tiles):
        e8, w8, r8, counts = _route(st, wts['router_bias'], counts, tile)
        routed.append((e8, w8, r8))
    counts = counts[:, 0]
    padded = (counts + MOE_ROWS - 1) // MOE_ROWS * MOE_ROWS
    pad_end = jnp.cumsum(padded)
    pstart = (pad_end - padded).astype(I32)
    n_rows = sum(g[0].shape[0] for g in groups) * TOP_K + N_EXPERTS * MOE_ROWS
    blk_start = jnp.arange(n_rows // MOE_ROWS, dtype=I32) * MOE_ROWS
    blk_exp = jnp.minimum(jnp.sum(blk_start[:, None] >= pad_end[None, :], axis=1), N_EXPERTS - 1).astype(I32)
    own = blk_exp[:, None] == jnp.arange(N_EXPERTS, dtype=I32)[None, :]
    blk_end = jnp.sum(jnp.where(own, (pstart + counts)[None, :], 0), axis=1)
    blk_valid = jnp.clip(blk_end - blk_start, 0, MOE_ROWS).astype(I32)
    pstart_col = pstart.astype(F32).reshape(N_EXPERTS, 1)
    dests, shared, xs = [], [], None
    for (x1, xp, st, p2d), (e8, w8, r8), tile in zip(groups, routed, tiles):
        d8 = _dest(e8, r8, pstart_col, tile).T.reshape(-1)
        xs, sh = _dispatch(d8, xp, wts['w_sh_gate'], wts['w_sh_up'], wts['w_sh_down'], n_rows, tile, xs)
        dests.append(d8)
        shared.append(sh)
    ys = _moe(blk_exp, blk_valid, xs, wts['w_exp_gate'], wts['w_exp_up'], wts['w_exp_down'])
    return [_final(d8, x1, w8.T, p2d, sh, wts['ln2_g'], wts['ln2_b'], wts['w_ple_proj'], wts['w_ple_gate'], ys,
                   min(COMBINE_TILE, tile))
            for (x1, xp, st, p2d), (e8, w8, r8), d8, sh, tile in zip(groups, routed, dests, shared, tiles)]


def _mix_group(x2d, oa, orr, gm, p2d, wts):
    x1, st, xp = _mix(x2d, oa, orr, gm, wts['w_branch_attn'], wts['w_branch_ret'], wts['w_out'],
                      wts['ln1_g'], wts['ln1_b'], wts['w_router_t'], min(512, x2d.shape[0]))
    return x1, xp, st, p2d


def kernel(x_prompt, x_sample, cache_attn_k, cache_attn_v, state_retention, p_prompt, p_sample,
           w_in, attn_sinks, rel_bias_table, ret_gn_gain, w_branch_attn, w_branch_ret, w_out,
           ln1_g, ln1_b, w_router, router_bias, w_exp_gate, w_exp_up, w_exp_down,
           w_sh_gate, w_sh_up, w_sh_down, ln2_g, ln2_b, w_ple_proj, w_ple_gate):
    b, s, _ = x_prompt.shape
    bd, l, _ = x_sample.shape
    i = 0
    row = lambda a: a.reshape(1, -1).astype(F32)
    wts = dict(
        w_branch_attn=_pad_heads(w_branch_attn[i].astype(BF16).T, N_HEADS_A).T,
        w_branch_ret=w_branch_ret[i].astype(BF16),
        w_out=w_out[i].astype(BF16), ln1_g=row(ln1_g[i]), ln1_b=row(ln1_b[i]),
        w_router_t=w_router[i].T.astype(BF16), router_bias=router_bias[i].reshape(N_EXPERTS, 1).astype(F32),
        w_exp_gate=w_exp_gate[i], w_exp_up=w_exp_up[i], w_exp_down=w_exp_down[i],
        w_sh_gate=w_sh_gate[i].astype(BF16), w_sh_up=w_sh_up[i].astype(BF16), w_sh_down=w_sh_down[i].astype(BF16),
        ln2_g=row(ln2_g[i]), ln2_b=row(ln2_b[i]),
        w_ple_proj=w_ple_proj[i].astype(BF16), w_ple_gate=w_ple_gate[i].astype(BF16))
    w_in_bf = w_in[i].astype(BF16)
    w_in_bf = jnp.concatenate([_pad_heads(w_in_bf[:, :QA_W] * (HEAD_DIM_A ** -0.5), N_HEADS_A),
                               _pad_heads(w_in_bf[:, OFF_KV:OFF_QR], 2 * N_KV_A), w_in_bf[:, OFF_KV:]], axis=1)
    gain = row(ret_gn_gain[i])
    sinks = attn_sinks[i].astype(F32).reshape(N_KV_A, GROUP_A)

    xp = x_prompt.reshape(b * s, D_MODEL)
    cos_p, sin_p = _rotary_tables(jnp.arange(s, dtype=I32))
    qkv, kvf, qkr, vg, gm = _inproj(xp, w_in_bf, cos_p, sin_p, min(512, s))
    bias_p = jnp.pad(_rel_bias(rel_bias_table, CHUNK, WINDOW + CHUNK), ((0, 0), (0, 0), (0, KEY_PAD - WINDOW - CHUNK)))
    sink_p = jnp.repeat(sinks, CHUNK, axis=1)[..., None]
    fill_p = jnp.where(jnp.arange(KEY_PAD)[None, None, :] == WINDOW + CHUNK, sink_p, NEG_INF).astype(F32)
    oa = _attn_prompt(qkv, bias_p, fill_p, b, s)
    s0 = jnp.zeros((b, N_HEADS_R, DK_R, DK_R), F32)
    orr, ns_p = _retention(qkr, vg, s0, _decay_tables(CHUNK), gain, b, s, CHUNK)
    group_p = _mix_group(xp, oa, orr, gm, p_prompt[i].reshape(b * s, D_PLE), wts)
    kv_tail = kvf.reshape(b, s, 2, N_KV_A, HEAD_DIM_A)[:, s - WINDOW:]
    nk_p, nv_p = kv_tail[:, :, 0], kv_tail[:, :, 1]

    xs = x_sample.reshape(bd * l, D_MODEL)
    cos_s, sin_s = _rotary_tables(jnp.tile(PAST_LEN + jnp.arange(l, dtype=I32), bd))
    qkv, kvf, qkr, vg, gm = _inproj(xs, w_in_bf, cos_s, sin_s, bd * l)
    sink_s = jnp.repeat(sinks, l, axis=1)[..., None]
    oa, nk_s, nv_s = _attn_sample(qkv, kvf, cache_attn_k[i].reshape(bd, WINDOW, KA_W),
                                  cache_attn_v[i].reshape(bd, WINDOW, KA_W),
                                  _rel_bias(rel_bias_table, l, WINDOW + l), sink_s, bd, l)
    orr, ns_s = _retention(qkr, vg, state_retention[i].astype(F32), _decay_tables(l), gain, bd, l, l)
    group_s = _mix_group(xs, oa, orr, gm, p_sample[i].reshape(bd * l, D_PLE), wts)
    y_p, y_s = _moe_ffn([group_p, group_s], wts)
    y_p, y_s = y_p.reshape(b, s, D_MODEL), y_s.reshape(bd, l, D_MODEL)
    shape_kv = (bd, WINDOW, N_KV_A, HEAD_DIM_A)
    return (y_p, y_s, nk_p[None], nv_p[None], ns_p[None],
            nk_s.reshape(shape_kv)[None], nv_s.reshape(shape_kv)[None], ns_s[None])
```

```python
import functools
import math

import jax
import jax.numpy as jnp
from jax import lax
from jax.experimental import pallas as pl
from jax.experimental.pallas import tpu as pltpu

F32 = jnp.float32
BF16 = jnp.bfloat16
I32 = jnp.int32

D_MODEL = 1024
CHUNK = 64
WINDOW = 128
N_HEADS_A = 8
N_KV_A = 2
GROUP_A = 4
HEAD_DIM_A = 64
N_BUCKETS = 32
MAX_DIST = 128
N_HEADS_R = 4
DK_R = 128
ROPE_BASE = 10000.0
N_EXPERTS = 256
TOP_K = 8
N_GROUPS = 8
GROUP_SIZE = N_EXPERTS // N_GROUPS
TOPK_GROUPS = 4
D_EXPERT = 256
ROUTED_SCALE = 2.5
D_PLE = 256
LN_EPS = 1e-5
NEG_INF = -1e30
PAST_LEN = 2048
DEPTH = 1
DN_ALPHA = (2 * DEPTH) ** 0.25

QA_W = N_HEADS_A * HEAD_DIM_A
KA_W = N_KV_A * HEAD_DIM_A
QR_W = N_HEADS_R * DK_R
OFF_KV = QA_W
OFF_QR = QA_W + 2 * KA_W
OFF_VR = OFF_QR + 2 * QR_W
OFF_GM = OFF_VR + 2 * QR_W
IN_W = OFF_GM + 2 * D_MODEL

HEAD_PAD = 128
PQ_W = N_HEADS_A * HEAD_PAD
PKV_W = 2 * N_KV_A * HEAD_PAD
A_KVF = PQ_W + PKV_W
A_QR = A_KVF + 2 * KA_W
A_VR = A_QR + 2 * QR_W
A_GM = A_VR + 2 * QR_W
A_W = A_GM + 2 * D_MODEL

MOE_ROWS = 512
TOKEN_TILE = 512
COMBINE_TILE = 256
VMEM_LIMIT = 56 * 1024 * 1024


def _cparams(sem, vmem=VMEM_LIMIT):
    return pltpu.CompilerParams(dimension_semantics=sem, vmem_limit_bytes=vmem)


def _layer_norm(h, g, b):
    mu = jnp.mean(h, axis=-1, keepdims=True)
    d = h - mu
    var = jnp.mean(d * d, axis=-1, keepdims=True)
    return d * lax.rsqrt(var + LN_EPS) * g + b


def _silu(x):
    return x * jax.nn.sigmoid(x)


U32 = jnp.uint32
PACK_W = D_MODEL // 2
PACK_SUB = PACK_W // 128


def _bf16_bits(x):
    return pltpu.bitcast(x.astype(BF16).astype(F32), U32)


def _store_packed(ref, x):
    n = x.shape[0]
    words = _bf16_bits(x[:, :PACK_W]) | (_bf16_bits(x[:, PACK_W:]) >> 16)
    for c in range(PACK_SUB):
        ref[pl.ds(c, n, stride=PACK_SUB), :] = words[:, c * 128:(c + 1) * 128]


def _load_packed(ref, n):
    words = jnp.concatenate([ref[pl.ds(c, n, stride=PACK_SUB), :] for c in range(PACK_SUB)], axis=1)
    hi = pltpu.bitcast(words & jnp.uint32(0xFFFF0000), F32)
    lo = pltpu.bitcast(words << 16, F32)
    return hi, lo


def _inproj_kernel(x_ref, w_ref, cos_ref, sin_ref, qkv_ref, kvf_ref, qkr_ref, vg_ref, gm_ref, *, tiles_per_seq):
    xb = x_ref[...].astype(BF16)

    def mm(lo, hi):
        return jnp.dot(xb, w_ref[:, lo:hi], preferred_element_type=F32)

    qkv_ref[:, 0:PQ_W] = mm(0, PQ_W).astype(BF16)
    qkv_ref[:, PQ_W:A_KVF] = mm(PQ_W, A_KVF).astype(BF16)
    last_tile = pl.program_id(0) % tiles_per_seq == tiles_per_seq - 1

    @pl.when(last_tile)
    def _():
        kvf_ref[...] = mm(A_KVF, A_QR)

    @pl.when(jnp.logical_not(last_tile))
    def _():
        kvf_ref[...] = jnp.zeros_like(kvf_ref)
    cos = cos_ref[...]
    sin = sin_ref[...]
    for part in range(2):
        z = mm(A_QR + part * QR_W, A_QR + (part + 1) * QR_W)
        for h in range(N_HEADS_R):
            zh = z[:, h * DK_R:(h + 1) * DK_R]
            r = zh * cos + pltpu.roll(zh, DK_R // 2, axis=1) * sin
            if part == 1:
                r = r * (DK_R ** -0.5)
            c0 = part * QR_W + h * DK_R
            qkr_ref[:, c0:c0 + DK_R] = r.astype(BF16)
    vg_ref[...] = mm(A_VR, A_GM).astype(BF16)
    for half in range(2):
        lo = A_GM + half * D_MODEL
        gm_ref[:, half * D_MODEL:(half + 1) * D_MODEL] = jax.nn.sigmoid(mm(lo, lo + D_MODEL)).astype(BF16)


def _inproj(x2d, w_bf, cos_tab, sin_tab, tm):
    t = x2d.shape[0]
    nper = cos_tab.shape[0] // tm
    row = lambda i: (i, 0)
    return pl.pallas_call(
        functools.partial(_inproj_kernel, tiles_per_seq=nper),
        grid=(t // tm,),
        in_specs=[
            pl.BlockSpec((tm, D_MODEL), row),
            pl.BlockSpec((D_MODEL, A_W), lambda i: (0, 0)),
            pl.BlockSpec((tm, DK_R), lambda i: (i % nper, 0)),
            pl.BlockSpec((tm, DK_R), lambda i: (i % nper, 0)),
        ],
        out_specs=[
            pl.BlockSpec((tm, A_KVF), row),
            pl.BlockSpec((tm, 2 * KA_W), row),
            pl.BlockSpec((tm, 2 * QR_W), row),
            pl.BlockSpec((tm, 2 * QR_W), row),
            pl.BlockSpec((tm, 2 * D_MODEL), row),
        ],
        out_shape=[
            jax.ShapeDtypeStruct((t, A_KVF), BF16),
            jax.ShapeDtypeStruct((t, 2 * KA_W), F32),
            jax.ShapeDtypeStruct((t, 2 * QR_W), BF16),
            jax.ShapeDtypeStruct((t, 2 * QR_W), BF16),
            jax.ShapeDtypeStruct((t, 2 * D_MODEL), BF16),
        ],
        compiler_params=_cparams(("parallel",)),
        name="inproj",
    )(x2d, w_bf, cos_tab, sin_tab)


def _attend(q4, k, v, bias, sink):
    s = lax.dot_general(q4, k, (((1,), (1,)), ((), ())), preferred_element_type=F32) + bias
    m = jnp.maximum(jnp.max(s, axis=-1, keepdims=True), sink)
    e = jnp.exp(s - m)
    p = e / (jnp.sum(e, axis=-1, keepdims=True) + jnp.exp(sink - m))
    return jnp.dot(p.astype(BF16), v, preferred_element_type=F32)


KEY_PAD = 256


def _attend_sink_column(q4, k, v, bias, fill, valid):
    s = lax.dot_general(q4, k, (((1,), (1,)), ((), ())), preferred_element_type=F32)
    s = jnp.where(valid, s + bias, fill)
    e = jnp.exp(s - jnp.max(s, axis=-1, keepdims=True)).astype(BF16)
    dv = v.shape[1]
    ones = jnp.ones((KEY_PAD, dv), BF16)
    v_rows = jnp.concatenate([v, jnp.zeros((KEY_PAD - WINDOW - CHUNK, dv), BF16)], axis=0)
    both = jnp.dot(e, jnp.concatenate([v_rows, ones], axis=1), preferred_element_type=F32)
    return both[:, 0:dv] / both[:, dv:]


def _attn_prompt_kernel(q_ref, kvc_ref, kvp_ref, bias_ref, fill_ref, o_ref, kv_buf, *, n_chunks):
    i = pl.program_id(1)
    qb = n_chunks * CHUNK
    kv_buf[0:WINDOW, :] = kvp_ref[...]
    kv_buf[WINDOW:WINDOW + qb, :] = kvc_ref[...]
    kv_buf[WINDOW + qb:, :] = jnp.zeros((KEY_PAD - WINDOW - CHUNK, PKV_W), BF16)
    col = lax.broadcasted_iota(I32, (GROUP_A * CHUNK, KEY_PAD), 1)
    key_col = jnp.where(col < WINDOW + CHUNK, col, -1)

    def chunk(j, carry):
        r0 = pl.multiple_of(j * CHUNK, CHUNK)
        valid = key_col >= jnp.maximum(WINDOW - (i * n_chunks + j) * CHUNK, 0)
        for kv in range(N_KV_A):
            q4 = jnp.concatenate(
                [q_ref[pl.ds(r0, CHUNK), (kv * GROUP_A + g) * HEAD_PAD:(kv * GROUP_A + g + 1) * HEAD_PAD]
                 for g in range(GROUP_A)], axis=0)
            k = kv_buf[pl.ds(r0, KEY_PAD), kv * HEAD_PAD:(kv + 1) * HEAD_PAD]
            v = kv_buf[pl.ds(r0, WINDOW + CHUNK), (N_KV_A + kv) * HEAD_PAD:(N_KV_A + kv + 1) * HEAD_PAD]
            o4 = _attend_sink_column(q4, k, v, bias_ref[kv], fill_ref[kv], valid)
            for g in range(GROUP_A):
                c0 = (kv * GROUP_A + g) * HEAD_PAD
                o_ref[pl.ds(r0, CHUNK), c0:c0 + HEAD_PAD] = o4[g * CHUNK:(g + 1) * CHUNK].astype(BF16)
        return carry

    lax.fori_loop(0, n_chunks, chunk, 0, unroll=8 if n_chunks % 8 == 0 else 1)


def _attn_prompt(qkv, bias, fill, b, s):
    qb = min(512, s)
    n_chunks = qb // CHUNK
    nq = s // qb
    per = qb // WINDOW
    kv_col = PQ_W // PKV_W
    return pl.pallas_call(
        functools.partial(_attn_prompt_kernel, n_chunks=n_chunks),
        grid=(b, nq),
        in_specs=[
            pl.BlockSpec((qb, PQ_W), lambda bi, i: (bi * nq + i, 0)),
            pl.BlockSpec((qb, PKV_W), lambda bi, i: (bi * nq + i, kv_col)),
            pl.BlockSpec((WINDOW, PKV_W), lambda bi, i: (jnp.maximum((bi * nq + i) * per - 1, 0), kv_col)),
            pl.BlockSpec((N_KV_A, GROUP_A * CHUNK, KEY_PAD), lambda bi, i: (0, 0, 0)),
            pl.BlockSpec((N_KV_A, GROUP_A * CHUNK, KEY_PAD), lambda bi, i: (0, 0, 0)),
        ],
        out_specs=pl.BlockSpec((qb, PQ_W), lambda bi, i: (bi * nq + i, 0)),
        out_shape=jax.ShapeDtypeStruct((b * s, PQ_W), BF16),
        scratch_shapes=[pltpu.VMEM((qb + KEY_PAD - CHUNK, PKV_W), BF16)],
        compiler_params=_cparams(("parallel", "parallel")),
        name="attn_prompt",
    )(qkv, qkv, qkv, bias, fill)


def _attn_sample_kernel(q_ref, kvf_ref, ck_ref, cv_ref, bias_ref, sink_ref, o_ref, nk_ref, nv_ref, *, l):
    k_all = jnp.concatenate([ck_ref[0], kvf_ref[:, 0:KA_W]], axis=0)
    v_all = jnp.concatenate([cv_ref[0], kvf_ref[:, KA_W:2 * KA_W]], axis=0)
    nk_ref[0] = k_all[l:]
    nv_ref[0] = v_all[l:]
    kb = k_all.astype(BF16)
    vb = v_all.astype(BF16)
    o_ref[...] = jnp.zeros_like(o_ref)
    for kv in range(N_KV_A):
        q4 = jnp.concatenate(
            [q_ref[:, (kv * GROUP_A + g) * HEAD_PAD:(kv * GROUP_A + g) * HEAD_PAD + HEAD_DIM_A]
             for g in range(GROUP_A)], axis=0)
        o4 = _attend(q4, kb[:, kv * HEAD_DIM_A:(kv + 1) * HEAD_DIM_A], vb[:, kv * HEAD_DIM_A:(kv + 1) * HEAD_DIM_A],
                     bias_ref[kv], sink_ref[kv])
        for g in range(GROUP_A):
            c0 = (kv * GROUP_A + g) * HEAD_PAD
            o_ref[:, c0:c0 + HEAD_DIM_A] = o4[g * l:(g + 1) * l].astype(BF16)


def _attn_sample(qkv, kvf, cache_k, cache_v, bias, sink, bd, l):
    cache_spec = pl.BlockSpec((1, WINDOW, KA_W), lambda bi: (bi, 0, 0))
    cache_shape = jax.ShapeDtypeStruct((bd, WINDOW, KA_W), F32)
    return pl.pallas_call(
        functools.partial(_attn_sample_kernel, l=l),
        grid=(bd,),
        in_specs=[
            pl.BlockSpec((l, A_KVF), lambda bi: (bi, 0)),
            pl.BlockSpec((l, 2 * KA_W), lambda bi: (bi, 0)),
            cache_spec, cache_spec,
            pl.BlockSpec((N_KV_A, GROUP_A * l, WINDOW + l), lambda bi: (0, 0, 0)),
            pl.BlockSpec((N_KV_A, GROUP_A * l, 1), lambda bi: (0, 0, 0)),
        ],
        out_specs=[pl.BlockSpec((l, PQ_W), lambda bi: (bi, 0)), cache_spec, cache_spec],
        out_shape=[jax.ShapeDtypeStruct((bd * l, PQ_W), BF16), cache_shape, cache_shape],
        compiler_params=_cparams(("parallel",)),
        name="attn_sample",
    )(qkv, kvf, cache_k, cache_v, bias, sink)


def _retention_kernel(q_ref, k_ref, v_ref, g_ref, s0_ref, di_ref, qd_ref, kd_ref, gc_ref, gain_ref,
                      o_ref, s_ref, *, n_chunks, c):
    @pl.when(pl.program_id(1) == 0)
    def _():
        s_ref[...] = s0_ref[...]

    def chunk(n, carry):
        r0 = pl.multiple_of(n * c, c)
        for h in range(N_HEADS_R):
            cols = slice(h * DK_R, (h + 1) * DK_R)
            qc = q_ref[pl.ds(r0, c), cols]
            kc = k_ref[pl.ds(r0, c), cols]
            vc = v_ref[pl.ds(r0, c), cols]
            state = s_ref[0, h]
            sc = lax.dot_general(qc, kc, (((1,), (1,)), ((), ())), preferred_element_type=F32) * di_ref[h]
            o = jnp.dot(sc.astype(BF16), vc, preferred_element_type=F32)
            q_dec = (qc.astype(F32) * qd_ref[h]).astype(BF16)
            o = o + jnp.dot(q_dec, state.astype(BF16), preferred_element_type=F32)
            k_dec = (kc.astype(F32) * kd_ref[h]).astype(BF16)
            s_ref[0, h] = gc_ref[h] * state + lax.dot_general(k_dec, vc, (((0,), (0,)), ((), ())),
                                                              preferred_element_type=F32)
            mu = jnp.mean(o, axis=-1, keepdims=True)
            d = o - mu
            var = jnp.mean(d * d, axis=-1, keepdims=True)
            y = d * lax.rsqrt(var + LN_EPS) * gain_ref[:, cols]
            o_ref[pl.ds(r0, c), cols] = (y * _silu(g_ref[pl.ds(r0, c), cols].astype(F32))).astype(BF16)
        return carry

    lax.fori_loop(0, n_chunks, chunk, 0, unroll=4 if n_chunks % 4 == 0 else 1)


def _retention(qkr, vg, s0, tabs, gain, b, s, c):
    di, qd, kd, gc = tabs
    st = min(1024, s)
    nt = s // st
    seq = lambda col: pl.BlockSpec((st, QR_W), lambda bi, j: (bi * nt + j, col))
    full = lambda a: pl.BlockSpec(a.shape, lambda bi, j: (0,) * a.ndim)
    st_spec = pl.BlockSpec((1, N_HEADS_R, DK_R, DK_R), lambda bi, j: (bi, 0, 0, 0))
    return pl.pallas_call(
        functools.partial(_retention_kernel, n_chunks=st // c, c=c),
        grid=(b, nt),
        in_specs=[seq(0), seq(1), seq(0), seq(1), st_spec, full(di), full(qd), full(kd), full(gc), full(gain)],
        out_specs=[seq(0), st_spec],
        out_shape=[jax.ShapeDtypeStruct((b * s, QR_W), BF16), jax.ShapeDtypeStruct((b, N_HEADS_R, DK_R, DK_R), F32)],
        compiler_params=_cparams(("parallel", "arbitrary")),
        name="retention",
    )(qkr, qkr, vg, vg, s0, di, qd, kd, gc, gain)


def _mix_kernel(x_ref, oa_ref, or_ref, g_ref, wa_ref, wb_ref, wo_ref, lg_ref, lb_ref, wrt_ref, x1_ref, st_ref, xp_ref):
    a = jnp.dot(oa_ref[...], wa_ref[...], preferred_element_type=F32)
    b = jnp.dot(or_ref[...], wb_ref[...], preferred_element_type=F32)
    merged = g_ref[:, 0:D_MODEL].astype(F32) * a + g_ref[:, D_MODEL:].astype(F32) * b
    y = jnp.dot(merged.astype(BF16), wo_ref[...], preferred_element_type=F32)
    x1 = _layer_norm(DN_ALPHA * x_ref[...] + y, lg_ref[...], lb_ref[...])
    x1_ref[...] = x1
    _store_packed(xp_ref, x1)
    logits = lax.dot_general(wrt_ref[...], x1.astype(BF16), (((1,), (1,)), ((), ())), preferred_element_type=F32)
    st_ref[...] = jax.nn.sigmoid(logits)


def _mix(x2d, oa, orr, gm, wa, wb, wo, lg, lb, wrt, tm):
    t = x2d.shape[0]
    row = lambda w: pl.BlockSpec((tm, w), lambda i: (i, 0))
    full = lambda a: pl.BlockSpec(a.shape, lambda i: (0,) * a.ndim)
    return pl.pallas_call(
        _mix_kernel,
        grid=(t // tm,),
        in_specs=[row(D_MODEL), row(PQ_W), row(QR_W), row(2 * D_MODEL),
                  full(wa), full(wb), full(wo), full(lg), full(lb), full(wrt)],
        out_specs=[row(D_MODEL), pl.BlockSpec((N_EXPERTS, tm), lambda i: (0, i)),
                   pl.BlockSpec((tm * PACK_SUB, 128), lambda i: (i, 0))],
        out_shape=[jax.ShapeDtypeStruct((t, D_MODEL), F32), jax.ShapeDtypeStruct((N_EXPERTS, t), F32),
                   jax.ShapeDtypeStruct((t * PACK_SUB, 128), U32)],
        compiler_params=_cparams(("parallel",)),
        name="mix",
    )(x2d, oa, orr, gm, wa, wb, wo, lg, lb, wrt)


def _route_kernel(s_ref, bias_ref, cnt0_ref, e_ref, w_ref, r_ref, cnt_ref, carry_ref):
    @pl.when(pl.program_id(0) == 0)
    def _():
        carry_ref[...] = cnt0_ref[...].astype(F32)

    s = s_ref[...]
    tl = s.shape[1]
    choice = s + bias_ref[...]
    row = lax.broadcasted_iota(I32, (N_EXPERTS, tl), 0)
    row_g = lax.broadcasted_iota(I32, (GROUP_SIZE, tl), 0)
    neg = -jnp.inf

    scores = []
    for g in range(N_GROUPS):
        blk = choice[g * GROUP_SIZE:(g + 1) * GROUP_SIZE]
        m1 = jnp.max(blk, axis=0, keepdims=True)
        i1 = jnp.min(jnp.where(blk == m1, row_g, GROUP_SIZE), axis=0, keepdims=True)
        m2 = jnp.max(jnp.where(row_g == i1, neg, blk), axis=0, keepdims=True)
        scores.append(m1 + m2)
    sc = jnp.concatenate(scores, axis=0)
    gi = lax.broadcasted_iota(I32, sc.shape, 0)
    grank = jnp.zeros(sc.shape, I32)
    for g in range(N_GROUPS):
        other = sc[g:g + 1]
        ahead = jnp.where(other > sc, 1, jnp.where(other == sc, jnp.where(gi > g, 1, 0), 0))
        grank = grank + ahead
    cm = jnp.concatenate(
        [jnp.where(grank[g:g + 1] < TOPK_GROUPS, choice[g * GROUP_SIZE:(g + 1) * GROUP_SIZE], neg)
         for g in range(N_GROUPS)], axis=0)

    experts, weights = [], []
    for _ in range(TOP_K):
        m = jnp.max(cm, axis=0, keepdims=True)
        idx = jnp.min(jnp.where(cm == m, row, N_EXPERTS), axis=0, keepdims=True)
        hit = row == idx
        weights.append(jnp.sum(jnp.where(hit, s, 0.0), axis=0, keepdims=True))
        cm = jnp.where(hit, neg, cm)
        experts.append(idx)
    e8 = jnp.concatenate(experts, axis=0)
    w8 = jnp.concatenate(weights, axis=0)
    e_ref[...] = e8
    w_ref[...] = w8 / jnp.sum(w8, axis=0, keepdims=True) * ROUTED_SCALE

    before = (lax.broadcasted_iota(I32, (tl, tl), 0) < lax.broadcasted_iota(I32, (tl, tl), 1))
    before = jnp.where(before, 1.0, 0.0).astype(BF16)
    carry = carry_ref[...]
    ranks = []
    for k in range(TOP_K):
        hit = row == experts[k]
        onehot = jnp.where(hit, 1.0, 0.0)
        prefix = jnp.dot(onehot.astype(BF16), before, preferred_element_type=F32)
        ranks.append(jnp.sum(jnp.where(hit, prefix + carry, 0.0), axis=0, keepdims=True))
        carry = carry + jnp.sum(onehot, axis=1, keepdims=True)
    r_ref[...] = jnp.concatenate(ranks, axis=0).astype(I32)
    carry_ref[...] = carry
    cnt_ref[...] = carry.astype(I32)


def _route(st, bias_col, counts0, tl):
    t = st.shape[1]
    tok = pl.BlockSpec((TOP_K, tl), lambda i: (0, i))
    return pl.pallas_call(
        _route_kernel,
        grid=(t // tl,),
        in_specs=[pl.BlockSpec((N_EXPERTS, tl), lambda i: (0, i)), pl.BlockSpec((N_EXPERTS, 1), lambda i: (0, 0)),
                  pl.BlockSpec((N_EXPERTS, 1), lambda i: (0, 0))],
        out_specs=[tok, tok, tok, pl.BlockSpec((N_EXPERTS, 1), lambda i: (0, 0))],
        out_shape=[jax.ShapeDtypeStruct((TOP_K, t), I32), jax.ShapeDtypeStruct((TOP_K, t), F32),
                   jax.ShapeDtypeStruct((TOP_K, t), I32), jax.ShapeDtypeStruct((N_EXPERTS, 1), I32)],
        scratch_shapes=[pltpu.VMEM((N_EXPERTS, 1), F32)],
        compiler_params=_cparams(("arbitrary",)),
        name="route",
    )(st, bias_col, counts0)


def _dest_kernel(e_ref, r_ref, ps_ref, d_ref):
    tl = e_ref.shape[1]
    row = lax.broadcasted_iota(I32, (N_EXPERTS, tl), 0)
    ps = ps_ref[...]
    starts = [jnp.sum(jnp.where(row == e_ref[k:k + 1, :], ps, 0.0), axis=0, keepdims=True) for k in range(TOP_K)]
    d_ref[...] = jnp.concatenate(starts, axis=0).astype(I32) + r_ref[...]


def _dest(e8, r8, pstart_col, tl):
    t = e8.shape[1]
    tok = pl.BlockSpec((TOP_K, tl), lambda i: (0, i))
    return pl.pallas_call(
        _dest_kernel,
        grid=(t // tl,),
        in_specs=[tok, tok, pl.BlockSpec((N_EXPERTS, 1), lambda i: (0, 0))],
        out_specs=tok,
        out_shape=jax.ShapeDtypeStruct((TOP_K, t), I32),
        compiler_params=_cparams(("parallel",)),
        name="dest",
    )(e8, r8, pstart_col)


TOKEN_GROUP = 8


def _row_copy(src, src_tok, dst, dst_tok, sem):
    s0 = pl.multiple_of(src_tok * PACK_SUB, PACK_SUB)
    d0 = pl.multiple_of(dst_tok * PACK_SUB, PACK_SUB)
    return pltpu.make_async_copy(src.at[pl.ds(s0, PACK_SUB)], dst.at[pl.ds(d0, PACK_SUB)], sem)


def _wait_bytes_of(ref, sem):
    pltpu.make_async_copy(ref, ref, sem).wait()


def _dispatch_kernel(d_ref, x_ref, wsg_ref, wsu_ref, wsd_ref, *refs, td):
    xs_hbm, sh_ref, sem = refs[-3:]

    def start(g, carry):
        for j in range(TOKEN_GROUP):
            t = g * TOKEN_GROUP + j
            for k in range(TOP_K):
                _row_copy(x_ref, t, xs_hbm, d_ref[t * TOP_K + k], sem).start(priority=k % 2)
        return carry

    lax.fori_loop(0, td // TOKEN_GROUP, start, 0)

    hi, lo = _load_packed(x_ref, td)
    xb = jnp.concatenate([hi.astype(BF16), lo.astype(BF16)], axis=1)
    hs = _silu(jnp.dot(xb, wsg_ref[...], preferred_element_type=F32)) * jnp.dot(xb, wsu_ref[...],
                                                                               preferred_element_type=F32)
    sh_ref[...] = jnp.dot(hs.astype(BF16), wsd_ref[...], preferred_element_type=F32)

    for k in range(TOP_K):
        _wait_bytes_of(x_ref, sem)


def _dispatch(d8, xp, wsg, wsu, wsd, n_rows, td, xs_prev=None):
    t = d8.shape[0] // TOP_K
    full = lambda a: pl.BlockSpec(a.shape, lambda i: (0,) * a.ndim)
    in_specs = [pl.BlockSpec((td * TOP_K,), lambda i: (i,), memory_space=pltpu.SMEM),
                pl.BlockSpec((td * PACK_SUB, 128), lambda i: (i, 0)), full(wsg), full(wsu), full(wsd)]
    args = [d8, xp, wsg, wsu, wsd]
    if xs_prev is not None:
        in_specs.append(pl.BlockSpec(memory_space=pl.ANY))
        args.append(xs_prev)
    return pl.pallas_call(
        functools.partial(_dispatch_kernel, td=td),
        grid=(t // td,),
        in_specs=in_specs,
        out_specs=[pl.BlockSpec(memory_space=pl.ANY), pl.BlockSpec((td, D_MODEL), lambda i: (i, 0))],
        scratch_shapes=[pltpu.SemaphoreType.DMA(())],
        out_shape=[jax.ShapeDtypeStruct((n_rows * PACK_SUB, 128), U32), jax.ShapeDtypeStruct((t, D_MODEL), F32)],
        input_output_aliases={} if xs_prev is None else {5: 0},
        compiler_params=_cparams(("arbitrary",)),
        name="dispatch",
    )(*args)


X_RING = 3


def _moe_kernel(be_ref, nv_ref, x_hbm, wg_ref, wu_ref, wd_ref, y_ref, x_ring, x_sem, wg_bf, wu_bf, wd_bf):
    j = pl.program_id(0)
    n = pl.num_programs(0)
    nv = nv_ref[j]
    blk_rows = MOE_ROWS * PACK_SUB

    def x_copy(blk):
        r0 = pl.multiple_of(blk * blk_rows, blk_rows)
        slot = blk % X_RING
        return pltpu.make_async_copy(x_hbm.at[pl.ds(r0, blk_rows)], x_ring.at[slot], x_sem.at[slot])

    def start_if_real(blk):
        @pl.when((blk < n) & (nv_ref[jnp.minimum(blk, n - 1)] > 0))
        def _():
            x_copy(blk).start()

    @pl.when(j == 0)
    def _():
        for blk in range(X_RING - 1):
            start_if_real(jnp.int32(blk))

    start_if_real(j + X_RING - 1)

    @pl.when((j == 0) | (be_ref[j] != be_ref[jnp.maximum(j - 1, 0)]))
    def _():
        wg_bf[...] = wg_ref[0].astype(BF16)
        wu_bf[...] = wu_ref[0].astype(BF16)
        wd_bf[...] = wd_ref[0].astype(BF16)

    @pl.when(nv > 0)
    def _():
        x_copy(j).wait()
        hi, lo = _load_packed(x_ring.at[j % X_RING], MOE_ROWS)
        valid = lax.broadcasted_iota(I32, hi.shape, 0) < nv
        xb = jnp.concatenate([jnp.where(valid, hi, 0.0).astype(BF16), jnp.where(valid, lo, 0.0).astype(BF16)], axis=1)
        g = jnp.dot(xb, wg_bf[...], preferred_element_type=F32)
        u = jnp.dot(xb, wu_bf[...], preferred_element_type=F32)
        h = (_silu(g) * u).astype(BF16)
        _store_packed(y_ref, jnp.dot(h, wd_bf[...], preferred_element_type=F32))

    @pl.when(nv == 0)
    def _():
        y_ref[...] = jnp.zeros_like(y_ref)


def _moe(blk_exp, blk_valid, xs, wg, wu, wd):
    n_blocks = blk_exp.shape[0]
    rows = pl.BlockSpec((MOE_ROWS * PACK_SUB, 128), lambda j, be, nv: (j, 0))
    return pl.pallas_call(
        _moe_kernel,
        grid_spec=pltpu.PrefetchScalarGridSpec(
            num_scalar_prefetch=2,
            grid=(n_blocks,),
            in_specs=[pl.BlockSpec(memory_space=pl.ANY),
                      pl.BlockSpec((1, D_MODEL, D_EXPERT), lambda j, be, nv: (be[j], 0, 0)),
                      pl.BlockSpec((1, D_MODEL, D_EXPERT), lambda j, be, nv: (be[j], 0, 0)),
                      pl.BlockSpec((1, D_EXPERT, D_MODEL), lambda j, be, nv: (be[j], 0, 0))],
            out_specs=rows,
            scratch_shapes=[pltpu.VMEM((X_RING, MOE_ROWS * PACK_SUB, 128), U32), pltpu.SemaphoreType.DMA((X_RING,)),
                            pltpu.VMEM((D_MODEL, D_EXPERT), BF16), pltpu.VMEM((D_MODEL, D_EXPERT), BF16),
                            pltpu.VMEM((D_EXPERT, D_MODEL), BF16)],
        ),
        out_shape=jax.ShapeDtypeStruct(xs.shape, U32),
        compiler_params=_cparams(("arbitrary",)),
        name="moe",
    )(blk_exp, blk_valid, xs, wg, wu, wd)


COMBINE_ROWS = 32


def _gather_rows(ys_hbm, d_ref, slot_buf, sem, t0, n):
    for j in range(n):
        t = t0 + j
        for k in range(TOP_K):
            _row_copy(ys_hbm, d_ref[t * TOP_K + k], slot_buf.at[k], t, sem).start(priority=k % 2)


def _final_kernel(d_ref, dnext_ref, x1_ref, w_ref, p_ref, sh_ref, lg_ref, lb_ref,
                  wpp_ref, wpg_ref, ys_hbm, out_ref, gbuf, ybuf, sem, *, tf):
    i = pl.program_id(0)
    slot = i % 2

    @pl.when(i == 0)
    def _():
        def first(g, carry):
            _gather_rows(ys_hbm, d_ref, gbuf.at[0], sem.at[0], g * TOKEN_GROUP, TOKEN_GROUP)
            return carry

        lax.fori_loop(0, tf // TOKEN_GROUP, first, 0)

    _wait_bytes_of(gbuf.at[slot], sem.at[slot])

    def combine(sb, carry):
        r0 = pl.multiple_of(sb * COMBINE_ROWS, COMBINE_ROWS)
        _gather_rows(ys_hbm, dnext_ref, gbuf.at[1 - slot], sem.at[1 - slot], r0, COMBINE_ROWS)
        y_hi = jnp.zeros((COMBINE_ROWS, PACK_W), F32)
        y_lo = jnp.zeros((COMBINE_ROWS, PACK_W), F32)
        for k in range(TOP_K):
            rows = gbuf.at[slot, k, pl.ds(pl.multiple_of(r0 * PACK_SUB, COMBINE_ROWS * PACK_SUB),
                                          COMBINE_ROWS * PACK_SUB)]
            hi, lo = _load_packed(rows, COMBINE_ROWS)
            wk = w_ref[pl.ds(r0, COMBINE_ROWS), k:k + 1]
            y_hi = y_hi + wk * hi
            y_lo = y_lo + wk * lo
        ybuf[pl.ds(r0, COMBINE_ROWS), 0:PACK_W] = y_hi
        ybuf[pl.ds(r0, COMBINE_ROWS), PACK_W:] = y_lo
        return carry

    lax.fori_loop(0, tf // COMBINE_ROWS, combine, 0)

    @pl.when(i == pl.num_programs(0) - 1)
    def _():
        _wait_bytes_of(gbuf.at[1 - slot], sem.at[1 - slot])

    y = sh_ref[...] + ybuf[...]
    x2 = _layer_norm(DN_ALPHA * x1_ref[...] + y, lg_ref[...], lb_ref[...])
    gate = jax.nn.sigmoid(jnp.dot(x2.astype(BF16), wpg_ref[...], preferred_element_type=F32))
    out_ref[...] = x2 + gate * jnp.dot(p_ref[...].astype(BF16), wpp_ref[...], preferred_element_type=F32)


def _final(d8, x1, w_tok, p2d, sh, lg, lb, wpp, wpg, ys, tf):
    t = x1.shape[0]
    n_tiles = t // tf
    row = lambda w: pl.BlockSpec((tf, w), lambda i: (i, 0))
    full = lambda a: pl.BlockSpec(a.shape, lambda i: (0,) * a.ndim)
    return pl.pallas_call(
        functools.partial(_final_kernel, tf=tf),
        grid=(n_tiles,),
        in_specs=[pl.BlockSpec((tf * TOP_K,), lambda i: (i,), memory_space=pltpu.SMEM),
                  pl.BlockSpec((tf * TOP_K,), lambda i: (jnp.minimum(i + 1, n_tiles - 1),), memory_space=pltpu.SMEM),
                  row(D_MODEL), row(TOP_K), row(D_PLE), row(D_MODEL),
                  full(lg), full(lb), full(wpp), full(wpg),
                  pl.BlockSpec(memory_space=pl.ANY)],
        out_specs=row(D_MODEL),
        scratch_shapes=[pltpu.VMEM((2, TOP_K, tf * PACK_SUB, 128), U32), pltpu.VMEM((tf, D_MODEL), F32),
                        pltpu.SemaphoreType.DMA((2,))],
        out_shape=jax.ShapeDtypeStruct((t, D_MODEL), F32),
        compiler_params=_cparams(("arbitrary",)),
        name="final",
    )(d8, d8, x1, w_tok, p2d, sh, lg, lb, wpp, wpg, ys)


def _t5_bucket(rel):
    nb = N_BUCKETS // 2
    max_exact = nb // 2
    ret = jnp.where(rel > 0, nb, 0)
    n = jnp.abs(rel)
    nf = jnp.maximum(n, 1).astype(F32)
    large = max_exact + (jnp.log(nf / max_exact) / math.log(MAX_DIST / max_exact) * (nb - max_exact)).astype(I32)
    large = jnp.minimum(large, nb - 1)
    return ret + jnp.where(n < max_exact, n, large)


def _rel_bias(table, q_len, k_len):
    rel = (jnp.arange(k_len, dtype=I32)[None, :] - WINDOW) - jnp.arange(q_len, dtype=I32)[:, None]
    hit = _t5_bucket(rel)[:, :, None, None] == jnp.arange(N_BUCKETS, dtype=I32)[None, None, :, None]
    b = jnp.sum(jnp.where(hit, table.astype(F32)[None, None], 0.0), axis=2)
    return jnp.transpose(b, (2, 0, 1)).reshape(N_KV_A, GROUP_A * q_len, k_len).astype(F32)


def _rotary_tables(pos):
    half = DK_R // 2
    inv = ROPE_BASE ** (-jnp.arange(half, dtype=F32) / half)
    ang = pos.astype(F32)[:, None] * inv[None, :]
    cos, sin = jnp.cos(ang), jnp.sin(ang)
    return jnp.concatenate([cos, cos], -1), jnp.concatenate([-sin, sin], -1)


def _decay_tables(c):
    log_gamma = jnp.log1p(-jnp.exp2(-5.0 - jnp.arange(N_HEADS_R, dtype=F32)))
    idx = jnp.arange(c, dtype=F32)
    di = jnp.exp(jnp.abs(idx[:, None] - idx[None, :])[None] * log_gamma[:, None, None])
    qd = jnp.exp((idx[None, :] + 1.0) * log_gamma[:, None])
    kd = jnp.exp((c - 1.0 - idx[None, :]) * log_gamma[:, None])
    gc = jnp.exp(c * log_gamma)
    bc = lambda a: jnp.broadcast_to(a[:, :, None], (N_HEADS_R, c, DK_R))
    return di, bc(qd), bc(kd), jnp.broadcast_to(gc[:, None, None], (N_HEADS_R, DK_R, DK_R))


def _pad_heads(w, n_heads):
    rows = w.shape[0]
    w = w.reshape(rows, n_heads, HEAD_DIM_A)
    w = jnp.pad(w, ((0, 0), (0, 0), (0, HEAD_PAD - HEAD_DIM_A)))
    return w.reshape(rows, n_heads * HEAD_PAD)


def _moe_ffn(groups, wts):
    tiles = [min(TOKEN_TILE, g[0].shape[0]) for g in groups]
    counts = jnp.zeros((N_EXPERTS, 1), I32)
    routed = []
    for (x1, xp, st, p2d), tile in zip(groups, tiles):
        e8, w8, r8, counts = _route(st, wts['router_bias'], counts, tile)
        routed.append((e8, w8, r8))
    counts = counts[:, 0]
    padded = (counts + MOE_ROWS - 1) // MOE_ROWS * MOE_ROWS
    pad_end = jnp.cumsum(padded)
    pstart = (pad_end - padded).astype(I32)
    n_rows = sum(g[0].shape[0] for g in groups) * TOP_K + N_EXPERTS * MOE_ROWS
    blk_start = jnp.arange(n_rows // MOE_ROWS, dtype=I32) * MOE_ROWS
    blk_exp = jnp.minimum(jnp.sum(blk_start[:, None] >= pad_end[None, :], axis=1), N_EXPERTS - 1).astype(I32)
    own = blk_exp[:, None] == jnp.arange(N_EXPERTS, dtype=I32)[None, :]
    blk_end = jnp.sum(jnp.where(own, (pstart + counts)[None, :], 0), axis=1)
    blk_valid = jnp.clip(blk_end - blk_start, 0, MOE_ROWS).astype(I32)
    pstart_col = pstart.astype(F32).reshape(N_EXPERTS, 1)
    dests, shared, xs = [], [], None
    for (x1, xp, st, p2d), (e8, w8, r8), tile in zip(groups, routed, tiles):
        d8 = _dest(e8, r8, pstart_col, tile).T.reshape(-1)
        xs, sh = _dispatch(d8, xp, wts['w_sh_gate'], wts['w_sh_up'], wts['w_sh_down'], n_rows, tile, xs)
        dests.append(d8)
        shared.append(sh)
    ys = _moe(blk_exp, blk_valid, xs, wts['w_exp_gate'], wts['w_exp_up'], wts['w_exp_down'])
    return [_final(d8, x1, w8.T, p2d, sh, wts['ln2_g'], wts['ln2_b'], wts['w_ple_proj'], wts['w_ple_gate'], ys,
                   min(COMBINE_TILE, tile))
            for (x1, xp, st, p2d), (e8, w8, r8), d8, sh, tile in zip(groups, routed, dests, shared, tiles)]


def _mix_group(x2d, oa, orr, gm, p2d, wts):
    x1, st, xp = _mix(x2d, oa, orr, gm, wts['w_branch_attn'], wts['w_branch_ret'], wts['w_out'],
                      wts['ln1_g'], wts['ln1_b'], wts['w_router_t'], min(512, x2d.shape[0]))
    return x1, xp, st, p2d


def kernel(x_prompt, x_sample, cache_attn_k, cache_attn_v, state_retention, p_prompt, p_sample,
           w_in, attn_sinks, rel_bias_table, ret_gn_gain, w_branch_attn, w_branch_ret, w_out,
           ln1_g, ln1_b, w_router, router_bias, w_exp_gate, w_exp_up, w_exp_down,
           w_sh_gate, w_sh_up, w_sh_down, ln2_g, ln2_b, w_ple_proj, w_ple_gate):
    b, s, _ = x_prompt.shape
    bd, l, _ = x_sample.shape
    i = 0
    row = lambda a: a.reshape(1, -1).astype(F32)
    wts = dict(
        w_branch_attn=_pad_heads(w_branch_attn[i].astype(BF16).T, N_HEADS_A).T,
        w_branch_ret=w_branch_ret[i].astype(BF16),
        w_out=w_out[i].astype(BF16), ln1_g=row(ln1_g[i]), ln1_b=row(ln1_b[i]),
        w_router_t=w_router[i].T.astype(BF16), router_bias=router_bias[i].reshape(N_EXPERTS, 1).astype(F32),
        w_exp_gate=w_exp_gate[i], w_exp_up=w_exp_up[i], w_exp_down=w_exp_down[i],
        w_sh_gate=w_sh_gate[i].astype(BF16), w_sh_up=w_sh_up[i].astype(BF16), w_sh_down=w_sh_down[i].astype(BF16),
        ln2_g=row(ln2_g[i]), ln2_b=row(ln2_b[i]),
        w_ple_proj=w_ple_proj[i].astype(BF16), w_ple_gate=w_ple_gate[i].astype(BF16))
    w_in_bf = w_in[i].astype(BF16)
    w_in_bf = jnp.concatenate([_pad_heads(w_in_bf[:, :QA_W] * (HEAD_DIM_A ** -0.5), N_HEADS_A),
                               _pad_heads(w_in_bf[:, OFF_KV:OFF_QR], 2 * N_KV_A), w_in_bf[:, OFF_KV:]], axis=1)
    gain = row(ret_gn_gain[i])
    sinks = attn_sinks[i].astype(F32).reshape(N_KV_A, GROUP_A)

    xp = x_prompt.reshape(b * s, D_MODEL)
    cos_p, sin_p = _rotary_tables(jnp.arange(s, dtype=I32))
    qkv, kvf, qkr, vg, gm = _inproj(xp, w_in_bf, cos_p, sin_p, min(512, s))
    bias_p = jnp.pad(_rel_bias(rel_bias_table, CHUNK, WINDOW + CHUNK), ((0, 0), (0, 0), (0, KEY_PAD - WINDOW - CHUNK)))
    sink_p = jnp.repeat(sinks, CHUNK, axis=1)[..., None]
    fill_p = jnp.where(jnp.arange(KEY_PAD)[None, None, :] == WINDOW + CHUNK, sink_p, NEG_INF).astype(F32)
    oa = _attn_prompt(qkv, bias_p, fill_p, b, s)
    s0 = jnp.zeros((b, N_HEADS_R, DK_R, DK_R), F32)
    orr, ns_p = _retention(qkr, vg, s0, _decay_tables(CHUNK), gain, b, s, CHUNK)
    group_p = _mix_group(xp, oa, orr, gm, p_prompt[i].reshape(b * s, D_PLE), wts)
    kv_tail = kvf.reshape(b, s, 2, N_KV_A, HEAD_DIM_A)[:, s - WINDOW:]
    nk_p, nv_p = kv_tail[:, :, 0], kv_tail[:, :, 1]

    xs = x_sample.reshape(bd * l, D_MODEL)
    cos_s, sin_s = _rotary_tables(jnp.tile(PAST_LEN + jnp.arange(l, dtype=I32), bd))
    qkv, kvf, qkr, vg, gm = _inproj(xs, w_in_bf, cos_s, sin_s, bd * l)
    sink_s = jnp.repeat(sinks, l, axis=1)[..., None]
    oa, nk_s, nv_s = _attn_sample(qkv, kvf, cache_attn_k[i].reshape(bd, WINDOW, KA_W),
                                  cache_attn_v[i].reshape(bd, WINDOW, KA_W),
                                  _rel_bias(rel_bias_table, l, WINDOW + l), sink_s, bd, l)
    orr, ns_s = _retention(qkr, vg, state_retention[i].astype(F32), _decay_tables(l), gain, bd, l, l)
    group_s = _mix_group(xs, oa, orr, gm, p_sample[i].reshape(bd * l, D_PLE), wts)
    y_p, y_s = _moe_ffn([group_p, group_s], wts)
    y_p, y_s = y_p.reshape(b, s, D_MODEL), y_s.reshape(bd, l, D_MODEL)
    shape_kv = (bd, WINDOW, N_KV_A, HEAD_DIM_A)
    return (y_p, y_s, nk_p[None], nv_p[None], ns_p[None],
            nk_s.reshape(shape_kv)[None], nv_s.reshape(shape_kv)[None], ns_s[None])
```
